```python
import math
import jax, jax.numpy as jnp
from jax import lax
import numpy as np

D_MODEL = 1024
BATCH = 8
SEQ = 4096
DEPTH = 4

N_MIXERS = 3
EXPAND = 2
D_INNER = EXPAND * D_MODEL
NORM_EPS = 1e-6

GMLP_CHUNK = 128
GMLP_GROUPS = 8

S5_GROUP = 16
S5_STATE = 64
S5_GROUPS = D_INNER // S5_GROUP
S5_DT_MIN = 1e-3
S5_DT_MAX = 1e-1

MLA_HEADS = 16
MLA_NOPE = 128
MLA_ROPE = 64
MLA_V = D_INNER // MLA_HEADS
MLA_QK_DIM = MLA_NOPE + MLA_ROPE
MLA_Q_RANK = 384
MLA_KV_RANK = 128
MLA_SCALE = MLA_QK_DIM ** -0.5
ROPE_THETA = 10000.0
ATTN_QBLOCK = 128
NEG_INF = -1e30

kernel_name = "hybrid_gmlp_s5_mla_gated"


def _rmsnorm(x, g):
    xf = x.astype(jnp.float32)
    y = xf * lax.rsqrt(jnp.mean(xf * xf, axis=-1, keepdims=True) + NORM_EPS)
    return (y * g.astype(jnp.float32)).astype(x.dtype)


def _layernorm(x, g, b):
    xf = x.astype(jnp.float32)
    mu = jnp.mean(xf, axis=-1, keepdims=True)
    xc = xf - mu
    var = jnp.mean(xc * xc, axis=-1, keepdims=True)
    y = xc * lax.rsqrt(var + NORM_EPS) * g.astype(jnp.float32) + b.astype(jnp.float32)
    return y.astype(x.dtype)


def _rope(x, cos, sin):
    half = x.shape[-1] // 2
    x1 = x[..., :half].astype(jnp.float32)
    x2 = x[..., half:].astype(jnp.float32)
    return jnp.concatenate([x1 * cos - x2 * sin, x2 * cos + x1 * sin], axis=-1).astype(x.dtype)


def _gmlp_mixer(h, w_in, ln_g, ln_b, w_s, b_s, w_out):
    bsz, seq, _ = h.shape
    u, v, z = jnp.split(h @ w_in, 3, axis=-1)
    u = jax.nn.gelu(u)
    v = _layernorm(jax.nn.gelu(v), ln_g, ln_b)
    v = v.reshape(bsz, seq // GMLP_CHUNK, GMLP_CHUNK, GMLP_GROUPS, D_INNER // GMLP_GROUPS)
    causal = jnp.tril(jnp.ones((GMLP_CHUNK, GMLP_CHUNK), dtype=bool))
    w = jnp.where(causal[None], w_s, jnp.zeros((), w_s.dtype))
    s = jnp.einsum('gts,bcsgd->bctgd', w, v) + b_s.T[:, :, None]
    s = s.reshape(bsz, seq, D_INNER)
    return (u * s * jax.nn.silu(z)) @ w_out


def _s5_combine(left, right):
    a_l, b_l = left
    a_r, b_r = right
    return a_r * a_l, a_r * b_l + b_r


def _s5_mixer(h, w_in, a_re, a_im, log_step, b_re, b_im, c_re, c_im, d_skip, w_glu, b_glu, w_out):
    bsz, seq, _ = h.shape
    u, z = jnp.split(h @ w_in, 2, axis=-1)
    uf = u.astype(jnp.float32).reshape(bsz, seq, S5_GROUPS, S5_GROUP)
    lam = lax.complex(a_re.astype(jnp.float32), a_im.astype(jnp.float32))
    step = jnp.exp(log_step.astype(jnp.float32))[:, None]
    lam_bar = jnp.exp(lam * step)
    bmat = lax.complex(b_re.astype(jnp.float32), b_im.astype(jnp.float32))
    b_bar = ((lam_bar - 1.0) / lam)[..., None] * bmat
    bu = lax.complex(jnp.einsum('blgh,gph->lbgp', uf, jnp.real(b_bar)),
                     jnp.einsum('blgh,gph->lbgp', uf, jnp.imag(b_bar)))
    a_elems = jnp.broadcast_to(lam_bar, (seq, 1, S5_GROUPS, S5_STATE))
    _, xs = lax.associative_scan(_s5_combine, (a_elems, bu), axis=0)
    y = (jnp.einsum('lbgp,ghp->blgh', jnp.real(xs), c_re.astype(jnp.float32))
         - jnp.einsum('lbgp,ghp->blgh', jnp.imag(xs), c_im.astype(jnp.float32)))
    y = y + d_skip.astype(jnp.float32).reshape(S5_GROUPS, S5_GROUP) * uf
    y = jax.nn.gelu(y.reshape(bsz, seq, D_INNER)).astype(h.dtype)
    y = y * jax.nn.sigmoid(y @ w_glu + b_glu)
    return (y * jax.nn.silu(z)) @ w_out


def _mla_mixer(h, positions, w_in, q_norm_g, w_uq, kv_norm_g, w_ukv, w_out):
    bsz, seq, _ = h.shape
    c_q, c_kv, k_r, z = jnp.split(
        h @ w_in, [MLA_Q_RANK, MLA_Q_RANK + MLA_KV_RANK, MLA_Q_RANK + MLA_KV_RANK + MLA_ROPE], axis=-1)
    q = (_rmsnorm(c_q, q_norm_g) @ w_uq).reshape(bsz, seq, MLA_HEADS, MLA_QK_DIM)
    q_nope, q_rope = q[..., :MLA_NOPE], q[..., MLA_NOPE:]
    kv = (_rmsnorm(c_kv, kv_norm_g) @ w_ukv).reshape(bsz, seq, MLA_HEADS, MLA_NOPE + MLA_V)
    k_nope, v = kv[..., :MLA_NOPE], kv[..., MLA_NOPE:]
    inv_freq = ROPE_THETA ** (-jnp.arange(0, MLA_ROPE, 2, dtype=jnp.float32) / MLA_ROPE)
    ang = positions.astype(jnp.float32)[..., None] * inv_freq
    cos, sin = jnp.cos(ang), jnp.sin(ang)
    q_rope = _rope(q_rope, cos[:, :, None], sin[:, :, None])
    k_r = _rope(k_r, cos, sin)
    n_blk = seq // ATTN_QBLOCK

    def to_blocks(t):
        return t.reshape(bsz, n_blk, ATTN_QBLOCK, *t.shape[2:]).swapaxes(0, 1)

    kpos = jnp.arange(seq)

    def attend(args):
        qn_b, qr_b, blk = args
        s = (jnp.einsum('bqhd,bkhd->bhqk', qn_b, k_nope)
             + jnp.einsum('bqhd,bkd->bhqk', qr_b, k_r)).astype(jnp.float32) * MLA_SCALE
        qpos = blk * ATTN_QBLOCK + jnp.arange(ATTN_QBLOCK)
        s = jnp.where(kpos[None, :] <= qpos[:, None], s, NEG_INF)
        p = jax.nn.softmax(s, axis=-1).astype(v.dtype)
        return jnp.einsum('bhqk,bkhd->bqhd', p, v)

    o = lax.map(attend, (to_blocks(q_nope), to_blocks(q_rope), jnp.arange(n_blk)))
    o = o.swapaxes(0, 1).reshape(bsz, seq, MLA_HEADS * MLA_V)
    return (o * jax.nn.silu(z)) @ w_out


def _gain(key, n):
    return 1.0 + 0.02 * jax.random.normal(key, (n,), jnp.float32)


def _normal(key, shape, scale):
    return jax.random.normal(key, shape, jnp.float32) * scale


def _gmlp_params(key, p):
    k = jax.random.split(key, 7)
    return {
        p + 'norm_g': _gain(k[0], D_MODEL),
        p + 'w_in': _normal(k[1], (D_MODEL, 3 * D_INNER), D_MODEL ** -0.5),
        p + 'ln_g': _gain(k[2], D_INNER),
        p + 'ln_b': _normal(k[3], (D_INNER,), 0.02),
        p + 'w_s': _normal(k[4], (GMLP_GROUPS, GMLP_CHUNK, GMLP_CHUNK), GMLP_CHUNK ** -0.5),
        p + 'b_s': 1.0 + _normal(k[5], (GMLP_GROUPS, GMLP_CHUNK), 0.02),
        p + 'w_out': _normal(k[6], (D_INNER, D_MODEL), D_INNER ** -0.5),
    }


def _s5_params(key, p):
    k = jax.random.split(key, 14)
    n = jnp.arange(S5_STATE, dtype=jnp.float32)
    return {
        p + 'norm_g': _gain(k[0], D_MODEL),
        p + 'w_in': _normal(k[1], (D_MODEL, 2 * D_INNER), D_MODEL ** -0.5),
        p + 'a_re': -0.5 + _normal(k[2], (S5_GROUPS, S5_STATE), 0.01),
        p + 'a_im': math.pi * n[None, :] + _normal(k[3], (S5_GROUPS, S5_STATE), 0.01),
        p + 'log_step': jax.random.uniform(k[4], (S5_GROUPS,), jnp.float32,
                                           math.log(S5_DT_MIN), math.log(S5_DT_MAX)),
        p + 'b_re': _normal(k[5], (S5_GROUPS, S5_STATE, S5_GROUP), (2 * S5_GROUP) ** -0.5),
        p + 'b_im': _normal(k[6], (S5_GROUPS, S5_STATE, S5_GROUP), (2 * S5_GROUP) ** -0.5),
        p + 'c_re': _normal(k[7], (S5_GROUPS, S5_GROUP, S5_STATE), (2 * S5_STATE) ** -0.5),
        p + 'c_im': _normal(k[8], (S5_GROUPS, S5_GROUP, S5_STATE), (2 * S5_STATE) ** -0.5),
        p + 'd_skip': _normal(k[9], (D_INNER,), 1.0),
        p + 'w_glu': _normal(k[10], (D_INNER, D_INNER), D_INNER ** -0.5),
        p + 'b_glu': _normal(k[11], (D_INNER,), 0.02),
        p + 'w_out': _normal(k[12], (D_INNER, D_MODEL), D_INNER ** -0.5),
    }


def _mla_params(key, p):
    k = jax.random.split(key, 7)
    return {
        p + 'norm_g': _gain(k[0], D_MODEL),
        p + 'w_in': _normal(k[1], (D_MODEL, MLA_Q_RANK + MLA_KV_RANK + MLA_ROPE + D_INNER), D_MODEL ** -0.5),
        p + 'q_norm_g': _gain(k[2], MLA_Q_RANK),
        p + 'w_uq': _normal(k[3], (MLA_Q_RANK, MLA_HEADS * MLA_QK_DIM), MLA_Q_RANK ** -0.5),
        p + 'kv_norm_g': _gain(k[4], MLA_KV_RANK),
        p + 'w_ukv': _normal(k[5], (MLA_KV_RANK, MLA_HEADS * (MLA_NOPE + MLA_V)), MLA_KV_RANK ** -0.5),
        p + 'w_out': _normal(k[6], (MLA_HEADS * MLA_V, D_MODEL), D_INNER ** -0.5),
    }


def _fwd_setup_inputs(seed: int = 0) -> dict:
    key = jax.random.key(seed)
    keys = jax.random.split(key, DEPTH + 4)
    x = jax.random.normal(keys[0], (BATCH, SEQ, D_MODEL), jnp.float32)
    offset = jax.random.randint(keys[1], (BATCH, 1), 0, 1024, dtype=jnp.int32)
    positions = offset + jnp.arange(SEQ, dtype=jnp.int32)[None, :]
    inputs = {'x': x, 'positions': positions}
    makers = (_gmlp_params, _s5_params, _mla_params)
    for i in range(DEPTH):
        inputs.update(makers[i % N_MIXERS](keys[2 + i], 'l%d_' % i))
    inputs['final_norm_g'] = _gain(keys[2 + DEPTH], D_MODEL)
    return inputs


def _fwd_reference(x, positions,
              l0_norm_g, l0_w_in, l0_ln_g, l0_ln_b, l0_w_s, l0_b_s, l0_w_out,
              l1_norm_g, l1_w_in, l1_a_re, l1_a_im, l1_log_step, l1_b_re, l1_b_im, l1_c_re, l1_c_im,
              l1_d_skip, l1_w_glu, l1_b_glu, l1_w_out,
              l2_norm_g, l2_w_in, l2_q_norm_g, l2_w_uq, l2_kv_norm_g, l2_w_ukv, l2_w_out,
              l3_norm_g, l3_w_in, l3_ln_g, l3_ln_b, l3_w_s, l3_b_s, l3_w_out,
              final_norm_g):
    layer_params = (
        (l0_norm_g, (l0_w_in, l0_ln_g, l0_ln_b, l0_w_s, l0_b_s, l0_w_out)),
        (l1_norm_g, (l1_w_in, l1_a_re, l1_a_im, l1_log_step, l1_b_re, l1_b_im, l1_c_re, l1_c_im,
                     l1_d_skip, l1_w_glu, l1_b_glu, l1_w_out)),
        (l2_norm_g, (l2_w_in, l2_q_norm_g, l2_w_uq, l2_kv_norm_g, l2_w_ukv, l2_w_out)),
        (l3_norm_g, (l3_w_in, l3_ln_g, l3_ln_b, l3_w_s, l3_b_s, l3_w_out)),
    )
    h = x
    for i in range(DEPTH):
        norm_g, p = layer_params[i]
        hn = _rmsnorm(h, norm_g)
        kind = i % N_MIXERS
        if kind == 0:
            y = _gmlp_mixer(hn, *p)
        elif kind == 1:
            y = _s5_mixer(hn, *p)
        else:
            y = _mla_mixer(hn, positions, *p)
        h = h + y
    return _rmsnorm(h, final_norm_g)


import jax as _jax
import jax.numpy as _jnp

TWIN_FORMAT = 'train_step'
FWD_PARAMS = ['x', 'positions', 'l0_norm_g', 'l0_w_in', 'l0_ln_g', 'l0_ln_b', 'l0_w_s', 'l0_b_s', 'l0_w_out', 'l1_norm_g', 'l1_w_in', 'l1_a_re', 'l1_a_im', 'l1_log_step', 'l1_b_re', 'l1_b_im', 'l1_c_re', 'l1_c_im', 'l1_d_skip', 'l1_w_glu', 'l1_b_glu', 'l1_w_out', 'l2_norm_g', 'l2_w_in', 'l2_q_norm_g', 'l2_w_uq', 'l2_kv_norm_g', 'l2_w_ukv', 'l2_w_out', 'l3_norm_g', 'l3_w_in', 'l3_ln_g', 'l3_ln_b', 'l3_w_s', 'l3_b_s', 'l3_w_out', 'final_norm_g']
TWIN_WEIGHTS = ['l0_norm_g', 'l0_w_in', 'l0_ln_g', 'l0_ln_b', 'l0_w_s', 'l0_b_s', 'l0_w_out', 'l1_norm_g', 'l1_w_in', 'l1_a_re', 'l1_a_im', 'l1_log_step', 'l1_b_re', 'l1_b_im', 'l1_c_re', 'l1_c_im', 'l1_d_skip', 'l1_w_glu', 'l1_b_glu', 'l1_w_out', 'l2_norm_g', 'l2_w_in', 'l2_q_norm_g', 'l2_w_uq', 'l2_kv_norm_g', 'l2_w_ukv', 'l2_w_out', 'l3_norm_g', 'l3_w_in', 'l3_ln_g', 'l3_ln_b', 'l3_w_s', 'l3_b_s', 'l3_w_out', 'final_norm_g']
TWIN_DIFF_INPUT = 'x'
TWIN_INPUTS = ['x', 'positions', 'l0_norm_g', 'l0_w_in', 'l0_ln_g', 'l0_ln_b', 'l0_w_s', 'l0_b_s', 'l0_w_out', 'l1_norm_g', 'l1_w_in', 'l1_a_re', 'l1_a_im', 'l1_log_step', 'l1_b_re', 'l1_b_im', 'l1_c_re', 'l1_c_im', 'l1_d_skip', 'l1_w_glu', 'l1_b_glu', 'l1_w_out', 'l2_norm_g', 'l2_w_in', 'l2_q_norm_g', 'l2_w_uq', 'l2_kv_norm_g', 'l2_w_ukv', 'l2_w_out', 'l3_norm_g', 'l3_w_in', 'l3_ln_g', 'l3_ln_b', 'l3_w_s', 'l3_b_s', 'l3_w_out', 'final_norm_g', 'loss_target', 'm_l0_norm_g', 'm_l0_w_in', 'm_l0_ln_g', 'm_l0_ln_b', 'm_l0_w_s', 'm_l0_b_s', 'm_l0_w_out', 'm_l1_norm_g', 'm_l1_w_in', 'm_l1_a_re', 'm_l1_a_im', 'm_l1_log_step', 'm_l1_b_re', 'm_l1_b_im', 'm_l1_c_re', 'm_l1_c_im', 'm_l1_d_skip', 'm_l1_w_glu', 'm_l1_b_glu', 'm_l1_w_out', 'm_l2_norm_g', 'm_l2_w_in', 'm_l2_q_norm_g', 'm_l2_w_uq', 'm_l2_kv_norm_g', 'm_l2_w_ukv', 'm_l2_w_out', 'm_l3_norm_g', 'm_l3_w_in', 'm_l3_ln_g', 'm_l3_ln_b', 'm_l3_w_s', 'm_l3_b_s', 'm_l3_w_out', 'm_final_norm_g', 'v_l0_norm_g', 'v_l0_w_in', 'v_l0_ln_g', 'v_l0_ln_b', 'v_l0_w_s', 'v_l0_b_s', 'v_l0_w_out', 'v_l1_norm_g', 'v_l1_w_in', 'v_l1_a_re', 'v_l1_a_im', 'v_l1_log_step', 'v_l1_b_re', 'v_l1_b_im', 'v_l1_c_re', 'v_l1_c_im', 'v_l1_d_skip', 'v_l1_w_glu', 'v_l1_b_glu', 'v_l1_w_out', 'v_l2_norm_g', 'v_l2_w_in', 'v_l2_q_norm_g', 'v_l2_w_uq', 'v_l2_kv_norm_g', 'v_l2_w_ukv', 'v_l2_w_out', 'v_l3_norm_g', 'v_l3_w_in', 'v_l3_ln_g', 'v_l3_ln_b', 'v_l3_w_s', 'v_l3_b_s', 'v_l3_w_out', 'v_final_norm_g']
TWIN_OUTPUTS = ['loss', 'grad_x', 'grad_l0_norm_g', 'grad_l0_w_in', 'grad_l0_ln_g', 'grad_l0_ln_b', 'grad_l0_w_s', 'grad_l0_b_s', 'grad_l0_w_out', 'grad_l1_norm_g', 'grad_l1_w_in', 'grad_l1_a_re', 'grad_l1_a_im', 'grad_l1_log_step', 'grad_l1_b_re', 'grad_l1_b_im', 'grad_l1_c_re', 'grad_l1_c_im', 'grad_l1_d_skip', 'grad_l1_w_glu', 'grad_l1_b_glu', 'grad_l1_w_out', 'grad_l2_norm_g', 'grad_l2_w_in', 'grad_l2_q_norm_g', 'grad_l2_w_uq', 'grad_l2_kv_norm_g', 'grad_l2_w_ukv', 'grad_l2_w_out', 'grad_l3_norm_g', 'grad_l3_w_in', 'grad_l3_ln_g', 'grad_l3_ln_b', 'grad_l3_w_s', 'grad_l3_b_s', 'grad_l3_w_out', 'grad_final_norm_g', 'delta_l0_norm_g', 'delta_l0_w_in', 'delta_l0_ln_g', 'delta_l0_ln_b', 'delta_l0_w_s', 'delta_l0_b_s', 'delta_l0_w_out', 'delta_l1_norm_g', 'delta_l1_w_in', 'delta_l1_a_re', 'delta_l1_a_im', 'delta_l1_log_step', 'delta_l1_b_re', 'delta_l1_b_im', 'delta_l1_c_re', 'delta_l1_c_im', 'delta_l1_d_skip', 'delta_l1_w_glu', 'delta_l1_b_glu', 'delta_l1_w_out', 'delta_l2_norm_g', 'delta_l2_w_in', 'delta_l2_q_norm_g', 'delta_l2_w_uq', 'delta_l2_kv_norm_g', 'delta_l2_w_ukv', 'delta_l2_w_out', 'delta_l3_norm_g', 'delta_l3_w_in', 'delta_l3_ln_g', 'delta_l3_ln_b', 'delta_l3_w_s', 'delta_l3_b_s', 'delta_l3_w_out', 'delta_final_norm_g', 'new_m_l0_norm_g', 'new_m_l0_w_in', 'new_m_l0_ln_g', 'new_m_l0_ln_b', 'new_m_l0_w_s', 'new_m_l0_b_s', 'new_m_l0_w_out', 'new_m_l1_norm_g', 'new_m_l1_w_in', 'new_m_l1_a_re', 'new_m_l1_a_im', 'new_m_l1_log_step', 'new_m_l1_b_re', 'new_m_l1_b_im', 'new_m_l1_c_re', 'new_m_l1_c_im', 'new_m_l1_d_skip', 'new_m_l1_w_glu', 'new_m_l1_b_glu', 'new_m_l1_w_out', 'new_m_l2_norm_g', 'new_m_l2_w_in', 'new_m_l2_q_norm_g', 'new_m_l2_w_uq', 'new_m_l2_kv_norm_g', 'new_m_l2_w_ukv', 'new_m_l2_w_out', 'new_m_l3_norm_g', 'new_m_l3_w_in', 'new_m_l3_ln_g', 'new_m_l3_ln_b', 'new_m_l3_w_s', 'new_m_l3_b_s', 'new_m_l3_w_out', 'new_m_final_norm_g', 'new_v_l0_norm_g', 'new_v_l0_w_in', 'new_v_l0_ln_g', 'new_v_l0_ln_b', 'new_v_l0_w_s', 'new_v_l0_b_s', 'new_v_l0_w_out', 'new_v_l1_norm_g', 'new_v_l1_w_in', 'new_v_l1_a_re', 'new_v_l1_a_im', 'new_v_l1_log_step', 'new_v_l1_b_re', 'new_v_l1_b_im', 'new_v_l1_c_re', 'new_v_l1_c_im', 'new_v_l1_d_skip', 'new_v_l1_w_glu', 'new_v_l1_b_glu', 'new_v_l1_w_out', 'new_v_l2_norm_g', 'new_v_l2_w_in', 'new_v_l2_q_norm_g', 'new_v_l2_w_uq', 'new_v_l2_kv_norm_g', 'new_v_l2_w_ukv', 'new_v_l2_w_out', 'new_v_l3_norm_g', 'new_v_l3_w_in', 'new_v_l3_ln_g', 'new_v_l3_ln_b', 'new_v_l3_w_s', 'new_v_l3_b_s', 'new_v_l3_w_out', 'new_v_final_norm_g']
TWIN_LEAF_KINDS = {'loss': 'loss', 'grad_x': 'grad_x', 'grad_l0_norm_g': 'grad_w', 'grad_l0_w_in': 'grad_w', 'grad_l0_ln_g': 'grad_w', 'grad_l0_ln_b': 'grad_w', 'grad_l0_w_s': 'grad_w', 'grad_l0_b_s': 'grad_w', 'grad_l0_w_out': 'grad_w', 'grad_l1_norm_g': 'grad_w', 'grad_l1_w_in': 'grad_w', 'grad_l1_a_re': 'grad_w', 'grad_l1_a_im': 'grad_w', 'grad_l1_log_step': 'grad_w', 'grad_l1_b_re': 'grad_w', 'grad_l1_b_im': 'grad_w', 'grad_l1_c_re': 'grad_w', 'grad_l1_c_im': 'grad_w', 'grad_l1_d_skip': 'grad_w', 'grad_l1_w_glu': 'grad_w', 'grad_l1_b_glu': 'grad_w', 'grad_l1_w_out': 'grad_w', 'grad_l2_norm_g': 'grad_w', 'grad_l2_w_in': 'grad_w', 'grad_l2_q_norm_g': 'grad_w', 'grad_l2_w_uq': 'grad_w', 'grad_l2_kv_norm_g': 'grad_w', 'grad_l2_w_ukv': 'grad_w', 'grad_l2_w_out': 'grad_w', 'grad_l3_norm_g': 'grad_w', 'grad_l3_w_in': 'grad_w', 'grad_l3_ln_g': 'grad_w', 'grad_l3_ln_b': 'grad_w', 'grad_l3_w_s': 'grad_w', 'grad_l3_b_s': 'grad_w', 'grad_l3_w_out': 'grad_w', 'grad_final_norm_g': 'grad_w', 'delta_l0_norm_g': 'delta_w', 'delta_l0_w_in': 'delta_w', 'delta_l0_ln_g': 'delta_w', 'delta_l0_ln_b': 'delta_w', 'delta_l0_w_s': 'delta_w', 'delta_l0_b_s': 'delta_w', 'delta_l0_w_out': 'delta_w', 'delta_l1_norm_g': 'delta_w', 'delta_l1_w_in': 'delta_w', 'delta_l1_a_re': 'delta_w', 'delta_l1_a_im': 'delta_w', 'delta_l1_log_step': 'delta_w', 'delta_l1_b_re': 'delta_w', 'delta_l1_b_im': 'delta_w', 'delta_l1_c_re': 'delta_w', 'delta_l1_c_im': 'delta_w', 'delta_l1_d_skip': 'delta_w', 'delta_l1_w_glu': 'delta_w', 'delta_l1_b_glu': 'delta_w', 'delta_l1_w_out': 'delta_w', 'delta_l2_norm_g': 'delta_w', 'delta_l2_w_in': 'delta_w', 'delta_l2_q_norm_g': 'delta_w', 'delta_l2_w_uq': 'delta_w', 'delta_l2_kv_norm_g': 'delta_w', 'delta_l2_w_ukv': 'delta_w', 'delta_l2_w_out': 'delta_w', 'delta_l3_norm_g': 'delta_w', 'delta_l3_w_in': 'delta_w', 'delta_l3_ln_g': 'delta_w', 'delta_l3_ln_b': 'delta_w', 'delta_l3_w_s': 'delta_w', 'delta_l3_b_s': 'delta_w', 'delta_l3_w_out': 'delta_w', 'delta_final_norm_g': 'delta_w', 'new_m_l0_norm_g': 'new_m', 'new_m_l0_w_in': 'new_m', 'new_m_l0_ln_g': 'new_m', 'new_m_l0_ln_b': 'new_m', 'new_m_l0_w_s': 'new_m', 'new_m_l0_b_s': 'new_m', 'new_m_l0_w_out': 'new_m', 'new_m_l1_norm_g': 'new_m', 'new_m_l1_w_in': 'new_m', 'new_m_l1_a_re': 'new_m', 'new_m_l1_a_im': 'new_m', 'new_m_l1_log_step': 'new_m', 'new_m_l1_b_re': 'new_m', 'new_m_l1_b_im': 'new_m', 'new_m_l1_c_re': 'new_m', 'new_m_l1_c_im': 'new_m', 'new_m_l1_d_skip': 'new_m', 'new_m_l1_w_glu': 'new_m', 'new_m_l1_b_glu': 'new_m', 'new_m_l1_w_out': 'new_m', 'new_m_l2_norm_g': 'new_m', 'new_m_l2_w_in': 'new_m', 'new_m_l2_q_norm_g': 'new_m', 'new_m_l2_w_uq': 'new_m', 'new_m_l2_kv_norm_g': 'new_m', 'new_m_l2_w_ukv': 'new_m', 'new_m_l2_w_out': 'new_m', 'new_m_l3_norm_g': 'new_m', 'new_m_l3_w_in': 'new_m', 'new_m_l3_ln_g': 'new_m', 'new_m_l3_ln_b': 'new_m', 'new_m_l3_w_s': 'new_m', 'new_m_l3_b_s': 'new_m', 'new_m_l3_w_out': 'new_m', 'new_m_final_norm_g': 'new_m', 'new_v_l0_norm_g': 'new_v', 'new_v_l0_w_in': 'new_v', 'new_v_l0_ln_g': 'new_v', 'new_v_l0_ln_b': 'new_v', 'new_v_l0_w_s': 'new_v', 'new_v_l0_b_s': 'new_v', 'new_v_l0_w_out': 'new_v', 'new_v_l1_norm_g': 'new_v', 'new_v_l1_w_in': 'new_v', 'new_v_l1_a_re': 'new_v', 'new_v_l1_a_im': 'new_v', 'new_v_l1_log_step': 'new_v', 'new_v_l1_b_re': 'new_v', 'new_v_l1_b_im': 'new_v', 'new_v_l1_c_re': 'new_v', 'new_v_l1_c_im': 'new_v', 'new_v_l1_d_skip': 'new_v', 'new_v_l1_w_glu': 'new_v', 'new_v_l1_b_glu': 'new_v', 'new_v_l1_w_out': 'new_v', 'new_v_l2_norm_g': 'new_v', 'new_v_l2_w_in': 'new_v', 'new_v_l2_q_norm_g': 'new_v', 'new_v_l2_w_uq': 'new_v', 'new_v_l2_kv_norm_g': 'new_v', 'new_v_l2_w_ukv': 'new_v', 'new_v_l2_w_out': 'new_v', 'new_v_l3_norm_g': 'new_v', 'new_v_l3_w_in': 'new_v', 'new_v_l3_ln_g': 'new_v', 'new_v_l3_ln_b': 'new_v', 'new_v_l3_w_s': 'new_v', 'new_v_l3_b_s': 'new_v', 'new_v_l3_w_out': 'new_v', 'new_v_final_norm_g': 'new_v'}


def _forward(args):
    return _fwd_reference(*[args[k] for k in FWD_PARAMS])


def _output_shape():
    def fwd():
        inp = _fwd_setup_inputs(0)
        return _fwd_reference(*[inp[k] for k in FWD_PARAMS])
    out = _jax.eval_shape(fwd)
    return out.shape, out.dtype

N_MICROBATCH = 1
ADAM_LR = 0.001
ADAM_B1 = 0.9
ADAM_B2 = 0.999
ADAM_EPS = 1e-08
ADAM_WD = 0.01
ADAM_STEP = 10
PER_EXAMPLE_BATCH_AXIS = {'x': 0, 'positions': 0, 'loss_target': 0}
SHARED_INPUTS = []
_WEIGHT_DTYPES = {'l0_norm_g': _jnp.float32, 'l0_w_in': _jnp.float32, 'l0_ln_g': _jnp.float32, 'l0_ln_b': _jnp.float32, 'l0_w_s': _jnp.float32, 'l0_b_s': _jnp.float32, 'l0_w_out': _jnp.float32, 'l1_norm_g': _jnp.float32, 'l1_w_in': _jnp.float32, 'l1_a_re': _jnp.float32, 'l1_a_im': _jnp.float32, 'l1_log_step': _jnp.float32, 'l1_b_re': _jnp.float32, 'l1_b_im': _jnp.float32, 'l1_c_re': _jnp.float32, 'l1_c_im': _jnp.float32, 'l1_d_skip': _jnp.float32, 'l1_w_glu': _jnp.float32, 'l1_b_glu': _jnp.float32, 'l1_w_out': _jnp.float32, 'l2_norm_g': _jnp.float32, 'l2_w_in': _jnp.float32, 'l2_q_norm_g': _jnp.float32, 'l2_w_uq': _jnp.float32, 'l2_kv_norm_g': _jnp.float32, 'l2_w_ukv': _jnp.float32, 'l2_w_out': _jnp.float32, 'l3_norm_g': _jnp.float32, 'l3_w_in': _jnp.float32, 'l3_ln_g': _jnp.float32, 'l3_ln_b': _jnp.float32, 'l3_w_s': _jnp.float32, 'l3_b_s': _jnp.float32, 'l3_w_out': _jnp.float32, 'final_norm_g': _jnp.float32}
MOMENT_SCALE = {'l0_norm_g': 1.406132e-01, 'l0_w_in': 5.688298e-02, 'l0_ln_g': 3.473679e-02, 'l0_ln_b': 3.623426e-02, 'l0_w_s': 4.936780e-02, 'l0_b_s': 7.365093e-02, 'l0_w_out': 8.721552e-02, 'l1_norm_g': 5.340170e-02, 'l1_w_in': 2.651513e-02, 'l1_a_re': 1.339621e-03, 'l1_a_im': 1.344415e-03, 'l1_log_step': 9.341393e-01, 'l1_b_re': 8.620473e-04, 'l1_b_im': 8.756557e-04, 'l1_c_re': 1.746059e-03, 'l1_c_im': 1.732722e-03, 'l1_d_skip': 2.846632e-02, 'l1_w_glu': 7.314407e-03, 'l1_b_glu': 1.157139e-02, 'l1_w_out': 3.588596e-02, 'l2_norm_g': 4.364923e-02, 'l2_w_in': 2.680846e-02, 'l2_q_norm_g': 3.063353e-02, 'l2_w_uq': 1.105135e-02, 'l2_kv_norm_g': 8.783707e-02, 'l2_w_ukv': 1.445393e-02, 'l2_w_out': 2.403640e-02, 'l3_norm_g': 1.144582e-01, 'l3_w_in': 4.496384e-02, 'l3_ln_g': 2.805758e-02, 'l3_ln_b': 2.796336e-02, 'l3_w_s': 3.933874e-02, 'l3_b_s': 5.721313e-02, 'l3_w_out': 6.900865e-02, 'final_norm_g': 3.200925e+01}


def _to_microbatches(a, axis):
    t = _jnp.moveaxis(a, axis, 0)
    t = t.reshape((N_MICROBATCH, t.shape[0] // N_MICROBATCH) + t.shape[1:])
    return _jnp.moveaxis(t, 1, axis + 1)


def setup_inputs(seed: int = 0) -> dict:
    inp = _fwd_setup_inputs(seed)
    key = _jax.random.fold_in(_jax.random.key(seed), 7919)
    shape, _ = _output_shape()
    out = dict(inp)
    out["loss_target"] = _jax.random.normal(_jax.random.fold_in(key, 0), shape, _jnp.float32)
    for i, name in enumerate(TWIN_WEIGHTS):
        w = inp[name].astype(_jnp.float32)
        if MOMENT_SCALE is None:
            s = _jnp.sqrt(_jnp.mean(_jnp.square(w)) + 1e-30)
        else:
            s = MOMENT_SCALE[name]
        km, kv = _jax.random.split(_jax.random.fold_in(key, i + 1))
        out[name] = w
        out["m_" + name] = s * _jax.random.normal(km, w.shape, _jnp.float32)
        out["v_" + name] = (s * s) * _jax.random.uniform(kv, w.shape, _jnp.float32, 0.5, 1.5)
    if N_MICROBATCH > 1:
        for name, axis in PER_EXAMPLE_BATCH_AXIS.items():
            out[name] = _to_microbatches(out[name], axis)
    return {'x': out['x'], 'positions': out['positions'], 'l0_norm_g': out['l0_norm_g'], 'l0_w_in': out['l0_w_in'], 'l0_ln_g': out['l0_ln_g'], 'l0_ln_b': out['l0_ln_b'], 'l0_w_s': out['l0_w_s'], 'l0_b_s': out['l0_b_s'], 'l0_w_out': out['l0_w_out'], 'l1_norm_g': out['l1_norm_g'], 'l1_w_in': out['l1_w_in'], 'l1_a_re': out['l1_a_re'], 'l1_a_im': out['l1_a_im'], 'l1_log_step': out['l1_log_step'], 'l1_b_re': out['l1_b_re'], 'l1_b_im': out['l1_b_im'], 'l1_c_re': out['l1_c_re'], 'l1_c_im': out['l1_c_im'], 'l1_d_skip': out['l1_d_skip'], 'l1_w_glu': out['l1_w_glu'], 'l1_b_glu': out['l1_b_glu'], 'l1_w_out': out['l1_w_out'], 'l2_norm_g': out['l2_norm_g'], 'l2_w_in': out['l2_w_in'], 'l2_q_norm_g': out['l2_q_norm_g'], 'l2_w_uq': out['l2_w_uq'], 'l2_kv_norm_g': out['l2_kv_norm_g'], 'l2_w_ukv': out['l2_w_ukv'], 'l2_w_out': out['l2_w_out'], 'l3_norm_g': out['l3_norm_g'], 'l3_w_in': out['l3_w_in'], 'l3_ln_g': out['l3_ln_g'], 'l3_ln_b': out['l3_ln_b'], 'l3_w_s': out['l3_w_s'], 'l3_b_s': out['l3_b_s'], 'l3_w_out': out['l3_w_out'], 'final_norm_g': out['final_norm_g'], 'loss_target': out['loss_target'], 'm_l0_norm_g': out['m_l0_norm_g'], 'm_l0_w_in': out['m_l0_w_in'], 'm_l0_ln_g': out['m_l0_ln_g'], 'm_l0_ln_b': out['m_l0_ln_b'], 'm_l0_w_s': out['m_l0_w_s'], 'm_l0_b_s': out['m_l0_b_s'], 'm_l0_w_out': out['m_l0_w_out'], 'm_l1_norm_g': out['m_l1_norm_g'], 'm_l1_w_in': out['m_l1_w_in'], 'm_l1_a_re': out['m_l1_a_re'], 'm_l1_a_im': out['m_l1_a_im'], 'm_l1_log_step': out['m_l1_log_step'], 'm_l1_b_re': out['m_l1_b_re'], 'm_l1_b_im': out['m_l1_b_im'], 'm_l1_c_re': out['m_l1_c_re'], 'm_l1_c_im': out['m_l1_c_im'], 'm_l1_d_skip': out['m_l1_d_skip'], 'm_l1_w_glu': out['m_l1_w_glu'], 'm_l1_b_glu': out['m_l1_b_glu'], 'm_l1_w_out': out['m_l1_w_out'], 'm_l2_norm_g': out['m_l2_norm_g'], 'm_l2_w_in': out['m_l2_w_in'], 'm_l2_q_norm_g': out['m_l2_q_norm_g'], 'm_l2_w_uq': out['m_l2_w_uq'], 'm_l2_kv_norm_g': out['m_l2_kv_norm_g'], 'm_l2_w_ukv': out['m_l2_w_ukv'], 'm_l2_w_out': out['m_l2_w_out'], 'm_l3_norm_g': out['m_l3_norm_g'], 'm_l3_w_in': out['m_l3_w_in'], 'm_l3_ln_g': out['m_l3_ln_g'], 'm_l3_ln_b': out['m_l3_ln_b'], 'm_l3_w_s': out['m_l3_w_s'], 'm_l3_b_s': out['m_l3_b_s'], 'm_l3_w_out': out['m_l3_w_out'], 'm_final_norm_g': out['m_final_norm_g'], 'v_l0_norm_g': out['v_l0_norm_g'], 'v_l0_w_in': out['v_l0_w_in'], 'v_l0_ln_g': out['v_l0_ln_g'], 'v_l0_ln_b': out['v_l0_ln_b'], 'v_l0_w_s': out['v_l0_w_s'], 'v_l0_b_s': out['v_l0_b_s'], 'v_l0_w_out': out['v_l0_w_out'], 'v_l1_norm_g': out['v_l1_norm_g'], 'v_l1_w_in': out['v_l1_w_in'], 'v_l1_a_re': out['v_l1_a_re'], 'v_l1_a_im': out['v_l1_a_im'], 'v_l1_log_step': out['v_l1_log_step'], 'v_l1_b_re': out['v_l1_b_re'], 'v_l1_b_im': out['v_l1_b_im'], 'v_l1_c_re': out['v_l1_c_re'], 'v_l1_c_im': out['v_l1_c_im'], 'v_l1_d_skip': out['v_l1_d_skip'], 'v_l1_w_glu': out['v_l1_w_glu'], 'v_l1_b_glu': out['v_l1_b_glu'], 'v_l1_w_out': out['v_l1_w_out'], 'v_l2_norm_g': out['v_l2_norm_g'], 'v_l2_w_in': out['v_l2_w_in'], 'v_l2_q_norm_g': out['v_l2_q_norm_g'], 'v_l2_w_uq': out['v_l2_w_uq'], 'v_l2_kv_norm_g': out['v_l2_kv_norm_g'], 'v_l2_w_ukv': out['v_l2_w_ukv'], 'v_l2_w_out': out['v_l2_w_out'], 'v_l3_norm_g': out['v_l3_norm_g'], 'v_l3_w_in': out['v_l3_w_in'], 'v_l3_ln_g': out['v_l3_ln_g'], 'v_l3_ln_b': out['v_l3_ln_b'], 'v_l3_w_s': out['v_l3_w_s'], 'v_l3_b_s': out['v_l3_b_s'], 'v_l3_w_out': out['v_l3_w_out'], 'v_final_norm_g': out['v_final_norm_g']}


def _loss(weights, diff, rest, loss_target):
    with _jax.named_scope("forward"):
        args = {**rest, TWIN_DIFF_INPUT: diff, **{k: w.astype(_WEIGHT_DTYPES[k]) for k, w in weights.items()}}
        y = _forward(args)
    with _jax.named_scope("loss_head"):
        err = _jnp.square(y.astype(_jnp.float32) - loss_target)
        return 0.5 * _jnp.sum(_jnp.mean(err, axis=-1)) if err.ndim else 0.5 * err


def _adamw(w, g, m, v):
    m = ADAM_B1 * m + (1.0 - ADAM_B1) * g
    v = ADAM_B2 * v + (1.0 - ADAM_B2) * _jnp.square(g)
    m_hat = m / (1.0 - ADAM_B1 ** ADAM_STEP)
    v_hat = v / (1.0 - ADAM_B2 ** ADAM_STEP)
    delta = -ADAM_LR * (m_hat / (_jnp.sqrt(v_hat) + ADAM_EPS) + ADAM_WD * w)
    return delta, m, v


def reference(x, positions, l0_norm_g, l0_w_in, l0_ln_g, l0_ln_b, l0_w_s, l0_b_s, l0_w_out, l1_norm_g, l1_w_in, l1_a_re, l1_a_im, l1_log_step, l1_b_re, l1_b_im, l1_c_re, l1_c_im, l1_d_skip, l1_w_glu, l1_b_glu, l1_w_out, l2_norm_g, l2_w_in, l2_q_norm_g, l2_w_uq, l2_kv_norm_g, l2_w_ukv, l2_w_out, l3_norm_g, l3_w_in, l3_ln_g, l3_ln_b, l3_w_s, l3_b_s, l3_w_out, final_norm_g, loss_target, m_l0_norm_g, m_l0_w_in, m_l0_ln_g, m_l0_ln_b, m_l0_w_s, m_l0_b_s, m_l0_w_out, m_l1_norm_g, m_l1_w_in, m_l1_a_re, m_l1_a_im, m_l1_log_step, m_l1_b_re, m_l1_b_im, m_l1_c_re, m_l1_c_im, m_l1_d_skip, m_l1_w_glu, m_l1_b_glu, m_l1_w_out, m_l2_norm_g, m_l2_w_in, m_l2_q_norm_g, m_l2_w_uq, m_l2_kv_norm_g, m_l2_w_ukv, m_l2_w_out, m_l3_norm_g, m_l3_w_in, m_l3_ln_g, m_l3_ln_b, m_l3_w_s, m_l3_b_s, m_l3_w_out, m_final_norm_g, v_l0_norm_g, v_l0_w_in, v_l0_ln_g, v_l0_ln_b, v_l0_w_s, v_l0_b_s, v_l0_w_out, v_l1_norm_g, v_l1_w_in, v_l1_a_re, v_l1_a_im, v_l1_log_step, v_l1_b_re, v_l1_b_im, v_l1_c_re, v_l1_c_im, v_l1_d_skip, v_l1_w_glu, v_l1_b_glu, v_l1_w_out, v_l2_norm_g, v_l2_w_in, v_l2_q_norm_g, v_l2_w_uq, v_l2_kv_norm_g, v_l2_w_ukv, v_l2_w_out, v_l3_norm_g, v_l3_w_in, v_l3_ln_g, v_l3_ln_b, v_l3_w_s, v_l3_b_s, v_l3_w_out, v_final_norm_g):
    given = dict(x=x, positions=positions, l0_norm_g=l0_norm_g, l0_w_in=l0_w_in, l0_ln_g=l0_ln_g, l0_ln_b=l0_ln_b, l0_w_s=l0_w_s, l0_b_s=l0_b_s, l0_w_out=l0_w_out, l1_norm_g=l1_norm_g, l1_w_in=l1_w_in, l1_a_re=l1_a_re, l1_a_im=l1_a_im, l1_log_step=l1_log_step, l1_b_re=l1_b_re, l1_b_im=l1_b_im, l1_c_re=l1_c_re, l1_c_im=l1_c_im, l1_d_skip=l1_d_skip, l1_w_glu=l1_w_glu, l1_b_glu=l1_b_glu, l1_w_out=l1_w_out, l2_norm_g=l2_norm_g, l2_w_in=l2_w_in, l2_q_norm_g=l2_q_norm_g, l2_w_uq=l2_w_uq, l2_kv_norm_g=l2_kv_norm_g, l2_w_ukv=l2_w_ukv, l2_w_out=l2_w_out, l3_norm_g=l3_norm_g, l3_w_in=l3_w_in, l3_ln_g=l3_ln_g, l3_ln_b=l3_ln_b, l3_w_s=l3_w_s, l3_b_s=l3_b_s, l3_w_out=l3_w_out, final_norm_g=final_norm_g, loss_target=loss_target, m_l0_norm_g=m_l0_norm_g, m_l0_w_in=m_l0_w_in, m_l0_ln_g=m_l0_ln_g, m_l0_ln_b=m_l0_ln_b, m_l0_w_s=m_l0_w_s, m_l0_b_s=m_l0_b_s, m_l0_w_out=m_l0_w_out, m_l1_norm_g=m_l1_norm_g, m_l1_w_in=m_l1_w_in, m_l1_a_re=m_l1_a_re, m_l1_a_im=m_l1_a_im, m_l1_log_step=m_l1_log_step, m_l1_b_re=m_l1_b_re, m_l1_b_im=m_l1_b_im, m_l1_c_re=m_l1_c_re, m_l1_c_im=m_l1_c_im, m_l1_d_skip=m_l1_d_skip, m_l1_w_glu=m_l1_w_glu, m_l1_b_glu=m_l1_b_glu, m_l1_w_out=m_l1_w_out, m_l2_norm_g=m_l2_norm_g, m_l2_w_in=m_l2_w_in, m_l2_q_norm_g=m_l2_q_norm_g, m_l2_w_uq=m_l2_w_uq, m_l2_kv_norm_g=m_l2_kv_norm_g, m_l2_w_ukv=m_l2_w_ukv, m_l2_w_out=m_l2_w_out, m_l3_norm_g=m_l3_norm_g, m_l3_w_in=m_l3_w_in, m_l3_ln_g=m_l3_ln_g, m_l3_ln_b=m_l3_ln_b, m_l3_w_s=m_l3_w_s, m_l3_b_s=m_l3_b_s, m_l3_w_out=m_l3_w_out, m_final_norm_g=m_final_norm_g, v_l0_norm_g=v_l0_norm_g, v_l0_w_in=v_l0_w_in, v_l0_ln_g=v_l0_ln_g, v_l0_ln_b=v_l0_ln_b, v_l0_w_s=v_l0_w_s, v_l0_b_s=v_l0_b_s, v_l0_w_out=v_l0_w_out, v_l1_norm_g=v_l1_norm_g, v_l1_w_in=v_l1_w_in, v_l1_a_re=v_l1_a_re, v_l1_a_im=v_l1_a_im, v_l1_log_step=v_l1_log_step, v_l1_b_re=v_l1_b_re, v_l1_b_im=v_l1_b_im, v_l1_c_re=v_l1_c_re, v_l1_c_im=v_l1_c_im, v_l1_d_skip=v_l1_d_skip, v_l1_w_glu=v_l1_w_glu, v_l1_b_glu=v_l1_b_glu, v_l1_w_out=v_l1_w_out, v_l2_norm_g=v_l2_norm_g, v_l2_w_in=v_l2_w_in, v_l2_q_norm_g=v_l2_q_norm_g, v_l2_w_uq=v_l2_w_uq, v_l2_kv_norm_g=v_l2_kv_norm_g, v_l2_w_ukv=v_l2_w_ukv, v_l2_w_out=v_l2_w_out, v_l3_norm_g=v_l3_norm_g, v_l3_w_in=v_l3_w_in, v_l3_ln_g=v_l3_ln_g, v_l3_ln_b=v_l3_ln_b, v_l3_w_s=v_l3_w_s, v_l3_b_s=v_l3_b_s, v_l3_w_out=v_l3_w_out, v_final_norm_g=v_final_norm_g)
    weights = {n: given[n] for n in TWIN_WEIGHTS}
    shared = {n: given[n] for n in SHARED_INPUTS}
    per_example = {n: given[n] for n in ['x', 'positions']}
    grad_fn = _jax.value_and_grad(_loss, argnums=(0, 1))

    def one_microbatch(ex, loss_target):
        ex = dict(ex)
        diff = ex.pop(TWIN_DIFF_INPUT)
        return grad_fn(weights, diff, {**shared, **ex}, loss_target)

    if N_MICROBATCH == 1:
        loss, (grad_w, grad_x) = one_microbatch(per_example, given["loss_target"])
    else:
        def body(carry, xs):
            loss_sum, grad_sum = carry
            l_k, (gw_k, gx_k) = one_microbatch(xs[0], xs[1])
            with _jax.named_scope("update"):
                return (loss_sum + l_k, _jax.tree.map(_jnp.add, grad_sum, gw_k)), gx_k

        init = (_jnp.zeros((), _jnp.float32), _jax.tree.map(_jnp.zeros_like, weights))
        (loss, grad_w), grad_x = _jax.lax.scan(body, init, (per_example, given["loss_target"]))
    with _jax.named_scope("update"):
        delta_w, new_m, new_v = {}, {}, {}
        for n in TWIN_WEIGHTS:
            delta_w[n], new_m[n], new_v[n] = _adamw(weights[n], grad_w[n], given["m_" + n], given["v_" + n])
    return (loss, grad_x, *[grad_w[n] for n in TWIN_WEIGHTS], *[delta_w[n] for n in TWIN_WEIGHTS],
            *[new_m[n] for n in TWIN_WEIGHTS], *[new_v[n] for n in TWIN_WEIGHTS])
```

```python
import functools
import math

import numpy as np
import jax
import jax.numpy as jnp
from jax import lax
from jax.experimental import pallas as pl
from jax.experimental.pallas import tpu as pltpu

F32 = jnp.float32
BF16 = jnp.bfloat16

NORM_EPS = 1e-6
GMLP_CHUNK = 128
S5_GROUP = 16
S5_STATE = 64
S5_T = 16
MLA_NOPE = 128
MLA_ROPE = 64
MLA_V = 128
MLA_QK_DIM = MLA_NOPE + MLA_ROPE
MLA_Q_RANK = 384
MLA_KV_RANK = 128
MLA_HEAD_PAD = 256
MLA_SCALE = MLA_QK_DIM ** -0.5
ROPE_THETA = 10000.0
NEG_INF = -1e30
ADAM_LR = 0.001
ADAM_B1 = 0.9
ADAM_B2 = 0.999
ADAM_EPS = 1e-08
ADAM_WD = 0.01
ADAM_STEP = 10

N_DEV = 8
LANES = 128
FLAT_W = 1024
VMEM_LIMIT = 56 * 1024 * 1024
MESH = pl.DeviceIdType.MESH


def _pick(dim, pref, align=LANES):
    t = (min(pref, dim) // align) * align
    while t >= align:
        if dim % t == 0:
            return t
        t -= align
    return dim


def _params(sem=None):
    return pltpu.CompilerParams(dimension_semantics=sem, vmem_limit_bytes=VMEM_LIMIT)


_DIMS = {"nn": (((1,), (0,)), ((), ())), "nt": (((1,), (1,)), ((), ())), "tn": (((0,), (0,)), ((), ()))}


def _mm(a, b, *, mode="nn", out_dtype=F32, add=None, name, tm=512, tn=1024, tk=1024):
    if mode == "nn":
        (m, k), (_, n) = a.shape, b.shape
    elif mode == "nt":
        (m, k), (n, _) = a.shape, b.shape
    else:
        (k, m), (_, n) = a.shape, b.shape
    tm, tn, tk = _pick(m, tm, 8), _pick(n, tn), _pick(k, tk)
    nk = k // tk
    dims = _DIMS[mode]

    def body(*refs):
        if add is None:
            a_ref, b_ref, o_ref, acc_ref = refs
        else:
            a_ref, b_ref, r_ref, o_ref, acc_ref = refs
        kk = pl.program_id(2)

        @pl.when(kk == 0)
        def _():
            acc_ref[...] = jnp.zeros_like(acc_ref)

        acc_ref[...] += lax.dot_general(a_ref[...].astype(BF16), b_ref[...].astype(BF16), dims,
                                        preferred_element_type=F32)

        @pl.when(kk == nk - 1)
        def _():
            res = acc_ref[...]
            if add is not None:
                res = res + r_ref[...]
            o_ref[...] = res.astype(o_ref.dtype)

    a_spec = (pl.BlockSpec((tk, tm), lambda i, j, kk: (kk, i)) if mode == "tn"
              else pl.BlockSpec((tm, tk), lambda i, j, kk: (i, kk)))
    b_spec = (pl.BlockSpec((tn, tk), lambda i, j, kk: (j, kk)) if mode == "nt"
              else pl.BlockSpec((tk, tn), lambda i, j, kk: (kk, j)))
    in_specs = [a_spec, b_spec]
    args = [a, b]
    if add is not None:
        in_specs.append(pl.BlockSpec((tm, tn), lambda i, j, kk: (i, j)))
        args.append(add)
    return pl.pallas_call(
        body, name=name, grid=(m // tm, n // tn, nk),
        in_specs=in_specs, out_specs=pl.BlockSpec((tm, tn), lambda i, j, kk: (i, j)),
        out_shape=jax.ShapeDtypeStruct((m, n), out_dtype),
        scratch_shapes=[pltpu.VMEM((tm, tn), F32)],
        compiler_params=_params(("parallel", "parallel", "arbitrary")),
    )(*args)


def _rowwise(fn, tiled, full, out_tiled, out_acc, *, tm, name):
    rows = tiled[0].shape[0]
    tm = _pick(rows, tm, 8)
    nt, nf, no = len(tiled), len(full), len(out_tiled)

    def body(*refs):
        ins = [r[...] for r in refs[:nt + nf]]
        o_refs = refs[nt + nf:nt + nf + no]
        a_refs = refs[nt + nf + no:]
        outs = fn(*ins)
        if not isinstance(outs, (tuple, list)):
            outs = (outs,)
        for r, v in zip(o_refs, outs[:no]):
            r[...] = v.astype(r.dtype)
        if a_refs:
            @pl.when(pl.program_id(0) == 0)
            def _():
                for r in a_refs:
                    r[...] = jnp.zeros_like(r)

            for r, v in zip(a_refs, outs[no:]):
                r[...] += v.astype(r.dtype)

    def whole(shape):
        nd = len(shape)
        return pl.BlockSpec(tuple(shape), lambda i: (0,) * nd)

    in_specs = ([pl.BlockSpec((tm, t.shape[1]), lambda i: (i, 0)) for t in tiled]
                + [whole(f.shape) for f in full])
    out_specs = ([pl.BlockSpec((tm, o.shape[1]), lambda i: (i, 0)) for o in out_tiled]
                 + [whole(o.shape) for o in out_acc])
    outs = pl.pallas_call(
        body, name=name, grid=(rows // tm,), in_specs=in_specs, out_specs=out_specs,
        out_shape=list(out_tiled) + list(out_acc),
        compiler_params=_params(("arbitrary",)),
    )(*tiled, *full)
    return outs


def _sds(shape, dtype=F32):
    return jax.ShapeDtypeStruct(tuple(shape), dtype)


def _rms(x, g):
    return x * lax.rsqrt(jnp.mean(x * x, axis=-1, keepdims=True) + NORM_EPS) * g


def _layernorm(x, g, b):
    mu = jnp.mean(x, axis=-1, keepdims=True)
    xc = x - mu
    var = jnp.mean(xc * xc, axis=-1, keepdims=True)
    return xc * lax.rsqrt(var + NORM_EPS) * g + b


def _dot(a, b, mode="nn"):
    return lax.dot_general(a.astype(BF16), b.astype(BF16), _DIMS[mode], preferred_element_type=F32)


@jax.custom_vjp
def _bdot(a, b):
    return _dot(a, b)


def _bdot_fwd(a, b):
    return _bdot(a, b), (a, b)


def _bdot_bwd(res, ct):
    a, b = res
    return _dot(ct, b, "nt"), _dot(a, ct, "tn")


_bdot.defvjp(_bdot_fwd, _bdot_bwd)


def _rms_fwd(h, g, *, name):
    return _rowwise(lambda x, gg: _rms(x, gg), [h], [g], [_sds(h.shape, BF16)], [], tm=512, name=name)[0]


def _rms_bwd(h, g, d_hn, dh_out, *, name):
    def fn(x, ct, res, gg):
        _, vjp = jax.vjp(_rms, x, gg)
        dx, dg = vjp(ct)
        return res + dx, dg

    dh, dg = _rowwise(fn, [h, d_hn, dh_out], [g], [_sds(h.shape)], [_sds(g.shape)], tm=512, name=name)
    return dh, dg


def _gmlp_mid(uvz, ln_g, ln_b, w_s, b_st):
    di = ln_g.shape[1]
    ng, ck = w_s.shape[0], w_s.shape[1]
    dg = di // ng
    u = jax.nn.gelu(uvz[:, :di])
    v = _layernorm(jax.nn.gelu(uvz[:, di:2 * di]), ln_g, ln_b)
    z = uvz[:, 2 * di:]
    row = lax.broadcasted_iota(jnp.int32, (ck, ck), 0)
    col = lax.broadcasted_iota(jnp.int32, (ck, ck), 1)
    causal = col <= row
    blocks = []
    for c in range(uvz.shape[0] // ck):
        cols = []
        for g in range(ng):
            w = jnp.where(causal, w_s[g], 0.0)
            cols.append(_bdot(w, v[c * ck:(c + 1) * ck, g * dg:(g + 1) * dg]) + b_st[:, g:g + 1])
        blocks.append(jnp.concatenate(cols, axis=1))
    s = blocks[0] if len(blocks) == 1 else jnp.concatenate(blocks, axis=0)
    return u * s * jax.nn.silu(z)


def _gmlp_fwd(h, p):
    hn = _rms_fwd(h, p["norm_g"], name="gmlp_rms")
    uvz = _mm(hn, p["w_in"], name="gmlp_in")
    di = p["ln_g"].shape[1]
    gated = _rowwise(_gmlp_mid, [uvz], [p["ln_g"], p["ln_b"], p["w_s"], p["b_st"]],
                     [_sds((h.shape[0], di), BF16)], [], tm=256, name="gmlp_mid")[0]
    h_next = _mm(gated, p["w_out"], add=h, name="gmlp_out")
    return h_next, (h, hn, uvz, gated)


def _gmlp_bwd(dh_out, p, saved):
    h, hn, uvz, gated = saved
    d_gated = _mm(dh_out, p["w_out"], mode="nt", out_dtype=BF16, name="gmlp_dgated")
    g_w_out = _mm(gated, dh_out, mode="tn", name="gmlp_dwout")

    def fn(t, ct, ln_g, ln_b, w_s, b_st):
        _, vjp = jax.vjp(_gmlp_mid, t, ln_g, ln_b, w_s, b_st)
        return vjp(ct.astype(F32))

    d_uvz, g_ln_g, g_ln_b, g_w_s, g_b_st = _rowwise(
        fn, [uvz, d_gated], [p["ln_g"], p["ln_b"], p["w_s"], p["b_st"]],
        [_sds(uvz.shape, BF16)], [_sds(p["ln_g"].shape), _sds(p["ln_b"].shape), _sds(p["w_s"].shape),
                                  _sds(p["b_st"].shape)], tm=128, name="gmlp_mid_bwd")
    g_w_in = _mm(hn, d_uvz, mode="tn", name="gmlp_dwin")
    d_hn = _mm(d_uvz, p["w_in"], mode="nt", name="gmlp_dhn")
    dh_in, g_norm = _rms_bwd(h, p["norm_g"], d_hn, dh_out, name="gmlp_rms_bwd")
    grads = dict(norm_g=g_norm, w_in=g_w_in, ln_g=g_ln_g, ln_b=g_ln_b, w_s=g_w_s, b_s=g_b_st.T, w_out=g_w_out)
    return dh_in, grads


def _s5_operators(a_re, a_im, log_step, b_re, b_im, c_re, c_im):
    t_len = S5_T
    step = jnp.exp(log_step)[:, None]
    lr, li = a_re * step, a_im * step
    ks = jnp.arange(t_len + 1, dtype=F32)[:, None, None]
    mag = jnp.exp(ks * lr)
    pw_r, pw_i = mag * jnp.cos(ks * li), mag * jnp.sin(ks * li)
    nr, ni = pw_r[1] - 1.0, pw_i[1]
    den = a_re * a_re + a_im * a_im
    f_r, f_i = (nr * a_re + ni * a_im) / den, (ni * a_re - nr * a_im) / den
    bb_r = f_r[..., None] * b_re - f_i[..., None] * b_im
    bb_i = f_r[..., None] * b_im + f_i[..., None] * b_re
    hi = lax.Precision.HIGHEST
    cp_r = c_re[None] * pw_r[:, :, None, :] - c_im[None] * pw_i[:, :, None, :]
    cp_i = c_re[None] * pw_i[:, :, None, :] + c_im[None] * pw_r[:, :, None, :]
    kern = (jnp.einsum("gpi,kghp->kgih", bb_r, cp_r[:t_len], precision=hi)
            - jnp.einsum("gpi,kghp->kgih", bb_i, cp_i[:t_len], precision=hi))
    lag = jnp.arange(t_len)[None, :] - jnp.arange(t_len)[:, None]
    toep = jnp.where((lag >= 0)[:, :, None, None, None], kern[jnp.clip(lag, 0)], 0.0)
    n_g, n_h = a_re.shape[0], b_re.shape[2]
    toep = toep.transpose(2, 0, 3, 1, 4).reshape(n_g, t_len * n_h, t_len * n_h)
    rev_r, rev_i = pw_r[t_len - 1::-1][:t_len], pw_i[t_len - 1::-1][:t_len]
    we_r = rev_r[..., None] * bb_r[None] - rev_i[..., None] * bb_i[None]
    we_i = rev_r[..., None] * bb_i[None] + rev_i[..., None] * bb_r[None]
    wend = jnp.concatenate([we_r, we_i], axis=2).transpose(1, 0, 3, 2).reshape(n_g, t_len * n_h, -1)
    wo = jnp.concatenate([cp_r[1:], -cp_i[1:]], axis=3)
    wout = wo.transpose(1, 3, 0, 2).reshape(n_g, -1, t_len * n_h)
    a_r, a_i = pw_r[t_len], pw_i[t_len]
    a1 = jnp.concatenate([a_r, a_r], axis=1)
    a2 = jnp.concatenate([-a_i, a_i], axis=1)
    return toep, wend, wout, a1, a2


def _group_call(body, ins, outs, *, gb, name):
    n_g = ins[0].shape[0]

    def spec(a):
        return pl.BlockSpec((gb,) + tuple(a.shape[1:]), lambda i: (i, 0, 0))

    return pl.pallas_call(
        body, name=name, grid=(n_g // gb,), in_specs=[spec(a) for a in ins],
        out_specs=[spec(o) for o in outs], out_shape=list(outs),
        compiler_params=_params(("parallel",)),
    )(*ins)


def _s5_states(u_g, wend, *, gb=8):
    def body(u_ref, w_ref, s_ref):
        for g in range(gb):
            s_ref[g] = _dot(u_ref[g], w_ref[g])

    n_g, n_c = u_g.shape[0], u_g.shape[1]
    return _group_call(body, [u_g, wend], [_sds((n_g, n_c, wend.shape[2]))], gb=gb, name="s5_states")[0]


def _s5_outputs(u_g, toep, xprev, wout, *, gb=8):
    def body(u_ref, t_ref, x_ref, w_ref, y_ref):
        for g in range(gb):
            y_ref[g] = _dot(u_ref[g], t_ref[g]) + _dot(x_ref[g], w_ref[g])

    return _group_call(body, [u_g, toep, xprev, wout], [_sds(u_g.shape)], gb=gb, name="s5_outputs")[0]


def _s5_outputs_bwd(u_g, d_y, xprev, wout, *, gb=8):
    def body(u_ref, dy_ref, x_ref, w_ref, dt_ref, dw_ref, dx_ref):
        for g in range(gb):
            dy = dy_ref[g]
            dt_ref[g] = _dot(u_ref[g], dy, "tn")
            dw_ref[g] = _dot(x_ref[g], dy, "tn")
            dx_ref[g] = _dot(dy, w_ref[g], "nt")

    n_g, n_c, n_k = u_g.shape
    return _group_call(body, [u_g, d_y, xprev, wout],
                       [_sds((n_g, n_k, n_k)), _sds(wout.shape), _sds(xprev.shape)], gb=gb, name="s5_outputs_bwd")


def _s5_inputs_bwd(u_g, d_y, d_s, toep, wend, *, gb=8):
    def body(u_ref, dy_ref, ds_ref, t_ref, w_ref, du_ref, dw_ref):
        for g in range(gb):
            ds = ds_ref[g]
            du_ref[g] = _dot(dy_ref[g], t_ref[g], "nt") + _dot(ds, w_ref[g], "nt")
            dw_ref[g] = _dot(u_ref[g], ds, "tn")

    return _group_call(body, [u_g, d_y, d_s, toep, wend], [_sds(u_g.shape), _sds(wend.shape)], gb=gb,
                       name="s5_inputs_bwd")


def _swap_halves(x):
    return pltpu.roll(x, x.shape[-1] // 2, axis=x.ndim - 1)


def _s5_scan(s_t, a1, a2, *, gb=32):
    n_c, n_g, n_p = s_t.shape
    gb = min(gb, n_g)

    def body(s_ref, a1_ref, a2_ref, x_ref):
        a1v, a2v = a1_ref[...], a2_ref[...]

        def step(c, x):
            x_ref[c] = x
            return x * a1v + _swap_halves(x) * a2v + s_ref[c]

        lax.fori_loop(0, n_c, step, jnp.zeros((gb, n_p), F32))

    return pl.pallas_call(
        body, name="s5_scan", grid=(n_g // gb,),
        in_specs=[pl.BlockSpec((n_c, gb, n_p), lambda i: (0, i, 0)), pl.BlockSpec((gb, n_p), lambda i: (i, 0)),
                  pl.BlockSpec((gb, n_p), lambda i: (i, 0))],
        out_specs=pl.BlockSpec((n_c, gb, n_p), lambda i: (0, i, 0)), out_shape=_sds(s_t.shape),
        compiler_params=_params(("parallel",)),
    )(s_t, a1, a2)


def _s5_scan_bwd(d_xprev_t, xprev_t, a1, a2, *, gb=32):
    n_c, n_g, n_p = xprev_t.shape
    gb = min(gb, n_g)

    def body(dx_ref, x_ref, a1_ref, a2_ref, ds_ref, p1_ref, p2_ref):
        a1v, a2v = a1_ref[...], a2_ref[...]
        zero = jnp.zeros((gb, n_p), F32)
        ds_ref[n_c - 1] = zero

        def step(k, carry):
            gx_next, p1, p2 = carry
            c = n_c - 2 - k
            xp = x_ref[c + 1]
            gx = dx_ref[c + 1] + gx_next * a1v - _swap_halves(gx_next) * a2v
            ds_ref[c] = gx
            return gx, p1 + gx_next * xp, p2 + gx_next * _swap_halves(xp)

        _, p1, p2 = lax.fori_loop(0, n_c - 1, step, (zero, zero, zero))
        p1_ref[...] = p1
        p2_ref[...] = p2

    blk = pl.BlockSpec((n_c, gb, n_p), lambda i: (0, i, 0))
    vec = pl.BlockSpec((gb, n_p), lambda i: (i, 0))
    return pl.pallas_call(
        body, name="s5_scan_bwd", grid=(n_g // gb,), in_specs=[blk, blk, vec, vec],
        out_specs=[blk, vec, vec], out_shape=[_sds(xprev_t.shape), _sds(a1.shape), _sds(a1.shape)],
        compiler_params=_params(("parallel",)),
    )(d_xprev_t, xprev_t, a1, a2)


def _to_groups(t, n_g):
    n_l = t.shape[0]
    n_h = t.shape[1] // n_g
    return t.reshape(n_l // S5_T, S5_T, n_g, n_h).transpose(2, 0, 1, 3).reshape(n_g, n_l // S5_T, S5_T * n_h)


def _from_groups(t, n_l):
    n_g = t.shape[0]
    n_h = t.shape[2] // S5_T
    return t.reshape(n_g, n_l // S5_T, S5_T, n_h).transpose(1, 2, 0, 3).reshape(n_l, n_g * n_h)


def _s5_act(ys, uz, d_skip):
    di = d_skip.shape[1]
    return jax.nn.gelu(ys + d_skip * uz[:, :di])


def _s5_gate(g1, glu_pre, uz, b_glu):
    di = b_glu.shape[1]
    return g1 * jax.nn.sigmoid(glu_pre + b_glu) * jax.nn.silu(uz[:, di:])


def _s5_fwd(h, p):
    n_l = h.shape[0]
    di = p["d_skip"].shape[1]
    n_g = di // S5_GROUP
    hn = _rms_fwd(h, p["norm_g"], name="s5_rms")
    uz = _mm(hn, p["w_in"], name="s5_in")
    toep, wend, wout, a1, a2 = p["ops"]
    u_g = _to_groups(uz[:, :di].astype(BF16), n_g)
    s = _s5_states(u_g, wend)
    xprev = _s5_scan(s.transpose(1, 0, 2), a1, a2).transpose(1, 0, 2)
    ys = _from_groups(_s5_outputs(u_g, toep, xprev, wout), n_l)
    g1 = _rowwise(_s5_act, [ys, uz], [p["d_skip"]], [_sds((n_l, di), BF16)], [], tm=512, name="s5_act")[0]
    glu_pre = _mm(g1, p["w_glu"], name="s5_glu")

    def gate(ys_t, pre_t, uz_t, d_skip, b_glu):
        return _s5_gate(_s5_act(ys_t, uz_t, d_skip), pre_t, uz_t, b_glu)

    gated = _rowwise(gate, [ys, glu_pre, uz], [p["d_skip"], p["b_glu"]], [_sds((n_l, di), BF16)], [],
                     tm=512, name="s5_gate")[0]
    h_next = _mm(gated, p["w_out"], add=h, name="s5_out")
    return h_next, (h, hn, uz, u_g, xprev, ys, g1, glu_pre, gated)


def _s5_bwd(dh_out, p, saved, ops_vjp):
    h, hn, uz, u_g, xprev, ys, g1, glu_pre, gated = saved
    n_l = h.shape[0]
    di = p["d_skip"].shape[1]
    n_g = di // S5_GROUP
    toep, wend, wout, a1, a2 = p["ops"]
    d_gated = _mm(dh_out, p["w_out"], mode="nt", out_dtype=BF16, name="s5_dgated")
    g_w_out = _mm(gated, dh_out, mode="tn", name="s5_dwout")

    def gate_bwd(ys_t, pre_t, uz_t, ct, d_skip, b_glu):
        g1_t = _s5_act(ys_t, uz_t, d_skip)
        _, vjp = jax.vjp(_s5_gate, g1_t, pre_t, uz_t, b_glu)
        d_g1, d_pre, d_uz, d_b = vjp(ct.astype(F32))
        return d_g1, d_pre, d_uz, d_b

    d_g1_direct, d_pre, d_uz_gate, g_b_glu = _rowwise(
        gate_bwd, [ys, glu_pre, uz, d_gated], [p["d_skip"], p["b_glu"]],
        [_sds((n_l, di)), _sds((n_l, di), BF16), _sds(uz.shape)], [_sds(p["b_glu"].shape)], tm=256, name="s5_gate_bwd")
    g_w_glu = _mm(g1, d_pre, mode="tn", name="s5_dwglu")
    d_g1 = _mm(d_pre, p["w_glu"], mode="nt", add=d_g1_direct, name="s5_dg1")

    def act_bwd(ys_t, uz_t, ct, d_uz_t, d_skip):
        _, vjp = jax.vjp(_s5_act, ys_t, uz_t, d_skip)
        d_ys, d_uz, d_d = vjp(ct)
        return d_ys, d_uz + d_uz_t, d_d

    d_ys, d_uz_part, g_d_skip = _rowwise(
        act_bwd, [ys, uz, d_g1, d_uz_gate], [p["d_skip"]], [_sds((n_l, di), BF16), _sds(uz.shape)],
        [_sds(p["d_skip"].shape)], tm=256, name="s5_act_bwd")
    d_y = _to_groups(d_ys, n_g)
    d_toep, d_wout, d_xprev = _s5_outputs_bwd(u_g, d_y, xprev, wout)
    d_s_t, p1, p2 = _s5_scan_bwd(d_xprev.transpose(1, 0, 2), xprev.transpose(1, 0, 2), a1, a2)
    d_s = d_s_t.transpose(1, 0, 2)
    d_u_g, d_wend = _s5_inputs_bwd(u_g, d_y, d_s, toep, wend)
    d_u = _from_groups(d_u_g, n_l)
    d_uz = (d_uz_part + jnp.pad(d_u, ((0, 0), (0, di)))).astype(BF16)
    g_w_in = _mm(hn, d_uz, mode="tn", name="s5_dwin")
    d_hn = _mm(d_uz, p["w_in"], mode="nt", name="s5_dhn")
    dh_in, g_norm = _rms_bwd(h, p["norm_g"], d_hn, dh_out, name="s5_rms_bwd")
    g_ops = ops_vjp((d_toep, d_wend, d_wout, p1, p2))
    grads = dict(norm_g=g_norm, w_in=g_w_in, a_re=g_ops[0], a_im=g_ops[1], log_step=g_ops[2], b_re=g_ops[3],
                 b_im=g_ops[4], c_re=g_ops[5], c_im=g_ops[6], d_skip=g_d_skip, w_glu=g_w_glu, b_glu=g_b_glu,
                 w_out=g_w_out)
    return dh_in, grads


MLA_Z0 = MLA_Q_RANK + MLA_KV_RANK + LANES


def _rope_tile(t, cos_t, sin_t):
    q = LANES // 4
    lane = lax.broadcasted_iota(jnp.int32, t.shape, 1)
    swapped = jnp.where(lane < q, pltpu.roll(t, LANES - q, axis=1), pltpu.roll(t, q, axis=1))
    return t * cos_t + swapped * sin_t


def _mla_mid(proj, cos_t, sin_t, q_g, kv_g):
    cqn = _rms(proj[:, :MLA_Q_RANK], q_g)
    ckvn = _rms(proj[:, MLA_Q_RANK:MLA_Q_RANK + MLA_KV_RANK], kv_g)
    kr = _rope_tile(proj[:, MLA_Q_RANK + MLA_KV_RANK:MLA_Z0], cos_t, sin_t)
    return cqn, ckvn, kr


def _mla_rope_q(qp, cos_t, sin_t):
    parts = []
    for hd in range(qp.shape[1] // MLA_HEAD_PAD):
        base = hd * MLA_HEAD_PAD
        parts.append(qp[:, base:base + LANES])
        parts.append(_rope_tile(qp[:, base + LANES:base + MLA_HEAD_PAD], cos_t, sin_t))
    return jnp.concatenate(parts, axis=1)


def _mla_gate(o, proj):
    return o * jax.nn.silu(proj[:, MLA_Z0:])


def _scores(q, kv, kr, q0, k0):
    s = (_dot(q[:, :LANES], kv[:, :LANES], "nt") + _dot(q[:, LANES:], kr, "nt")) * MLA_SCALE
    qpos = q0 + lax.broadcasted_iota(jnp.int32, s.shape, 0)
    kpos = k0 + lax.broadcasted_iota(jnp.int32, s.shape, 1)
    mask = kpos <= qpos
    return jnp.where(mask, s, NEG_INF), mask


def _flash_fwd(qp, kv, kr, *, tq=512, tk=512):
    n_l = qp.shape[0]
    heads = qp.shape[1] // MLA_HEAD_PAD
    tq, tk = _pick(n_l, tq), _pick(n_l, tk)
    nq, nk = n_l // tq, n_l // tk

    def last_kv(i):
        return (i * tq + tq - 1) // tk

    def body(q_ref, kv_ref, kr_ref, o_ref, lse_ref, m_sc, l_sc, acc_sc):
        i, j = pl.program_id(1), pl.program_id(2)

        @pl.when(j == 0)
        def _():
            m_sc[...] = jnp.full_like(m_sc, NEG_INF)
            l_sc[...] = jnp.zeros_like(l_sc)
            acc_sc[...] = jnp.zeros_like(acc_sc)

        @pl.when(j <= last_kv(i))
        def _():
            kv_t = kv_ref[...]
            s, _ = _scores(q_ref[...], kv_t, kr_ref[...], i * tq, j * tk)
            m_old = m_sc[...]
            m_new = jnp.maximum(m_old, jnp.max(s, axis=1, keepdims=True))
            alpha = jnp.exp(m_old - m_new)
            pr = jnp.exp(s - m_new)
            l_sc[...] = alpha * l_sc[...] + jnp.sum(pr, axis=1, keepdims=True)
            acc_sc[...] = alpha * acc_sc[...] + _dot(pr, kv_t[:, LANES:])
            m_sc[...] = m_new

        @pl.when(j == nk - 1)
        def _():
            o_ref[...] = acc_sc[...] / l_sc[...]
            lse_ref[...] = m_sc[...] + jnp.log(l_sc[...])

    return pl.pallas_call(
        body, name="mla_flash_fwd", grid=(heads, nq, nk),
        in_specs=[pl.BlockSpec((tq, MLA_HEAD_PAD), lambda h, i, j: (i, h)),
                  pl.BlockSpec((tk, MLA_HEAD_PAD), lambda h, i, j: (jnp.minimum(j, last_kv(i)), h)),
                  pl.BlockSpec((tk, LANES), lambda h, i, j: (jnp.minimum(j, last_kv(i)), 0))],
        out_specs=[pl.BlockSpec((tq, MLA_V), lambda h, i, j: (i, h)),
                   pl.BlockSpec((None, tq, 1), lambda h, i, j: (h, i, 0))],
        out_shape=[_sds((n_l, heads * MLA_V)), _sds((heads, n_l, 1))],
        scratch_shapes=[pltpu.VMEM((tq, 1), F32), pltpu.VMEM((tq, 1), F32), pltpu.VMEM((tq, MLA_V), F32)],
        compiler_params=_params(("parallel", "parallel", "arbitrary")),
    )(qp, kv, kr)


def _flash_bwd_terms(q, kv, kr, d_o, o, lse, q0, k0):
    s, mask = _scores(q, kv, kr, q0, k0)
    pr = jnp.where(mask, jnp.exp(s - lse), 0.0)
    delta = jnp.sum(d_o.astype(F32) * o, axis=1, keepdims=True)
    d_p = _dot(d_o, kv[:, LANES:], "nt")
    d_s = pr * (d_p - delta) * MLA_SCALE
    return pr, d_s


def _flash_bwd_dq(qp, kv, kr, d_o, o, lse, *, tq=512, tk=512):
    n_l = qp.shape[0]
    heads = qp.shape[1] // MLA_HEAD_PAD
    tq, tk = _pick(n_l, tq), _pick(n_l, tk)
    nq, nk = n_l // tq, n_l // tk

    def last_kv(i):
        return (i * tq + tq - 1) // tk

    def body(q_ref, kv_ref, kr_ref, do_ref, o_ref, lse_ref, dq_ref, acc_sc):
        i, j = pl.program_id(1), pl.program_id(2)

        @pl.when(j == 0)
        def _():
            acc_sc[...] = jnp.zeros_like(acc_sc)

        @pl.when(j <= last_kv(i))
        def _():
            kv_t, kr_t = kv_ref[...], kr_ref[...]
            _, d_s = _flash_bwd_terms(q_ref[...], kv_t, kr_t, do_ref[...], o_ref[...], lse_ref[...], i * tq, j * tk)
            acc_sc[:, :LANES] += _dot(d_s, kv_t[:, :LANES])
            acc_sc[:, LANES:] += _dot(d_s, kr_t)

        @pl.when(j == nk - 1)
        def _():
            dq_ref[...] = acc_sc[...]

    kv_row = lambda h, i, j: jnp.minimum(j, last_kv(i))
    return pl.pallas_call(
        body, name="mla_flash_dq", grid=(heads, nq, nk),
        in_specs=[pl.BlockSpec((tq, MLA_HEAD_PAD), lambda h, i, j: (i, h)),
                  pl.BlockSpec((tk, MLA_HEAD_PAD), lambda h, i, j: (kv_row(h, i, j), h)),
                  pl.BlockSpec((tk, LANES), lambda h, i, j: (kv_row(h, i, j), 0)),
                  pl.BlockSpec((tq, MLA_V), lambda h, i, j: (i, h)),
                  pl.BlockSpec((tq, MLA_V), lambda h, i, j: (i, h)),
                  pl.BlockSpec((None, tq, 1), lambda h, i, j: (h, i, 0))],
        out_specs=pl.BlockSpec((tq, MLA_HEAD_PAD), lambda h, i, j: (i, h)),
        out_shape=_sds(qp.shape), scratch_shapes=[pltpu.VMEM((tq, MLA_HEAD_PAD), F32)],
        compiler_params=_params(("parallel", "parallel", "arbitrary")),
    )(qp, kv, kr, d_o, o, lse)


def _flash_bwd_dkv(qp, kv, kr, d_o, o, lse, *, tq=512, tk=512):
    n_l = qp.shape[0]
    heads = qp.shape[1] // MLA_HEAD_PAD
    tq, tk = _pick(n_l, tq), _pick(n_l, tk)
    nq, nk = n_l // tq, n_l // tk

    def first_q(j):
        return (j * tk) // tq

    def body(q_ref, kv_ref, kr_ref, do_ref, o_ref, lse_ref, dkv_ref, dkr_ref, dkv_sc, dkr_sc):
        j, h, i = pl.program_id(0), pl.program_id(1), pl.program_id(2)

        @pl.when(i == 0)
        def _():
            dkv_sc[...] = jnp.zeros_like(dkv_sc)

        @pl.when(jnp.logical_and(i == 0, h == 0))
        def _():
            dkr_sc[...] = jnp.zeros_like(dkr_sc)

        @pl.when(i >= first_q(j))
        def _():
            q_t, do_t = q_ref[...], do_ref[...]
            pr, d_s = _flash_bwd_terms(q_t, kv_ref[...], kr_ref[...], do_t, o_ref[...], lse_ref[...], i * tq, j * tk)
            dkv_sc[:, :LANES] += _dot(d_s, q_t[:, :LANES], "tn")
            dkv_sc[:, LANES:] += _dot(pr, do_t, "tn")
            dkr_sc[...] += _dot(d_s, q_t[:, LANES:], "tn")

        @pl.when(i == nq - 1)
        def _():
            dkv_ref[...] = dkv_sc[...].astype(dkv_ref.dtype)

        @pl.when(jnp.logical_and(i == nq - 1, h == heads - 1))
        def _():
            dkr_ref[...] = dkr_sc[...]

    q_row = lambda j, h, i: jnp.maximum(i, first_q(j))
    return pl.pallas_call(
        body, name="mla_flash_dkv", grid=(nk, heads, nq),
        in_specs=[pl.BlockSpec((tq, MLA_HEAD_PAD), lambda j, h, i: (q_row(j, h, i), h)),
                  pl.BlockSpec((tk, MLA_HEAD_PAD), lambda j, h, i: (j, h)),
                  pl.BlockSpec((tk, LANES), lambda j, h, i: (j, 0)),
                  pl.BlockSpec((tq, MLA_V), lambda j, h, i: (q_row(j, h, i), h)),
                  pl.BlockSpec((tq, MLA_V), lambda j, h, i: (q_row(j, h, i), h)),
                  pl.BlockSpec((None, tq, 1), lambda j, h, i: (h, q_row(j, h, i), 0))],
        out_specs=[pl.BlockSpec((tk, MLA_HEAD_PAD), lambda j, h, i: (j, h)),
                   pl.BlockSpec((tk, LANES), lambda j, h, i: (j, 0))],
        out_shape=[_sds(kv.shape, BF16), _sds(kr.shape)],
        scratch_shapes=[pltpu.VMEM((tk, MLA_HEAD_PAD), F32), pltpu.VMEM((tk, LANES), F32)],
        compiler_params=_params(("parallel", "arbitrary", "arbitrary")),
    )(qp, kv, kr, d_o, o, lse)


def _mla_fwd(h, p, rope):
    n_l = h.shape[0]
    cos_t, sin_t = rope
    hn = _rms_fwd(h, p["norm_g"], name="mla_rms")
    proj = _mm(hn, p["w_in"], name="mla_in", tn=896)
    cqn, ckvn, kr = _rowwise(_mla_mid, [proj, cos_t, sin_t], [p["q_norm_g"], p["kv_norm_g"]],
                             [_sds((n_l, MLA_Q_RANK), BF16), _sds((n_l, MLA_KV_RANK), BF16), _sds((n_l, LANES), BF16)],
                             [], tm=512, name="mla_mid")
    q_raw = _mm(cqn, p["w_uq"], name="mla_uq")
    qp = _rowwise(_mla_rope_q, [q_raw, cos_t, sin_t], [], [_sds(q_raw.shape, BF16)], [], tm=512, name="mla_rope_q")[0]
    kv = _mm(ckvn, p["w_ukv"], out_dtype=BF16, name="mla_ukv")
    o, lse = _flash_fwd(qp, kv, kr)
    gated = _rowwise(_mla_gate, [o, proj], [], [_sds(o.shape, BF16)], [], tm=512, name="mla_gate")[0]
    h_next = _mm(gated, p["w_out"], add=h, name="mla_out")
    return h_next, (h, hn, proj, cqn, ckvn, kr, qp, kv, o, lse, gated)


def _mla_bwd(dh_out, p, saved, rope):
    h, hn, proj, cqn, ckvn, kr, qp, kv, o, lse, gated = saved
    n_l = h.shape[0]
    cos_t, sin_t = rope
    d_gated = _mm(dh_out, p["w_out"], mode="nt", out_dtype=BF16, name="mla_dgated")
    g_w_out = _mm(gated, dh_out, mode="tn", name="mla_dwout")

    def gate_bwd(o_t, proj_t, ct):
        _, vjp = jax.vjp(lambda a, z: a * jax.nn.silu(z), o_t, proj_t[:, MLA_Z0:])
        return vjp(ct.astype(F32))

    d_o, d_z = _rowwise(gate_bwd, [o, proj, d_gated], [], [_sds(o.shape, BF16), _sds(o.shape)], [],
                        tm=512, name="mla_gate_bwd")
    d_qp = _flash_bwd_dq(qp, kv, kr, d_o, o, lse)
    d_kv, d_kr = _flash_bwd_dkv(qp, kv, kr, d_o, o, lse)

    def rope_q_bwd(ct, c_t, s_t):
        return _mla_rope_q(ct, c_t, -s_t)

    d_q_raw = _rowwise(rope_q_bwd, [d_qp, cos_t, sin_t], [], [_sds(d_qp.shape, BF16)], [], tm=512,
                       name="mla_rope_q_bwd")[0]
    g_w_uq = _mm(cqn, d_q_raw, mode="tn", name="mla_dwuq")
    d_cqn = _mm(d_q_raw, p["w_uq"], mode="nt", name="mla_dcqn")
    g_w_ukv = _mm(ckvn, d_kv, mode="tn", name="mla_dwukv")
    d_ckvn = _mm(d_kv, p["w_ukv"], mode="nt", name="mla_dckvn")

    def mid_bwd(proj_t, c_t, s_t, d_cq, d_ckv, d_kr_t, d_z_t, q_g, kv_g):
        _, vjp_q = jax.vjp(_rms, proj_t[:, :MLA_Q_RANK], q_g)
        _, vjp_kv = jax.vjp(_rms, proj_t[:, MLA_Q_RANK:MLA_Q_RANK + MLA_KV_RANK], kv_g)
        d_q_in, d_qg = vjp_q(d_cq)
        d_kv_in, d_kvg = vjp_kv(d_ckv)
        d_kr_in = _rope_tile(d_kr_t, c_t, -s_t)
        return jnp.concatenate([d_q_in, d_kv_in, d_kr_in, d_z_t], axis=1), d_qg, d_kvg

    d_proj, g_q_norm, g_kv_norm = _rowwise(
        mid_bwd, [proj, cos_t, sin_t, d_cqn, d_ckvn, d_kr, d_z], [p["q_norm_g"], p["kv_norm_g"]],
        [_sds(proj.shape, BF16)], [_sds(p["q_norm_g"].shape), _sds(p["kv_norm_g"].shape)], tm=512, name="mla_mid_bwd")
    g_w_in = _mm(hn, d_proj, mode="tn", name="mla_dwin", tn=896)
    d_hn = _mm(d_proj, p["w_in"], mode="nt", name="mla_dhn", tk=896)
    dh_in, g_norm = _rms_bwd(h, p["norm_g"], d_hn, dh_out, name="mla_rms_bwd")
    grads = dict(norm_g=g_norm, w_in=g_w_in, q_norm_g=g_q_norm, w_uq=g_w_uq, kv_norm_g=g_kv_norm, w_ukv=g_w_ukv,
                 w_out=g_w_out)
    return dh_in, grads


def _loss_head(h, g, target):
    def fn(x, t, gg):
        def local(xx, g2):
            err = _rms(xx, g2) - t
            return 0.5 * jnp.sum(jnp.mean(err * err, axis=-1))

        val, (dx, dg) = jax.value_and_grad(local, argnums=(0, 1))(x, gg)
        return dx, jnp.full((1, LANES), val, F32), dg

    dh, loss, dg = _rowwise(fn, [h, target], [g], [_sds(h.shape)], [_sds((1, LANES)), _sds(g.shape)], tm=512,
                            name="loss_head")
    return loss[0, 0], dh, dg


HBM_SPEC = pl.BlockSpec(memory_space=pltpu.HBM)


def _all_gather(shard, *, name):
    def body(x_ref, out_ref, send_sems, recv_sems, local_sem):
        x, y, c = lax.axis_index("x"), lax.axis_index("y"), lax.axis_index("c")
        me, sibling = (x, y, c), (x, y, 1 - c)
        chips = [(1 - x, y), (x, 1 - y), (1 - x, 1 - y)]

        def rows(px, py, pc):
            return out_ref.at[4 * px + 2 * py + pc]

        def copy(k, block, to, src=None):
            return pltpu.make_async_remote_copy(
                src_ref=rows(*block) if src is None else src, dst_ref=rows(*block),
                send_sem=send_sems.at[k], recv_sem=recv_sems.at[k], device_id=to, device_id_type=MESH)

        mine = pltpu.make_async_copy(x_ref, rows(*me), local_sem)
        mine.start()
        first = [copy(0, me, sibling, src=x_ref)]
        first += [copy(1 + j, me, (*chip, c), src=x_ref) for j, chip in enumerate(chips)]
        for cp in first:
            cp.start()
        passed = [copy(4 + j, (*chip, c), sibling) for j, chip in enumerate(chips)]
        for j, chip in enumerate(chips):
            copy(1 + j, (*chip, c), me).wait_recv()
            passed[j].start()
        copy(0, sibling, me).wait_recv()
        for j, chip in enumerate(chips):
            copy(4 + j, (*chip, 1 - c), me).wait_recv()
        for cp in first + passed:
            cp.wait_send()
        mine.wait()

    return pl.pallas_call(
        body, name=name, out_shape=jax.ShapeDtypeStruct((N_DEV,) + shard.shape, shard.dtype),
        in_specs=[HBM_SPEC], out_specs=HBM_SPEC,
        scratch_shapes=[pltpu.SemaphoreType.DMA((7,)), pltpu.SemaphoreType.DMA((7,)), pltpu.SemaphoreType.DMA],
    )(shard)


def _exchange(send, *, name):
    def body(s_ref, out_ref, send_sems, recv_sems, local_sem):
        x, y, c = lax.axis_index("x"), lax.axis_index("y"), lax.axis_index("c")
        mine = pltpu.make_async_copy(s_ref.at[4 * x + 2 * y + c], out_ref.at[0], local_sem)
        mine.start()
        copies = []
        for r in range(1, N_DEV):
            px = 1 - x if r & 4 else x
            py = 1 - y if r & 2 else y
            pc = 1 - c if r & 1 else c
            copies.append(pltpu.make_async_remote_copy(
                src_ref=s_ref.at[4 * px + 2 * py + pc], dst_ref=out_ref.at[r],
                send_sem=send_sems.at[r - 1], recv_sem=recv_sems.at[r - 1], device_id=(px, py, pc),
                device_id_type=MESH))
        for cp in copies:
            cp.start()
        for cp in copies:
            cp.wait_recv()
        for cp in copies:
            cp.wait_send()
        mine.wait()

    return pl.pallas_call(
        body, name=name, out_shape=jax.ShapeDtypeStruct(send.shape, send.dtype),
        in_specs=[HBM_SPEC], out_specs=HBM_SPEC,
        scratch_shapes=[pltpu.SemaphoreType.DMA((7,)), pltpu.SemaphoreType.DMA((7,)), pltpu.SemaphoreType.DMA],
    )(send)


def _sum_slots(recv, *, name):
    n_s, rows, width = recv.shape
    tr = _pick(rows, 256, 8)

    def body(r_ref, o_ref):
        acc = r_ref[0].astype(F32)
        for s in range(1, n_s):
            acc = acc + r_ref[s].astype(F32)
        o_ref[...] = acc

    return pl.pallas_call(
        body, name=name, grid=(rows // tr,),
        in_specs=[pl.BlockSpec((n_s, tr, width), lambda i: (0, i, 0))],
        out_specs=pl.BlockSpec((tr, width), lambda i: (i, 0)), out_shape=_sds((rows, width)),
        compiler_params=_params(("parallel",)),
    )(recv)


def _adamw(w, g, m, v, *, name):
    rows, width = w.shape
    tr = _pick(rows, 256, 8)

    def body(w_ref, g_ref, m_ref, v_ref, d_ref, nm_ref, nv_ref):
        gg = g_ref[...]
        m_new = ADAM_B1 * m_ref[...] + (1.0 - ADAM_B1) * gg
        v_new = ADAM_B2 * v_ref[...] + (1.0 - ADAM_B2) * jnp.square(gg)
        m_hat = m_new / (1.0 - ADAM_B1 ** ADAM_STEP)
        v_hat = v_new / (1.0 - ADAM_B2 ** ADAM_STEP)
        d_ref[...] = -ADAM_LR * (m_hat / (jnp.sqrt(v_hat) + ADAM_EPS) + ADAM_WD * w_ref[...])
        nm_ref[...] = m_new
        nv_ref[...] = v_new

    spec = pl.BlockSpec((tr, width), lambda i: (i, 0))
    return pl.pallas_call(
        body, name=name, grid=(rows // tr,), in_specs=[spec] * 4, out_specs=[spec] * 3,
        out_shape=[_sds(w.shape)] * 3, compiler_params=_params(("parallel",)),
    )(w, g, m, v)


KINDS = ("gmlp", "s5", "mla", "gmlp")
LAYER_NAMES = {
    "gmlp": ("norm_g", "w_in", "ln_g", "ln_b", "w_s", "b_s", "w_out"),
    "s5": ("norm_g", "w_in", "a_re", "a_im", "log_step", "b_re", "b_im", "c_re", "c_im", "d_skip", "w_glu", "b_glu",
           "w_out"),
    "mla": ("norm_g", "w_in", "q_norm_g", "w_uq", "kv_norm_g", "w_ukv", "w_out"),
}
COL_SHARDED = ("w_in", "w_uq", "w_ukv")
ROW_SHARDED = ("w_out", "w_glu")
WEIGHT_NAMES = tuple("l%d_%s" % (i, n) for i, k in enumerate(KINDS) for n in LAYER_NAMES[k]) + ("final_norm_g",)


def _is_sharded(name):
    return name.split("_", 1)[1] in COL_SHARDED + ROW_SHARDED


def _flatten(arrs, pad_rows_to):
    parts, sizes = [], []
    for a in arrs:
        flat = a.reshape(-1)
        pad = (-flat.shape[0]) % FLAT_W
        if pad:
            flat = jnp.pad(flat, (0, pad))
        parts.append(flat)
        sizes.append(flat.shape[0] // FLAT_W)
    rows = sum(sizes)
    pad_rows = (-rows) % pad_rows_to
    if pad_rows:
        parts.append(jnp.zeros((pad_rows * FLAT_W,), arrs[0].dtype))
    return jnp.concatenate(parts).reshape(-1, FLAT_W), sizes


def _unflatten(flat, shapes, sizes):
    out, row = [], 0
    for shape, n_rows in zip(shapes, sizes):
        n = int(np.prod(shape))
        out.append(flat[row:row + n_rows].reshape(-1)[:n].reshape(shape))
        row += n_rows
    return out


def _full_from_gathered(blocks, name):
    if name.split("_", 1)[1] in COL_SHARDED:
        return blocks.transpose(1, 0, 2).reshape(blocks.shape[1], -1)
    return blocks.reshape(-1, blocks.shape[2])


def _shards_of(full, name):
    if name.split("_", 1)[1] in COL_SHARDED:
        return full.reshape(full.shape[0], N_DEV, -1).transpose(1, 0, 2)
    return full.reshape(N_DEV, -1, full.shape[1])


def _rope_tables(positions):
    inv_freq = ROPE_THETA ** (-jnp.arange(0, MLA_ROPE, 2, dtype=F32) / MLA_ROPE)
    ang = positions.astype(F32)[:, None] * inv_freq
    cos, sin = jnp.cos(ang), jnp.sin(ang)
    zero = jnp.zeros((positions.shape[0], LANES - MLA_ROPE), F32)
    return jnp.concatenate([cos, cos, zero], axis=1), jnp.concatenate([-sin, sin, zero], axis=1)


def _row(v):
    return v.reshape(1, -1)


def kernel(x, positions, l0_norm_g, l0_w_in, l0_ln_g, l0_ln_b, l0_w_s, l0_b_s, l0_w_out, l1_norm_g, l1_w_in, l1_a_re, l1_a_im, l1_log_step, l1_b_re, l1_b_im, l1_c_re, l1_c_im, l1_d_skip, l1_w_glu, l1_b_glu, l1_w_out, l2_norm_g, l2_w_in, l2_q_norm_g, l2_w_uq, l2_kv_norm_g, l2_w_ukv, l2_w_out, l3_norm_g, l3_w_in, l3_ln_g, l3_ln_b, l3_w_s, l3_b_s, l3_w_out, final_norm_g, loss_target, m_l0_norm_g, m_l0_w_in, m_l0_ln_g, m_l0_ln_b, m_l0_w_s, m_l0_b_s, m_l0_w_out, m_l1_norm_g, m_l1_w_in, m_l1_a_re, m_l1_a_im, m_l1_log_step, m_l1_b_re, m_l1_b_im, m_l1_c_re, m_l1_c_im, m_l1_d_skip, m_l1_w_glu, m_l1_b_glu, m_l1_w_out, m_l2_norm_g, m_l2_w_in, m_l2_q_norm_g, m_l2_w_uq, m_l2_kv_norm_g, m_l2_w_ukv, m_l2_w_out, m_l3_norm_g, m_l3_w_in, m_l3_ln_g, m_l3_ln_b, m_l3_w_s, m_l3_b_s, m_l3_w_out, m_final_norm_g, v_l0_norm_g, v_l0_w_in, v_l0_ln_g, v_l0_ln_b, v_l0_w_s, v_l0_b_s, v_l0_w_out, v_l1_norm_g, v_l1_w_in, v_l1_a_re, v_l1_a_im, v_l1_log_step, v_l1_b_re, v_l1_b_im, v_l1_c_re, v_l1_c_im, v_l1_d_skip, v_l1_w_glu, v_l1_b_glu, v_l1_w_out, v_l2_norm_g, v_l2_w_in, v_l2_q_norm_g, v_l2_w_uq, v_l2_kv_norm_g, v_l2_w_ukv, v_l2_w_out, v_l3_norm_g, v_l3_w_in, v_l3_ln_g, v_l3_ln_b, v_l3_w_s, v_l3_b_s, v_l3_w_out, v_final_norm_g):
    args = locals()
    weights = {n: args[n] for n in WEIGHT_NAMES}
    mom_m = {n: args["m_" + n] for n in WEIGHT_NAMES}
    mom_v = {n: args["v_" + n] for n in WEIGHT_NAMES}
    return _train_step(x, positions, loss_target, weights, mom_m, mom_v)


def _train_step(x, positions, loss_target, weights, mom_m, mom_v):
    big = [n for n in WEIGHT_NAMES if _is_sharded(n)]
    small = [n for n in WEIGHT_NAMES if not _is_sharded(n)]

    w_flat, big_sizes = _flatten([weights[n] for n in big], 8)
    gathered = _all_gather(w_flat.astype(BF16), name="weights_all_gather")
    full, row = {}, 0
    for n, n_rows in zip(big, big_sizes):
        blocks = gathered[:, row:row + n_rows].reshape((N_DEV,) + weights[n].shape)
        full[n] = _full_from_gathered(blocks, n)
        row += n_rows

    layers, ops_vjps = [], {}
    for i, kind in enumerate(KINDS):
        pre = "l%d_" % i
        p = {n: (full[pre + n] if _is_sharded(pre + n) else weights[pre + n]) for n in LAYER_NAMES[kind]}
        p["norm_g"] = _row(p["norm_g"])
        if kind == "gmlp":
            p["ln_g"], p["ln_b"], p["b_st"] = _row(p["ln_g"]), _row(p["ln_b"]), p["b_s"].T
        elif kind == "s5":
            p["d_skip"], p["b_glu"] = _row(p["d_skip"]), _row(p["b_glu"])
            ops, ops_vjps[i] = jax.vjp(_s5_operators, *[p[n] for n in ("a_re", "a_im", "log_step", "b_re", "b_im",
                                                                       "c_re", "c_im")])
            p["ops"] = tuple(o.astype(BF16) for o in ops[:3]) + ops[3:]
        else:
            heads = p["w_uq"].shape[1] // MLA_QK_DIM
            w_in = p["w_in"]
            split = MLA_Q_RANK + MLA_KV_RANK + MLA_ROPE
            p["w_in"] = jnp.concatenate([w_in[:, :split], jnp.zeros((w_in.shape[0], LANES - MLA_ROPE), w_in.dtype),
                                         w_in[:, split:]], axis=1)
            p["w_uq"] = jnp.pad(p["w_uq"].reshape(-1, heads, MLA_QK_DIM),
                                ((0, 0), (0, 0), (0, MLA_HEAD_PAD - MLA_QK_DIM))).reshape(-1, heads * MLA_HEAD_PAD)
            p["q_norm_g"], p["kv_norm_g"] = _row(p["q_norm_g"]), _row(p["kv_norm_g"])
        layers.append(p)
    rope = _rope_tables(positions[0])

    h = x[0]
    saved = []
    for kind, p in zip(KINDS, layers):
        if kind == "gmlp":
            h, s = _gmlp_fwd(h, p)
        elif kind == "s5":
            h, s = _s5_fwd(h, p)
        else:
            h, s = _mla_fwd(h, p, rope)
        saved.append(s)
    loss_local, dh, g_final = _loss_head(h, _row(weights["final_norm_g"]), loss_target[0])
    loss = lax.psum(loss_local, ("x", "y", "c"))

    grads = {"final_norm_g": g_final.reshape(-1)}
    for i in reversed(range(len(KINDS))):
        kind, p = KINDS[i], layers[i]
        if kind == "gmlp":
            dh, g = _gmlp_bwd(dh, p, saved[i])
        elif kind == "s5":
            dh, g = _s5_bwd(dh, p, saved[i], ops_vjps[i])
        else:
            dh, g = _mla_bwd(dh, p, saved[i], rope)
            heads = weights["l%d_w_uq" % i].shape[1] * N_DEV // MLA_QK_DIM
            split = MLA_Q_RANK + MLA_KV_RANK + MLA_ROPE
            g["w_in"] = jnp.concatenate([g["w_in"][:, :split], g["w_in"][:, MLA_Z0:]], axis=1)
            g["w_uq"] = g["w_uq"].reshape(-1, heads, MLA_HEAD_PAD)[:, :, :MLA_QK_DIM].reshape(-1, heads * MLA_QK_DIM)
        for n, val in g.items():
            name = "l%d_%s" % (i, n)
            grads[name] = val.reshape(weights[name].shape) if not _is_sharded(name) else val

    small_flat, small_sizes = _flatten([grads[n] for n in small], 8 * N_DEV)
    small_rows = small_flat.shape[0] // N_DEV
    send_parts = [_shards_of(grads[n], n).reshape(N_DEV, -1, FLAT_W) for n in big]
    send_parts.append(small_flat.reshape(N_DEV, small_rows, FLAT_W))
    send = jnp.concatenate(send_parts, axis=1)
    big_rows = send.shape[1] - small_rows
    pad_rows = (-send.shape[1]) % 8
    if pad_rows:
        send = jnp.pad(send, ((0, 0), (0, pad_rows), (0, 0)))
    recv = _exchange(send.astype(BF16), name="grads_exchange")
    reduced = _sum_slots(recv, name="grads_sum")
    g_big_flat = reduced[:big_rows]
    g_small_all = _all_gather(reduced[big_rows:big_rows + small_rows], name="small_grads_all_gather")
    g_small_flat = g_small_all.reshape(-1, FLAT_W)

    def flat_of(tree, names, pad_to):
        return _flatten([tree[n] for n in names], pad_to)[0]

    rows_b = g_big_flat.shape[0]
    pad_b = (-rows_b) % 8
    if pad_b:
        g_big_flat = jnp.pad(g_big_flat, ((0, pad_b), (0, 0)))
    d_b, nm_b, nv_b = _adamw(flat_of(weights, big, 8), g_big_flat, flat_of(mom_m, big, 8), flat_of(mom_v, big, 8),
                             name="adamw_sharded")
    d_s, nm_s, nv_s = _adamw(flat_of(weights, small, 8 * N_DEV), g_small_flat, flat_of(mom_m, small, 8 * N_DEV),
                             flat_of(mom_v, small, 8 * N_DEV), name="adamw_replicated")

    big_shapes = [weights[n].shape for n in big]
    small_shapes = [weights[n].shape for n in small]
    outs = {}
    for prefix, fb, fs in (("grad_", g_big_flat, g_small_flat), ("delta_", d_b, d_s), ("new_m_", nm_b, nm_s),
                           ("new_v_", nv_b, nv_s)):
        for n, a in zip(big, _unflatten(fb, big_shapes, big_sizes)):
            outs[prefix + n] = a
        for n, a in zip(small, _unflatten(fs, small_shapes, small_sizes)):
            outs[prefix + n] = a
    result = [loss, dh[None]]
    for prefix in ("grad_", "delta_", "new_m_", "new_v_"):
        result += [outs[prefix + n] for n in WEIGHT_NAMES]
    return tuple(result)
```

```python
import functools
import math

import numpy as np
import jax
import jax.numpy as jnp
from jax import lax
from jax.experimental import pallas as pl
from jax.experimental.pallas import tpu as pltpu

F32 = jnp.float32
BF16 = jnp.bfloat16

NORM_EPS = 1e-6
GMLP_CHUNK = 128
S5_GROUP = 16
S5_STATE = 64
S5_T = 16
MLA_NOPE = 128
MLA_ROPE = 64
MLA_V = 128
MLA_QK_DIM = MLA_NOPE + MLA_ROPE
MLA_Q_RANK = 384
MLA_KV_RANK = 128
MLA_HEAD_PAD = 256
MLA_SCALE = MLA_QK_DIM ** -0.5
ROPE_THETA = 10000.0
NEG_INF = -1e30
ADAM_LR = 0.001
ADAM_B1 = 0.9
ADAM_B2 = 0.999
ADAM_EPS = 1e-08
ADAM_WD = 0.01
ADAM_STEP = 10

N_DEV = 8
LANES = 128
FLAT_W = 1024
VMEM_LIMIT = 56 * 1024 * 1024
MESH = pl.DeviceIdType.MESH


def _pick(dim, pref, align=LANES):
    t = (min(pref, dim) // align) * align
    while t >= align:
        if dim % t == 0:
            return t
        t -= align
    return dim


def _params(sem=None):
    return pltpu.CompilerParams(dimension_semantics=sem, vmem_limit_bytes=VMEM_LIMIT)


_DIMS = {"nn": (((1,), (0,)), ((), ())), "nt": (((1,), (1,)), ((), ())), "tn": (((0,), (0,)), ((), ()))}


def _mm(a, b, *, mode="nn", out_dtype=F32, add=None, name, tm=512, tn=1024, tk=1024):
    if mode == "nn":
        (m, k), (_, n) = a.shape, b.shape
    elif mode == "nt":
        (m, k), (n, _) = a.shape, b.shape
    else:
        (k, m), (_, n) = a.shape, b.shape
    tm, tn, tk = _pick(m, tm, 8), _pick(n, tn), _pick(k, tk)
    nk = k // tk
    dims = _DIMS[mode]

    def body(*refs):
        if add is None:
            a_ref, b_ref, o_ref, acc_ref = refs
        else:
            a_ref, b_ref, r_ref, o_ref, acc_ref = refs
        kk = pl.program_id(2)

        @pl.when(kk == 0)
        def _():
            acc_ref[...] = jnp.zeros_like(acc_ref)

        acc_ref[...] += lax.dot_general(a_ref[...].astype(BF16), b_ref[...].astype(BF16), dims,
                                        preferred_element_type=F32)

        @pl.when(kk == nk - 1)
        def _():
            res = acc_ref[...]
            if add is not None:
                res = res + r_ref[...]
            o_ref[...] = res.astype(o_ref.dtype)

    a_spec = (pl.BlockSpec((tk, tm), lambda i, j, kk: (kk, i)) if mode == "tn"
              else pl.BlockSpec((tm, tk), lambda i, j, kk: (i, kk)))
    b_spec = (pl.BlockSpec((tn, tk), lambda i, j, kk: (j, kk)) if mode == "nt"
              else pl.BlockSpec((tk, tn), lambda i, j, kk: (kk, j)))
    in_specs = [a_spec, b_spec]
    args = [a, b]
    if add is not None:
        in_specs.append(pl.BlockSpec((tm, tn), lambda i, j, kk: (i, j)))
        args.append(add)
    return pl.pallas_call(
        body, name=name, grid=(m // tm, n // tn, nk),
        in_specs=in_specs, out_specs=pl.BlockSpec((tm, tn), lambda i, j, kk: (i, j)),
        out_shape=jax.ShapeDtypeStruct((m, n), out_dtype),
        scratch_shapes=[pltpu.VMEM((tm, tn), F32)],
        compiler_params=_params(("parallel", "parallel", "arbitrary")),
    )(*args)


def _rowwise(fn, tiled, full, out_tiled, out_acc, *, tm, name):
    rows = tiled[0].shape[0]
    tm = _pick(rows, tm, 8)
    nt, nf, no = len(tiled), len(full), len(out_tiled)

    def body(*refs):
        ins = [r[...] for r in refs[:nt + nf]]
        o_refs = refs[nt + nf:nt + nf + no]
        a_refs = refs[nt + nf + no:]
        outs = fn(*ins)
        if not isinstance(outs, (tuple, list)):
            outs = (outs,)
        for r, v in zip(o_refs, outs[:no]):
            r[...] = v.astype(r.dtype)
        if a_refs:
            @pl.when(pl.program_id(0) == 0)
            def _():
                for r in a_refs:
                    r[...] = jnp.zeros_like(r)

            for r, v in zip(a_refs, outs[no:]):
                r[...] += v.astype(r.dtype)

    def whole(shape):
        nd = len(shape)
        return pl.BlockSpec(tuple(shape), lambda i: (0,) * nd)

    in_specs = ([pl.BlockSpec((tm, t.shape[1]), lambda i: (i, 0)) for t in tiled]
                + [whole(f.shape) for f in full])
    out_specs = ([pl.BlockSpec((tm, o.shape[1]), lambda i: (i, 0)) for o in out_tiled]
                 + [whole(o.shape) for o in out_acc])
    outs = pl.pallas_call(
        body, name=name, grid=(rows // tm,), in_specs=in_specs, out_specs=out_specs,
        out_shape=list(out_tiled) + list(out_acc),
        compiler_params=_params(("arbitrary",)),
    )(*tiled, *full)
    return outs


def _sds(shape, dtype=F32):
    return jax.ShapeDtypeStruct(tuple(shape), dtype)


def _rms(x, g):
    return x * lax.rsqrt(jnp.mean(x * x, axis=-1, keepdims=True) + NORM_EPS) * g


def _layernorm(x, g, b):
    mu = jnp.mean(x, axis=-1, keepdims=True)
    xc = x - mu
    var = jnp.mean(xc * xc, axis=-1, keepdims=True)
    return xc * lax.rsqrt(var + NORM_EPS) * g + b


def _dot(a, b, mode="nn"):
    return lax.dot_general(a.astype(BF16), b.astype(BF16), _DIMS[mode], preferred_element_type=F32)


@jax.custom_vjp
def _bdot(a, b):
    return _dot(a, b)


def _bdot_fwd(a, b):
    return _bdot(a, b), (a, b)


def _bdot_bwd(res, ct):
    a, b = res
    return _dot(ct, b, "nt"), _dot(a, ct, "tn")


_bdot.defvjp(_bdot_fwd, _bdot_bwd)


def _rms_fwd(h, g, *, name):
    return _rowwise(lambda x, gg: _rms(x, gg), [h], [g], [_sds(h.shape, BF16)], [], tm=512, name=name)[0]


def _rms_bwd(h, g, d_hn, dh_out, *, name):
    def fn(x, ct, res, gg):
        _, vjp = jax.vjp(_rms, x, gg)
        dx, dg = vjp(ct)
        return res + dx, dg

    dh, dg = _rowwise(fn, [h, d_hn, dh_out], [g], [_sds(h.shape)], [_sds(g.shape)], tm=512, name=name)
    return dh, dg


def _gmlp_mid(uvz, ln_g, ln_b, w_s, b_st):
    di = ln_g.shape[1]
    ng, ck = w_s.shape[0], w_s.shape[1]
    dg = di // ng
    u = jax.nn.gelu(uvz[:, :di])
    v = _layernorm(jax.nn.gelu(uvz[:, di:2 * di]), ln_g, ln_b)
    z = uvz[:, 2 * di:]
    row = lax.broadcasted_iota(jnp.int32, (ck, ck), 0)
    col = lax.broadcasted_iota(jnp.int32, (ck, ck), 1)
    causal = col <= row
    blocks = []
    for c in range(uvz.shape[0] // ck):
        cols = []
        for g in range(ng):
            w = jnp.where(causal, w_s[g], 0.0)
            cols.append(_bdot(w, v[c * ck:(c + 1) * ck, g * dg:(g + 1) * dg]) + b_st[:, g:g + 1])
        blocks.append(jnp.concatenate(cols, axis=1))
    s = blocks[0] if len(blocks) == 1 else jnp.concatenate(blocks, axis=0)
    return u * s * jax.nn.silu(z)


def _gmlp_fwd(h, p):
    hn = _rms_fwd(h, p["norm_g"], name="gmlp_rms")
    uvz = _mm(hn, p["w_in"], name="gmlp_in")
    di = p["ln_g"].shape[1]
    gated = _rowwise(_gmlp_mid, [uvz], [p["ln_g"], p["ln_b"], p["w_s"], p["b_st"]],
                     [_sds((h.shape[0], di), BF16)], [], tm=256, name="gmlp_mid")[0]
    h_next = _mm(gated, p["w_out"], add=h, name="gmlp_out")
    return h_next, (h, hn, uvz, gated)


def _gmlp_bwd(dh_out, p, saved):
    h, hn, uvz, gated = saved
    d_gated = _mm(dh_out, p["w_out"], mode="nt", out_dtype=BF16, name="gmlp_dgated")
    g_w_out = _mm(gated, dh_out, mode="tn", name="gmlp_dwout")

    def fn(t, ct, ln_g, ln_b, w_s, b_st):
        _, vjp = jax.vjp(_gmlp_mid, t, ln_g, ln_b, w_s, b_st)
        return vjp(ct.astype(F32))

    d_uvz, g_ln_g, g_ln_b, g_w_s, g_b_st = _rowwise(
        fn, [uvz, d_gated], [p["ln_g"], p["ln_b"], p["w_s"], p["b_st"]],
        [_sds(uvz.shape, BF16)], [_sds(p["ln_g"].shape), _sds(p["ln_b"].shape), _sds(p["w_s"].shape),
                                  _sds(p["b_st"].shape)], tm=128, name="gmlp_mid_bwd")
    g_w_in = _mm(hn, d_uvz, mode="tn", name="gmlp_dwin")
    d_hn = _mm(d_uvz, p["w_in"], mode="nt", name="gmlp_dhn")
    dh_in, g_norm = _rms_bwd(h, p["norm_g"], d_hn, dh_out, name="gmlp_rms_bwd")
    grads = dict(norm_g=g_norm, w_in=g_w_in, ln_g=g_ln_g, ln_b=g_ln_b, w_s=g_w_s, b_s=g_b_st.T, w_out=g_w_out)
    return dh_in, grads


def _s5_operators(a_re, a_im, log_step, b_re, b_im, c_re, c_im):
    t_len = S5_T
    step = jnp.exp(log_step)[:, None]
    lr, li = a_re * step, a_im * step
    ks = jnp.arange(t_len + 1, dtype=F32)[:, None, None]
    mag = jnp.exp(ks * lr)
    pw_r, pw_i = mag * jnp.cos(ks * li), mag * jnp.sin(ks * li)
    nr, ni = pw_r[1] - 1.0, pw_i[1]
    den = a_re * a_re + a_im * a_im
    f_r, f_i = (nr * a_re + ni * a_im) / den, (ni * a_re - nr * a_im) / den
    bb_r = f_r[..., None] * b_re - f_i[..., None] * b_im
    bb_i = f_r[..., None] * b_im + f_i[..., None] * b_re
    hi = lax.Precision.HIGHEST
    cp_r = c_re[None] * pw_r[:, :, None, :] - c_im[None] * pw_i[:, :, None, :]
    cp_i = c_re[None] * pw_i[:, :, None, :] + c_im[None] * pw_r[:, :, None, :]
    kern = (jnp.einsum("gpi,kghp->kgih", bb_r, cp_r[:t_len], precision=hi)
            - jnp.einsum("gpi,kghp->kgih", bb_i, cp_i[:t_len], precision=hi))
    lag = jnp.arange(t_len)[None, :] - jnp.arange(t_len)[:, None]
    toep = jnp.where((lag >= 0)[:, :, None, None, None], kern[jnp.clip(lag, 0)], 0.0)
    n_g, n_h = a_re.shape[0], b_re.shape[2]
    toep = toep.transpose(2, 0, 3, 1, 4).reshape(n_g, t_len * n_h, t_len * n_h)
    rev_r, rev_i = pw_r[t_len - 1::-1][:t_len], pw_i[t_len - 1::-1][:t_len]
    we_r = rev_r[..., None] * bb_r[None] - rev_i[..., None] * bb_i[None]
    we_i = rev_r[..., None] * bb_i[None] + rev_i[..., None] * bb_r[None]
    wend = jnp.concatenate([we_r, we_i], axis=2).transpose(1, 0, 3, 2).reshape(n_g, t_len * n_h, -1)
    wo = jnp.concatenate([cp_r[1:], -cp_i[1:]], axis=3)
    wout = wo.transpose(1, 3, 0, 2).reshape(n_g, -1, t_len * n_h)
    a_r, a_i = pw_r[t_len], pw_i[t_len]
    a1 = jnp.concatenate([a_r, a_r], axis=1)
    a2 = jnp.concatenate([-a_i, a_i], axis=1)
    return toep, wend, wout, a1, a2


def _group_call(body, ins, outs, *, gb, name):
    n_g = ins[0].shape[0]

    def spec(a):
        return pl.BlockSpec((gb,) + tuple(a.shape[1:]), lambda i: (i, 0, 0))

    return pl.pallas_call(
        body, name=name, grid=(n_g // gb,), in_specs=[spec(a) for a in ins],
        out_specs=[spec(o) for o in outs], out_shape=list(outs),
        compiler_params=_params(("parallel",)),
    )(*ins)


def _s5_states(u_g, wend, *, gb=8):
    def body(u_ref, w_ref, s_ref):
        for g in range(gb):
            s_ref[g] = _dot(u_ref[g], w_ref[g])

    n_g, n_c = u_g.shape[0], u_g.shape[1]
    return _group_call(body, [u_g, wend], [_sds((n_g, n_c, wend.shape[2]))], gb=gb, name="s5_states")[0]


def _s5_outputs(u_g, toep, xprev, wout, *, gb=8):
    def body(u_ref, t_ref, x_ref, w_ref, y_ref):
        for g in range(gb):
            y_ref[g] = _dot(u_ref[g], t_ref[g]) + _dot(x_ref[g], w_ref[g])

    return _group_call(body, [u_g, toep, xprev, wout], [_sds(u_g.shape)], gb=gb, name="s5_outputs")[0]


def _s5_outputs_bwd(u_g, d_y, xprev, wout, *, gb=8):
    def body(u_ref, dy_ref, x_ref, w_ref, dt_ref, dw_ref, dx_ref):
        for g in range(gb):
            dy = dy_ref[g]
            dt_ref[g] = _dot(u_ref[g], dy, "tn")
            dw_ref[g] = _dot(x_ref[g], dy, "tn")
            dx_ref[g] = _dot(dy, w_ref[g], "nt")

    n_g, n_c, n_k = u_g.shape
    return _group_call(body, [u_g, d_y, xprev, wout],
                       [_sds((n_g, n_k, n_k)), _sds(wout.shape), _sds(xprev.shape)], gb=gb, name="s5_outputs_bwd")


def _s5_inputs_bwd(u_g, d_y, d_s, toep, wend, *, gb=8):
    def body(u_ref, dy_ref, ds_ref, t_ref, w_ref, du_ref, dw_ref):
        for g in range(gb):
            ds = ds_ref[g]
            du_ref[g] = _dot(dy_ref[g], t_ref[g], "nt") + _dot(ds, w_ref[g], "nt")
            dw_ref[g] = _dot(u_ref[g], ds, "tn")

    return _group_call(body, [u_g, d_y, d_s, toep, wend], [_sds(u_g.shape), _sds(wend.shape)], gb=gb,
                       name="s5_inputs_bwd")


def _swap_halves(x):
    return pltpu.roll(x, x.shape[-1] // 2, axis=x.ndim - 1)


def _s5_scan(s_t, a1, a2, *, gb=32):
    n_c, n_g, n_p = s_t.shape
    gb = min(gb, n_g)

    def body(s_ref, a1_ref, a2_ref, x_ref):
        a1v, a2v = a1_ref[...], a2_ref[...]

        def step(c, x):
            x_ref[c] = x
            return x * a1v + _swap_halves(x) * a2v + s_ref[c]

        lax.fori_loop(0, n_c, step, jnp.zeros((gb, n_p), F32))

    return pl.pallas_call(
        body, name="s5_scan", grid=(n_g // gb,),
        in_specs=[pl.BlockSpec((n_c, gb, n_p), lambda i: (0, i, 0)), pl.BlockSpec((gb, n_p), lambda i: (i, 0)),
                  pl.BlockSpec((gb, n_p), lambda i: (i, 0))],
        out_specs=pl.BlockSpec((n_c, gb, n_p), lambda i: (0, i, 0)), out_shape=_sds(s_t.shape),
        compiler_params=_params(("parallel",)),
    )(s_t, a1, a2)


def _s5_scan_bwd(d_xprev_t, xprev_t, a1, a2, *, gb=32):
    n_c, n_g, n_p = xprev_t.shape
    gb = min(gb, n_g)

    def body(dx_ref, x_ref, a1_ref, a2_ref, ds_ref, p1_ref, p2_ref):
        a1v, a2v = a1_ref[...], a2_ref[...]
        zero = jnp.zeros((gb, n_p), F32)
        ds_ref[n_c - 1] = zero

        def step(k, carry):
            gx_next, p1, p2 = carry
            c = n_c - 2 - k
            xp = x_ref[c + 1]
            gx = dx_ref[c + 1] + gx_next * a1v - _swap_halves(gx_next) * a2v
            ds_ref[c] = gx
            return gx, p1 + gx_next * xp, p2 + gx_next * _swap_halves(xp)

        _, p1, p2 = lax.fori_loop(0, n_c - 1, step, (zero, zero, zero))
        p1_ref[...] = p1
        p2_ref[...] = p2

    blk = pl.BlockSpec((n_c, gb, n_p), lambda i: (0, i, 0))
    vec = pl.BlockSpec((gb, n_p), lambda i: (i, 0))
    return pl.pallas_call(
        body, name="s5_scan_bwd", grid=(n_g // gb,), in_specs=[blk, blk, vec, vec],
        out_specs=[blk, vec, vec], out_shape=[_sds(xprev_t.shape), _sds(a1.shape), _sds(a1.shape)],
        compiler_params=_params(("parallel",)),
    )(d_xprev_t, xprev_t, a1, a2)


def _to_groups(t, n_g):
    n_l = t.shape[0]
    n_h = t.shape[1] // n_g
    return t.reshape(n_l // S5_T, S5_T, n_g, n_h).transpose(2, 0, 1, 3).reshape(n_g, n_l // S5_T, S5_T * n_h)


def _from_groups(t, n_l):
    n_g = t.shape[0]
    n_h = t.shape[2] // S5_T
    return t.reshape(n_g, n_l // S5_T, S5_T, n_h).transpose(1, 2, 0, 3).reshape(n_l, n_g * n_h)


def _s5_act(ys, uz, d_skip):
    di = d_skip.shape[1]
    return jax.nn.gelu(ys + d_skip * uz[:, :di])


def _s5_gate(g1, glu_pre, uz, b_glu):
    di = b_glu.shape[1]
    return g1 * jax.nn.sigmoid(glu_pre + b_glu) * jax.nn.silu(uz[:, di:])


def _s5_fwd(h, p):
    n_l = h.shape[0]
    di = p["d_skip"].shape[1]
    n_g = di // S5_GROUP
    hn = _rms_fwd(h, p["norm_g"], name="s5_rms")
    uz = _mm(hn, p["w_in"], name="s5_in")
    toep, wend, wout, a1, a2 = p["ops"]
    u_g = _to_groups(uz[:, :di].astype(BF16), n_g)
    s = _s5_states(u_g, wend)
    xprev = _s5_scan(s.transpose(1, 0, 2), a1, a2).transpose(1, 0, 2)
    ys = _from_groups(_s5_outputs(u_g, toep, xprev, wout), n_l)
    g1 = _rowwise(_s5_act, [ys, uz], [p["d_skip"]], [_sds((n_l, di), BF16)], [], tm=512, name="s5_act")[0]
    glu_pre = _mm(g1, p["w_glu"], name="s5_glu")

    def gate(ys_t, pre_t, uz_t, d_skip, b_glu):
        return _s5_gate(_s5_act(ys_t, uz_t, d_skip), pre_t, uz_t, b_glu)

    gated = _rowwise(gate, [ys, glu_pre, uz], [p["d_skip"], p["b_glu"]], [_sds((n_l, di), BF16)], [],
                     tm=512, name="s5_gate")[0]
    h_next = _mm(gated, p["w_out"], add=h, name="s5_out")
    return h_next, (h, hn, uz, u_g, xprev, ys, g1, glu_pre, gated)


def _s5_bwd(dh_out, p, saved, ops_vjp):
    h, hn, uz, u_g, xprev, ys, g1, glu_pre, gated = saved
    n_l = h.shape[0]
    di = p["d_skip"].shape[1]
    n_g = di // S5_GROUP
    toep, wend, wout, a1, a2 = p["ops"]
    d_gated = _mm(dh_out, p["w_out"], mode="nt", out_dtype=BF16, name="s5_dgated")
    g_w_out = _mm(gated, dh_out, mode="tn", name="s5_dwout")

    def gate_bwd(ys_t, pre_t, uz_t, ct, d_skip, b_glu):
        g1_t = _s5_act(ys_t, uz_t, d_skip)
        _, vjp = jax.vjp(_s5_gate, g1_t, pre_t, uz_t, b_glu)
        d_g1, d_pre, d_uz, d_b = vjp(ct.astype(F32))
        return d_g1, d_pre, d_uz, d_b

    d_g1_direct, d_pre, d_uz_gate, g_b_glu = _rowwise(
        gate_bwd, [ys, glu_pre, uz, d_gated], [p["d_skip"], p["b_glu"]],
        [_sds((n_l, di)), _sds((n_l, di), BF16), _sds(uz.shape)], [_sds(p["b_glu"].shape)], tm=256, name="s5_gate_bwd")
    g_w_glu = _mm(g1, d_pre, mode="tn", name="s5_dwglu")
    d_g1 = _mm(d_pre, p["w_glu"], mode="nt", add=d_g1_direct, name="s5_dg1")

    def act_bwd(ys_t, uz_t, ct, d_uz_t, d_skip):
        _, vjp = jax.vjp(_s5_act, ys_t, uz_t, d_skip)
        d_ys, d_uz, d_d = vjp(ct)
        return d_ys, d_uz + d_uz_t, d_d

    d_ys, d_uz_part, g_d_skip = _rowwise(
        act_bwd, [ys, uz, d_g1, d_uz_gate], [p["d_skip"]], [_sds((n_l, di), BF16), _sds(uz.shape)],
        [_sds(p["d_skip"].shape)], tm=256, name="s5_act_bwd")
    d_y = _to_groups(d_ys, n_g)
    d_toep, d_wout, d_xprev = _s5_outputs_bwd(u_g, d_y, xprev, wout)
    d_s_t, p1, p2 = _s5_scan_bwd(d_xprev.transpose(1, 0, 2), xprev.transpose(1, 0, 2), a1, a2)
    d_s = d_s_t.transpose(1, 0, 2)
    d_u_g, d_wend = _s5_inputs_bwd(u_g, d_y, d_s, toep, wend)
    d_u = _from_groups(d_u_g, n_l)
    d_uz = (d_uz_part + jnp.pad(d_u, ((0, 0), (0, di)))).astype(BF16)
    g_w_in = _mm(hn, d_uz, mode="tn", name="s5_dwin")
    d_hn = _mm(d_uz, p["w_in"], mode="nt", name="s5_dhn")
    dh_in, g_norm = _rms_bwd(h, p["norm_g"], d_hn, dh_out, name="s5_rms_bwd")
    g_ops = ops_vjp((d_toep, d_wend, d_wout, p1, p2))
    grads = dict(norm_g=g_norm, w_in=g_w_in, a_re=g_ops[0], a_im=g_ops[1], log_step=g_ops[2], b_re=g_ops[3],
                 b_im=g_ops[4], c_re=g_ops[5], c_im=g_ops[6], d_skip=g_d_skip, w_glu=g_w_glu, b_glu=g_b_glu,
                 w_out=g_w_out)
    return dh_in, grads


MLA_Z0 = MLA_Q_RANK + MLA_KV_RANK + LANES


def _rope_tile(t, cos_t, sin_t):
    q = LANES // 4
    lane = lax.broadcasted_iota(jnp.int32, t.shape, 1)
    swapped = jnp.where(lane < q, pltpu.roll(t, LANES - q, axis=1), pltpu.roll(t, q, axis=1))
    return t * cos_t + swapped * sin_t


def _mla_mid(proj, cos_t, sin_t, q_g, kv_g):
    cqn = _rms(proj[:, :MLA_Q_RANK], q_g)
    ckvn = _rms(proj[:, MLA_Q_RANK:MLA_Q_RANK + MLA_KV_RANK], kv_g)
    kr = _rope_tile(proj[:, MLA_Q_RANK + MLA_KV_RANK:MLA_Z0], cos_t, sin_t)
    return cqn, ckvn, kr


def _mla_rope_q(qp, cos_t, sin_t):
    parts = []
    for hd in range(qp.shape[1] // MLA_HEAD_PAD):
        base = hd * MLA_HEAD_PAD
        parts.append(qp[:, base:base + LANES])
        parts.append(_rope_tile(qp[:, base + LANES:base + MLA_HEAD_PAD], cos_t, sin_t))
    return jnp.concatenate(parts, axis=1)


def _mla_gate(o, proj):
    return o * jax.nn.silu(proj[:, MLA_Z0:])


LOG2E = math.log2(math.e)
SCORE_LOG2 = MLA_SCALE * LOG2E
FLASH_SPLIT = 2


def _causal_pairs(n_blk, kv_major):
    if kv_major:
        pairs = [(i, j) for j in range(n_blk) for i in range(j, n_blk)]
    else:
        pairs = [(i, j) for i in range(n_blk) for j in range(i + 1)]
    return (jnp.asarray([p[0] for p in pairs], jnp.int32), jnp.asarray([p[1] for p in pairs], jnp.int32))


def _raw_scores(q, kcat, row0, diagonal):
    s = _dot(q, kcat, "nt")
    if diagonal:
        qpos = row0 + lax.broadcasted_iota(jnp.int32, s.shape, 0)
        kpos = lax.broadcasted_iota(jnp.int32, s.shape, 1)
        s = jnp.where(kpos <= qpos, s, NEG_INF)
    return s


def _lanes(x, width):
    return jnp.tile(x, (1, width // LANES))


def _flash_fwd(qp, kv, kr, *, blk=512):
    n_l = qp.shape[0]
    heads = qp.shape[1] // MLA_HEAD_PAD
    blk = _pick(n_l, blk)
    n_blk = n_l // blk
    half = blk // FLASH_SPLIT
    qi, kj = _causal_pairs(n_blk, kv_major=False)

    def body(qi_ref, kj_ref, q_ref, kv_ref, kr_ref, o_ref, lse_ref, m_sc, l_sc, acc_sc):
        p = pl.program_id(1)
        i, j = qi_ref[p], kj_ref[p]

        @pl.when(j == 0)
        def _():
            m_sc[...] = jnp.full_like(m_sc, NEG_INF)
            l_sc[...] = jnp.zeros_like(l_sc)
            acc_sc[...] = jnp.zeros_like(acc_sc)

        def update(diagonal):
            kcat = jnp.concatenate([kv_ref[:, :LANES], kr_ref[...]], axis=1)
            v = kv_ref[:, LANES:]
            for r in range(FLASH_SPLIT):
                rows = slice(r * half, (r + 1) * half)
                s = _raw_scores(q_ref[rows, :], kcat, r * half, diagonal)
                m_old = m_sc[rows, :]
                m_new = jnp.maximum(m_old, jnp.max(s, axis=1, keepdims=True))
                alpha = jnp.exp2((m_old - m_new) * SCORE_LOG2)
                pr = jnp.exp2((s - _lanes(m_new, blk)) * SCORE_LOG2)
                l_sc[rows, :] = alpha * l_sc[rows, :] + jnp.sum(pr, axis=1, keepdims=True)
                acc_sc[rows, :] = alpha * acc_sc[rows, :] + _dot(pr, v)
                m_sc[rows, :] = m_new

        @pl.when(j < i)
        def _():
            update(False)

        @pl.when(j == i)
        def _():
            update(True)
            o_ref[...] = acc_sc[...] / l_sc[...]
            lse_ref[...] = m_sc[...] * MLA_SCALE + jnp.log(l_sc[...])

    grid_spec = pltpu.PrefetchScalarGridSpec(
        num_scalar_prefetch=2, grid=(heads, qi.shape[0]),
        in_specs=[pl.BlockSpec((blk, MLA_HEAD_PAD), lambda h, p, qi_r, kj_r: (qi_r[p], h)),
                  pl.BlockSpec((blk, MLA_HEAD_PAD), lambda h, p, qi_r, kj_r: (kj_r[p], h)),
                  pl.BlockSpec((blk, LANES), lambda h, p, qi_r, kj_r: (kj_r[p], 0))],
        out_specs=[pl.BlockSpec((blk, MLA_V), lambda h, p, qi_r, kj_r: (qi_r[p], h)),
                   pl.BlockSpec((None, blk, LANES), lambda h, p, qi_r, kj_r: (h, qi_r[p], 0))],
        scratch_shapes=[pltpu.VMEM((blk, LANES), F32), pltpu.VMEM((blk, LANES), F32), pltpu.VMEM((blk, MLA_V), F32)])
    return pl.pallas_call(
        body, name="mla_flash_fwd", grid_spec=grid_spec,
        out_shape=[_sds((n_l, heads * MLA_V)), _sds((heads, n_l, LANES))],
        compiler_params=_params(("parallel", "arbitrary")),
    )(qi, kj, qp, kv, kr)


def _flash_bwd(qp, kv, kr, d_o, lse, delta, *, blk=512):
    n_l = qp.shape[0]
    heads = qp.shape[1] // MLA_HEAD_PAD
    blk = _pick(n_l, blk)
    n_blk = n_l // blk
    half = blk // FLASH_SPLIT
    qi, kj = _causal_pairs(n_blk, kv_major=True)
    n_pairs = qi.shape[0]

    def body(qi_ref, kj_ref, q_ref, kv_ref, kr_ref, do_ref, lse_ref, dl_ref, dq_ref, dkv_ref, dkr_ref, dk_sc, dv_sc):
        h, p = pl.program_id(0), pl.program_id(1)
        i, j = qi_ref[p], kj_ref[p]

        @pl.when(p == 0)
        def _():
            dq_ref[...] = jnp.zeros_like(dq_ref)

        @pl.when(jnp.logical_and(p == 0, h == 0))
        def _():
            dkr_ref[...] = jnp.zeros_like(dkr_ref)

        @pl.when(i == j)
        def _():
            dk_sc[...] = jnp.zeros_like(dk_sc)
            dv_sc[...] = jnp.zeros_like(dv_sc)

        def update(diagonal):
            kcat = jnp.concatenate([kv_ref[:, :LANES], kr_ref[...]], axis=1)
            v = kv_ref[:, LANES:]
            for r in range(FLASH_SPLIT):
                rows = slice(r * half, (r + 1) * half)
                q_t, do_t = q_ref[rows, :], do_ref[rows, :]
                s = _raw_scores(q_t, kcat, r * half, diagonal)
                pr = jnp.exp2(s * SCORE_LOG2 - _lanes(lse_ref[rows, :] * LOG2E, blk))
                d_p = _dot(do_t, v, "nt")
                d_s = pr * (d_p - _lanes(dl_ref[rows, :], blk))
                dk_sc[...] += _dot(d_s, q_t, "tn")
                dv_sc[...] += _dot(pr, do_t, "tn")
                q_rows = pl.ds(pl.multiple_of(i * blk + r * half, half), half)
                dq_ref[q_rows, :] += _dot(d_s, kcat)

        @pl.when(i > j)
        def _():
            update(False)

        @pl.when(i == j)
        def _():
            update(True)

        @pl.when(i == n_blk - 1)
        def _():
            dk = dk_sc[...] * MLA_SCALE
            dkv_ref[:, :LANES] = dk[:, :LANES].astype(dkv_ref.dtype)
            dkv_ref[:, LANES:] = dv_sc[...].astype(dkv_ref.dtype)
            k_rows = pl.ds(pl.multiple_of(j * blk, blk), blk)
            dkr_ref[k_rows, :] += dk[:, LANES:]

        @pl.when(p == n_pairs - 1)
        def _():
            dq_ref[...] = dq_ref[...] * MLA_SCALE

    at_q = lambda h, p, qi_r, kj_r: (qi_r[p], h)
    at_kv = lambda h, p, qi_r, kj_r: (kj_r[p], h)
    grid_spec = pltpu.PrefetchScalarGridSpec(
        num_scalar_prefetch=2, grid=(heads, n_pairs),
        in_specs=[pl.BlockSpec((blk, MLA_HEAD_PAD), at_q),
                  pl.BlockSpec((blk, MLA_HEAD_PAD), at_kv),
                  pl.BlockSpec((blk, LANES), lambda h, p, qi_r, kj_r: (kj_r[p], 0)),
                  pl.BlockSpec((blk, MLA_V), at_q),
                  pl.BlockSpec((None, blk, LANES), lambda h, p, qi_r, kj_r: (h, qi_r[p], 0)),
                  pl.BlockSpec((blk, LANES), at_q)],
        out_specs=[pl.BlockSpec((n_l, MLA_HEAD_PAD), lambda h, p, qi_r, kj_r: (0, h)),
                   pl.BlockSpec((blk, MLA_HEAD_PAD), at_kv),
                   pl.BlockSpec((n_l, LANES), lambda h, p, qi_r, kj_r: (0, 0))],
        scratch_shapes=[pltpu.VMEM((blk, MLA_HEAD_PAD), F32), pltpu.VMEM((blk, MLA_V), F32)])
    return pl.pallas_call(
        body, name="mla_flash_bwd", grid_spec=grid_spec,
        out_shape=[_sds(qp.shape), _sds(kv.shape, BF16), _sds(kr.shape)],
        compiler_params=_params(("arbitrary", "arbitrary")),
    )(qi, kj, qp, kv, kr, d_o, lse, delta)


def _mla_fwd(h, p, rope):
    n_l = h.shape[0]
    cos_t, sin_t = rope
    hn = _rms_fwd(h, p["norm_g"], name="mla_rms")
    proj = _mm(hn, p["w_in"], name="mla_in", tn=896)
    cqn, ckvn, kr = _rowwise(_mla_mid, [proj, cos_t, sin_t], [p["q_norm_g"], p["kv_norm_g"]],
                             [_sds((n_l, MLA_Q_RANK), BF16), _sds((n_l, MLA_KV_RANK), BF16), _sds((n_l, LANES), BF16)],
                             [], tm=512, name="mla_mid")
    q_raw = _mm(cqn, p["w_uq"], name="mla_uq")
    qp = _rowwise(_mla_rope_q, [q_raw, cos_t, sin_t], [], [_sds(q_raw.shape, BF16)], [], tm=512, name="mla_rope_q")[0]
    kv = _mm(ckvn, p["w_ukv"], out_dtype=BF16, name="mla_ukv")
    o, lse = _flash_fwd(qp, kv, kr)
    gated = _rowwise(_mla_gate, [o, proj], [], [_sds(o.shape, BF16)], [], tm=512, name="mla_gate")[0]
    h_next = _mm(gated, p["w_out"], add=h, name="mla_out")
    return h_next, (h, hn, proj, cqn, ckvn, kr, qp, kv, o, lse, gated)


def _mla_bwd(dh_out, p, saved, rope):
    h, hn, proj, cqn, ckvn, kr, qp, kv, o, lse, gated = saved
    n_l = h.shape[0]
    cos_t, sin_t = rope
    d_gated = _mm(dh_out, p["w_out"], mode="nt", out_dtype=BF16, name="mla_dgated")
    g_w_out = _mm(gated, dh_out, mode="tn", name="mla_dwout")

    def gate_bwd(o_t, proj_t, ct):
        _, vjp = jax.vjp(lambda a, z: a * jax.nn.silu(z), o_t, proj_t[:, MLA_Z0:])
        d_o_t, d_z_t = vjp(ct.astype(F32))
        prod = d_o_t * o_t
        delta = jnp.concatenate(
            [jnp.broadcast_to(jnp.sum(prod[:, hd * MLA_V:(hd + 1) * MLA_V], axis=1, keepdims=True),
                              (prod.shape[0], MLA_V)) for hd in range(prod.shape[1] // MLA_V)], axis=1)
        return d_o_t, d_z_t, delta

    d_o, d_z, delta = _rowwise(gate_bwd, [o, proj, d_gated], [], [_sds(o.shape, BF16), _sds(o.shape), _sds(o.shape)],
                               [], tm=512, name="mla_gate_bwd")
    d_qp, d_kv, d_kr = _flash_bwd(qp, kv, kr, d_o, lse, delta)

    def rope_q_bwd(ct, c_t, s_t):
        return _mla_rope_q(ct, c_t, -s_t)

    d_q_raw = _rowwise(rope_q_bwd, [d_qp, cos_t, sin_t], [], [_sds(d_qp.shape, BF16)], [], tm=512,
                       name="mla_rope_q_bwd")[0]
    g_w_uq = _mm(cqn, d_q_raw, mode="tn", name="mla_dwuq")
    d_cqn = _mm(d_q_raw, p["w_uq"], mode="nt", name="mla_dcqn")
    g_w_ukv = _mm(ckvn, d_kv, mode="tn", name="mla_dwukv")
    d_ckvn = _mm(d_kv, p["w_ukv"], mode="nt", name="mla_dckvn")

    def mid_bwd(proj_t, c_t, s_t, d_cq, d_ckv, d_kr_t, d_z_t, q_g, kv_g):
        _, vjp_q = jax.vjp(_rms, proj_t[:, :MLA_Q_RANK], q_g)
        _, vjp_kv = jax.vjp(_rms, proj_t[:, MLA_Q_RANK:MLA_Q_RANK + MLA_KV_RANK], kv_g)
        d_q_in, d_qg = vjp_q(d_cq)
        d_kv_in, d_kvg = vjp_kv(d_ckv)
        d_kr_in = _rope_tile(d_kr_t, c_t, -s_t)
        return jnp.concatenate([d_q_in, d_kv_in, d_kr_in, d_z_t], axis=1), d_qg, d_kvg

    d_proj, g_q_norm, g_kv_norm = _rowwise(
        mid_bwd, [proj, cos_t, sin_t, d_cqn, d_ckvn, d_kr, d_z], [p["q_norm_g"], p["kv_norm_g"]],
        [_sds(proj.shape, BF16)], [_sds(p["q_norm_g"].shape), _sds(p["kv_norm_g"].shape)], tm=512, name="mla_mid_bwd")
    g_w_in = _mm(hn, d_proj, mode="tn", name="mla_dwin", tn=896)
    d_hn = _mm(d_proj, p["w_in"], mode="nt", name="mla_dhn", tk=896)
    dh_in, g_norm = _rms_bwd(h, p["norm_g"], d_hn, dh_out, name="mla_rms_bwd")
    grads = dict(norm_g=g_norm, w_in=g_w_in, q_norm_g=g_q_norm, w_uq=g_w_uq, kv_norm_g=g_kv_norm, w_ukv=g_w_ukv,
                 w_out=g_w_out)
    return dh_in, grads


def _loss_head(h, g, target):
    def fn(x, t, gg):
        def local(xx, g2):
            err = _rms(xx, g2) - t
            return 0.5 * jnp.sum(jnp.mean(err * err, axis=-1))

        val, (dx, dg) = jax.value_and_grad(local, argnums=(0, 1))(x, gg)
        return dx, jnp.full((1, LANES), val, F32), dg

    dh, loss, dg = _rowwise(fn, [h, target], [g], [_sds(h.shape)], [_sds((1, LANES)), _sds(g.shape)], tm=512,
                            name="loss_head")
    return loss[0, 0], dh, dg


HBM_SPEC = pl.BlockSpec(memory_space=pltpu.HBM)


def _all_gather(shard, *, name):
    def body(x_ref, out_ref, send_sems, recv_sems, local_sem):
        x, y, c = lax.axis_index("x"), lax.axis_index("y"), lax.axis_index("c")
        me, sibling = (x, y, c), (x, y, 1 - c)
        chips = [(1 - x, y), (x, 1 - y), (1 - x, 1 - y)]

        def rows(px, py, pc):
            return out_ref.at[4 * px + 2 * py + pc]

        def copy(k, block, to, src=None):
            return pltpu.make_async_remote_copy(
                src_ref=rows(*block) if src is None else src, dst_ref=rows(*block),
                send_sem=send_sems.at[k], recv_sem=recv_sems.at[k], device_id=to, device_id_type=MESH)

        mine = pltpu.make_async_copy(x_ref, rows(*me), local_sem)
        mine.start()
        first = [copy(0, me, sibling, src=x_ref)]
        first += [copy(1 + j, me, (*chip, c), src=x_ref) for j, chip in enumerate(chips)]
        for cp in first:
            cp.start()
        passed = [copy(4 + j, (*chip, c), sibling) for j, chip in enumerate(chips)]
        for j, chip in enumerate(chips):
            copy(1 + j, (*chip, c), me).wait_recv()
            passed[j].start()
        copy(0, sibling, me).wait_recv()
        for j, chip in enumerate(chips):
            copy(4 + j, (*chip, 1 - c), me).wait_recv()
        for cp in first + passed:
            cp.wait_send()
        mine.wait()

    return pl.pallas_call(
        body, name=name, out_shape=jax.ShapeDtypeStruct((N_DEV,) + shard.shape, shard.dtype),
        in_specs=[HBM_SPEC], out_specs=HBM_SPEC,
        scratch_shapes=[pltpu.SemaphoreType.DMA((7,)), pltpu.SemaphoreType.DMA((7,)), pltpu.SemaphoreType.DMA],
    )(shard)


def _exchange(send, *, name):
    def body(s_ref, out_ref, send_sems, recv_sems, local_sem):
        x, y, c = lax.axis_index("x"), lax.axis_index("y"), lax.axis_index("c")
        mine = pltpu.make_async_copy(s_ref.at[4 * x + 2 * y + c], out_ref.at[0], local_sem)
        mine.start()
        copies = []
        for r in range(1, N_DEV):
            px = 1 - x if r & 4 else x
            py = 1 - y if r & 2 else y
            pc = 1 - c if r & 1 else c
            copies.append(pltpu.make_async_remote_copy(
                src_ref=s_ref.at[4 * px + 2 * py + pc], dst_ref=out_ref.at[r],
                send_sem=send_sems.at[r - 1], recv_sem=recv_sems.at[r - 1], device_id=(px, py, pc),
                device_id_type=MESH))
        for cp in copies:
            cp.start()
        for cp in copies:
            cp.wait_recv()
        for cp in copies:
            cp.wait_send()
        mine.wait()

    return pl.pallas_call(
        body, name=name, out_shape=jax.ShapeDtypeStruct(send.shape, send.dtype),
        in_specs=[HBM_SPEC], out_specs=HBM_SPEC,
        scratch_shapes=[pltpu.SemaphoreType.DMA((7,)), pltpu.SemaphoreType.DMA((7,)), pltpu.SemaphoreType.DMA],
    )(send)


def _sum_slots(recv, *, name):
    n_s, rows, width = recv.shape
    tr = _pick(rows, 256, 8)

    def body(r_ref, o_ref):
        acc = r_ref[0].astype(F32)
        for s in range(1, n_s):
            acc = acc + r_ref[s].astype(F32)
        o_ref[...] = acc

    return pl.pallas_call(
        body, name=name, grid=(rows // tr,),
        in_specs=[pl.BlockSpec((n_s, tr, width), lambda i: (0, i, 0))],
        out_specs=pl.BlockSpec((tr, width), lambda i: (i, 0)), out_shape=_sds((rows, width)),
        compiler_params=_params(("parallel",)),
    )(recv)


def _adamw(w, g, m, v, *, name):
    rows, width = w.shape
    tr = _pick(rows, 256, 8)

    def body(w_ref, g_ref, m_ref, v_ref, d_ref, nm_ref, nv_ref):
        gg = g_ref[...]
        m_new = ADAM_B1 * m_ref[...] + (1.0 - ADAM_B1) * gg
        v_new = ADAM_B2 * v_ref[...] + (1.0 - ADAM_B2) * jnp.square(gg)
        m_hat = m_new / (1.0 - ADAM_B1 ** ADAM_STEP)
        v_hat = v_new / (1.0 - ADAM_B2 ** ADAM_STEP)
        d_ref[...] = -ADAM_LR * (m_hat / (jnp.sqrt(v_hat) + ADAM_EPS) + ADAM_WD * w_ref[...])
        nm_ref[...] = m_new
        nv_ref[...] = v_new

    spec = pl.BlockSpec((tr, width), lambda i: (i, 0))
    return pl.pallas_call(
        body, name=name, grid=(rows // tr,), in_specs=[spec] * 4, out_specs=[spec] * 3,
        out_shape=[_sds(w.shape)] * 3, compiler_params=_params(("parallel",)),
    )(w, g, m, v)


KINDS = ("gmlp", "s5", "mla", "gmlp")
LAYER_NAMES = {
    "gmlp": ("norm_g", "w_in", "ln_g", "ln_b", "w_s", "b_s", "w_out"),
    "s5": ("norm_g", "w_in", "a_re", "a_im", "log_step", "b_re", "b_im", "c_re", "c_im", "d_skip", "w_glu", "b_glu",
           "w_out"),
    "mla": ("norm_g", "w_in", "q_norm_g", "w_uq", "kv_norm_g", "w_ukv", "w_out"),
}
COL_SHARDED = ("w_in", "w_uq", "w_ukv")
ROW_SHARDED = ("w_out", "w_glu")
WEIGHT_NAMES = tuple("l%d_%s" % (i, n) for i, k in enumerate(KINDS) for n in LAYER_NAMES[k]) + ("final_norm_g",)


def _is_sharded(name):
    return name.split("_", 1)[1] in COL_SHARDED + ROW_SHARDED


def _flatten(arrs, pad_rows_to):
    parts, sizes = [], []
    for a in arrs:
        flat = a.reshape(-1)
        pad = (-flat.shape[0]) % FLAT_W
        if pad:
            flat = jnp.pad(flat, (0, pad))
        parts.append(flat)
        sizes.append(flat.shape[0] // FLAT_W)
    rows = sum(sizes)
    pad_rows = (-rows) % pad_rows_to
    if pad_rows:
        parts.append(jnp.zeros((pad_rows * FLAT_W,), arrs[0].dtype))
    return jnp.concatenate(parts).reshape(-1, FLAT_W), sizes


def _unflatten(flat, shapes, sizes):
    out, row = [], 0
    for shape, n_rows in zip(shapes, sizes):
        n = int(np.prod(shape))
        out.append(flat[row:row + n_rows].reshape(-1)[:n].reshape(shape))
        row += n_rows
    return out


def _full_from_gathered(blocks, name):
    if name.split("_", 1)[1] in COL_SHARDED:
        return blocks.transpose(1, 0, 2).reshape(blocks.shape[1], -1)
    return blocks.reshape(-1, blocks.shape[2])


def _shards_of(full, name):
    if name.split("_", 1)[1] in COL_SHARDED:
        return full.reshape(full.shape[0], N_DEV, -1).transpose(1, 0, 2)
    return full.reshape(N_DEV, -1, full.shape[1])


def _rope_tables(positions):
    inv_freq = ROPE_THETA ** (-jnp.arange(0, MLA_ROPE, 2, dtype=F32) / MLA_ROPE)
    ang = positions.astype(F32)[:, None] * inv_freq
    cos, sin = jnp.cos(ang), jnp.sin(ang)
    zero = jnp.zeros((positions.shape[0], LANES - MLA_ROPE), F32)
    return jnp.concatenate([cos, cos, zero], axis=1), jnp.concatenate([-sin, sin, zero], axis=1)


def _row(v):
    return v.reshape(1, -1)


def kernel(x, positions, l0_norm_g, l0_w_in, l0_ln_g, l0_ln_b, l0_w_s, l0_b_s, l0_w_out, l1_norm_g, l1_w_in, l1_a_re, l1_a_im, l1_log_step, l1_b_re, l1_b_im, l1_c_re, l1_c_im, l1_d_skip, l1_w_glu, l1_b_glu, l1_w_out, l2_norm_g, l2_w_in, l2_q_norm_g, l2_w_uq, l2_kv_norm_g, l2_w_ukv, l2_w_out, l3_norm_g, l3_w_in, l3_ln_g, l3_ln_b, l3_w_s, l3_b_s, l3_w_out, final_norm_g, loss_target, m_l0_norm_g, m_l0_w_in, m_l0_ln_g, m_l0_ln_b, m_l0_w_s, m_l0_b_s, m_l0_w_out, m_l1_norm_g, m_l1_w_in, m_l1_a_re, m_l1_a_im, m_l1_log_step, m_l1_b_re, m_l1_b_im, m_l1_c_re, m_l1_c_im, m_l1_d_skip, m_l1_w_glu, m_l1_b_glu, m_l1_w_out, m_l2_norm_g, m_l2_w_in, m_l2_q_norm_g, m_l2_w_uq, m_l2_kv_norm_g, m_l2_w_ukv, m_l2_w_out, m_l3_norm_g, m_l3_w_in, m_l3_ln_g, m_l3_ln_b, m_l3_w_s, m_l3_b_s, m_l3_w_out, m_final_norm_g, v_l0_norm_g, v_l0_w_in, v_l0_ln_g, v_l0_ln_b, v_l0_w_s, v_l0_b_s, v_l0_w_out, v_l1_norm_g, v_l1_w_in, v_l1_a_re, v_l1_a_im, v_l1_log_step, v_l1_b_re, v_l1_b_im, v_l1_c_re, v_l1_c_im, v_l1_d_skip, v_l1_w_glu, v_l1_b_glu, v_l1_w_out, v_l2_norm_g, v_l2_w_in, v_l2_q_norm_g, v_l2_w_uq, v_l2_kv_norm_g, v_l2_w_ukv, v_l2_w_out, v_l3_norm_g, v_l3_w_in, v_l3_ln_g, v_l3_ln_b, v_l3_w_s, v_l3_b_s, v_l3_w_out, v_final_norm_g):
    args = locals()
    weights = {n: args[n] for n in WEIGHT_NAMES}
    mom_m = {n: args["m_" + n] for n in WEIGHT_NAMES}
    mom_v = {n: args["v_" + n] for n in WEIGHT_NAMES}
    return _train_step(x, positions, loss_target, weights, mom_m, mom_v)


def _train_step(x, positions, loss_target, weights, mom_m, mom_v):
    big = [n for n in WEIGHT_NAMES if _is_sharded(n)]
    small = [n for n in WEIGHT_NAMES if not _is_sharded(n)]

    w_flat, big_sizes = _flatten([weights[n] for n in big], 8)
    gathered = _all_gather(w_flat.astype(BF16), name="weights_all_gather")
    full, row = {}, 0
    for n, n_rows in zip(big, big_sizes):
        blocks = gathered[:, row:row + n_rows].reshape((N_DEV,) + weights[n].shape)
        full[n] = _full_from_gathered(blocks, n)
        row += n_rows

    layers, ops_vjps = [], {}
    for i, kind in enumerate(KINDS):
        pre = "l%d_" % i
        p = {n: (full[pre + n] if _is_sharded(pre + n) else weights[pre + n]) for n in LAYER_NAMES[kind]}
        p["norm_g"] = _row(p["norm_g"])
        if kind == "gmlp":
            p["ln_g"], p["ln_b"], p["b_st"] = _row(p["ln_g"]), _row(p["ln_b"]), p["b_s"].T
        elif kind == "s5":
            p["d_skip"], p["b_glu"] = _row(p["d_skip"]), _row(p["b_glu"])
            ops, ops_vjps[i] = jax.vjp(_s5_operators, *[p[n] for n in ("a_re", "a_im", "log_step", "b_re", "b_im",
                                                                       "c_re", "c_im")])
            p["ops"] = tuple(o.astype(BF16) for o in ops[:3]) + ops[3:]
        else:
            heads = p["w_uq"].shape[1] // MLA_QK_DIM
            w_in = p["w_in"]
            split = MLA_Q_RANK + MLA_KV_RANK + MLA_ROPE
            p["w_in"] = jnp.concatenate([w_in[:, :split], jnp.zeros((w_in.shape[0], LANES - MLA_ROPE), w_in.dtype),
                                         w_in[:, split:]], axis=1)
            p["w_uq"] = jnp.pad(p["w_uq"].reshape(-1, heads, MLA_QK_DIM),
                                ((0, 0), (0, 0), (0, MLA_HEAD_PAD - MLA_QK_DIM))).reshape(-1, heads * MLA_HEAD_PAD)
            p["q_norm_g"], p["kv_norm_g"] = _row(p["q_norm_g"]), _row(p["kv_norm_g"])
        layers.append(p)
    rope = _rope_tables(positions[0])

    h = x[0]
    saved = []
    for kind, p in zip(KINDS, layers):
        if kind == "gmlp":
            h, s = _gmlp_fwd(h, p)
        elif kind == "s5":
            h, s = _s5_fwd(h, p)
        else:
            h, s = _mla_fwd(h, p, rope)
        saved.append(s)
    loss_local, dh, g_final = _loss_head(h, _row(weights["final_norm_g"]), loss_target[0])
    loss = lax.psum(loss_local, ("x", "y", "c"))

    grads = {"final_norm_g": g_final.reshape(-1)}
    for i in reversed(range(len(KINDS))):
        kind, p = KINDS[i], layers[i]
        if kind == "gmlp":
            dh, g = _gmlp_bwd(dh, p, saved[i])
        elif kind == "s5":
            dh, g = _s5_bwd(dh, p, saved[i], ops_vjps[i])
        else:
            dh, g = _mla_bwd(dh, p, saved[i], rope)
            heads = weights["l%d_w_uq" % i].shape[1] * N_DEV // MLA_QK_DIM
            split = MLA_Q_RANK + MLA_KV_RANK + MLA_ROPE
            g["w_in"] = jnp.concatenate([g["w_in"][:, :split], g["w_in"][:, MLA_Z0:]], axis=1)
            g["w_uq"] = g["w_uq"].reshape(-1, heads, MLA_HEAD_PAD)[:, :, :MLA_QK_DIM].reshape(-1, heads * MLA_QK_DIM)
        for n, val in g.items():
            name = "l%d_%s" % (i, n)
            grads[name] = val.reshape(weights[name].shape) if not _is_sharded(name) else val

    small_flat, small_sizes = _flatten([grads[n] for n in small], 8 * N_DEV)
    small_rows = small_flat.shape[0] // N_DEV
    send_parts = [_shards_of(grads[n], n).reshape(N_DEV, -1, FLAT_W) for n in big]
    send_parts.append(small_flat.reshape(N_DEV, small_rows, FLAT_W))
    send = jnp.concatenate(send_parts, axis=1)
    big_rows = send.shape[1] - small_rows
    pad_rows = (-send.shape[1]) % 8
    if pad_rows:
        send = jnp.pad(send, ((0, 0), (0, pad_rows), (0, 0)))
    recv = _exchange(send.astype(BF16), name="grads_exchange")
    reduced = _sum_slots(recv, name="grads_sum")
    g_big_flat = reduced[:big_rows]
    g_small_all = _all_gather(reduced[big_rows:big_rows + small_rows], name="small_grads_all_gather")
    g_small_flat = g_small_all.reshape(-1, FLAT_W)

    def flat_of(tree, names, pad_to):
        return _flatten([tree[n] for n in names], pad_to)[0]

    rows_b = g_big_flat.shape[0]
    pad_b = (-rows_b) % 8
    if pad_b:
        g_big_flat = jnp.pad(g_big_flat, ((0, pad_b), (0, 0)))
    d_b, nm_b, nv_b = _adamw(flat_of(weights, big, 8), g_big_flat, flat_of(mom_m, big, 8), flat_of(mom_v, big, 8),
                             name="adamw_sharded")
    d_s, nm_s, nv_s = _adamw(flat_of(weights, small, 8 * N_DEV), g_small_flat, flat_of(mom_m, small, 8 * N_DEV),
                             flat_of(mom_v, small, 8 * N_DEV), name="adamw_replicated")

    big_shapes = [weights[n].shape for n in big]
    small_shapes = [weights[n].shape for n in small]
    outs = {}
    for prefix, fb, fs in (("grad_", g_big_flat, g_small_flat), ("delta_", d_b, d_s), ("new_m_", nm_b, nm_s),
                           ("new_v_", nv_b, nv_s)):
        for n, a in zip(big, _unflatten(fb, big_shapes, big_sizes)):
            outs[prefix + n] = a
        for n, a in zip(small, _unflatten(fs, small_shapes, small_sizes)):
            outs[prefix + n] = a
    result = [loss, dh[None]]
    for prefix in ("grad_", "delta_", "new_m_", "new_v_"):
        result += [outs[prefix + n] for n in WEIGHT_NAMES]
    return tuple(result)
```

```python
import functools
import math

import numpy as np
import jax
import jax.numpy as jnp
from jax import lax
from jax.experimental import pallas as pl
from jax.experimental.pallas import tpu as pltpu

F32 = jnp.float32
BF16 = jnp.bfloat16

NORM_EPS = 1e-6
GMLP_CHUNK = 128
S5_GROUP = 16
S5_STATE = 64
S5_T = 16
MLA_NOPE = 128
MLA_ROPE = 64
MLA_V = 128
MLA_QK_DIM = MLA_NOPE + MLA_ROPE
MLA_Q_RANK = 384
MLA_KV_RANK = 128
MLA_HEAD_PAD = 256
MLA_SCALE = MLA_QK_DIM ** -0.5
ROPE_THETA = 10000.0
NEG_INF = -1e30
ADAM_LR = 0.001
ADAM_B1 = 0.9
ADAM_B2 = 0.999
ADAM_EPS = 1e-08
ADAM_WD = 0.01
ADAM_STEP = 10

N_DEV = 8
LANES = 128
FLAT_W = 1024
VMEM_LIMIT = 56 * 1024 * 1024
MESH = pl.DeviceIdType.MESH


def _pick(dim, pref, align=LANES):
    t = (min(pref, dim) // align) * align
    while t >= align:
        if dim % t == 0:
            return t
        t -= align
    return dim


def _params(sem=None):
    return pltpu.CompilerParams(dimension_semantics=sem, vmem_limit_bytes=VMEM_LIMIT)


_DIMS = {"nn": (((1,), (0,)), ((), ())), "nt": (((1,), (1,)), ((), ())), "tn": (((0,), (0,)), ((), ()))}


def _mm(a, b, *, mode="nn", out_dtype=F32, add=None, name, tm=512, tn=1024, tk=1024):
    if mode == "nn":
        (m, k), (_, n) = a.shape, b.shape
    elif mode == "nt":
        (m, k), (n, _) = a.shape, b.shape
    else:
        (k, m), (_, n) = a.shape, b.shape
    tm, tn, tk = _pick(m, tm, 8), _pick(n, tn), _pick(k, tk)
    nk = k // tk
    dims = _DIMS[mode]

    def body(*refs):
        if add is None:
            a_ref, b_ref, o_ref, acc_ref = refs
        else:
            a_ref, b_ref, r_ref, o_ref, acc_ref = refs
        kk = pl.program_id(2)

        @pl.when(kk == 0)
        def _():
            acc_ref[...] = jnp.zeros_like(acc_ref)

        acc_ref[...] += lax.dot_general(a_ref[...].astype(BF16), b_ref[...].astype(BF16), dims,
                                        preferred_element_type=F32)

        @pl.when(kk == nk - 1)
        def _():
            res = acc_ref[...]
            if add is not None:
                res = res + r_ref[...]
            o_ref[...] = res.astype(o_ref.dtype)

    a_spec = (pl.BlockSpec((tk, tm), lambda i, j, kk: (kk, i)) if mode == "tn"
              else pl.BlockSpec((tm, tk), lambda i, j, kk: (i, kk)))
    b_spec = (pl.BlockSpec((tn, tk), lambda i, j, kk: (j, kk)) if mode == "nt"
              else pl.BlockSpec((tk, tn), lambda i, j, kk: (kk, j)))
    in_specs = [a_spec, b_spec]
    args = [a, b]
    if add is not None:
        in_specs.append(pl.BlockSpec((tm, tn), lambda i, j, kk: (i, j)))
        args.append(add)
    return pl.pallas_call(
        body, name=name, grid=(m // tm, n // tn, nk),
        in_specs=in_specs, out_specs=pl.BlockSpec((tm, tn), lambda i, j, kk: (i, j)),
        out_shape=jax.ShapeDtypeStruct((m, n), out_dtype),
        scratch_shapes=[pltpu.VMEM((tm, tn), F32)],
        compiler_params=_params(("parallel", "parallel", "arbitrary")),
    )(*args)


def _rowwise(fn, tiled, full, out_tiled, out_acc, *, tm, name):
    rows = tiled[0].shape[0]
    tm = _pick(rows, tm, 8)
    nt, nf, no = len(tiled), len(full), len(out_tiled)

    def body(*refs):
        ins = [r[...] for r in refs[:nt + nf]]
        o_refs = refs[nt + nf:nt + nf + no]
        a_refs = refs[nt + nf + no:]
        outs = fn(*ins)
        if not isinstance(outs, (tuple, list)):
            outs = (outs,)
        for r, v in zip(o_refs, outs[:no]):
            r[...] = v.astype(r.dtype)
        if a_refs:
            @pl.when(pl.program_id(0) == 0)
            def _():
                for r in a_refs:
                    r[...] = jnp.zeros_like(r)

            for r, v in zip(a_refs, outs[no:]):
                r[...] += v.astype(r.dtype)

    def whole(shape):
        nd = len(shape)
        return pl.BlockSpec(tuple(shape), lambda i: (0,) * nd)

    in_specs = ([pl.BlockSpec((tm, t.shape[1]), lambda i: (i, 0)) for t in tiled]
                + [whole(f.shape) for f in full])
    out_specs = ([pl.BlockSpec((tm, o.shape[1]), lambda i: (i, 0)) for o in out_tiled]
                 + [whole(o.shape) for o in out_acc])
    outs = pl.pallas_call(
        body, name=name, grid=(rows // tm,), in_specs=in_specs, out_specs=out_specs,
        out_shape=list(out_tiled) + list(out_acc),
        compiler_params=_params(("arbitrary",)),
    )(*tiled, *full)
    return outs


def _sds(shape, dtype=F32):
    return jax.ShapeDtypeStruct(tuple(shape), dtype)


def _rms(x, g):
    return x * lax.rsqrt(jnp.mean(x * x, axis=-1, keepdims=True) + NORM_EPS) * g


def _layernorm(x, g, b):
    mu = jnp.mean(x, axis=-1, keepdims=True)
    xc = x - mu
    var = jnp.mean(xc * xc, axis=-1, keepdims=True)
    return xc * lax.rsqrt(var + NORM_EPS) * g + b


def _dot(a, b, mode="nn"):
    return lax.dot_general(a.astype(BF16), b.astype(BF16), _DIMS[mode], preferred_element_type=F32)


@jax.custom_vjp
def _bdot(a, b):
    return _dot(a, b)


def _bdot_fwd(a, b):
    return _bdot(a, b), (a, b)


def _bdot_bwd(res, ct):
    a, b = res
    return _dot(ct, b, "nt"), _dot(a, ct, "tn")


_bdot.defvjp(_bdot_fwd, _bdot_bwd)


def _rms_fwd(h, g, *, name):
    return _rowwise(lambda x, gg: _rms(x, gg), [h], [g], [_sds(h.shape, BF16)], [], tm=512, name=name)[0]


def _rms_bwd(h, g, d_hn, dh_out, *, name):
    def fn(x, ct, res, gg):
        _, vjp = jax.vjp(_rms, x, gg)
        dx, dg = vjp(ct)
        return res + dx, dg

    dh, dg = _rowwise(fn, [h, d_hn, dh_out], [g], [_sds(h.shape)], [_sds(g.shape)], tm=512, name=name)
    return dh, dg


def _gmlp_mid(uvz, ln_g, ln_b, w_s, b_st):
    di = ln_g.shape[1]
    ng, ck = w_s.shape[0], w_s.shape[1]
    dg = di // ng
    u = jax.nn.gelu(uvz[:, :di])
    v = _layernorm(jax.nn.gelu(uvz[:, di:2 * di]), ln_g, ln_b)
    z = uvz[:, 2 * di:]
    row = lax.broadcasted_iota(jnp.int32, (ck, ck), 0)
    col = lax.broadcasted_iota(jnp.int32, (ck, ck), 1)
    causal = col <= row
    blocks = []
    for c in range(uvz.shape[0] // ck):
        cols = []
        for g in range(ng):
            w = jnp.where(causal, w_s[g], 0.0)
            cols.append(_bdot(w, v[c * ck:(c + 1) * ck, g * dg:(g + 1) * dg]) + b_st[:, g:g + 1])
        blocks.append(jnp.concatenate(cols, axis=1))
    s = blocks[0] if len(blocks) == 1 else jnp.concatenate(blocks, axis=0)
    return u * s * jax.nn.silu(z)


def _gmlp_fwd(h, p):
    hn = _rms_fwd(h, p["norm_g"], name="gmlp_rms")
    uvz = _mm(hn, p["w_in"], name="gmlp_in")
    di = p["ln_g"].shape[1]
    gated = _rowwise(_gmlp_mid, [uvz], [p["ln_g"], p["ln_b"], p["w_s"], p["b_st"]],
                     [_sds((h.shape[0], di), BF16)], [], tm=256, name="gmlp_mid")[0]
    h_next = _mm(gated, p["w_out"], add=h, name="gmlp_out")
    return h_next, (h, hn, uvz, gated)


def _gmlp_bwd(dh_out, p, saved):
    h, hn, uvz, gated = saved
    d_gated = _mm(dh_out, p["w_out"], mode="nt", out_dtype=BF16, name="gmlp_dgated")
    g_w_out = _mm(gated, dh_out, mode="tn", name="gmlp_dwout")

    def fn(t, ct, ln_g, ln_b, w_s, b_st):
        _, vjp = jax.vjp(_gmlp_mid, t, ln_g, ln_b, w_s, b_st)
        return vjp(ct.astype(F32))

    d_uvz, g_ln_g, g_ln_b, g_w_s, g_b_st = _rowwise(
        fn, [uvz, d_gated], [p["ln_g"], p["ln_b"], p["w_s"], p["b_st"]],
        [_sds(uvz.shape, BF16)], [_sds(p["ln_g"].shape), _sds(p["ln_b"].shape), _sds(p["w_s"].shape),
                                  _sds(p["b_st"].shape)], tm=128, name="gmlp_mid_bwd")
    g_w_in = _mm(hn, d_uvz, mode="tn", name="gmlp_dwin")
    d_hn = _mm(d_uvz, p["w_in"], mode="nt", name="gmlp_dhn")
    dh_in, g_norm = _rms_bwd(h, p["norm_g"], d_hn, dh_out, name="gmlp_rms_bwd")
    grads = dict(norm_g=g_norm, w_in=g_w_in, ln_g=g_ln_g, ln_b=g_ln_b, w_s=g_w_s, b_s=g_b_st.T, w_out=g_w_out)
    return dh_in, grads


def _s5_operators(a_re, a_im, log_step, b_re, b_im, c_re, c_im):
    t_len = S5_T
    step = jnp.exp(log_step)[:, None]
    lr, li = a_re * step, a_im * step
    ks = jnp.arange(t_len + 1, dtype=F32)[:, None, None]
    mag = jnp.exp(ks * lr)
    pw_r, pw_i = mag * jnp.cos(ks * li), mag * jnp.sin(ks * li)
    nr, ni = pw_r[1] - 1.0, pw_i[1]
    den = a_re * a_re + a_im * a_im
    f_r, f_i = (nr * a_re + ni * a_im) / den, (ni * a_re - nr * a_im) / den
    bb_r = f_r[..., None] * b_re - f_i[..., None] * b_im
    bb_i = f_r[..., None] * b_im + f_i[..., None] * b_re
    hi = lax.Precision.HIGHEST
    cp_r = c_re[None] * pw_r[:, :, None, :] - c_im[None] * pw_i[:, :, None, :]
    cp_i = c_re[None] * pw_i[:, :, None, :] + c_im[None] * pw_r[:, :, None, :]
    kern = (jnp.einsum("gpi,kghp->kgih", bb_r, cp_r[:t_len], precision=hi)
            - jnp.einsum("gpi,kghp->kgih", bb_i, cp_i[:t_len], precision=hi))
    lag = jnp.arange(t_len)[None, :] - jnp.arange(t_len)[:, None]
    toep = jnp.where((lag >= 0)[:, :, None, None, None], kern[jnp.clip(lag, 0)], 0.0)
    n_g, n_h = a_re.shape[0], b_re.shape[2]
    toep = toep.transpose(2, 0, 3, 1, 4).reshape(n_g, t_len * n_h, t_len * n_h)
    rev_r, rev_i = pw_r[t_len - 1::-1][:t_len], pw_i[t_len - 1::-1][:t_len]
    we_r = rev_r[..., None] * bb_r[None] - rev_i[..., None] * bb_i[None]
    we_i = rev_r[..., None] * bb_i[None] + rev_i[..., None] * bb_r[None]
    wend = jnp.concatenate([we_r, we_i], axis=2).transpose(1, 0, 3, 2).reshape(n_g, t_len * n_h, -1)
    wo = jnp.concatenate([cp_r[1:], -cp_i[1:]], axis=3)
    wout = wo.transpose(1, 3, 0, 2).reshape(n_g, -1, t_len * n_h)
    a_r, a_i = pw_r[t_len], pw_i[t_len]
    a1 = jnp.concatenate([a_r, a_r], axis=1)
    a2 = jnp.concatenate([-a_i, a_i], axis=1)
    return toep, wend, wout, a1, a2


def _group_call(body, ins, outs, *, gb, name):
    n_g = ins[0].shape[0]

    def spec(a):
        return pl.BlockSpec((gb,) + tuple(a.shape[1:]), lambda i: (i, 0, 0))

    return pl.pallas_call(
        body, name=name, grid=(n_g // gb,), in_specs=[spec(a) for a in ins],
        out_specs=[spec(o) for o in outs], out_shape=list(outs),
        compiler_params=_params(("parallel",)),
    )(*ins)


def _s5_states(u_g, wend, *, gb=8):
    def body(u_ref, w_ref, s_ref):
        for g in range(gb):
            s_ref[g] = _dot(u_ref[g], w_ref[g])

    n_g, n_c = u_g.shape[0], u_g.shape[1]
    return _group_call(body, [u_g, wend], [_sds((n_g, n_c, wend.shape[2]))], gb=gb, name="s5_states")[0]


def _s5_outputs(u_g, toep, xprev, wout, *, gb=8):
    def body(u_ref, t_ref, x_ref, w_ref, y_ref):
        for g in range(gb):
            y_ref[g] = _dot(u_ref[g], t_ref[g]) + _dot(x_ref[g], w_ref[g])

    return _group_call(body, [u_g, toep, xprev, wout], [_sds(u_g.shape)], gb=gb, name="s5_outputs")[0]


def _s5_outputs_bwd(u_g, d_y, xprev, wout, *, gb=8):
    def body(u_ref, dy_ref, x_ref, w_ref, dt_ref, dw_ref, dx_ref):
        for g in range(gb):
            dy = dy_ref[g]
            dt_ref[g] = _dot(u_ref[g], dy, "tn")
            dw_ref[g] = _dot(x_ref[g], dy, "tn")
            dx_ref[g] = _dot(dy, w_ref[g], "nt")

    n_g, n_c, n_k = u_g.shape
    return _group_call(body, [u_g, d_y, xprev, wout],
                       [_sds((n_g, n_k, n_k)), _sds(wout.shape), _sds(xprev.shape)], gb=gb, name="s5_outputs_bwd")


def _s5_inputs_bwd(u_g, d_y, d_s, toep, wend, *, gb=8):
    def body(u_ref, dy_ref, ds_ref, t_ref, w_ref, du_ref, dw_ref):
        for g in range(gb):
            ds = ds_ref[g]
            du_ref[g] = _dot(dy_ref[g], t_ref[g], "nt") + _dot(ds, w_ref[g], "nt")
            dw_ref[g] = _dot(u_ref[g], ds, "tn")

    return _group_call(body, [u_g, d_y, d_s, toep, wend], [_sds(u_g.shape), _sds(wend.shape)], gb=gb,
                       name="s5_inputs_bwd")


def _swap_halves(x):
    return pltpu.roll(x, x.shape[-1] // 2, axis=x.ndim - 1)


def _s5_scan(s_t, a1, a2, *, gb=32):
    n_c, n_g, n_p = s_t.shape
    gb = min(gb, n_g)

    def body(s_ref, a1_ref, a2_ref, x_ref):
        a1v, a2v = a1_ref[...], a2_ref[...]

        def step(c, x):
            x_ref[c] = x
            return x * a1v + _swap_halves(x) * a2v + s_ref[c]

        lax.fori_loop(0, n_c, step, jnp.zeros((gb, n_p), F32))

    return pl.pallas_call(
        body, name="s5_scan", grid=(n_g // gb,),
        in_specs=[pl.BlockSpec((n_c, gb, n_p), lambda i: (0, i, 0)), pl.BlockSpec((gb, n_p), lambda i: (i, 0)),
                  pl.BlockSpec((gb, n_p), lambda i: (i, 0))],
        out_specs=pl.BlockSpec((n_c, gb, n_p), lambda i: (0, i, 0)), out_shape=_sds(s_t.shape),
        compiler_params=_params(("parallel",)),
    )(s_t, a1, a2)


def _s5_scan_bwd(d_xprev_t, xprev_t, a1, a2, *, gb=32):
    n_c, n_g, n_p = xprev_t.shape
    gb = min(gb, n_g)

    def body(dx_ref, x_ref, a1_ref, a2_ref, ds_ref, p1_ref, p2_ref):
        a1v, a2v = a1_ref[...], a2_ref[...]
        zero = jnp.zeros((gb, n_p), F32)
        ds_ref[n_c - 1] = zero

        def step(k, carry):
            gx_next, p1, p2 = carry
            c = n_c - 2 - k
            xp = x_ref[c + 1]
            gx = dx_ref[c + 1] + gx_next * a1v - _swap_halves(gx_next) * a2v
            ds_ref[c] = gx
            return gx, p1 + gx_next * xp, p2 + gx_next * _swap_halves(xp)

        _, p1, p2 = lax.fori_loop(0, n_c - 1, step, (zero, zero, zero))
        p1_ref[...] = p1
        p2_ref[...] = p2

    blk = pl.BlockSpec((n_c, gb, n_p), lambda i: (0, i, 0))
    vec = pl.BlockSpec((gb, n_p), lambda i: (i, 0))
    return pl.pallas_call(
        body, name="s5_scan_bwd", grid=(n_g // gb,), in_specs=[blk, blk, vec, vec],
        out_specs=[blk, vec, vec], out_shape=[_sds(xprev_t.shape), _sds(a1.shape), _sds(a1.shape)],
        compiler_params=_params(("parallel",)),
    )(d_xprev_t, xprev_t, a1, a2)


GROUPS_PER_TILE = LANES // S5_GROUP


def _to_groups(t, n_g):
    n_l = t.shape[0]
    n_c = n_l // S5_T
    gpt = min(GROUPS_PER_TILE, n_g)
    width = gpt * S5_GROUP

    def body(x_ref, o_ref):
        tr = [x_ref[pl.ds(s, n_c, stride=S5_T), :].T for s in range(S5_T)]
        for gl in range(gpt):
            rows = slice(gl * S5_GROUP, (gl + 1) * S5_GROUP)
            stacked = jnp.concatenate([tr[s][rows, :] for s in range(S5_T)], axis=0)
            o_ref[gl] = stacked.T.astype(o_ref.dtype)

    return pl.pallas_call(
        body, name="s5_to_groups", grid=(n_g // gpt,),
        in_specs=[pl.BlockSpec((n_l, width), lambda b: (0, b))],
        out_specs=pl.BlockSpec((gpt, n_c, S5_T * S5_GROUP), lambda b: (b, 0, 0)),
        out_shape=_sds((n_g, n_c, S5_T * S5_GROUP), BF16), compiler_params=_params(("parallel",)),
    )(t)


def _from_groups(t, n_l):
    n_g, n_c = t.shape[0], t.shape[1]
    gpt = min(GROUPS_PER_TILE, n_g)
    width = gpt * S5_GROUP

    def body(y_ref, o_ref):
        ytr = [y_ref[gl].T for gl in range(gpt)]
        for s in range(S5_T):
            rows = slice(s * S5_GROUP, (s + 1) * S5_GROUP)
            piece = jnp.concatenate([ytr[gl][rows, :] for gl in range(gpt)], axis=0)
            o_ref[pl.ds(s, n_c, stride=S5_T), :] = piece.T

    return pl.pallas_call(
        body, name="s5_from_groups", grid=(n_g // gpt,),
        in_specs=[pl.BlockSpec((gpt, n_c, S5_T * S5_GROUP), lambda b: (b, 0, 0))],
        out_specs=pl.BlockSpec((n_l, width), lambda b: (0, b)),
        out_shape=_sds((n_l, n_g * S5_GROUP)), compiler_params=_params(("parallel",)),
    )(t)


def _s5_act(ys, uz, d_skip):
    di = d_skip.shape[1]
    return jax.nn.gelu(ys + d_skip * uz[:, :di])


def _s5_gate(g1, glu_pre, uz, b_glu):
    di = b_glu.shape[1]
    return g1 * jax.nn.sigmoid(glu_pre + b_glu) * jax.nn.silu(uz[:, di:])


def _s5_fwd(h, p):
    n_l = h.shape[0]
    di = p["d_skip"].shape[1]
    n_g = di // S5_GROUP
    hn = _rms_fwd(h, p["norm_g"], name="s5_rms")
    uz = _mm(hn, p["w_in"], name="s5_in")
    toep, wend, wout, a1, a2 = p["ops"]
    u_g = _to_groups(uz, n_g)
    s = _s5_states(u_g, wend)
    xprev = _s5_scan(s.transpose(1, 0, 2), a1, a2).transpose(1, 0, 2)
    ys = _from_groups(_s5_outputs(u_g, toep, xprev, wout), n_l)
    g1 = _rowwise(_s5_act, [ys, uz], [p["d_skip"]], [_sds((n_l, di), BF16)], [], tm=512, name="s5_act")[0]
    glu_pre = _mm(g1, p["w_glu"], name="s5_glu")

    def gate(ys_t, pre_t, uz_t, d_skip, b_glu):
        return _s5_gate(_s5_act(ys_t, uz_t, d_skip), pre_t, uz_t, b_glu)

    gated = _rowwise(gate, [ys, glu_pre, uz], [p["d_skip"], p["b_glu"]], [_sds((n_l, di), BF16)], [],
                     tm=512, name="s5_gate")[0]
    h_next = _mm(gated, p["w_out"], add=h, name="s5_out")
    return h_next, (h, hn, uz, u_g, xprev, ys, g1, glu_pre, gated)


def _s5_bwd(dh_out, p, saved, ops_vjp):
    h, hn, uz, u_g, xprev, ys, g1, glu_pre, gated = saved
    n_l = h.shape[0]
    di = p["d_skip"].shape[1]
    n_g = di // S5_GROUP
    toep, wend, wout, a1, a2 = p["ops"]
    d_gated = _mm(dh_out, p["w_out"], mode="nt", out_dtype=BF16, name="s5_dgated")
    g_w_out = _mm(gated, dh_out, mode="tn", name="s5_dwout")

    def gate_bwd(ys_t, pre_t, uz_t, ct, d_skip, b_glu):
        g1_t = _s5_act(ys_t, uz_t, d_skip)
        _, vjp = jax.vjp(_s5_gate, g1_t, pre_t, uz_t, b_glu)
        d_g1, d_pre, d_uz, d_b = vjp(ct.astype(F32))
        return d_g1, d_pre, d_uz, d_b

    d_g1_direct, d_pre, d_uz_gate, g_b_glu = _rowwise(
        gate_bwd, [ys, glu_pre, uz, d_gated], [p["d_skip"], p["b_glu"]],
        [_sds((n_l, di)), _sds((n_l, di), BF16), _sds(uz.shape)], [_sds(p["b_glu"].shape)], tm=256, name="s5_gate_bwd")
    g_w_glu = _mm(g1, d_pre, mode="tn", name="s5_dwglu")
    d_g1 = _mm(d_pre, p["w_glu"], mode="nt", add=d_g1_direct, name="s5_dg1")

    def act_bwd(ys_t, uz_t, ct, d_uz_t, d_skip):
        _, vjp = jax.vjp(_s5_act, ys_t, uz_t, d_skip)
        d_ys, d_uz, d_d = vjp(ct)
        return d_ys, d_uz + d_uz_t, d_d

    d_ys, d_uz_part, g_d_skip = _rowwise(
        act_bwd, [ys, uz, d_g1, d_uz_gate], [p["d_skip"]], [_sds((n_l, di)), _sds(uz.shape)],
        [_sds(p["d_skip"].shape)], tm=256, name="s5_act_bwd")
    d_y = _to_groups(d_ys, n_g)
    d_toep, d_wout, d_xprev = _s5_outputs_bwd(u_g, d_y, xprev, wout)
    d_s_t, p1, p2 = _s5_scan_bwd(d_xprev.transpose(1, 0, 2), xprev.transpose(1, 0, 2), a1, a2)
    d_s = d_s_t.transpose(1, 0, 2)
    d_u_g, d_wend = _s5_inputs_bwd(u_g, d_y, d_s, toep, wend)
    d_u = _from_groups(d_u_g, n_l)
    d_uz = _rowwise(lambda part, du: jnp.concatenate([part[:, :di] + du, part[:, di:]], axis=1),
                    [d_uz_part, d_u], [], [_sds(uz.shape, BF16)], [], tm=512, name="s5_duz")[0]
    g_w_in = _mm(hn, d_uz, mode="tn", name="s5_dwin")
    d_hn = _mm(d_uz, p["w_in"], mode="nt", name="s5_dhn")
    dh_in, g_norm = _rms_bwd(h, p["norm_g"], d_hn, dh_out, name="s5_rms_bwd")
    g_ops = ops_vjp((d_toep, d_wend, d_wout, p1, p2))
    grads = dict(norm_g=g_norm, w_in=g_w_in, a_re=g_ops[0], a_im=g_ops[1], log_step=g_ops[2], b_re=g_ops[3],
                 b_im=g_ops[4], c_re=g_ops[5], c_im=g_ops[6], d_skip=g_d_skip, w_glu=g_w_glu, b_glu=g_b_glu,
                 w_out=g_w_out)
    return dh_in, grads


MLA_Z0 = MLA_Q_RANK + MLA_KV_RANK + LANES


def _rope_tile(t, cos_t, sin_t):
    q = LANES // 4
    lane = lax.broadcasted_iota(jnp.int32, t.shape, 1)
    swapped = jnp.where(lane < q, pltpu.roll(t, LANES - q, axis=1), pltpu.roll(t, q, axis=1))
    return t * cos_t + swapped * sin_t


def _mla_mid(proj, cos_t, sin_t, q_g, kv_g):
    cqn = _rms(proj[:, :MLA_Q_RANK], q_g)
    ckvn = _rms(proj[:, MLA_Q_RANK:MLA_Q_RANK + MLA_KV_RANK], kv_g)
    kr = _rope_tile(proj[:, MLA_Q_RANK + MLA_KV_RANK:MLA_Z0], cos_t, sin_t)
    return cqn, ckvn, kr


def _mla_rope_q(qp, cos_t, sin_t):
    parts = []
    for hd in range(qp.shape[1] // MLA_HEAD_PAD):
        base = hd * MLA_HEAD_PAD
        parts.append(qp[:, base:base + LANES])
        parts.append(_rope_tile(qp[:, base + LANES:base + MLA_HEAD_PAD], cos_t, sin_t))
    return jnp.concatenate(parts, axis=1)


def _mla_gate(o, proj):
    return o * jax.nn.silu(proj[:, MLA_Z0:])


LOG2E = math.log2(math.e)
SCORE_LOG2 = MLA_SCALE * LOG2E
FLASH_SPLIT = 2


def _causal_pairs(n_blk, kv_major):
    if kv_major:
        pairs = [(i, j) for j in range(n_blk) for i in range(j, n_blk)]
    else:
        pairs = [(i, j) for i in range(n_blk) for j in range(i + 1)]
    return (jnp.asarray([p[0] for p in pairs], jnp.int32), jnp.asarray([p[1] for p in pairs], jnp.int32))


def _raw_scores(q, kcat, row0, diagonal):
    s = _dot(q, kcat, "nt")
    if diagonal:
        qpos = row0 + lax.broadcasted_iota(jnp.int32, s.shape, 0)
        kpos = lax.broadcasted_iota(jnp.int32, s.shape, 1)
        s = jnp.where(kpos <= qpos, s, NEG_INF)
    return s


def _lanes(x, width):
    return jnp.tile(x, (1, width // LANES))


def _flash_fwd(qp, kv, kr, *, blk=512):
    n_l = qp.shape[0]
    heads = qp.shape[1] // MLA_HEAD_PAD
    blk = _pick(n_l, blk)
    n_blk = n_l // blk
    half = blk // FLASH_SPLIT
    qi, kj = _causal_pairs(n_blk, kv_major=False)

    def body(qi_ref, kj_ref, q_ref, kv_ref, kr_ref, o_ref, lse_ref, m_sc, l_sc, acc_sc):
        p = pl.program_id(1)
        i, j = qi_ref[p], kj_ref[p]

        @pl.when(j == 0)
        def _():
            m_sc[...] = jnp.full_like(m_sc, NEG_INF)
            l_sc[...] = jnp.zeros_like(l_sc)
            acc_sc[...] = jnp.zeros_like(acc_sc)

        def update(diagonal):
            kcat = jnp.concatenate([kv_ref[:, :LANES], kr_ref[...]], axis=1)
            v = kv_ref[:, LANES:]
            for r in range(FLASH_SPLIT):
                rows = slice(r * half, (r + 1) * half)
                s = _raw_scores(q_ref[rows, :], kcat, r * half, diagonal)
                m_old = m_sc[rows, :]
                m_new = jnp.maximum(m_old, jnp.max(s, axis=1, keepdims=True))
                alpha = jnp.exp2((m_old - m_new) * SCORE_LOG2)
                pr = jnp.exp2((s - _lanes(m_new, blk)) * SCORE_LOG2)
                l_sc[rows, :] = alpha * l_sc[rows, :] + jnp.sum(pr, axis=1, keepdims=True)
                acc_sc[rows, :] = alpha * acc_sc[rows, :] + _dot(pr, v)
                m_sc[rows, :] = m_new

        @pl.when(j < i)
        def _():
            update(False)

        @pl.when(j == i)
        def _():
            update(True)
            o_ref[...] = acc_sc[...] / l_sc[...]
            lse_ref[...] = m_sc[...] * MLA_SCALE + jnp.log(l_sc[...])

    grid_spec = pltpu.PrefetchScalarGridSpec(
        num_scalar_prefetch=2, grid=(heads, qi.shape[0]),
        in_specs=[pl.BlockSpec((blk, MLA_HEAD_PAD), lambda h, p, qi_r, kj_r: (qi_r[p], h)),
                  pl.BlockSpec((blk, MLA_HEAD_PAD), lambda h, p, qi_r, kj_r: (kj_r[p], h)),
                  pl.BlockSpec((blk, LANES), lambda h, p, qi_r, kj_r: (kj_r[p], 0))],
        out_specs=[pl.BlockSpec((blk, MLA_V), lambda h, p, qi_r, kj_r: (qi_r[p], h)),
                   pl.BlockSpec((None, blk, LANES), lambda h, p, qi_r, kj_r: (h, qi_r[p], 0))],
        scratch_shapes=[pltpu.VMEM((blk, LANES), F32), pltpu.VMEM((blk, LANES), F32), pltpu.VMEM((blk, MLA_V), F32)])
    return pl.pallas_call(
        body, name="mla_flash_fwd", grid_spec=grid_spec,
        out_shape=[_sds((n_l, heads * MLA_V)), _sds((heads, n_l, LANES))],
        compiler_params=_params(("parallel", "arbitrary")),
    )(qi, kj, qp, kv, kr)


def _flash_bwd(qp, kv, kr, d_o, lse, delta, *, blk=512):
    n_l = qp.shape[0]
    heads = qp.shape[1] // MLA_HEAD_PAD
    blk = _pick(n_l, blk)
    n_blk = n_l // blk
    half = blk // FLASH_SPLIT
    qi, kj = _causal_pairs(n_blk, kv_major=True)
    n_pairs = qi.shape[0]

    def body(qi_ref, kj_ref, q_ref, kv_ref, kr_ref, do_ref, lse_ref, dl_ref, dq_ref, dkv_ref, dkr_ref, dk_sc, dv_sc):
        h, p = pl.program_id(0), pl.program_id(1)
        i, j = qi_ref[p], kj_ref[p]

        @pl.when(p == 0)
        def _():
            dq_ref[...] = jnp.zeros_like(dq_ref)

        @pl.when(jnp.logical_and(p == 0, h == 0))
        def _():
            dkr_ref[...] = jnp.zeros_like(dkr_ref)

        @pl.when(i == j)
        def _():
            dk_sc[...] = jnp.zeros_like(dk_sc)
            dv_sc[...] = jnp.zeros_like(dv_sc)

        def update(diagonal):
            kcat = jnp.concatenate([kv_ref[:, :LANES], kr_ref[...]], axis=1)
            v = kv_ref[:, LANES:]
            for r in range(FLASH_SPLIT):
                rows = slice(r * half, (r + 1) * half)
                q_t, do_t = q_ref[rows, :], do_ref[rows, :]
                s = _raw_scores(q_t, kcat, r * half, diagonal)
                pr = jnp.exp2(s * SCORE_LOG2 - _lanes(lse_ref[rows, :] * LOG2E, blk))
                d_p = _dot(do_t, v, "nt")
                d_s = pr * (d_p - _lanes(dl_ref[rows, :], blk))
                dk_sc[...] += _dot(d_s, q_t, "tn")
                dv_sc[...] += _dot(pr, do_t, "tn")
                q_rows = pl.ds(pl.multiple_of(i * blk + r * half, half), half)
                dq_ref[q_rows, :] += _dot(d_s, kcat)

        @pl.when(i > j)
        def _():
            update(False)

        @pl.when(i == j)
        def _():
            update(True)

        @pl.when(i == n_blk - 1)
        def _():
            dk = dk_sc[...] * MLA_SCALE
            dkv_ref[:, :LANES] = dk[:, :LANES].astype(dkv_ref.dtype)
            dkv_ref[:, LANES:] = dv_sc[...].astype(dkv_ref.dtype)
            k_rows = pl.ds(pl.multiple_of(j * blk, blk), blk)
            dkr_ref[k_rows, :] += dk[:, LANES:]

        @pl.when(p == n_pairs - 1)
        def _():
            dq_ref[...] = dq_ref[...] * MLA_SCALE

    at_q = lambda h, p, qi_r, kj_r: (qi_r[p], h)
    at_kv = lambda h, p, qi_r, kj_r: (kj_r[p], h)
    grid_spec = pltpu.PrefetchScalarGridSpec(
        num_scalar_prefetch=2, grid=(heads, n_pairs),
        in_specs=[pl.BlockSpec((blk, MLA_HEAD_PAD), at_q),
                  pl.BlockSpec((blk, MLA_HEAD_PAD), at_kv),
                  pl.BlockSpec((blk, LANES), lambda h, p, qi_r, kj_r: (kj_r[p], 0)),
                  pl.BlockSpec((blk, MLA_V), at_q),
                  pl.BlockSpec((None, blk, LANES), lambda h, p, qi_r, kj_r: (h, qi_r[p], 0)),
                  pl.BlockSpec((blk, LANES), at_q)],
        out_specs=[pl.BlockSpec((n_l, MLA_HEAD_PAD), lambda h, p, qi_r, kj_r: (0, h)),
                   pl.BlockSpec((blk, MLA_HEAD_PAD), at_kv),
                   pl.BlockSpec((n_l, LANES), lambda h, p, qi_r, kj_r: (0, 0))],
        scratch_shapes=[pltpu.VMEM((blk, MLA_HEAD_PAD), F32), pltpu.VMEM((blk, MLA_V), F32)])
    return pl.pallas_call(
        body, name="mla_flash_bwd", grid_spec=grid_spec,
        out_shape=[_sds(qp.shape), _sds(kv.shape, BF16), _sds(kr.shape)],
        compiler_params=_params(("arbitrary", "arbitrary")),
    )(qi, kj, qp, kv, kr, d_o, lse, delta)


def _mla_fwd(h, p, rope):
    n_l = h.shape[0]
    cos_t, sin_t = rope
    hn = _rms_fwd(h, p["norm_g"], name="mla_rms")
    proj = _mm(hn, p["w_in"], name="mla_in", tn=896)
    cqn, ckvn, kr = _rowwise(_mla_mid, [proj, cos_t, sin_t], [p["q_norm_g"], p["kv_norm_g"]],
                             [_sds((n_l, MLA_Q_RANK), BF16), _sds((n_l, MLA_KV_RANK), BF16), _sds((n_l, LANES), BF16)],
                             [], tm=512, name="mla_mid")
    q_raw = _mm(cqn, p["w_uq"], name="mla_uq")
    qp = _rowwise(_mla_rope_q, [q_raw, cos_t, sin_t], [], [_sds(q_raw.shape, BF16)], [], tm=512, name="mla_rope_q")[0]
    kv = _mm(ckvn, p["w_ukv"], out_dtype=BF16, name="mla_ukv")
    o, lse = _flash_fwd(qp, kv, kr)
    gated = _rowwise(_mla_gate, [o, proj], [], [_sds(o.shape, BF16)], [], tm=512, name="mla_gate")[0]
    h_next = _mm(gated, p["w_out"], add=h, name="mla_out")
    return h_next, (h, hn, proj, cqn, ckvn, kr, qp, kv, o, lse, gated)


def _mla_bwd(dh_out, p, saved, rope):
    h, hn, proj, cqn, ckvn, kr, qp, kv, o, lse, gated = saved
    n_l = h.shape[0]
    cos_t, sin_t = rope
    d_gated = _mm(dh_out, p["w_out"], mode="nt", out_dtype=BF16, name="mla_dgated")
    g_w_out = _mm(gated, dh_out, mode="tn", name="mla_dwout")

    def gate_bwd(o_t, proj_t, ct):
        _, vjp = jax.vjp(lambda a, z: a * jax.nn.silu(z), o_t, proj_t[:, MLA_Z0:])
        d_o_t, d_z_t = vjp(ct.astype(F32))
        prod = d_o_t * o_t
        delta = jnp.concatenate(
            [jnp.broadcast_to(jnp.sum(prod[:, hd * MLA_V:(hd + 1) * MLA_V], axis=1, keepdims=True),
                              (prod.shape[0], MLA_V)) for hd in range(prod.shape[1] // MLA_V)], axis=1)
        return d_o_t, d_z_t, delta

    d_o, d_z, delta = _rowwise(gate_bwd, [o, proj, d_gated], [], [_sds(o.shape, BF16), _sds(o.shape), _sds(o.shape)],
                               [], tm=512, name="mla_gate_bwd")
    d_qp, d_kv, d_kr = _flash_bwd(qp, kv, kr, d_o, lse, delta)

    def rope_q_bwd(ct, c_t, s_t):
        return _mla_rope_q(ct, c_t, -s_t)

    d_q_raw = _rowwise(rope_q_bwd, [d_qp, cos_t, sin_t], [], [_sds(d_qp.shape, BF16)], [], tm=512,
                       name="mla_rope_q_bwd")[0]
    g_w_uq = _mm(cqn, d_q_raw, mode="tn", name="mla_dwuq")
    d_cqn = _mm(d_q_raw, p["w_uq"], mode="nt", name="mla_dcqn")
    g_w_ukv = _mm(ckvn, d_kv, mode="tn", name="mla_dwukv")
    d_ckvn = _mm(d_kv, p["w_ukv"], mode="nt", name="mla_dckvn")

    def mid_bwd(proj_t, c_t, s_t, d_cq, d_ckv, d_kr_t, d_z_t, q_g, kv_g):
        _, vjp_q = jax.vjp(_rms, proj_t[:, :MLA_Q_RANK], q_g)
        _, vjp_kv = jax.vjp(_rms, proj_t[:, MLA_Q_RANK:MLA_Q_RANK + MLA_KV_RANK], kv_g)
        d_q_in, d_qg = vjp_q(d_cq)
        d_kv_in, d_kvg = vjp_kv(d_ckv)
        d_kr_in = _rope_tile(d_kr_t, c_t, -s_t)
        return jnp.concatenate([d_q_in, d_kv_in, d_kr_in, d_z_t], axis=1), d_qg, d_kvg

    d_proj, g_q_norm, g_kv_norm = _rowwise(
        mid_bwd, [proj, cos_t, sin_t, d_cqn, d_ckvn, d_kr, d_z], [p["q_norm_g"], p["kv_norm_g"]],
        [_sds(proj.shape, BF16)], [_sds(p["q_norm_g"].shape), _sds(p["kv_norm_g"].shape)], tm=512, name="mla_mid_bwd")
    g_w_in = _mm(hn, d_proj, mode="tn", name="mla_dwin", tn=896)
    d_hn = _mm(d_proj, p["w_in"], mode="nt", name="mla_dhn", tk=896)
    dh_in, g_norm = _rms_bwd(h, p["norm_g"], d_hn, dh_out, name="mla_rms_bwd")
    grads = dict(norm_g=g_norm, w_in=g_w_in, q_norm_g=g_q_norm, w_uq=g_w_uq, kv_norm_g=g_kv_norm, w_ukv=g_w_ukv,
                 w_out=g_w_out)
    return dh_in, grads


def _loss_head(h, g, target):
    def fn(x, t, gg):
        def local(xx, g2):
            err = _rms(xx, g2) - t
            return 0.5 * jnp.sum(jnp.mean(err * err, axis=-1))

        val, (dx, dg) = jax.value_and_grad(local, argnums=(0, 1))(x, gg)
        return dx, jnp.full((1, LANES), val, F32), dg

    dh, loss, dg = _rowwise(fn, [h, target], [g], [_sds(h.shape)], [_sds((1, LANES)), _sds(g.shape)], tm=512,
                            name="loss_head")
    return loss[0, 0], dh, dg


HBM_SPEC = pl.BlockSpec(memory_space=pltpu.HBM)


def _all_gather(shard, *, name):
    def body(x_ref, out_ref, send_sems, recv_sems, local_sem):
        x, y, c = lax.axis_index("x"), lax.axis_index("y"), lax.axis_index("c")
        me, sibling = (x, y, c), (x, y, 1 - c)
        chips = [(1 - x, y), (x, 1 - y), (1 - x, 1 - y)]

        def rows(px, py, pc):
            return out_ref.at[4 * px + 2 * py + pc]

        def copy(k, block, to, src=None):
            return pltpu.make_async_remote_copy(
                src_ref=rows(*block) if src is None else src, dst_ref=rows(*block),
                send_sem=send_sems.at[k], recv_sem=recv_sems.at[k], device_id=to, device_id_type=MESH)

        mine = pltpu.make_async_copy(x_ref, rows(*me), local_sem)
        mine.start()
        first = [copy(0, me, sibling, src=x_ref)]
        first += [copy(1 + j, me, (*chip, c), src=x_ref) for j, chip in enumerate(chips)]
        for cp in first:
            cp.start()
        passed = [copy(4 + j, (*chip, c), sibling) for j, chip in enumerate(chips)]
        for j, chip in enumerate(chips):
            copy(1 + j, (*chip, c), me).wait_recv()
            passed[j].start()
        copy(0, sibling, me).wait_recv()
        for j, chip in enumerate(chips):
            copy(4 + j, (*chip, 1 - c), me).wait_recv()
        for cp in first + passed:
            cp.wait_send()
        mine.wait()

    return pl.pallas_call(
        body, name=name, out_shape=jax.ShapeDtypeStruct((N_DEV,) + shard.shape, shard.dtype),
        in_specs=[HBM_SPEC], out_specs=HBM_SPEC,
        scratch_shapes=[pltpu.SemaphoreType.DMA((7,)), pltpu.SemaphoreType.DMA((7,)), pltpu.SemaphoreType.DMA],
    )(shard)


def _exchange(src, routes, *, name):
    n_routes = len(routes)

    def body(s_ref, out_ref, send_sems, recv_sems):
        x, y, c = lax.axis_index("x"), lax.axis_index("y"), lax.axis_index("c")
        local, remote = [], []
        for k, (flip, block) in enumerate(routes):
            src_blk = s_ref.at[block(x, y, c)]
            if flip == 0:
                local.append(pltpu.make_async_copy(src_blk, out_ref.at[k], send_sems.at[k]))
            else:
                peer = (1 - x if flip & 4 else x, 1 - y if flip & 2 else y, 1 - c if flip & 1 else c)
                remote.append(pltpu.make_async_remote_copy(
                    src_ref=src_blk, dst_ref=out_ref.at[k], send_sem=send_sems.at[k], recv_sem=recv_sems.at[k],
                    device_id=peer, device_id_type=MESH))
        for cp in local + remote:
            cp.start()
        for cp in remote:
            cp.wait_recv()
        for cp in remote:
            cp.wait_send()
        for cp in local:
            cp.wait()

    return pl.pallas_call(
        body, name=name, out_shape=jax.ShapeDtypeStruct((n_routes,) + src.shape[1:], src.dtype),
        in_specs=[HBM_SPEC], out_specs=HBM_SPEC,
        scratch_shapes=[pltpu.SemaphoreType.DMA((n_routes,)), pltpu.SemaphoreType.DMA((n_routes,))],
    )(src)


def _reduce_scatter(send, *, name):
    def chip_block(k, other_core):
        return lambda x, y, c: (4 * (1 - x if k & 2 else x) + 2 * (1 - y if k & 1 else y)
                                + (1 - c if other_core else c))

    pair = _exchange(send, [(1, chip_block(k, True)) for k in range(4)] + [(0, chip_block(k, False)) for k in range(4)],
                     name=name + "_pair")
    n_chips = 4
    rows, width = send.shape[1], send.shape[2]
    tr = _pick(rows, 256, 8)

    def add_body(p_ref, o_ref):
        for k in range(n_chips):
            o_ref[k] = (p_ref[k].astype(F32) + p_ref[k + n_chips].astype(F32)).astype(o_ref.dtype)

    chip_sums = pl.pallas_call(
        add_body, name=name + "_pair_sum", grid=(rows // tr,),
        in_specs=[pl.BlockSpec((2 * n_chips, tr, width), lambda i: (0, i, 0))],
        out_specs=pl.BlockSpec((n_chips, tr, width), lambda i: (0, i, 0)),
        out_shape=jax.ShapeDtypeStruct((n_chips, rows, width), send.dtype), compiler_params=_params(("parallel",)),
    )(pair)
    recv = _exchange(chip_sums, [(2 * k, (lambda kk: lambda x, y, c: kk)(k)) for k in range(n_chips)],
                     name=name + "_chips")
    return _sum_slots(recv, name=name + "_sum")


def _sum_slots(recv, *, name):
    n_s, rows, width = recv.shape
    tr = _pick(rows, 256, 8)

    def body(r_ref, o_ref):
        acc = r_ref[0].astype(F32)
        for s in range(1, n_s):
            acc = acc + r_ref[s].astype(F32)
        o_ref[...] = acc

    return pl.pallas_call(
        body, name=name, grid=(rows // tr,),
        in_specs=[pl.BlockSpec((n_s, tr, width), lambda i: (0, i, 0))],
        out_specs=pl.BlockSpec((tr, width), lambda i: (i, 0)), out_shape=_sds((rows, width)),
        compiler_params=_params(("parallel",)),
    )(recv)


def _adamw(w, g, m, v, *, name):
    rows, width = w.shape
    tr = _pick(rows, 256, 8)

    def body(w_ref, g_ref, m_ref, v_ref, d_ref, nm_ref, nv_ref):
        gg = g_ref[...]
        m_new = ADAM_B1 * m_ref[...] + (1.0 - ADAM_B1) * gg
        v_new = ADAM_B2 * v_ref[...] + (1.0 - ADAM_B2) * jnp.square(gg)
        m_hat = m_new / (1.0 - ADAM_B1 ** ADAM_STEP)
        v_hat = v_new / (1.0 - ADAM_B2 ** ADAM_STEP)
        d_ref[...] = -ADAM_LR * (m_hat / (jnp.sqrt(v_hat) + ADAM_EPS) + ADAM_WD * w_ref[...])
        nm_ref[...] = m_new
        nv_ref[...] = v_new

    spec = pl.BlockSpec((tr, width), lambda i: (i, 0))
    return pl.pallas_call(
        body, name=name, grid=(rows // tr,), in_specs=[spec] * 4, out_specs=[spec] * 3,
        out_shape=[_sds(w.shape)] * 3, compiler_params=_params(("parallel",)),
    )(w, g, m, v)


KINDS = ("gmlp", "s5", "mla", "gmlp")
LAYER_NAMES = {
    "gmlp": ("norm_g", "w_in", "ln_g", "ln_b", "w_s", "b_s", "w_out"),
    "s5": ("norm_g", "w_in", "a_re", "a_im", "log_step", "b_re", "b_im", "c_re", "c_im", "d_skip", "w_glu", "b_glu",
           "w_out"),
    "mla": ("norm_g", "w_in", "q_norm_g", "w_uq", "kv_norm_g", "w_ukv", "w_out"),
}
COL_SHARDED = ("w_in", "w_uq", "w_ukv")
ROW_SHARDED = ("w_out", "w_glu")
WEIGHT_NAMES = tuple("l%d_%s" % (i, n) for i, k in enumerate(KINDS) for n in LAYER_NAMES[k]) + ("final_norm_g",)


def _is_sharded(name):
    return name.split("_", 1)[1] in COL_SHARDED + ROW_SHARDED


def _flatten(arrs, pad_rows_to):
    parts, sizes = [], []
    for a in arrs:
        flat = a.reshape(-1)
        pad = (-flat.shape[0]) % FLAT_W
        if pad:
            flat = jnp.pad(flat, (0, pad))
        parts.append(flat)
        sizes.append(flat.shape[0] // FLAT_W)
    rows = sum(sizes)
    pad_rows = (-rows) % pad_rows_to
    if pad_rows:
        parts.append(jnp.zeros((pad_rows * FLAT_W,), arrs[0].dtype))
    return jnp.concatenate(parts).reshape(-1, FLAT_W), sizes


def _unflatten(flat, shapes, sizes):
    out, row = [], 0
    for shape, n_rows in zip(shapes, sizes):
        n = int(np.prod(shape))
        out.append(flat[row:row + n_rows].reshape(-1)[:n].reshape(shape))
        row += n_rows
    return out


def _full_from_gathered(blocks, name):
    if name.split("_", 1)[1] in COL_SHARDED:
        return blocks.transpose(1, 0, 2).reshape(blocks.shape[1], -1)
    return blocks.reshape(-1, blocks.shape[2])


def _shards_of(full, name):
    if name.split("_", 1)[1] in COL_SHARDED:
        return full.reshape(full.shape[0], N_DEV, -1).transpose(1, 0, 2)
    return full.reshape(N_DEV, -1, full.shape[1])


def _rope_tables(positions):
    inv_freq = ROPE_THETA ** (-jnp.arange(0, MLA_ROPE, 2, dtype=F32) / MLA_ROPE)
    ang = positions.astype(F32)[:, None] * inv_freq
    cos, sin = jnp.cos(ang), jnp.sin(ang)
    zero = jnp.zeros((positions.shape[0], LANES - MLA_ROPE), F32)
    return jnp.concatenate([cos, cos, zero], axis=1), jnp.concatenate([-sin, sin, zero], axis=1)


def _row(v):
    return v.reshape(1, -1)


def kernel(x, positions, l0_norm_g, l0_w_in, l0_ln_g, l0_ln_b, l0_w_s, l0_b_s, l0_w_out, l1_norm_g, l1_w_in, l1_a_re, l1_a_im, l1_log_step, l1_b_re, l1_b_im, l1_c_re, l1_c_im, l1_d_skip, l1_w_glu, l1_b_glu, l1_w_out, l2_norm_g, l2_w_in, l2_q_norm_g, l2_w_uq, l2_kv_norm_g, l2_w_ukv, l2_w_out, l3_norm_g, l3_w_in, l3_ln_g, l3_ln_b, l3_w_s, l3_b_s, l3_w_out, final_norm_g, loss_target, m_l0_norm_g, m_l0_w_in, m_l0_ln_g, m_l0_ln_b, m_l0_w_s, m_l0_b_s, m_l0_w_out, m_l1_norm_g, m_l1_w_in, m_l1_a_re, m_l1_a_im, m_l1_log_step, m_l1_b_re, m_l1_b_im, m_l1_c_re, m_l1_c_im, m_l1_d_skip, m_l1_w_glu, m_l1_b_glu, m_l1_w_out, m_l2_norm_g, m_l2_w_in, m_l2_q_norm_g, m_l2_w_uq, m_l2_kv_norm_g, m_l2_w_ukv, m_l2_w_out, m_l3_norm_g, m_l3_w_in, m_l3_ln_g, m_l3_ln_b, m_l3_w_s, m_l3_b_s, m_l3_w_out, m_final_norm_g, v_l0_norm_g, v_l0_w_in, v_l0_ln_g, v_l0_ln_b, v_l0_w_s, v_l0_b_s, v_l0_w_out, v_l1_norm_g, v_l1_w_in, v_l1_a_re, v_l1_a_im, v_l1_log_step, v_l1_b_re, v_l1_b_im, v_l1_c_re, v_l1_c_im, v_l1_d_skip, v_l1_w_glu, v_l1_b_glu, v_l1_w_out, v_l2_norm_g, v_l2_w_in, v_l2_q_norm_g, v_l2_w_uq, v_l2_kv_norm_g, v_l2_w_ukv, v_l2_w_out, v_l3_norm_g, v_l3_w_in, v_l3_ln_g, v_l3_ln_b, v_l3_w_s, v_l3_b_s, v_l3_w_out, v_final_norm_g):
    args = locals()
    weights = {n: args[n] for n in WEIGHT_NAMES}
    mom_m = {n: args["m_" + n] for n in WEIGHT_NAMES}
    mom_v = {n: args["v_" + n] for n in WEIGHT_NAMES}
    return _train_step(x, positions, loss_target, weights, mom_m, mom_v)


def _train_step(x, positions, loss_target, weights, mom_m, mom_v):
    big = [n for n in WEIGHT_NAMES if _is_sharded(n)]
    small = [n for n in WEIGHT_NAMES if not _is_sharded(n)]

    w_flat, big_sizes = _flatten([weights[n] for n in big], 8)
    gathered = _all_gather(w_flat.astype(BF16), name="weights_all_gather")
    full, row = {}, 0
    for n, n_rows in zip(big, big_sizes):
        blocks = gathered[:, row:row + n_rows].reshape((N_DEV,) + weights[n].shape)
        full[n] = _full_from_gathered(blocks, n)
        row += n_rows

    layers, ops_vjps = [], {}
    for i, kind in enumerate(KINDS):
        pre = "l%d_" % i
        p = {n: (full[pre + n] if _is_sharded(pre + n) else weights[pre + n]) for n in LAYER_NAMES[kind]}
        p["norm_g"] = _row(p["norm_g"])
        if kind == "gmlp":
            p["ln_g"], p["ln_b"], p["b_st"] = _row(p["ln_g"]), _row(p["ln_b"]), p["b_s"].T
        elif kind == "s5":
            p["d_skip"], p["b_glu"] = _row(p["d_skip"]), _row(p["b_glu"])
            ops, ops_vjps[i] = jax.vjp(_s5_operators, *[p[n] for n in ("a_re", "a_im", "log_step", "b_re", "b_im",
                                                                       "c_re", "c_im")])
            p["ops"] = tuple(o.astype(BF16) for o in ops[:3]) + ops[3:]
        else:
            heads = p["w_uq"].shape[1] // MLA_QK_DIM
            w_in = p["w_in"]
            split = MLA_Q_RANK + MLA_KV_RANK + MLA_ROPE
            p["w_in"] = jnp.concatenate([w_in[:, :split], jnp.zeros((w_in.shape[0], LANES - MLA_ROPE), w_in.dtype),
                                         w_in[:, split:]], axis=1)
            p["w_uq"] = jnp.pad(p["w_uq"].reshape(-1, heads, MLA_QK_DIM),
                                ((0, 0), (0, 0), (0, MLA_HEAD_PAD - MLA_QK_DIM))).reshape(-1, heads * MLA_HEAD_PAD)
            p["q_norm_g"], p["kv_norm_g"] = _row(p["q_norm_g"]), _row(p["kv_norm_g"])
        layers.append(p)
    rope = _rope_tables(positions[0])

    h = x[0]
    saved = []
    for kind, p in zip(KINDS, layers):
        if kind == "gmlp":
            h, s = _gmlp_fwd(h, p)
        elif kind == "s5":
            h, s = _s5_fwd(h, p)
        else:
            h, s = _mla_fwd(h, p, rope)
        saved.append(s)
    loss_local, dh, g_final = _loss_head(h, _row(weights["final_norm_g"]), loss_target[0])
    loss = lax.psum(loss_local, ("x", "y", "c"))

    grads = {"final_norm_g": g_final.reshape(-1)}
    for i in reversed(range(len(KINDS))):
        kind, p = KINDS[i], layers[i]
        if kind == "gmlp":
            dh, g = _gmlp_bwd(dh, p, saved[i])
        elif kind == "s5":
            dh, g = _s5_bwd(dh, p, saved[i], ops_vjps[i])
        else:
            dh, g = _mla_bwd(dh, p, saved[i], rope)
            heads = weights["l%d_w_uq" % i].shape[1] * N_DEV // MLA_QK_DIM
            split = MLA_Q_RANK + MLA_KV_RANK + MLA_ROPE
            g["w_in"] = jnp.concatenate([g["w_in"][:, :split], g["w_in"][:, MLA_Z0:]], axis=1)
            g["w_uq"] = g["w_uq"].reshape(-1, heads, MLA_HEAD_PAD)[:, :, :MLA_QK_DIM].reshape(-1, heads * MLA_QK_DIM)
        for n, val in g.items():
            name = "l%d_%s" % (i, n)
            grads[name] = val.reshape(weights[name].shape) if not _is_sharded(name) else val

    small_flat, small_sizes = _flatten([grads[n] for n in small], 8 * N_DEV)
    small_rows = small_flat.shape[0] // N_DEV
    send_parts = [_shards_of(grads[n], n).reshape(N_DEV, -1, FLAT_W) for n in big]
    send_parts.append(small_flat.reshape(N_DEV, small_rows, FLAT_W))
    send = jnp.concatenate(send_parts, axis=1)
    big_rows = send.shape[1] - small_rows
    pad_rows = (-send.shape[1]) % 8
    if pad_rows:
        send = jnp.pad(send, ((0, 0), (0, pad_rows), (0, 0)))
    reduced = _reduce_scatter(send.astype(BF16), name="grads")
    g_big_flat = reduced[:big_rows]
    g_small_all = _all_gather(reduced[big_rows:big_rows + small_rows], name="small_grads_all_gather")
    g_small_flat = g_small_all.reshape(-1, FLAT_W)

    def flat_of(tree, names, pad_to):
        return _flatten([tree[n] for n in names], pad_to)[0]

    rows_b = g_big_flat.shape[0]
    pad_b = (-rows_b) % 8
    if pad_b:
        g_big_flat = jnp.pad(g_big_flat, ((0, pad_b), (0, 0)))
    d_b, nm_b, nv_b = _adamw(flat_of(weights, big, 8), g_big_flat, flat_of(mom_m, big, 8), flat_of(mom_v, big, 8),
                             name="adamw_sharded")
    d_s, nm_s, nv_s = _adamw(flat_of(weights, small, 8 * N_DEV), g_small_flat, flat_of(mom_m, small, 8 * N_DEV),
                             flat_of(mom_v, small, 8 * N_DEV), name="adamw_replicated")

    big_shapes = [weights[n].shape for n in big]
    small_shapes = [weights[n].shape for n in small]
    outs = {}
    for prefix, fb, fs in (("grad_", g_big_flat, g_small_flat), ("delta_", d_b, d_s), ("new_m_", nm_b, nm_s),
                           ("new_v_", nv_b, nv_s)):
        for n, a in zip(big, _unflatten(fb, big_shapes, big_sizes)):
            outs[prefix + n] = a
        for n, a in zip(small, _unflatten(fs, small_shapes, small_sizes)):
            outs[prefix + n] = a
    result = [loss, dh[None]]
    for prefix in ("grad_", "delta_", "new_m_", "new_v_"):
        result += [outs[prefix + n] for n in WEIGHT_NAMES]
    return tuple(result)
```

```python
import functools
import math

import numpy as np
import jax
import jax.numpy as jnp
from jax import lax
from jax.experimental import pallas as pl
from jax.experimental.pallas import tpu as pltpu

F32 = jnp.float32
BF16 = jnp.bfloat16

NORM_EPS = 1e-6
GMLP_CHUNK = 128
S5_GROUP = 16
S5_STATE = 64
S5_T = 16
MLA_NOPE = 128
MLA_ROPE = 64
MLA_V = 128
MLA_QK_DIM = MLA_NOPE + MLA_ROPE
MLA_Q_RANK = 384
MLA_KV_RANK = 128
MLA_HEAD_PAD = 256
MLA_SCALE = MLA_QK_DIM ** -0.5
ROPE_THETA = 10000.0
NEG_INF = -1e30
ADAM_LR = 0.001
ADAM_B1 = 0.9
ADAM_B2 = 0.999
ADAM_EPS = 1e-08
ADAM_WD = 0.01
ADAM_STEP = 10

N_DEV = 8
LANES = 128
FLAT_W = 1024
VMEM_LIMIT = 56 * 1024 * 1024
MESH = pl.DeviceIdType.MESH


def _pick(dim, pref, align=LANES):
    t = (min(pref, dim) // align) * align
    while t >= align:
        if dim % t == 0:
            return t
        t -= align
    return dim


def _params(sem=None):
    return pltpu.CompilerParams(dimension_semantics=sem, vmem_limit_bytes=VMEM_LIMIT)


_DIMS = {"nn": (((1,), (0,)), ((), ())), "nt": (((1,), (1,)), ((), ())), "tn": (((0,), (0,)), ((), ()))}


def _mm(a, b, *, mode="nn", out_dtype=F32, add=None, name, tm=512, tn=1024, tk=1024):
    if mode == "nn":
        (m, k), (_, n) = a.shape, b.shape
    elif mode == "nt":
        (m, k), (n, _) = a.shape, b.shape
    else:
        (k, m), (_, n) = a.shape, b.shape
    tm, tn, tk = _pick(m, tm, 8), _pick(n, tn), _pick(k, tk)
    nk = k // tk
    dims = _DIMS[mode]

    def body(*refs):
        if add is None:
            a_ref, b_ref, o_ref, acc_ref = refs
        else:
            a_ref, b_ref, r_ref, o_ref, acc_ref = refs
        kk = pl.program_id(2)

        @pl.when(kk == 0)
        def _():
            acc_ref[...] = jnp.zeros_like(acc_ref)

        acc_ref[...] += lax.dot_general(a_ref[...].astype(BF16), b_ref[...].astype(BF16), dims,
                                        preferred_element_type=F32)

        @pl.when(kk == nk - 1)
        def _():
            res = acc_ref[...]
            if add is not None:
                res = res + r_ref[...]
            o_ref[...] = res.astype(o_ref.dtype)

    a_spec = (pl.BlockSpec((tk, tm), lambda i, j, kk: (kk, i)) if mode == "tn"
              else pl.BlockSpec((tm, tk), lambda i, j, kk: (i, kk)))
    b_spec = (pl.BlockSpec((tn, tk), lambda i, j, kk: (j, kk)) if mode == "nt"
              else pl.BlockSpec((tk, tn), lambda i, j, kk: (kk, j)))
    in_specs = [a_spec, b_spec]
    args = [a, b]
    if add is not None:
        in_specs.append(pl.BlockSpec((tm, tn), lambda i, j, kk: (i, j)))
        args.append(add)
    return pl.pallas_call(
        body, name=name, grid=(m // tm, n // tn, nk),
        in_specs=in_specs, out_specs=pl.BlockSpec((tm, tn), lambda i, j, kk: (i, j)),
        out_shape=jax.ShapeDtypeStruct((m, n), out_dtype),
        scratch_shapes=[pltpu.VMEM((tm, tn), F32)],
        compiler_params=_params(("parallel", "parallel", "arbitrary")),
    )(*args)


def _rowwise(fn, tiled, full, out_tiled, out_acc, *, tm, name):
    rows = tiled[0].shape[0]
    tm = _pick(rows, tm, 8)
    nt, nf, no = len(tiled), len(full), len(out_tiled)

    def body(*refs):
        ins = [r[...] for r in refs[:nt + nf]]
        o_refs = refs[nt + nf:nt + nf + no]
        a_refs = refs[nt + nf + no:]
        outs = fn(*ins)
        if not isinstance(outs, (tuple, list)):
            outs = (outs,)
        for r, v in zip(o_refs, outs[:no]):
            r[...] = v.astype(r.dtype)
        if a_refs:
            @pl.when(pl.program_id(0) == 0)
            def _():
                for r in a_refs:
                    r[...] = jnp.zeros_like(r)

            for r, v in zip(a_refs, outs[no:]):
                r[...] += v.astype(r.dtype)

    def whole(shape):
        nd = len(shape)
        return pl.BlockSpec(tuple(shape), lambda i: (0,) * nd)

    in_specs = ([pl.BlockSpec((tm, t.shape[1]), lambda i: (i, 0)) for t in tiled]
                + [whole(f.shape) for f in full])
    out_specs = ([pl.BlockSpec((tm, o.shape[1]), lambda i: (i, 0)) for o in out_tiled]
                 + [whole(o.shape) for o in out_acc])
    outs = pl.pallas_call(
        body, name=name, grid=(rows // tm,), in_specs=in_specs, out_specs=out_specs,
        out_shape=list(out_tiled) + list(out_acc),
        compiler_params=_params(("arbitrary",)),
    )(*tiled, *full)
    return outs


def _sds(shape, dtype=F32):
    return jax.ShapeDtypeStruct(tuple(shape), dtype)


def _rms(x, g):
    return x * lax.rsqrt(jnp.mean(x * x, axis=-1, keepdims=True) + NORM_EPS) * g


def _layernorm(x, g, b):
    mu = jnp.mean(x, axis=-1, keepdims=True)
    xc = x - mu
    var = jnp.mean(xc * xc, axis=-1, keepdims=True)
    return xc * lax.rsqrt(var + NORM_EPS) * g + b


def _dot(a, b, mode="nn"):
    return lax.dot_general(a.astype(BF16), b.astype(BF16), _DIMS[mode], preferred_element_type=F32)


@jax.custom_vjp
def _bdot(a, b):
    return _dot(a, b)


def _bdot_fwd(a, b):
    return _bdot(a, b), (a, b)


def _bdot_bwd(res, ct):
    a, b = res
    return _dot(ct, b, "nt"), _dot(a, ct, "tn")


_bdot.defvjp(_bdot_fwd, _bdot_bwd)


def _rms_fwd(h, g, *, name):
    return _rowwise(lambda x, gg: _rms(x, gg), [h], [g], [_sds(h.shape, BF16)], [], tm=512, name=name)[0]


def _rms_bwd(h, g, d_hn, dh_out, *, name):
    def fn(x, ct, res, gg):
        _, vjp = jax.vjp(_rms, x, gg)
        dx, dg = vjp(ct)
        return res + dx, dg

    dh, dg = _rowwise(fn, [h, d_hn, dh_out], [g], [_sds(h.shape)], [_sds(g.shape)], tm=512, name=name)
    return dh, dg


def _gmlp_mid(uvz, ln_g, ln_b, w_s, b_st):
    di = ln_g.shape[1]
    ng, ck = w_s.shape[0], w_s.shape[1]
    dg = di // ng
    u = jax.nn.gelu(uvz[:, :di])
    v = _layernorm(jax.nn.gelu(uvz[:, di:2 * di]), ln_g, ln_b)
    z = uvz[:, 2 * di:]
    row = lax.broadcasted_iota(jnp.int32, (ck, ck), 0)
    col = lax.broadcasted_iota(jnp.int32, (ck, ck), 1)
    causal = col <= row
    blocks = []
    for c in range(uvz.shape[0] // ck):
        cols = []
        for g in range(ng):
            w = jnp.where(causal, w_s[g], 0.0)
            cols.append(_bdot(w, v[c * ck:(c + 1) * ck, g * dg:(g + 1) * dg]) + b_st[:, g:g + 1])
        blocks.append(jnp.concatenate(cols, axis=1))
    s = blocks[0] if len(blocks) == 1 else jnp.concatenate(blocks, axis=0)
    return u * s * jax.nn.silu(z)


def _gmlp_fwd(h, p):
    hn = _rms_fwd(h, p["norm_g"], name="gmlp_rms")
    uvz = _mm(hn, p["w_in"], name="gmlp_in")
    di = p["ln_g"].shape[1]
    gated = _rowwise(_gmlp_mid, [uvz], [p["ln_g"], p["ln_b"], p["w_s"], p["b_st"]],
                     [_sds((h.shape[0], di), BF16)], [], tm=256, name="gmlp_mid")[0]
    h_next = _mm(gated, p["w_out"], add=h, name="gmlp_out")
    return h_next, (h, hn, uvz, gated)


def _gmlp_bwd(dh_out, p, saved):
    h, hn, uvz, gated = saved
    d_gated = _mm(dh_out, p["w_out"], mode="nt", out_dtype=BF16, name="gmlp_dgated")
    g_w_out = _mm(gated, dh_out, mode="tn", name="gmlp_dwout")

    def fn(t, ct, ln_g, ln_b, w_s, b_st):
        _, vjp = jax.vjp(_gmlp_mid, t, ln_g, ln_b, w_s, b_st)
        return vjp(ct.astype(F32))

    d_uvz, g_ln_g, g_ln_b, g_w_s, g_b_st = _rowwise(
        fn, [uvz, d_gated], [p["ln_g"], p["ln_b"], p["w_s"], p["b_st"]],
        [_sds(uvz.shape, BF16)], [_sds(p["ln_g"].shape), _sds(p["ln_b"].shape), _sds(p["w_s"].shape),
                                  _sds(p["b_st"].shape)], tm=128, name="gmlp_mid_bwd")
    g_w_in = _mm(hn, d_uvz, mode="tn", name="gmlp_dwin")
    d_hn = _mm(d_uvz, p["w_in"], mode="nt", name="gmlp_dhn")
    dh_in, g_norm = _rms_bwd(h, p["norm_g"], d_hn, dh_out, name="gmlp_rms_bwd")
    grads = dict(norm_g=g_norm, w_in=g_w_in, ln_g=g_ln_g, ln_b=g_ln_b, w_s=g_w_s, b_s=g_b_st.T, w_out=g_w_out)
    return dh_in, grads


def _s5_operators(a_re, a_im, log_step, b_re, b_im, c_re, c_im):
    t_len = S5_T
    step = jnp.exp(log_step)[:, None]
    lr, li = a_re * step, a_im * step
    ks = jnp.arange(t_len + 1, dtype=F32)[:, None, None]
    mag = jnp.exp(ks * lr)
    pw_r, pw_i = mag * jnp.cos(ks * li), mag * jnp.sin(ks * li)
    nr, ni = pw_r[1] - 1.0, pw_i[1]
    den = a_re * a_re + a_im * a_im
    f_r, f_i = (nr * a_re + ni * a_im) / den, (ni * a_re - nr * a_im) / den
    bb_r = f_r[..., None] * b_re - f_i[..., None] * b_im
    bb_i = f_r[..., None] * b_im + f_i[..., None] * b_re
    hi = lax.Precision.HIGHEST
    cp_r = c_re[None] * pw_r[:, :, None, :] - c_im[None] * pw_i[:, :, None, :]
    cp_i = c_re[None] * pw_i[:, :, None, :] + c_im[None] * pw_r[:, :, None, :]
    kern = (jnp.einsum("gpi,kghp->kgih", bb_r, cp_r[:t_len], precision=hi)
            - jnp.einsum("gpi,kghp->kgih", bb_i, cp_i[:t_len], precision=hi))
    lag = jnp.arange(t_len)[None, :] - jnp.arange(t_len)[:, None]
    toep = jnp.where((lag >= 0)[:, :, None, None, None], kern[jnp.clip(lag, 0)], 0.0)
    n_g, n_h = a_re.shape[0], b_re.shape[2]
    toep = toep.transpose(2, 0, 3, 1, 4).reshape(n_g, t_len * n_h, t_len * n_h)
    rev_r, rev_i = pw_r[t_len - 1::-1][:t_len], pw_i[t_len - 1::-1][:t_len]
    we_r = rev_r[..., None] * bb_r[None] - rev_i[..., None] * bb_i[None]
    we_i = rev_r[..., None] * bb_i[None] + rev_i[..., None] * bb_r[None]
    wend = jnp.concatenate([we_r, we_i], axis=2).transpose(1, 0, 3, 2).reshape(n_g, t_len * n_h, -1)
    wo = jnp.concatenate([cp_r[1:], -cp_i[1:]], axis=3)
    wout = wo.transpose(1, 3, 0, 2).reshape(n_g, -1, t_len * n_h)
    a_r, a_i = pw_r[t_len], pw_i[t_len]
    a1 = jnp.concatenate([a_r, a_r], axis=1)
    a2 = jnp.concatenate([-a_i, a_i], axis=1)
    return toep, wend, wout, a1, a2


def _group_call(body, ins, outs, *, gb, name):
    n_g = ins[0].shape[0]

    def spec(a):
        return pl.BlockSpec((gb,) + tuple(a.shape[1:]), lambda i: (i, 0, 0))

    return pl.pallas_call(
        body, name=name, grid=(n_g // gb,), in_specs=[spec(a) for a in ins],
        out_specs=[spec(o) for o in outs], out_shape=list(outs),
        compiler_params=_params(("parallel",)),
    )(*ins)


def _s5_states(u_g, wend, *, gb=8):
    def body(u_ref, w_ref, s_ref):
        for g in range(gb):
            s_ref[g] = _dot(u_ref[g], w_ref[g])

    n_g, n_c = u_g.shape[0], u_g.shape[1]
    return _group_call(body, [u_g, wend], [_sds((n_g, n_c, wend.shape[2]))], gb=gb, name="s5_states")[0]


def _s5_outputs(u_g, toep, xprev, wout, *, gb=8):
    def body(u_ref, t_ref, x_ref, w_ref, y_ref):
        for g in range(gb):
            y_ref[g] = _dot(u_ref[g], t_ref[g]) + _dot(x_ref[g], w_ref[g])

    return _group_call(body, [u_g, toep, xprev, wout], [_sds(u_g.shape)], gb=gb, name="s5_outputs")[0]


def _s5_outputs_bwd(u_g, d_y, xprev, wout, *, gb=8):
    def body(u_ref, dy_ref, x_ref, w_ref, dt_ref, dw_ref, dx_ref):
        for g in range(gb):
            dy = dy_ref[g]
            dt_ref[g] = _dot(u_ref[g], dy, "tn")
            dw_ref[g] = _dot(x_ref[g], dy, "tn")
            dx_ref[g] = _dot(dy, w_ref[g], "nt")

    n_g, n_c, n_k = u_g.shape
    return _group_call(body, [u_g, d_y, xprev, wout],
                       [_sds((n_g, n_k, n_k)), _sds(wout.shape), _sds(xprev.shape)], gb=gb, name="s5_outputs_bwd")


def _s5_inputs_bwd(u_g, d_y, d_s, toep, wend, *, gb=8):
    def body(u_ref, dy_ref, ds_ref, t_ref, w_ref, du_ref, dw_ref):
        for g in range(gb):
            ds = ds_ref[g]
            du_ref[g] = _dot(dy_ref[g], t_ref[g], "nt") + _dot(ds, w_ref[g], "nt")
            dw_ref[g] = _dot(u_ref[g], ds, "tn")

    return _group_call(body, [u_g, d_y, d_s, toep, wend], [_sds(u_g.shape), _sds(wend.shape)], gb=gb,
                       name="s5_inputs_bwd")


def _swap_halves(x):
    return pltpu.roll(x, x.shape[-1] // 2, axis=x.ndim - 1)


def _s5_scan(s_t, a1, a2, *, gb=32):
    n_c, n_g, n_p = s_t.shape
    gb = min(gb, n_g)

    def body(s_ref, a1_ref, a2_ref, x_ref):
        a1v, a2v = a1_ref[...], a2_ref[...]

        def step(c, x):
            x_ref[c] = x
            return x * a1v + _swap_halves(x) * a2v + s_ref[c]

        lax.fori_loop(0, n_c, step, jnp.zeros((gb, n_p), F32))

    return pl.pallas_call(
        body, name="s5_scan", grid=(n_g // gb,),
        in_specs=[pl.BlockSpec((n_c, gb, n_p), lambda i: (0, i, 0)), pl.BlockSpec((gb, n_p), lambda i: (i, 0)),
                  pl.BlockSpec((gb, n_p), lambda i: (i, 0))],
        out_specs=pl.BlockSpec((n_c, gb, n_p), lambda i: (0, i, 0)), out_shape=_sds(s_t.shape),
        compiler_params=_params(("parallel",)),
    )(s_t, a1, a2)


def _s5_scan_bwd(d_xprev_t, xprev_t, a1, a2, *, gb=32):
    n_c, n_g, n_p = xprev_t.shape
    gb = min(gb, n_g)

    def body(dx_ref, x_ref, a1_ref, a2_ref, ds_ref, p1_ref, p2_ref):
        a1v, a2v = a1_ref[...], a2_ref[...]
        zero = jnp.zeros((gb, n_p), F32)
        ds_ref[n_c - 1] = zero

        def step(k, carry):
            gx_next, p1, p2 = carry
            c = n_c - 2 - k
            xp = x_ref[c + 1]
            gx = dx_ref[c + 1] + gx_next * a1v - _swap_halves(gx_next) * a2v
            ds_ref[c] = gx
            return gx, p1 + gx_next * xp, p2 + gx_next * _swap_halves(xp)

        _, p1, p2 = lax.fori_loop(0, n_c - 1, step, (zero, zero, zero))
        p1_ref[...] = p1
        p2_ref[...] = p2

    blk = pl.BlockSpec((n_c, gb, n_p), lambda i: (0, i, 0))
    vec = pl.BlockSpec((gb, n_p), lambda i: (i, 0))
    return pl.pallas_call(
        body, name="s5_scan_bwd", grid=(n_g // gb,), in_specs=[blk, blk, vec, vec],
        out_specs=[blk, vec, vec], out_shape=[_sds(xprev_t.shape), _sds(a1.shape), _sds(a1.shape)],
        compiler_params=_params(("parallel",)),
    )(d_xprev_t, xprev_t, a1, a2)


GROUPS_PER_TILE = LANES // S5_GROUP


def _to_groups(t, n_g):
    n_l = t.shape[0]
    n_c = n_l // S5_T
    gpt = min(GROUPS_PER_TILE, n_g)
    width = gpt * S5_GROUP

    def body(x_ref, o_ref):
        tr = [x_ref[pl.ds(s, n_c, stride=S5_T), :].T for s in range(S5_T)]
        for gl in range(gpt):
            rows = slice(gl * S5_GROUP, (gl + 1) * S5_GROUP)
            stacked = jnp.concatenate([tr[s][rows, :] for s in range(S5_T)], axis=0)
            o_ref[gl] = stacked.T.astype(o_ref.dtype)

    return pl.pallas_call(
        body, name="s5_to_groups", grid=(n_g // gpt,),
        in_specs=[pl.BlockSpec((n_l, width), lambda b: (0, b))],
        out_specs=pl.BlockSpec((gpt, n_c, S5_T * S5_GROUP), lambda b: (b, 0, 0)),
        out_shape=_sds((n_g, n_c, S5_T * S5_GROUP), BF16), compiler_params=_params(("parallel",)),
    )(t)


def _from_groups(t, n_l):
    n_g, n_c = t.shape[0], t.shape[1]
    gpt = min(GROUPS_PER_TILE, n_g)
    width = gpt * S5_GROUP

    def body(y_ref, o_ref):
        ytr = [y_ref[gl].T for gl in range(gpt)]
        for s in range(S5_T):
            rows = slice(s * S5_GROUP, (s + 1) * S5_GROUP)
            piece = jnp.concatenate([ytr[gl][rows, :] for gl in range(gpt)], axis=0)
            o_ref[pl.ds(s, n_c, stride=S5_T), :] = piece.T

    return pl.pallas_call(
        body, name="s5_from_groups", grid=(n_g // gpt,),
        in_specs=[pl.BlockSpec((gpt, n_c, S5_T * S5_GROUP), lambda b: (b, 0, 0))],
        out_specs=pl.BlockSpec((n_l, width), lambda b: (0, b)),
        out_shape=_sds((n_l, n_g * S5_GROUP)), compiler_params=_params(("parallel",)),
    )(t)


def _s5_act(ys, uz, d_skip):
    di = d_skip.shape[1]
    return jax.nn.gelu(ys + d_skip * uz[:, :di])


def _s5_gate(g1, glu_pre, uz, b_glu):
    di = b_glu.shape[1]
    return g1 * jax.nn.sigmoid(glu_pre + b_glu) * jax.nn.silu(uz[:, di:])


def _s5_fwd(h, p):
    n_l = h.shape[0]
    di = p["d_skip"].shape[1]
    n_g = di // S5_GROUP
    hn = _rms_fwd(h, p["norm_g"], name="s5_rms")
    uz = _mm(hn, p["w_in"], name="s5_in")
    toep, wend, wout, a1, a2 = p["ops"]
    u_g = _to_groups(uz, n_g)
    s = _s5_states(u_g, wend)
    xprev = _s5_scan(s.transpose(1, 0, 2), a1, a2).transpose(1, 0, 2)
    ys = _from_groups(_s5_outputs(u_g, toep, xprev, wout), n_l)
    g1 = _rowwise(_s5_act, [ys, uz], [p["d_skip"]], [_sds((n_l, di), BF16)], [], tm=512, name="s5_act")[0]
    glu_pre = _mm(g1, p["w_glu"], name="s5_glu")

    def gate(ys_t, pre_t, uz_t, d_skip, b_glu):
        return _s5_gate(_s5_act(ys_t, uz_t, d_skip), pre_t, uz_t, b_glu)

    gated = _rowwise(gate, [ys, glu_pre, uz], [p["d_skip"], p["b_glu"]], [_sds((n_l, di), BF16)], [],
                     tm=512, name="s5_gate")[0]
    h_next = _mm(gated, p["w_out"], add=h, name="s5_out")
    return h_next, (h, hn, uz, u_g, xprev, ys, g1, glu_pre, gated)


def _s5_bwd(dh_out, p, saved, ops_vjp):
    h, hn, uz, u_g, xprev, ys, g1, glu_pre, gated = saved
    n_l = h.shape[0]
    di = p["d_skip"].shape[1]
    n_g = di // S5_GROUP
    toep, wend, wout, a1, a2 = p["ops"]
    d_gated = _mm(dh_out, p["w_out"], mode="nt", out_dtype=BF16, name="s5_dgated")
    g_w_out = _mm(gated, dh_out, mode="tn", name="s5_dwout")

    def gate_bwd(ys_t, pre_t, uz_t, ct, d_skip, b_glu):
        g1_t = _s5_act(ys_t, uz_t, d_skip)
        _, vjp = jax.vjp(_s5_gate, g1_t, pre_t, uz_t, b_glu)
        d_g1, d_pre, d_uz, d_b = vjp(ct.astype(F32))
        return d_g1, d_pre, d_uz, d_b

    d_g1_direct, d_pre, d_uz_gate, g_b_glu = _rowwise(
        gate_bwd, [ys, glu_pre, uz, d_gated], [p["d_skip"], p["b_glu"]],
        [_sds((n_l, di)), _sds((n_l, di), BF16), _sds(uz.shape)], [_sds(p["b_glu"].shape)], tm=256, name="s5_gate_bwd")
    g_w_glu = _mm(g1, d_pre, mode="tn", name="s5_dwglu")
    d_g1 = _mm(d_pre, p["w_glu"], mode="nt", add=d_g1_direct, name="s5_dg1")

    def act_bwd(ys_t, uz_t, ct, d_uz_t, d_skip):
        _, vjp = jax.vjp(_s5_act, ys_t, uz_t, d_skip)
        d_ys, d_uz, d_d = vjp(ct)
        return d_ys, d_uz + d_uz_t, d_d

    d_ys, d_uz_part, g_d_skip = _rowwise(
        act_bwd, [ys, uz, d_g1, d_uz_gate], [p["d_skip"]], [_sds((n_l, di)), _sds(uz.shape)],
        [_sds(p["d_skip"].shape)], tm=256, name="s5_act_bwd")
    d_y = _to_groups(d_ys, n_g)
    d_toep, d_wout, d_xprev = _s5_outputs_bwd(u_g, d_y, xprev, wout)
    d_s_t, p1, p2 = _s5_scan_bwd(d_xprev.transpose(1, 0, 2), xprev.transpose(1, 0, 2), a1, a2)
    d_s = d_s_t.transpose(1, 0, 2)
    d_u_g, d_wend = _s5_inputs_bwd(u_g, d_y, d_s, toep, wend)
    d_u = _from_groups(d_u_g, n_l)
    d_uz = _rowwise(lambda part, du: jnp.concatenate([part[:, :di] + du, part[:, di:]], axis=1),
                    [d_uz_part, d_u], [], [_sds(uz.shape, BF16)], [], tm=512, name="s5_duz")[0]
    g_w_in = _mm(hn, d_uz, mode="tn", name="s5_dwin")
    d_hn = _mm(d_uz, p["w_in"], mode="nt", name="s5_dhn")
    dh_in, g_norm = _rms_bwd(h, p["norm_g"], d_hn, dh_out, name="s5_rms_bwd")
    g_ops = ops_vjp((d_toep, d_wend, d_wout, p1, p2))
    grads = dict(norm_g=g_norm, w_in=g_w_in, a_re=g_ops[0], a_im=g_ops[1], log_step=g_ops[2], b_re=g_ops[3],
                 b_im=g_ops[4], c_re=g_ops[5], c_im=g_ops[6], d_skip=g_d_skip, w_glu=g_w_glu, b_glu=g_b_glu,
                 w_out=g_w_out)
    return dh_in, grads


MLA_Z0 = MLA_Q_RANK + MLA_KV_RANK + LANES


def _rope_tile(t, cos_t, sin_t):
    q = LANES // 4
    lane = lax.broadcasted_iota(jnp.int32, t.shape, 1)
    swapped = jnp.where(lane < q, pltpu.roll(t, LANES - q, axis=1), pltpu.roll(t, q, axis=1))
    return t * cos_t + swapped * sin_t


def _mla_mid(proj, cos_t, sin_t, q_g, kv_g):
    cqn = _rms(proj[:, :MLA_Q_RANK], q_g)
    ckvn = _rms(proj[:, MLA_Q_RANK:MLA_Q_RANK + MLA_KV_RANK], kv_g)
    kr = _rope_tile(proj[:, MLA_Q_RANK + MLA_KV_RANK:MLA_Z0], cos_t, sin_t)
    return cqn, ckvn, kr


def _mla_rope_q(qp, cos_t, sin_t):
    parts = []
    for hd in range(qp.shape[1] // MLA_HEAD_PAD):
        base = hd * MLA_HEAD_PAD
        parts.append(qp[:, base:base + LANES])
        parts.append(_rope_tile(qp[:, base + LANES:base + MLA_HEAD_PAD], cos_t, sin_t))
    return jnp.concatenate(parts, axis=1)


def _mla_gate(o, proj):
    return o * jax.nn.silu(proj[:, MLA_Z0:])


LOG2E = math.log2(math.e)
SCORE_LOG2 = MLA_SCALE * LOG2E
FLASH_SPLIT = 2


def _causal_pairs(n_blk, kv_major):
    if kv_major:
        pairs = [(i, j) for j in range(n_blk) for i in range(j, n_blk)]
    else:
        pairs = [(i, j) for i in range(n_blk) for j in range(i + 1)]
    return (jnp.asarray([p[0] for p in pairs], jnp.int32), jnp.asarray([p[1] for p in pairs], jnp.int32))


def _raw_scores(q, kcat, row0, diagonal):
    s = _dot(q, kcat, "nt")
    if diagonal:
        qpos = row0 + lax.broadcasted_iota(jnp.int32, s.shape, 0)
        kpos = lax.broadcasted_iota(jnp.int32, s.shape, 1)
        s = jnp.where(kpos <= qpos, s, NEG_INF)
    return s


def _lanes(x, width):
    return jnp.tile(x, (1, width // LANES))


def _flash_fwd(qp, kv, kr, *, blk=512):
    n_l = qp.shape[0]
    heads = qp.shape[1] // MLA_HEAD_PAD
    blk = _pick(n_l, blk)
    n_blk = n_l // blk
    half = blk // FLASH_SPLIT
    qi, kj = _causal_pairs(n_blk, kv_major=False)

    def body(qi_ref, kj_ref, q_ref, kv_ref, kr_ref, o_ref, lse_ref, m_sc, l_sc, acc_sc):
        p = pl.program_id(1)
        i, j = qi_ref[p], kj_ref[p]

        @pl.when(j == 0)
        def _():
            m_sc[...] = jnp.full_like(m_sc, NEG_INF)
            l_sc[...] = jnp.zeros_like(l_sc)
            acc_sc[...] = jnp.zeros_like(acc_sc)

        def update(diagonal):
            kcat = jnp.concatenate([kv_ref[:, :LANES], kr_ref[...]], axis=1)
            v = kv_ref[:, LANES:]
            for r in range(FLASH_SPLIT):
                rows = slice(r * half, (r + 1) * half)
                s = _raw_scores(q_ref[rows, :], kcat, r * half, diagonal)
                m_old = m_sc[rows, :]
                m_new = jnp.maximum(m_old, jnp.max(s, axis=1, keepdims=True))
                alpha = jnp.exp2((m_old - m_new) * SCORE_LOG2)
                pr = jnp.exp2((s - _lanes(m_new, blk)) * SCORE_LOG2)
                l_sc[rows, :] = alpha * l_sc[rows, :] + jnp.sum(pr, axis=1, keepdims=True)
                acc_sc[rows, :] = alpha * acc_sc[rows, :] + _dot(pr, v)
                m_sc[rows, :] = m_new

        @pl.when(j < i)
        def _():
            update(False)

        @pl.when(j == i)
        def _():
            update(True)
            o_ref[...] = acc_sc[...] / l_sc[...]
            lse_ref[...] = m_sc[...] * MLA_SCALE + jnp.log(l_sc[...])

    grid_spec = pltpu.PrefetchScalarGridSpec(
        num_scalar_prefetch=2, grid=(heads, qi.shape[0]),
        in_specs=[pl.BlockSpec((blk, MLA_HEAD_PAD), lambda h, p, qi_r, kj_r: (qi_r[p], h)),
                  pl.BlockSpec((blk, MLA_HEAD_PAD), lambda h, p, qi_r, kj_r: (kj_r[p], h)),
                  pl.BlockSpec((blk, LANES), lambda h, p, qi_r, kj_r: (kj_r[p], 0))],
        out_specs=[pl.BlockSpec((blk, MLA_V), lambda h, p, qi_r, kj_r: (qi_r[p], h)),
                   pl.BlockSpec((None, blk, LANES), lambda h, p, qi_r, kj_r: (h, qi_r[p], 0))],
        scratch_shapes=[pltpu.VMEM((blk, LANES), F32), pltpu.VMEM((blk, LANES), F32), pltpu.VMEM((blk, MLA_V), F32)])
    return pl.pallas_call(
        body, name="mla_flash_fwd", grid_spec=grid_spec,
        out_shape=[_sds((n_l, heads * MLA_V)), _sds((heads, n_l, LANES))],
        compiler_params=_params(("parallel", "arbitrary")),
    )(qi, kj, qp, kv, kr)


def _flash_bwd(qp, kv, kr, d_o, lse, delta, *, blk=512):
    n_l = qp.shape[0]
    heads = qp.shape[1] // MLA_HEAD_PAD
    blk = _pick(n_l, blk)
    n_blk = n_l // blk
    half = blk // FLASH_SPLIT
    qi, kj = _causal_pairs(n_blk, kv_major=True)
    n_pairs = qi.shape[0]

    def body(qi_ref, kj_ref, q_ref, kv_ref, kr_ref, do_ref, lse_ref, dl_ref, dq_ref, dkv_ref, dkr_ref, dk_sc, dv_sc):
        h, p = pl.program_id(0), pl.program_id(1)
        i, j = qi_ref[p], kj_ref[p]

        @pl.when(p == 0)
        def _():
            dq_ref[...] = jnp.zeros_like(dq_ref)

        @pl.when(jnp.logical_and(p == 0, h == 0))
        def _():
            dkr_ref[...] = jnp.zeros_like(dkr_ref)

        @pl.when(i == j)
        def _():
            dk_sc[...] = jnp.zeros_like(dk_sc)
            dv_sc[...] = jnp.zeros_like(dv_sc)

        def update(diagonal):
            kcat = jnp.concatenate([kv_ref[:, :LANES], kr_ref[...]], axis=1)
            v = kv_ref[:, LANES:]
            for r in range(FLASH_SPLIT):
                rows = slice(r * half, (r + 1) * half)
                q_t, do_t = q_ref[rows, :], do_ref[rows, :]
                s = _raw_scores(q_t, kcat, r * half, diagonal)
                pr = jnp.exp2(s * SCORE_LOG2 - _lanes(lse_ref[rows, :] * LOG2E, blk))
                d_p = _dot(do_t, v, "nt")
                d_s = pr * (d_p - _lanes(dl_ref[rows, :], blk))
                dk_sc[...] += _dot(d_s, q_t, "tn")
                dv_sc[...] += _dot(pr, do_t, "tn")
                q_rows = pl.ds(pl.multiple_of(i * blk + r * half, half), half)
                dq_ref[q_rows, :] += _dot(d_s, kcat)

        @pl.when(i > j)
        def _():
            update(False)

        @pl.when(i == j)
        def _():
            update(True)

        @pl.when(i == n_blk - 1)
        def _():
            dk = dk_sc[...] * MLA_SCALE
            dkv_ref[:, :LANES] = dk[:, :LANES].astype(dkv_ref.dtype)
            dkv_ref[:, LANES:] = dv_sc[...].astype(dkv_ref.dtype)
            k_rows = pl.ds(pl.multiple_of(j * blk, blk), blk)
            dkr_ref[k_rows, :] += dk[:, LANES:]

        @pl.when(p == n_pairs - 1)
        def _():
            dq_ref[...] = dq_ref[...] * MLA_SCALE

    at_q = lambda h, p, qi_r, kj_r: (qi_r[p], h)
    at_kv = lambda h, p, qi_r, kj_r: (kj_r[p], h)
    grid_spec = pltpu.PrefetchScalarGridSpec(
        num_scalar_prefetch=2, grid=(heads, n_pairs),
        in_specs=[pl.BlockSpec((blk, MLA_HEAD_PAD), at_q),
                  pl.BlockSpec((blk, MLA_HEAD_PAD), at_kv),
                  pl.BlockSpec((blk, LANES), lambda h, p, qi_r, kj_r: (kj_r[p], 0)),
                  pl.BlockSpec((blk, MLA_V), at_q),
                  pl.BlockSpec((None, blk, LANES), lambda h, p, qi_r, kj_r: (h, qi_r[p], 0)),
                  pl.BlockSpec((blk, LANES), at_q)],
        out_specs=[pl.BlockSpec((n_l, MLA_HEAD_PAD), lambda h, p, qi_r, kj_r: (0, h)),
                   pl.BlockSpec((blk, MLA_HEAD_PAD), at_kv),
                   pl.BlockSpec((n_l, LANES), lambda h, p, qi_r, kj_r: (0, 0))],
        scratch_shapes=[pltpu.VMEM((blk, MLA_HEAD_PAD), F32), pltpu.VMEM((blk, MLA_V), F32)])
    return pl.pallas_call(
        body, name="mla_flash_bwd", grid_spec=grid_spec,
        out_shape=[_sds(qp.shape), _sds(kv.shape, BF16), _sds(kr.shape)],
        compiler_params=_params(("arbitrary", "arbitrary")),
    )(qi, kj, qp, kv, kr, d_o, lse, delta)


def _mla_fwd(h, p, rope):
    n_l = h.shape[0]
    cos_t, sin_t = rope
    hn = _rms_fwd(h, p["norm_g"], name="mla_rms")
    proj = _mm(hn, p["w_in"], name="mla_in", tn=896)
    cqn, ckvn, kr = _rowwise(_mla_mid, [proj, cos_t, sin_t], [p["q_norm_g"], p["kv_norm_g"]],
                             [_sds((n_l, MLA_Q_RANK), BF16), _sds((n_l, MLA_KV_RANK), BF16), _sds((n_l, LANES), BF16)],
                             [], tm=512, name="mla_mid")
    q_raw = _mm(cqn, p["w_uq"], name="mla_uq")
    qp = _rowwise(_mla_rope_q, [q_raw, cos_t, sin_t], [], [_sds(q_raw.shape, BF16)], [], tm=512, name="mla_rope_q")[0]
    kv = _mm(ckvn, p["w_ukv"], out_dtype=BF16, name="mla_ukv")
    o, lse = _flash_fwd(qp, kv, kr)
    gated = _rowwise(_mla_gate, [o, proj], [], [_sds(o.shape, BF16)], [], tm=512, name="mla_gate")[0]
    h_next = _mm(gated, p["w_out"], add=h, name="mla_out")
    return h_next, (h, hn, proj, cqn, ckvn, kr, qp, kv, o, lse, gated)


def _mla_bwd(dh_out, p, saved, rope):
    h, hn, proj, cqn, ckvn, kr, qp, kv, o, lse, gated = saved
    n_l = h.shape[0]
    cos_t, sin_t = rope
    d_gated = _mm(dh_out, p["w_out"], mode="nt", out_dtype=BF16, name="mla_dgated")
    g_w_out = _mm(gated, dh_out, mode="tn", name="mla_dwout")

    def gate_bwd(o_t, proj_t, ct):
        _, vjp = jax.vjp(lambda a, z: a * jax.nn.silu(z), o_t, proj_t[:, MLA_Z0:])
        d_o_t, d_z_t = vjp(ct.astype(F32))
        prod = d_o_t * o_t
        delta = jnp.concatenate(
            [jnp.broadcast_to(jnp.sum(prod[:, hd * MLA_V:(hd + 1) * MLA_V], axis=1, keepdims=True),
                              (prod.shape[0], MLA_V)) for hd in range(prod.shape[1] // MLA_V)], axis=1)
        return d_o_t, d_z_t, delta

    d_o, d_z, delta = _rowwise(gate_bwd, [o, proj, d_gated], [], [_sds(o.shape, BF16), _sds(o.shape), _sds(o.shape)],
                               [], tm=512, name="mla_gate_bwd")
    d_qp, d_kv, d_kr = _flash_bwd(qp, kv, kr, d_o, lse, delta)

    def rope_q_bwd(ct, c_t, s_t):
        return _mla_rope_q(ct, c_t, -s_t)

    d_q_raw = _rowwise(rope_q_bwd, [d_qp, cos_t, sin_t], [], [_sds(d_qp.shape, BF16)], [], tm=512,
                       name="mla_rope_q_bwd")[0]
    g_w_uq = _mm(cqn, d_q_raw, mode="tn", name="mla_dwuq")
    d_cqn = _mm(d_q_raw, p["w_uq"], mode="nt", name="mla_dcqn")
    g_w_ukv = _mm(ckvn, d_kv, mode="tn", name="mla_dwukv")
    d_ckvn = _mm(d_kv, p["w_ukv"], mode="nt", name="mla_dckvn")

    def mid_bwd(proj_t, c_t, s_t, d_cq, d_ckv, d_kr_t, d_z_t, q_g, kv_g):
        _, vjp_q = jax.vjp(_rms, proj_t[:, :MLA_Q_RANK], q_g)
        _, vjp_kv = jax.vjp(_rms, proj_t[:, MLA_Q_RANK:MLA_Q_RANK + MLA_KV_RANK], kv_g)
        d_q_in, d_qg = vjp_q(d_cq)
        d_kv_in, d_kvg = vjp_kv(d_ckv)
        d_kr_in = _rope_tile(d_kr_t, c_t, -s_t)
        return jnp.concatenate([d_q_in, d_kv_in, d_kr_in, d_z_t], axis=1), d_qg, d_kvg

    d_proj, g_q_norm, g_kv_norm = _rowwise(
        mid_bwd, [proj, cos_t, sin_t, d_cqn, d_ckvn, d_kr, d_z], [p["q_norm_g"], p["kv_norm_g"]],
        [_sds(proj.shape, BF16)], [_sds(p["q_norm_g"].shape), _sds(p["kv_norm_g"].shape)], tm=512, name="mla_mid_bwd")
    g_w_in = _mm(hn, d_proj, mode="tn", name="mla_dwin", tn=896)
    d_hn = _mm(d_proj, p["w_in"], mode="nt", name="mla_dhn", tk=896)
    dh_in, g_norm = _rms_bwd(h, p["norm_g"], d_hn, dh_out, name="mla_rms_bwd")
    grads = dict(norm_g=g_norm, w_in=g_w_in, q_norm_g=g_q_norm, w_uq=g_w_uq, kv_norm_g=g_kv_norm, w_ukv=g_w_ukv,
                 w_out=g_w_out)
    return dh_in, grads


def _loss_head(h, g, target):
    def fn(x, t, gg):
        def local(xx, g2):
            err = _rms(xx, g2) - t
            return 0.5 * jnp.sum(jnp.mean(err * err, axis=-1))

        val, (dx, dg) = jax.value_and_grad(local, argnums=(0, 1))(x, gg)
        return dx, jnp.full((1, LANES), val, F32), dg

    dh, loss, dg = _rowwise(fn, [h, target], [g], [_sds(h.shape)], [_sds((1, LANES)), _sds(g.shape)], tm=512,
                            name="loss_head")
    return loss[0, 0], dh, dg


HBM_SPEC = pl.BlockSpec(memory_space=pltpu.HBM)


def _all_gather(shard, *, name):
    def body(x_ref, out_ref, send_sems, recv_sems, local_sem):
        x, y, c = lax.axis_index("x"), lax.axis_index("y"), lax.axis_index("c")
        me, sibling = (x, y, c), (x, y, 1 - c)
        chips = [(1 - x, y), (x, 1 - y), (1 - x, 1 - y)]

        def rows(px, py, pc):
            return out_ref.at[4 * px + 2 * py + pc]

        def copy(k, block, to, src=None):
            return pltpu.make_async_remote_copy(
                src_ref=rows(*block) if src is None else src, dst_ref=rows(*block),
                send_sem=send_sems.at[k], recv_sem=recv_sems.at[k], device_id=to, device_id_type=MESH)

        mine = pltpu.make_async_copy(x_ref, rows(*me), local_sem)
        mine.start()
        first = [copy(0, me, sibling, src=x_ref)]
        first += [copy(1 + j, me, (*chip, c), src=x_ref) for j, chip in enumerate(chips)]
        for cp in first:
            cp.start()
        passed = [copy(4 + j, (*chip, c), sibling) for j, chip in enumerate(chips)]
        for j, chip in enumerate(chips):
            copy(1 + j, (*chip, c), me).wait_recv()
            passed[j].start()
        copy(0, sibling, me).wait_recv()
        for j, chip in enumerate(chips):
            copy(4 + j, (*chip, 1 - c), me).wait_recv()
        for cp in first + passed:
            cp.wait_send()
        mine.wait()

    return pl.pallas_call(
        body, name=name, out_shape=jax.ShapeDtypeStruct((N_DEV,) + shard.shape, shard.dtype),
        in_specs=[HBM_SPEC], out_specs=HBM_SPEC,
        scratch_shapes=[pltpu.SemaphoreType.DMA((7,)), pltpu.SemaphoreType.DMA((7,)), pltpu.SemaphoreType.DMA],
    )(shard)


def _exchange(src, routes, *, name):
    n_routes = len(routes)

    def body(s_ref, out_ref, send_sems, recv_sems):
        x, y, c = lax.axis_index("x"), lax.axis_index("y"), lax.axis_index("c")
        local, remote = [], []
        for k, (flip, block) in enumerate(routes):
            src_blk = s_ref.at[block(x, y, c)]
            if flip == 0:
                local.append(pltpu.make_async_copy(src_blk, out_ref.at[k], send_sems.at[k]))
            else:
                peer = (1 - x if flip & 4 else x, 1 - y if flip & 2 else y, 1 - c if flip & 1 else c)
                remote.append(pltpu.make_async_remote_copy(
                    src_ref=src_blk, dst_ref=out_ref.at[k], send_sem=send_sems.at[k], recv_sem=recv_sems.at[k],
                    device_id=peer, device_id_type=MESH))
        for cp in local + remote:
            cp.start()
        for cp in remote:
            cp.wait_recv()
        for cp in remote:
            cp.wait_send()
        for cp in local:
            cp.wait()

    return pl.pallas_call(
        body, name=name, out_shape=jax.ShapeDtypeStruct((n_routes,) + src.shape[1:], src.dtype),
        in_specs=[HBM_SPEC], out_specs=HBM_SPEC,
        scratch_shapes=[pltpu.SemaphoreType.DMA((n_routes,)), pltpu.SemaphoreType.DMA((n_routes,))],
    )(src)


def _reduce_scatter(send, *, name):
    def chip_block(k, other_core):
        return lambda x, y, c: (4 * (1 - x if k & 2 else x) + 2 * (1 - y if k & 1 else y)
                                + (1 - c if other_core else c))

    n_chips = 4
    pair = _exchange(send, [(1, chip_block(k, True)) for k in range(n_chips)], name=name + "_pair")
    rows, width = send.shape[1], send.shape[2]
    tr = _pick(rows, 256, 8)
    x, y, c = lax.axis_index("x"), lax.axis_index("y"), lax.axis_index("c")
    own_ids = jnp.stack([chip_block(k, False)(x, y, c) for k in range(n_chips)]).astype(jnp.int32)

    def add_body(ids_ref, *refs):
        own_refs, p_ref, o_ref = refs[:n_chips], refs[n_chips], refs[n_chips + 1]
        for k in range(n_chips):
            o_ref[k] = (own_refs[k][...].astype(F32) + p_ref[k].astype(F32)).astype(o_ref.dtype)

    own_spec = lambda k: pl.BlockSpec((None, tr, width), lambda i, ids: (ids[k], i, 0))
    chip_sums = pl.pallas_call(
        add_body, name=name + "_pair_sum",
        grid_spec=pltpu.PrefetchScalarGridSpec(
            num_scalar_prefetch=1, grid=(rows // tr,),
            in_specs=[own_spec(k) for k in range(n_chips)] + [pl.BlockSpec((n_chips, tr, width), lambda i, ids: (0, i, 0))],
            out_specs=pl.BlockSpec((n_chips, tr, width), lambda i, ids: (0, i, 0))),
        out_shape=jax.ShapeDtypeStruct((n_chips, rows, width), send.dtype), compiler_params=_params(("parallel",)),
    )(own_ids, *([send] * n_chips), pair)
    recv = _exchange(chip_sums, [(2 * k, (lambda kk: lambda x, y, c: kk)(k)) for k in range(1, n_chips)],
                     name=name + "_chips")

    def sum_body(q_ref, r_ref, o_ref):
        acc = q_ref[...].astype(F32)
        for k in range(n_chips - 1):
            acc = acc + r_ref[k].astype(F32)
        o_ref[...] = acc

    return pl.pallas_call(
        sum_body, name=name + "_sum", grid=(rows // tr,),
        in_specs=[pl.BlockSpec((None, tr, width), lambda i: (0, i, 0)),
                  pl.BlockSpec((n_chips - 1, tr, width), lambda i: (0, i, 0))],
        out_specs=pl.BlockSpec((tr, width), lambda i: (i, 0)), out_shape=_sds((rows, width)),
        compiler_params=_params(("parallel",)),
    )(chip_sums, recv)


def _adamw(w, g, m, v, *, name):
    rows, width = w.shape
    tr = _pick(rows, 256, 8)

    def body(w_ref, g_ref, m_ref, v_ref, d_ref, nm_ref, nv_ref):
        gg = g_ref[...]
        m_new = ADAM_B1 * m_ref[...] + (1.0 - ADAM_B1) * gg
        v_new = ADAM_B2 * v_ref[...] + (1.0 - ADAM_B2) * jnp.square(gg)
        m_hat = m_new / (1.0 - ADAM_B1 ** ADAM_STEP)
        v_hat = v_new / (1.0 - ADAM_B2 ** ADAM_STEP)
        d_ref[...] = -ADAM_LR * (m_hat / (jnp.sqrt(v_hat) + ADAM_EPS) + ADAM_WD * w_ref[...])
        nm_ref[...] = m_new
        nv_ref[...] = v_new

    spec = pl.BlockSpec((tr, width), lambda i: (i, 0))
    return pl.pallas_call(
        body, name=name, grid=(rows // tr,), in_specs=[spec] * 4, out_specs=[spec] * 3,
        out_shape=[_sds(w.shape)] * 3, compiler_params=_params(("parallel",)),
    )(w, g, m, v)


KINDS = ("gmlp", "s5", "mla", "gmlp")
LAYER_NAMES = {
    "gmlp": ("norm_g", "w_in", "ln_g", "ln_b", "w_s", "b_s", "w_out"),
    "s5": ("norm_g", "w_in", "a_re", "a_im", "log_step", "b_re", "b_im", "c_re", "c_im", "d_skip", "w_glu", "b_glu",
           "w_out"),
    "mla": ("norm_g", "w_in", "q_norm_g", "w_uq", "kv_norm_g", "w_ukv", "w_out"),
}
COL_SHARDED = ("w_in", "w_uq", "w_ukv")
ROW_SHARDED = ("w_out", "w_glu")
WEIGHT_NAMES = tuple("l%d_%s" % (i, n) for i, k in enumerate(KINDS) for n in LAYER_NAMES[k]) + ("final_norm_g",)


def _is_sharded(name):
    return name.split("_", 1)[1] in COL_SHARDED + ROW_SHARDED


def _flatten(arrs, pad_rows_to):
    parts, sizes = [], []
    for a in arrs:
        flat = a.reshape(-1)
        pad = (-flat.shape[0]) % FLAT_W
        if pad:
            flat = jnp.pad(flat, (0, pad))
        parts.append(flat)
        sizes.append(flat.shape[0] // FLAT_W)
    rows = sum(sizes)
    pad_rows = (-rows) % pad_rows_to
    if pad_rows:
        parts.append(jnp.zeros((pad_rows * FLAT_W,), arrs[0].dtype))
    return jnp.concatenate(parts).reshape(-1, FLAT_W), sizes


def _unflatten(flat, shapes, sizes):
    out, row = [], 0
    for shape, n_rows in zip(shapes, sizes):
        n = int(np.prod(shape))
        out.append(flat[row:row + n_rows].reshape(-1)[:n].reshape(shape))
        row += n_rows
    return out


def _full_from_gathered(blocks, name):
    if name.split("_", 1)[1] in COL_SHARDED:
        return blocks.transpose(1, 0, 2).reshape(blocks.shape[1], -1)
    return blocks.reshape(-1, blocks.shape[2])


def _shards_of(full, name):
    if name.split("_", 1)[1] in COL_SHARDED:
        return full.reshape(full.shape[0], N_DEV, -1).transpose(1, 0, 2)
    return full.reshape(N_DEV, -1, full.shape[1])


def _rope_tables(positions):
    inv_freq = ROPE_THETA ** (-jnp.arange(0, MLA_ROPE, 2, dtype=F32) / MLA_ROPE)
    ang = positions.astype(F32)[:, None] * inv_freq
    cos, sin = jnp.cos(ang), jnp.sin(ang)
    zero = jnp.zeros((positions.shape[0], LANES - MLA_ROPE), F32)
    return jnp.concatenate([cos, cos, zero], axis=1), jnp.concatenate([-sin, sin, zero], axis=1)


def _row(v):
    return v.reshape(1, -1)


def kernel(x, positions, l0_norm_g, l0_w_in, l0_ln_g, l0_ln_b, l0_w_s, l0_b_s, l0_w_out, l1_norm_g, l1_w_in, l1_a_re, l1_a_im, l1_log_step, l1_b_re, l1_b_im, l1_c_re, l1_c_im, l1_d_skip, l1_w_glu, l1_b_glu, l1_w_out, l2_norm_g, l2_w_in, l2_q_norm_g, l2_w_uq, l2_kv_norm_g, l2_w_ukv, l2_w_out, l3_norm_g, l3_w_in, l3_ln_g, l3_ln_b, l3_w_s, l3_b_s, l3_w_out, final_norm_g, loss_target, m_l0_norm_g, m_l0_w_in, m_l0_ln_g, m_l0_ln_b, m_l0_w_s, m_l0_b_s, m_l0_w_out, m_l1_norm_g, m_l1_w_in, m_l1_a_re, m_l1_a_im, m_l1_log_step, m_l1_b_re, m_l1_b_im, m_l1_c_re, m_l1_c_im, m_l1_d_skip, m_l1_w_glu, m_l1_b_glu, m_l1_w_out, m_l2_norm_g, m_l2_w_in, m_l2_q_norm_g, m_l2_w_uq, m_l2_kv_norm_g, m_l2_w_ukv, m_l2_w_out, m_l3_norm_g, m_l3_w_in, m_l3_ln_g, m_l3_ln_b, m_l3_w_s, m_l3_b_s, m_l3_w_out, m_final_norm_g, v_l0_norm_g, v_l0_w_in, v_l0_ln_g, v_l0_ln_b, v_l0_w_s, v_l0_b_s, v_l0_w_out, v_l1_norm_g, v_l1_w_in, v_l1_a_re, v_l1_a_im, v_l1_log_step, v_l1_b_re, v_l1_b_im, v_l1_c_re, v_l1_c_im, v_l1_d_skip, v_l1_w_glu, v_l1_b_glu, v_l1_w_out, v_l2_norm_g, v_l2_w_in, v_l2_q_norm_g, v_l2_w_uq, v_l2_kv_norm_g, v_l2_w_ukv, v_l2_w_out, v_l3_norm_g, v_l3_w_in, v_l3_ln_g, v_l3_ln_b, v_l3_w_s, v_l3_b_s, v_l3_w_out, v_final_norm_g):
    args = locals()
    weights = {n: args[n] for n in WEIGHT_NAMES}
    mom_m = {n: args["m_" + n] for n in WEIGHT_NAMES}
    mom_v = {n: args["v_" + n] for n in WEIGHT_NAMES}
    return _train_step(x, positions, loss_target, weights, mom_m, mom_v)


def _train_step(x, positions, loss_target, weights, mom_m, mom_v):
    big = [n for n in WEIGHT_NAMES if _is_sharded(n)]
    small = [n for n in WEIGHT_NAMES if not _is_sharded(n)]

    w_flat, big_sizes = _flatten([weights[n] for n in big], 8)
    gathered = _all_gather(w_flat.astype(BF16), name="weights_all_gather")
    full, row = {}, 0
    for n, n_rows in zip(big, big_sizes):
        blocks = gathered[:, row:row + n_rows].reshape((N_DEV,) + weights[n].shape)
        full[n] = _full_from_gathered(blocks, n)
        row += n_rows

    layers, ops_vjps = [], {}
    for i, kind in enumerate(KINDS):
        pre = "l%d_" % i
        p = {n: (full[pre + n] if _is_sharded(pre + n) else weights[pre + n]) for n in LAYER_NAMES[kind]}
        p["norm_g"] = _row(p["norm_g"])
        if kind == "gmlp":
            p["ln_g"], p["ln_b"], p["b_st"] = _row(p["ln_g"]), _row(p["ln_b"]), p["b_s"].T
        elif kind == "s5":
            p["d_skip"], p["b_glu"] = _row(p["d_skip"]), _row(p["b_glu"])
            ops, ops_vjps[i] = jax.vjp(_s5_operators, *[p[n] for n in ("a_re", "a_im", "log_step", "b_re", "b_im",
                                                                       "c_re", "c_im")])
            p["ops"] = tuple(o.astype(BF16) for o in ops[:3]) + ops[3:]
        else:
            heads = p["w_uq"].shape[1] // MLA_QK_DIM
            w_in = p["w_in"]
            split = MLA_Q_RANK + MLA_KV_RANK + MLA_ROPE
            p["w_in"] = jnp.concatenate([w_in[:, :split], jnp.zeros((w_in.shape[0], LANES - MLA_ROPE), w_in.dtype),
                                         w_in[:, split:]], axis=1)
            p["w_uq"] = jnp.pad(p["w_uq"].reshape(-1, heads, MLA_QK_DIM),
                                ((0, 0), (0, 0), (0, MLA_HEAD_PAD - MLA_QK_DIM))).reshape(-1, heads * MLA_HEAD_PAD)
            p["q_norm_g"], p["kv_norm_g"] = _row(p["q_norm_g"]), _row(p["kv_norm_g"])
        layers.append(p)
    rope = _rope_tables(positions[0])

    h = x[0]
    saved = []
    for kind, p in zip(KINDS, layers):
        if kind == "gmlp":
            h, s = _gmlp_fwd(h, p)
        elif kind == "s5":
            h, s = _s5_fwd(h, p)
        else:
            h, s = _mla_fwd(h, p, rope)
        saved.append(s)
    loss_local, dh, g_final = _loss_head(h, _row(weights["final_norm_g"]), loss_target[0])
    loss = lax.psum(loss_local, ("x", "y", "c"))

    grads = {"final_norm_g": g_final.reshape(-1)}
    for i in reversed(range(len(KINDS))):
        kind, p = KINDS[i], layers[i]
        if kind == "gmlp":
            dh, g = _gmlp_bwd(dh, p, saved[i])
        elif kind == "s5":
            dh, g = _s5_bwd(dh, p, saved[i], ops_vjps[i])
        else:
            dh, g = _mla_bwd(dh, p, saved[i], rope)
            heads = weights["l%d_w_uq" % i].shape[1] * N_DEV // MLA_QK_DIM
            split = MLA_Q_RANK + MLA_KV_RANK + MLA_ROPE
            g["w_in"] = jnp.concatenate([g["w_in"][:, :split], g["w_in"][:, MLA_Z0:]], axis=1)
            g["w_uq"] = g["w_uq"].reshape(-1, heads, MLA_HEAD_PAD)[:, :, :MLA_QK_DIM].reshape(-1, heads * MLA_QK_DIM)
        for n, val in g.items():
            name = "l%d_%s" % (i, n)
            grads[name] = val.reshape(weights[name].shape) if not _is_sharded(name) else val

    small_flat, small_sizes = _flatten([grads[n] for n in small], 8 * N_DEV)
    small_rows = small_flat.shape[0] // N_DEV
    send_parts = [_shards_of(grads[n], n).reshape(N_DEV, -1, FLAT_W) for n in big]
    send_parts.append(small_flat.reshape(N_DEV, small_rows, FLAT_W))
    send = jnp.concatenate(send_parts, axis=1)
    big_rows = send.shape[1] - small_rows
    pad_rows = (-send.shape[1]) % 8
    if pad_rows:
        send = jnp.pad(send, ((0, 0), (0, pad_rows), (0, 0)))
    reduced = _reduce_scatter(send.astype(BF16), name="grads")
    g_big_flat = reduced[:big_rows]
    g_small_all = _all_gather(reduced[big_rows:big_rows + small_rows], name="small_grads_all_gather")
    g_small_flat = g_small_all.reshape(-1, FLAT_W)

    def flat_of(tree, names, pad_to):
        return _flatten([tree[n] for n in names], pad_to)[0]

    rows_b = g_big_flat.shape[0]
    pad_b = (-rows_b) % 8
    if pad_b:
        g_big_flat = jnp.pad(g_big_flat, ((0, pad_b), (0, 0)))
    d_b, nm_b, nv_b = _adamw(flat_of(weights, big, 8), g_big_flat, flat_of(mom_m, big, 8), flat_of(mom_v, big, 8),
                             name="adamw_sharded")
    d_s, nm_s, nv_s = _adamw(flat_of(weights, small, 8 * N_DEV), g_small_flat, flat_of(mom_m, small, 8 * N_DEV),
                             flat_of(mom_v, small, 8 * N_DEV), name="adamw_replicated")

    big_shapes = [weights[n].shape for n in big]
    small_shapes = [weights[n].shape for n in small]
    outs = {}
    for prefix, fb, fs in (("grad_", g_big_flat, g_small_flat), ("delta_", d_b, d_s), ("new_m_", nm_b, nm_s),
                           ("new_v_", nv_b, nv_s)):
        for n, a in zip(big, _unflatten(fb, big_shapes, big_sizes)):
            outs[prefix + n] = a
        for n, a in zip(small, _unflatten(fs, small_shapes, small_sizes)):
            outs[prefix + n] = a
    result = [loss, dh[None]]
    for prefix in ("grad_", "delta_", "new_m_", "new_v_"):
        result += [outs[prefix + n] for n in WEIGHT_NAMES]
    return tuple(result)
```

```python
import functools
import math

import numpy as np
import jax
import jax.numpy as jnp
from jax import lax
from jax.experimental import pallas as pl
from jax.experimental.pallas import tpu as pltpu

F32 = jnp.float32
BF16 = jnp.bfloat16

NORM_EPS = 1e-6
GMLP_CHUNK = 128
S5_GROUP = 16
S5_STATE = 64
S5_T = 16
MLA_NOPE = 128
MLA_ROPE = 64
MLA_V = 128
MLA_QK_DIM = MLA_NOPE + MLA_ROPE
MLA_Q_RANK = 384
MLA_KV_RANK = 128
MLA_HEAD_PAD = 256
MLA_SCALE = MLA_QK_DIM ** -0.5
ROPE_THETA = 10000.0
NEG_INF = -1e30
ADAM_LR = 0.001
ADAM_B1 = 0.9
ADAM_B2 = 0.999
ADAM_EPS = 1e-08
ADAM_WD = 0.01
ADAM_STEP = 10

N_DEV = 8
LANES = 128
FLAT_W = 1024
VMEM_LIMIT = 56 * 1024 * 1024
MESH = pl.DeviceIdType.MESH


def _pick(dim, pref, align=LANES):
    t = (min(pref, dim) // align) * align
    while t >= align:
        if dim % t == 0:
            return t
        t -= align
    return dim


def _params(sem=None):
    return pltpu.CompilerParams(dimension_semantics=sem, vmem_limit_bytes=VMEM_LIMIT)


_DIMS = {"nn": (((1,), (0,)), ((), ())), "nt": (((1,), (1,)), ((), ())), "tn": (((0,), (0,)), ((), ()))}


def _mm(a, b, *, mode="nn", out_dtype=F32, add=None, name, tm=512, tn=1024, tk=2048):
    if mode == "nn":
        (m, k), (_, n) = a.shape, b.shape
    elif mode == "nt":
        (m, k), (n, _) = a.shape, b.shape
    else:
        (k, m), (_, n) = a.shape, b.shape
    tm, tn, tk = _pick(m, tm, 8), _pick(n, tn), _pick(k, tk)
    nk = k // tk
    dims = _DIMS[mode]

    def body(*refs):
        a_ref, b_ref = refs[:2]
        r_ref = refs[2] if add is not None else None
        o_ref = refs[3] if add is not None else refs[2]
        part = lax.dot_general(a_ref[...].astype(BF16), b_ref[...].astype(BF16), dims, preferred_element_type=F32)

        def finish(res):
            if add is not None:
                res = res + r_ref[...]
            o_ref[...] = res.astype(o_ref.dtype)

        if nk == 1:
            finish(part)
            return
        acc_ref = refs[-1]
        kk = pl.program_id(2)

        @pl.when(kk == 0)
        def _():
            acc_ref[...] = part

        @pl.when(kk > 0)
        def _():
            acc_ref[...] += part

        @pl.when(kk == nk - 1)
        def _():
            finish(acc_ref[...])

    a_spec = (pl.BlockSpec((tk, tm), lambda i, j, kk: (kk, i)) if mode == "tn"
              else pl.BlockSpec((tm, tk), lambda i, j, kk: (i, kk)))
    b_spec = (pl.BlockSpec((tn, tk), lambda i, j, kk: (j, kk)) if mode == "nt"
              else pl.BlockSpec((tk, tn), lambda i, j, kk: (kk, j)))
    in_specs = [a_spec, b_spec]
    args = [a, b]
    if add is not None:
        in_specs.append(pl.BlockSpec((tm, tn), lambda i, j, kk: (i, j)))
        args.append(add)
    return pl.pallas_call(
        body, name=name, grid=(m // tm, n // tn, nk),
        in_specs=in_specs, out_specs=pl.BlockSpec((tm, tn), lambda i, j, kk: (i, j)),
        out_shape=jax.ShapeDtypeStruct((m, n), out_dtype),
        scratch_shapes=[pltpu.VMEM((tm, tn), F32)] if nk > 1 else [],
        compiler_params=_params(("parallel", "parallel", "arbitrary")),
    )(*args)


def _rowwise(fn, tiled, full, out_tiled, out_acc, *, tm, name):
    rows = tiled[0].shape[0]
    tm = _pick(rows, tm, 8)
    nt, nf, no = len(tiled), len(full), len(out_tiled)

    def body(*refs):
        ins = [r[...] for r in refs[:nt + nf]]
        o_refs = refs[nt + nf:nt + nf + no]
        a_refs = refs[nt + nf + no:]
        outs = fn(*ins)
        if not isinstance(outs, (tuple, list)):
            outs = (outs,)
        for r, v in zip(o_refs, outs[:no]):
            r[...] = v.astype(r.dtype)
        if a_refs:
            @pl.when(pl.program_id(0) == 0)
            def _():
                for r in a_refs:
                    r[...] = jnp.zeros_like(r)

            for r, v in zip(a_refs, outs[no:]):
                r[...] += v.astype(r.dtype)

    def whole(shape):
        nd = len(shape)
        return pl.BlockSpec(tuple(shape), lambda i: (0,) * nd)

    in_specs = ([pl.BlockSpec((tm, t.shape[1]), lambda i: (i, 0)) for t in tiled]
                + [whole(f.shape) for f in full])
    out_specs = ([pl.BlockSpec((tm, o.shape[1]), lambda i: (i, 0)) for o in out_tiled]
                 + [whole(o.shape) for o in out_acc])
    outs = pl.pallas_call(
        body, name=name, grid=(rows // tm,), in_specs=in_specs, out_specs=out_specs,
        out_shape=list(out_tiled) + list(out_acc),
        compiler_params=_params(("arbitrary",)),
    )(*tiled, *full)
    return outs


def _sds(shape, dtype=F32):
    return jax.ShapeDtypeStruct(tuple(shape), dtype)


def _rms(x, g):
    return x * lax.rsqrt(jnp.mean(x * x, axis=-1, keepdims=True) + NORM_EPS) * g


def _layernorm(x, g, b):
    mu = jnp.mean(x, axis=-1, keepdims=True)
    xc = x - mu
    var = jnp.mean(xc * xc, axis=-1, keepdims=True)
    return xc * lax.rsqrt(var + NORM_EPS) * g + b


def _dot(a, b, mode="nn"):
    return lax.dot_general(a.astype(BF16), b.astype(BF16), _DIMS[mode], preferred_element_type=F32)


@jax.custom_vjp
def _bdot(a, b):
    return _dot(a, b)


def _bdot_fwd(a, b):
    return _bdot(a, b), (a, b)


def _bdot_bwd(res, ct):
    a, b = res
    return _dot(ct, b, "nt"), _dot(a, ct, "tn")


_bdot.defvjp(_bdot_fwd, _bdot_bwd)


def _rms_fwd(h, g, *, name):
    return _rowwise(lambda x, gg: _rms(x, gg), [h], [g], [_sds(h.shape, BF16)], [], tm=512, name=name)[0]


def _rms_bwd(h, g, d_hn, dh_out, *, name):
    def fn(x, ct, res, gg):
        _, vjp = jax.vjp(_rms, x, gg)
        dx, dg = vjp(ct)
        return res + dx, dg

    dh, dg = _rowwise(fn, [h, d_hn, dh_out], [g], [_sds(h.shape)], [_sds(g.shape)], tm=512, name=name)
    return dh, dg


def _gmlp_mid(uvz, ln_g, ln_b, w_s, b_st):
    di = ln_g.shape[1]
    ng, ck = w_s.shape[0], w_s.shape[1]
    dg = di // ng
    u = jax.nn.gelu(uvz[:, :di])
    v = _layernorm(jax.nn.gelu(uvz[:, di:2 * di]), ln_g, ln_b)
    z = uvz[:, 2 * di:]
    row = lax.broadcasted_iota(jnp.int32, (ck, ck), 0)
    col = lax.broadcasted_iota(jnp.int32, (ck, ck), 1)
    causal = col <= row
    blocks = []
    for c in range(uvz.shape[0] // ck):
        cols = []
        for g in range(ng):
            w = jnp.where(causal, w_s[g], 0.0)
            cols.append(_bdot(w, v[c * ck:(c + 1) * ck, g * dg:(g + 1) * dg]) + b_st[:, g:g + 1])
        blocks.append(jnp.concatenate(cols, axis=1))
    s = blocks[0] if len(blocks) == 1 else jnp.concatenate(blocks, axis=0)
    return u * s * jax.nn.silu(z)


def _gmlp_fwd(h, p):
    hn = _rms_fwd(h, p["norm_g"], name="gmlp_rms")
    uvz = _mm(hn, p["w_in"], name="gmlp_in")
    di = p["ln_g"].shape[1]
    gated = _rowwise(_gmlp_mid, [uvz], [p["ln_g"], p["ln_b"], p["w_s"], p["b_st"]],
                     [_sds((h.shape[0], di), BF16)], [], tm=256, name="gmlp_mid")[0]
    h_next = _mm(gated, p["w_out"], add=h, name="gmlp_out")
    return h_next, (h, hn, uvz, gated)


def _gmlp_bwd(dh_out, p, saved):
    h, hn, uvz, gated = saved
    d_gated = _mm(dh_out, p["w_out"], mode="nt", out_dtype=BF16, name="gmlp_dgated")
    g_w_out = _mm(gated, dh_out, mode="tn", name="gmlp_dwout")

    def fn(t, ct, ln_g, ln_b, w_s, b_st):
        _, vjp = jax.vjp(_gmlp_mid, t, ln_g, ln_b, w_s, b_st)
        return vjp(ct.astype(F32))

    d_uvz, g_ln_g, g_ln_b, g_w_s, g_b_st = _rowwise(
        fn, [uvz, d_gated], [p["ln_g"], p["ln_b"], p["w_s"], p["b_st"]],
        [_sds(uvz.shape, BF16)], [_sds(p["ln_g"].shape), _sds(p["ln_b"].shape), _sds(p["w_s"].shape),
                                  _sds(p["b_st"].shape)], tm=128, name="gmlp_mid_bwd")
    g_w_in = _mm(hn, d_uvz, mode="tn", name="gmlp_dwin")
    d_hn = _mm(d_uvz, p["w_in"], mode="nt", name="gmlp_dhn")
    dh_in, g_norm = _rms_bwd(h, p["norm_g"], d_hn, dh_out, name="gmlp_rms_bwd")
    grads = dict(norm_g=g_norm, w_in=g_w_in, ln_g=g_ln_g, ln_b=g_ln_b, w_s=g_w_s, b_s=g_b_st.T, w_out=g_w_out)
    return dh_in, grads


def _s5_operators(a_re, a_im, log_step, b_re, b_im, c_re, c_im):
    t_len = S5_T
    step = jnp.exp(log_step)[:, None]
    lr, li = a_re * step, a_im * step
    ks = jnp.arange(t_len + 1, dtype=F32)[:, None, None]
    mag = jnp.exp(ks * lr)
    pw_r, pw_i = mag * jnp.cos(ks * li), mag * jnp.sin(ks * li)
    nr, ni = pw_r[1] - 1.0, pw_i[1]
    den = a_re * a_re + a_im * a_im
    f_r, f_i = (nr * a_re + ni * a_im) / den, (ni * a_re - nr * a_im) / den
    bb_r = f_r[..., None] * b_re - f_i[..., None] * b_im
    bb_i = f_r[..., None] * b_im + f_i[..., None] * b_re
    hi = lax.Precision.HIGHEST
    cp_r = c_re[None] * pw_r[:, :, None, :] - c_im[None] * pw_i[:, :, None, :]
    cp_i = c_re[None] * pw_i[:, :, None, :] + c_im[None] * pw_r[:, :, None, :]
    kern = (jnp.einsum("gpi,kghp->kgih", bb_r, cp_r[:t_len], precision=hi)
            - jnp.einsum("gpi,kghp->kgih", bb_i, cp_i[:t_len], precision=hi))
    n_g, n_h = a_re.shape[0], b_re.shape[2]
    kcat = kern.transpose(1, 2, 0, 3).reshape(n_g, n_h, t_len * n_h)
    toep = jnp.stack([jnp.pad(kcat[:, :, :(t_len - s) * n_h], ((0, 0), (0, 0), (s * n_h, 0)))
                      for s in range(t_len)], axis=1).reshape(n_g, t_len * n_h, t_len * n_h)
    rev_r, rev_i = pw_r[t_len - 1::-1][:t_len], pw_i[t_len - 1::-1][:t_len]
    we_r = rev_r[..., None] * bb_r[None] - rev_i[..., None] * bb_i[None]
    we_i = rev_r[..., None] * bb_i[None] + rev_i[..., None] * bb_r[None]
    wend = jnp.concatenate([we_r, we_i], axis=2).transpose(1, 0, 3, 2).reshape(n_g, t_len * n_h, -1)
    wo = jnp.concatenate([cp_r[1:], -cp_i[1:]], axis=3)
    wout = wo.transpose(1, 3, 0, 2).reshape(n_g, -1, t_len * n_h)
    a_r, a_i = pw_r[t_len], pw_i[t_len]
    a1 = jnp.concatenate([a_r, a_r], axis=1)
    a2 = jnp.concatenate([-a_i, a_i], axis=1)
    return toep, wend, wout, a1, a2


def _group_call(body, ins, outs, *, gb, name):
    n_g = ins[0].shape[0]

    def spec(a):
        return pl.BlockSpec((gb,) + tuple(a.shape[1:]), lambda i: (i, 0, 0))

    return pl.pallas_call(
        body, name=name, grid=(n_g // gb,), in_specs=[spec(a) for a in ins],
        out_specs=[spec(o) for o in outs], out_shape=list(outs),
        compiler_params=_params(("parallel",)),
    )(*ins)


def _s5_states(u_g, wend, *, gb=8):
    def body(u_ref, w_ref, s_ref):
        for g in range(gb):
            s_ref[g] = _dot(u_ref[g], w_ref[g])

    n_g, n_c = u_g.shape[0], u_g.shape[1]
    return _group_call(body, [u_g, wend], [_sds((n_g, n_c, wend.shape[2]))], gb=gb, name="s5_states")[0]


def _s5_outputs(u_g, toep, xprev, wout, *, gb=8):
    def body(u_ref, t_ref, x_ref, w_ref, y_ref):
        for g in range(gb):
            y_ref[g] = _dot(u_ref[g], t_ref[g]) + _dot(x_ref[g], w_ref[g])

    return _group_call(body, [u_g, toep, xprev, wout], [_sds(u_g.shape)], gb=gb, name="s5_outputs")[0]


def _s5_outputs_bwd(u_g, d_y, xprev, wout, *, gb=8):
    def body(u_ref, dy_ref, x_ref, w_ref, dt_ref, dw_ref, dx_ref):
        for g in range(gb):
            dy = dy_ref[g]
            dt_ref[g] = _dot(u_ref[g], dy, "tn")
            dw_ref[g] = _dot(x_ref[g], dy, "tn")
            dx_ref[g] = _dot(dy, w_ref[g], "nt")

    n_g, n_c, n_k = u_g.shape
    return _group_call(body, [u_g, d_y, xprev, wout],
                       [_sds((n_g, n_k, n_k)), _sds(wout.shape), _sds(xprev.shape)], gb=gb, name="s5_outputs_bwd")


def _s5_inputs_bwd(u_g, d_y, d_s, toep, wend, *, gb=8):
    def body(u_ref, dy_ref, ds_ref, t_ref, w_ref, du_ref, dw_ref):
        for g in range(gb):
            ds = ds_ref[g]
            du_ref[g] = _dot(dy_ref[g], t_ref[g], "nt") + _dot(ds, w_ref[g], "nt")
            dw_ref[g] = _dot(u_ref[g], ds, "tn")

    return _group_call(body, [u_g, d_y, d_s, toep, wend], [_sds(u_g.shape), _sds(wend.shape)], gb=gb,
                       name="s5_inputs_bwd")


def _swap_halves(x):
    return pltpu.roll(x, x.shape[-1] // 2, axis=x.ndim - 1)


def _s5_scan(s_t, a1, a2, *, gb=32):
    n_c, n_g, n_p = s_t.shape
    gb = min(gb, n_g)

    def body(s_ref, a1_ref, a2_ref, x_ref):
        a1v, a2v = a1_ref[...], a2_ref[...]
        a2s = _swap_halves(a2v)

        def step(c, carry):
            x, xs = carry
            x_ref[c] = x
            s = s_ref[c]
            return x * a1v + xs * a2v + s, xs * a1v + x * a2s + _swap_halves(s)

        zero = jnp.zeros((gb, n_p), F32)
        lax.fori_loop(0, n_c, step, (zero, zero))

    return pl.pallas_call(
        body, name="s5_scan", grid=(n_g // gb,),
        in_specs=[pl.BlockSpec((n_c, gb, n_p), lambda i: (0, i, 0)), pl.BlockSpec((gb, n_p), lambda i: (i, 0)),
                  pl.BlockSpec((gb, n_p), lambda i: (i, 0))],
        out_specs=pl.BlockSpec((n_c, gb, n_p), lambda i: (0, i, 0)), out_shape=_sds(s_t.shape),
        compiler_params=_params(("parallel",)),
    )(s_t, a1, a2)


def _s5_scan_bwd(d_xprev_t, xprev_t, a1, a2, *, gb=32):
    n_c, n_g, n_p = xprev_t.shape
    gb = min(gb, n_g)

    def body(dx_ref, x_ref, a1_ref, a2_ref, ds_ref, p1_ref, p2_ref):
        a1v, a2v = a1_ref[...], a2_ref[...]
        a2s = _swap_halves(a2v)
        zero = jnp.zeros((gb, n_p), F32)
        ds_ref[n_c - 1] = zero

        def step(k, carry):
            gx_next, gs_next, p1, p2 = carry
            c = n_c - 2 - k
            xp = x_ref[c + 1]
            d = dx_ref[c + 1]
            gx = d + gx_next * a1v - gs_next * a2v
            gs = _swap_halves(d) + gs_next * a1v - gx_next * a2s
            ds_ref[c] = gx
            return gx, gs, p1 + gx_next * xp, p2 + gx_next * _swap_halves(xp)

        _, _, p1, p2 = lax.fori_loop(0, n_c - 1, step, (zero, zero, zero, zero))
        p1_ref[...] = p1
        p2_ref[...] = p2

    blk = pl.BlockSpec((n_c, gb, n_p), lambda i: (0, i, 0))
    vec = pl.BlockSpec((gb, n_p), lambda i: (i, 0))
    return pl.pallas_call(
        body, name="s5_scan_bwd", grid=(n_g // gb,), in_specs=[blk, blk, vec, vec],
        out_specs=[blk, vec, vec], out_shape=[_sds(xprev_t.shape), _sds(a1.shape), _sds(a1.shape)],
        compiler_params=_params(("parallel",)),
    )(d_xprev_t, xprev_t, a1, a2)


GROUPS_PER_TILE = LANES // S5_GROUP


def _to_groups(t, n_g):
    n_l = t.shape[0]
    n_c = n_l // S5_T
    gpt = min(GROUPS_PER_TILE, n_g)
    width = gpt * S5_GROUP

    def body(x_ref, o_ref):
        tr = [x_ref[pl.ds(s, n_c, stride=S5_T), :].T for s in range(S5_T)]
        for gl in range(gpt):
            rows = slice(gl * S5_GROUP, (gl + 1) * S5_GROUP)
            stacked = jnp.concatenate([tr[s][rows, :] for s in range(S5_T)], axis=0)
            o_ref[gl] = stacked.T.astype(o_ref.dtype)

    return pl.pallas_call(
        body, name="s5_to_groups", grid=(n_g // gpt,),
        in_specs=[pl.BlockSpec((n_l, width), lambda b: (0, b))],
        out_specs=pl.BlockSpec((gpt, n_c, S5_T * S5_GROUP), lambda b: (b, 0, 0)),
        out_shape=_sds((n_g, n_c, S5_T * S5_GROUP), BF16), compiler_params=_params(("parallel",)),
    )(t)


def _from_groups(t, n_l):
    n_g, n_c = t.shape[0], t.shape[1]
    gpt = min(GROUPS_PER_TILE, n_g)
    width = gpt * S5_GROUP

    def body(y_ref, o_ref):
        ytr = [y_ref[gl].T for gl in range(gpt)]
        for s in range(S5_T):
            rows = slice(s * S5_GROUP, (s + 1) * S5_GROUP)
            piece = jnp.concatenate([ytr[gl][rows, :] for gl in range(gpt)], axis=0)
            o_ref[pl.ds(s, n_c, stride=S5_T), :] = piece.T

    return pl.pallas_call(
        body, name="s5_from_groups", grid=(n_g // gpt,),
        in_specs=[pl.BlockSpec((gpt, n_c, S5_T * S5_GROUP), lambda b: (b, 0, 0))],
        out_specs=pl.BlockSpec((n_l, width), lambda b: (0, b)),
        out_shape=_sds((n_l, n_g * S5_GROUP)), compiler_params=_params(("parallel",)),
    )(t)


def _s5_act(ys, uz, d_skip):
    di = d_skip.shape[1]
    return jax.nn.gelu(ys + d_skip * uz[:, :di])


def _s5_gate(g1, glu_pre, uz, b_glu):
    di = b_glu.shape[1]
    return g1 * jax.nn.sigmoid(glu_pre + b_glu) * jax.nn.silu(uz[:, di:])


def _s5_fwd(h, p):
    n_l = h.shape[0]
    di = p["d_skip"].shape[1]
    n_g = di // S5_GROUP
    hn = _rms_fwd(h, p["norm_g"], name="s5_rms")
    uz = _mm(hn, p["w_in"], name="s5_in")
    toep, wend, wout, a1, a2 = p["ops"]
    u_g = _to_groups(uz, n_g)
    s = _s5_states(u_g, wend)
    xprev = _s5_scan(s.transpose(1, 0, 2), a1, a2).transpose(1, 0, 2)
    ys = _from_groups(_s5_outputs(u_g, toep, xprev, wout), n_l)
    g1 = _rowwise(_s5_act, [ys, uz], [p["d_skip"]], [_sds((n_l, di), BF16)], [], tm=512, name="s5_act")[0]
    glu_pre = _mm(g1, p["w_glu"], name="s5_glu")

    def gate(ys_t, pre_t, uz_t, d_skip, b_glu):
        return _s5_gate(_s5_act(ys_t, uz_t, d_skip), pre_t, uz_t, b_glu)

    gated = _rowwise(gate, [ys, glu_pre, uz], [p["d_skip"], p["b_glu"]], [_sds((n_l, di), BF16)], [],
                     tm=512, name="s5_gate")[0]
    h_next = _mm(gated, p["w_out"], add=h, name="s5_out")
    return h_next, (h, hn, uz, u_g, xprev, ys, g1, glu_pre, gated)


def _s5_bwd(dh_out, p, saved, ops_vjp):
    h, hn, uz, u_g, xprev, ys, g1, glu_pre, gated = saved
    n_l = h.shape[0]
    di = p["d_skip"].shape[1]
    n_g = di // S5_GROUP
    toep, wend, wout, a1, a2 = p["ops"]
    d_gated = _mm(dh_out, p["w_out"], mode="nt", out_dtype=BF16, name="s5_dgated")
    g_w_out = _mm(gated, dh_out, mode="tn", name="s5_dwout")

    def gate_bwd(ys_t, pre_t, uz_t, ct, d_skip, b_glu):
        g1_t = _s5_act(ys_t, uz_t, d_skip)
        _, vjp = jax.vjp(_s5_gate, g1_t, pre_t, uz_t, b_glu)
        d_g1, d_pre, d_uz, d_b = vjp(ct.astype(F32))
        return d_g1, d_pre, d_uz, d_b

    d_g1_direct, d_pre, d_uz_gate, g_b_glu = _rowwise(
        gate_bwd, [ys, glu_pre, uz, d_gated], [p["d_skip"], p["b_glu"]],
        [_sds((n_l, di)), _sds((n_l, di), BF16), _sds(uz.shape)], [_sds(p["b_glu"].shape)], tm=256, name="s5_gate_bwd")
    g_w_glu = _mm(g1, d_pre, mode="tn", name="s5_dwglu")
    d_g1 = _mm(d_pre, p["w_glu"], mode="nt", add=d_g1_direct, name="s5_dg1")

    def act_bwd(ys_t, uz_t, ct, d_uz_t, d_skip):
        _, vjp = jax.vjp(_s5_act, ys_t, uz_t, d_skip)
        d_ys, d_uz, d_d = vjp(ct)
        return d_ys, d_uz + d_uz_t, d_d

    d_ys, d_uz_part, g_d_skip = _rowwise(
        act_bwd, [ys, uz, d_g1, d_uz_gate], [p["d_skip"]], [_sds((n_l, di)), _sds(uz.shape)],
        [_sds(p["d_skip"].shape)], tm=256, name="s5_act_bwd")
    d_y = _to_groups(d_ys, n_g)
    d_toep, d_wout, d_xprev = _s5_outputs_bwd(u_g, d_y, xprev, wout)
    d_s_t, p1, p2 = _s5_scan_bwd(d_xprev.transpose(1, 0, 2), xprev.transpose(1, 0, 2), a1, a2)
    d_s = d_s_t.transpose(1, 0, 2)
    d_u_g, d_wend = _s5_inputs_bwd(u_g, d_y, d_s, toep, wend)
    d_u = _from_groups(d_u_g, n_l)
    d_uz = _rowwise(lambda part, du: jnp.concatenate([part[:, :di] + du, part[:, di:]], axis=1),
                    [d_uz_part, d_u], [], [_sds(uz.shape, BF16)], [], tm=512, name="s5_duz")[0]
    g_w_in = _mm(hn, d_uz, mode="tn", name="s5_dwin")
    d_hn = _mm(d_uz, p["w_in"], mode="nt", name="s5_dhn")
    dh_in, g_norm = _rms_bwd(h, p["norm_g"], d_hn, dh_out, name="s5_rms_bwd")
    g_ops = ops_vjp((d_toep, d_wend, d_wout, p1, p2))
    grads = dict(norm_g=g_norm, w_in=g_w_in, a_re=g_ops[0], a_im=g_ops[1], log_step=g_ops[2], b_re=g_ops[3],
                 b_im=g_ops[4], c_re=g_ops[5], c_im=g_ops[6], d_skip=g_d_skip, w_glu=g_w_glu, b_glu=g_b_glu,
                 w_out=g_w_out)
    return dh_in, grads


MLA_Z0 = MLA_Q_RANK + MLA_KV_RANK + LANES


def _rope_tile(t, cos_t, sin_t):
    q = LANES // 4
    lane = lax.broadcasted_iota(jnp.int32, t.shape, 1)
    swapped = jnp.where(lane < q, pltpu.roll(t, LANES - q, axis=1), pltpu.roll(t, q, axis=1))
    return t * cos_t + swapped * sin_t


def _mla_mid(proj, cos_t, sin_t, q_g, kv_g):
    cqn = _rms(proj[:, :MLA_Q_RANK], q_g)
    ckvn = _rms(proj[:, MLA_Q_RANK:MLA_Q_RANK + MLA_KV_RANK], kv_g)
    kr = _rope_tile(proj[:, MLA_Q_RANK + MLA_KV_RANK:MLA_Z0], cos_t, sin_t)
    return cqn, ckvn, kr


def _mla_rope_q(qp, cos_t, sin_t):
    parts = []
    for hd in range(qp.shape[1] // MLA_HEAD_PAD):
        base = hd * MLA_HEAD_PAD
        parts.append(qp[:, base:base + LANES])
        parts.append(_rope_tile(qp[:, base + LANES:base + MLA_HEAD_PAD], cos_t, sin_t))
    return jnp.concatenate(parts, axis=1)


def _mla_gate(o, proj):
    return o * jax.nn.silu(proj[:, MLA_Z0:])


LOG2E = math.log2(math.e)
SCORE_LOG2 = MLA_SCALE * LOG2E
FLASH_SPLIT = 2


def _causal_pairs(n_blk, kv_major):
    if kv_major:
        pairs = [(i, j) for j in range(n_blk) for i in range(j, n_blk)]
    else:
        pairs = [(i, j) for i in range(n_blk) for j in range(i + 1)]
    return (jnp.asarray([p[0] for p in pairs], jnp.int32), jnp.asarray([p[1] for p in pairs], jnp.int32))


def _raw_scores(q, kcat, row0, diagonal):
    s = _dot(q, kcat, "nt")
    if diagonal:
        qpos = row0 + lax.broadcasted_iota(jnp.int32, s.shape, 0)
        kpos = lax.broadcasted_iota(jnp.int32, s.shape, 1)
        s = jnp.where(kpos <= qpos, s, NEG_INF)
    return s


def _lanes(x, width):
    return jnp.tile(x, (1, width // LANES))


def _flash_fwd(qp, kv, kr, *, blk=512):
    n_l = qp.shape[0]
    heads = qp.shape[1] // MLA_HEAD_PAD
    blk = _pick(n_l, blk)
    n_blk = n_l // blk
    half = blk // FLASH_SPLIT
    qi, kj = _causal_pairs(n_blk, kv_major=False)

    def body(qi_ref, kj_ref, q_ref, kv_ref, kr_ref, o_ref, lse_ref, m_sc, l_sc, acc_sc):
        p = pl.program_id(1)
        i, j = qi_ref[p], kj_ref[p]

        @pl.when(j == 0)
        def _():
            m_sc[...] = jnp.full_like(m_sc, NEG_INF)
            l_sc[...] = jnp.zeros_like(l_sc)
            acc_sc[...] = jnp.zeros_like(acc_sc)

        def update(diagonal):
            kcat = jnp.concatenate([kv_ref[:, :LANES], kr_ref[...]], axis=1)
            v = kv_ref[:, LANES:]
            for r in range(FLASH_SPLIT):
                rows = slice(r * half, (r + 1) * half)
                s = _raw_scores(q_ref[rows, :], kcat, r * half, diagonal)
                m_old = m_sc[rows, :]
                m_new = jnp.maximum(m_old, jnp.max(s, axis=1, keepdims=True))
                alpha = jnp.exp2((m_old - m_new) * SCORE_LOG2)
                pr = jnp.exp2((s - _lanes(m_new, blk)) * SCORE_LOG2)
                l_sc[rows, :] = alpha * l_sc[rows, :] + jnp.sum(pr, axis=1, keepdims=True)
                acc_sc[rows, :] = alpha * acc_sc[rows, :] + _dot(pr, v)
                m_sc[rows, :] = m_new

        @pl.when(j < i)
        def _():
            update(False)

        @pl.when(j == i)
        def _():
            update(True)
            o_ref[...] = acc_sc[...] / l_sc[...]
            lse_ref[...] = m_sc[...] * MLA_SCALE + jnp.log(l_sc[...])

    grid_spec = pltpu.PrefetchScalarGridSpec(
        num_scalar_prefetch=2, grid=(heads, qi.shape[0]),
        in_specs=[pl.BlockSpec((blk, MLA_HEAD_PAD), lambda h, p, qi_r, kj_r: (qi_r[p], h)),
                  pl.BlockSpec((blk, MLA_HEAD_PAD), lambda h, p, qi_r, kj_r: (kj_r[p], h)),
                  pl.BlockSpec((blk, LANES), lambda h, p, qi_r, kj_r: (kj_r[p], 0))],
        out_specs=[pl.BlockSpec((blk, MLA_V), lambda h, p, qi_r, kj_r: (qi_r[p], h)),
                   pl.BlockSpec((None, blk, LANES), lambda h, p, qi_r, kj_r: (h, qi_r[p], 0))],
        scratch_shapes=[pltpu.VMEM((blk, LANES), F32), pltpu.VMEM((blk, LANES), F32), pltpu.VMEM((blk, MLA_V), F32)])
    return pl.pallas_call(
        body, name="mla_flash_fwd", grid_spec=grid_spec,
        out_shape=[_sds((n_l, heads * MLA_V)), _sds((heads, n_l, LANES))],
        compiler_params=_params(("parallel", "arbitrary")),
    )(qi, kj, qp, kv, kr)


def _flash_bwd(qp, kv, kr, d_o, lse, delta, *, blk=512):
    n_l = qp.shape[0]
    heads = qp.shape[1] // MLA_HEAD_PAD
    blk = _pick(n_l, blk)
    n_blk = n_l // blk
    half = blk // FLASH_SPLIT
    qi, kj = _causal_pairs(n_blk, kv_major=True)
    n_pairs = qi.shape[0]

    def body(qi_ref, kj_ref, q_ref, kv_ref, kr_ref, do_ref, lse_ref, dl_ref, dq_ref, dkv_ref, dkr_ref, dk_sc, dv_sc):
        h, p = pl.program_id(0), pl.program_id(1)
        i, j = qi_ref[p], kj_ref[p]

        @pl.when(p == 0)
        def _():
            dq_ref[...] = jnp.zeros_like(dq_ref)

        @pl.when(jnp.logical_and(p == 0, h == 0))
        def _():
            dkr_ref[...] = jnp.zeros_like(dkr_ref)

        @pl.when(i == j)
        def _():
            dk_sc[...] = jnp.zeros_like(dk_sc)
            dv_sc[...] = jnp.zeros_like(dv_sc)

        def update(diagonal):
            kcat = jnp.concatenate([kv_ref[:, :LANES], kr_ref[...]], axis=1)
            v = kv_ref[:, LANES:]
            for r in range(FLASH_SPLIT):
                rows = slice(r * half, (r + 1) * half)
                q_t, do_t = q_ref[rows, :], do_ref[rows, :]
                s = _raw_scores(q_t, kcat, r * half, diagonal)
                pr = jnp.exp2(s * SCORE_LOG2 - _lanes(lse_ref[rows, :] * LOG2E, blk))
                d_p = _dot(do_t, v, "nt")
                d_s = pr * (d_p - _lanes(dl_ref[rows, :], blk))
                dk_sc[...] += _dot(d_s, q_t, "tn")
                dv_sc[...] += _dot(pr, do_t, "tn")
                q_rows = pl.ds(pl.multiple_of(i * blk + r * half, half), half)
                dq_ref[q_rows, :] += _dot(d_s, kcat)

        @pl.when(i > j)
        def _():
            update(False)

        @pl.when(i == j)
        def _():
            update(True)

        @pl.when(i == n_blk - 1)
        def _():
            dk = dk_sc[...] * MLA_SCALE
            dkv_ref[:, :LANES] = dk[:, :LANES].astype(dkv_ref.dtype)
            dkv_ref[:, LANES:] = dv_sc[...].astype(dkv_ref.dtype)
            k_rows = pl.ds(pl.multiple_of(j * blk, blk), blk)
            dkr_ref[k_rows, :] += dk[:, LANES:]

        @pl.when(p == n_pairs - 1)
        def _():
            dq_ref[...] = dq_ref[...] * MLA_SCALE

    at_q = lambda h, p, qi_r, kj_r: (qi_r[p], h)
    at_kv = lambda h, p, qi_r, kj_r: (kj_r[p], h)
    grid_spec = pltpu.PrefetchScalarGridSpec(
        num_scalar_prefetch=2, grid=(heads, n_pairs),
        in_specs=[pl.BlockSpec((blk, MLA_HEAD_PAD), at_q),
                  pl.BlockSpec((blk, MLA_HEAD_PAD), at_kv),
                  pl.BlockSpec((blk, LANES), lambda h, p, qi_r, kj_r: (kj_r[p], 0)),
                  pl.BlockSpec((blk, MLA_V), at_q),
                  pl.BlockSpec((None, blk, LANES), lambda h, p, qi_r, kj_r: (h, qi_r[p], 0)),
                  pl.BlockSpec((blk, LANES), at_q)],
        out_specs=[pl.BlockSpec((n_l, MLA_HEAD_PAD), lambda h, p, qi_r, kj_r: (0, h)),
                   pl.BlockSpec((blk, MLA_HEAD_PAD), at_kv),
                   pl.BlockSpec((n_l, LANES), lambda h, p, qi_r, kj_r: (0, 0))],
        scratch_shapes=[pltpu.VMEM((blk, MLA_HEAD_PAD), F32), pltpu.VMEM((blk, MLA_V), F32)])
    return pl.pallas_call(
        body, name="mla_flash_bwd", grid_spec=grid_spec,
        out_shape=[_sds(qp.shape), _sds(kv.shape, BF16), _sds(kr.shape)],
        compiler_params=_params(("arbitrary", "arbitrary")),
    )(qi, kj, qp, kv, kr, d_o, lse, delta)


def _mla_fwd(h, p, rope):
    n_l = h.shape[0]
    cos_t, sin_t = rope
    hn = _rms_fwd(h, p["norm_g"], name="mla_rms")
    proj = _mm(hn, p["w_in"], name="mla_in", tn=896)
    cqn, ckvn, kr = _rowwise(_mla_mid, [proj, cos_t, sin_t], [p["q_norm_g"], p["kv_norm_g"]],
                             [_sds((n_l, MLA_Q_RANK), BF16), _sds((n_l, MLA_KV_RANK), BF16), _sds((n_l, LANES), BF16)],
                             [], tm=512, name="mla_mid")
    q_raw = _mm(cqn, p["w_uq"], name="mla_uq")
    qp = _rowwise(_mla_rope_q, [q_raw, cos_t, sin_t], [], [_sds(q_raw.shape, BF16)], [], tm=512, name="mla_rope_q")[0]
    kv = _mm(ckvn, p["w_ukv"], out_dtype=BF16, name="mla_ukv")
    o, lse = _flash_fwd(qp, kv, kr)
    gated = _rowwise(_mla_gate, [o, proj], [], [_sds(o.shape, BF16)], [], tm=512, name="mla_gate")[0]
    h_next = _mm(gated, p["w_out"], add=h, name="mla_out")
    return h_next, (h, hn, proj, cqn, ckvn, kr, qp, kv, o, lse, gated)


def _mla_bwd(dh_out, p, saved, rope):
    h, hn, proj, cqn, ckvn, kr, qp, kv, o, lse, gated = saved
    n_l = h.shape[0]
    cos_t, sin_t = rope
    d_gated = _mm(dh_out, p["w_out"], mode="nt", out_dtype=BF16, name="mla_dgated")
    g_w_out = _mm(gated, dh_out, mode="tn", name="mla_dwout")

    def gate_bwd(o_t, proj_t, ct):
        _, vjp = jax.vjp(lambda a, z: a * jax.nn.silu(z), o_t, proj_t[:, MLA_Z0:])
        d_o_t, d_z_t = vjp(ct.astype(F32))
        prod = d_o_t * o_t
        delta = jnp.concatenate(
            [jnp.broadcast_to(jnp.sum(prod[:, hd * MLA_V:(hd + 1) * MLA_V], axis=1, keepdims=True),
                              (prod.shape[0], MLA_V)) for hd in range(prod.shape[1] // MLA_V)], axis=1)
        return d_o_t, d_z_t, delta

    d_o, d_z, delta = _rowwise(gate_bwd, [o, proj, d_gated], [], [_sds(o.shape, BF16), _sds(o.shape), _sds(o.shape)],
                               [], tm=512, name="mla_gate_bwd")
    d_qp, d_kv, d_kr = _flash_bwd(qp, kv, kr, d_o, lse, delta)

    def rope_q_bwd(ct, c_t, s_t):
        return _mla_rope_q(ct, c_t, -s_t)

    d_q_raw = _rowwise(rope_q_bwd, [d_qp, cos_t, sin_t], [], [_sds(d_qp.shape, BF16)], [], tm=512,
                       name="mla_rope_q_bwd")[0]
    g_w_uq = _mm(cqn, d_q_raw, mode="tn", name="mla_dwuq")
    d_cqn = _mm(d_q_raw, p["w_uq"], mode="nt", name="mla_dcqn")
    g_w_ukv = _mm(ckvn, d_kv, mode="tn", name="mla_dwukv")
    d_ckvn = _mm(d_kv, p["w_ukv"], mode="nt", name="mla_dckvn")

    def mid_bwd(proj_t, c_t, s_t, d_cq, d_ckv, d_kr_t, d_z_t, q_g, kv_g):
        _, vjp_q = jax.vjp(_rms, proj_t[:, :MLA_Q_RANK], q_g)
        _, vjp_kv = jax.vjp(_rms, proj_t[:, MLA_Q_RANK:MLA_Q_RANK + MLA_KV_RANK], kv_g)
        d_q_in, d_qg = vjp_q(d_cq)
        d_kv_in, d_kvg = vjp_kv(d_ckv)
        d_kr_in = _rope_tile(d_kr_t, c_t, -s_t)
        return jnp.concatenate([d_q_in, d_kv_in, d_kr_in, d_z_t], axis=1), d_qg, d_kvg

    d_proj, g_q_norm, g_kv_norm = _rowwise(
        mid_bwd, [proj, cos_t, sin_t, d_cqn, d_ckvn, d_kr, d_z], [p["q_norm_g"], p["kv_norm_g"]],
        [_sds(proj.shape, BF16)], [_sds(p["q_norm_g"].shape), _sds(p["kv_norm_g"].shape)], tm=512, name="mla_mid_bwd")
    g_w_in = _mm(hn, d_proj, mode="tn", name="mla_dwin", tn=896)
    d_hn = _mm(d_proj, p["w_in"], mode="nt", name="mla_dhn", tk=896)
    dh_in, g_norm = _rms_bwd(h, p["norm_g"], d_hn, dh_out, name="mla_rms_bwd")
    grads = dict(norm_g=g_norm, w_in=g_w_in, q_norm_g=g_q_norm, w_uq=g_w_uq, kv_norm_g=g_kv_norm, w_ukv=g_w_ukv,
                 w_out=g_w_out)
    return dh_in, grads


def _loss_head(h, g, target):
    def fn(x, t, gg):
        def local(xx, g2):
            err = _rms(xx, g2) - t
            return 0.5 * jnp.sum(jnp.mean(err * err, axis=-1))

        val, (dx, dg) = jax.value_and_grad(local, argnums=(0, 1))(x, gg)
        return dx, jnp.full((1, LANES), val, F32), dg

    dh, loss, dg = _rowwise(fn, [h, target], [g], [_sds(h.shape)], [_sds((1, LANES)), _sds(g.shape)], tm=512,
                            name="loss_head")
    return loss[0, 0], dh, dg


HBM_SPEC = pl.BlockSpec(memory_space=pltpu.HBM)


def _all_gather(shard, *, name):
    def body(x_ref, out_ref, send_sems, recv_sems, local_sem):
        x, y, c = lax.axis_index("x"), lax.axis_index("y"), lax.axis_index("c")
        me, sibling = (x, y, c), (x, y, 1 - c)
        chips = [(1 - x, y), (x, 1 - y), (1 - x, 1 - y)]

        def rows(px, py, pc):
            return out_ref.at[4 * px + 2 * py + pc]

        def copy(k, block, to, src=None):
            return pltpu.make_async_remote_copy(
                src_ref=rows(*block) if src is None else src, dst_ref=rows(*block),
                send_sem=send_sems.at[k], recv_sem=recv_sems.at[k], device_id=to, device_id_type=MESH)

        mine = pltpu.make_async_copy(x_ref, rows(*me), local_sem)
        mine.start()
        first = [copy(0, me, sibling, src=x_ref)]
        first += [copy(1 + j, me, (*chip, c), src=x_ref) for j, chip in enumerate(chips)]
        for cp in first:
            cp.start()
        passed = [copy(4 + j, (*chip, c), sibling) for j, chip in enumerate(chips)]
        for j, chip in enumerate(chips):
            copy(1 + j, (*chip, c), me).wait_recv()
            passed[j].start()
        copy(0, sibling, me).wait_recv()
        for j, chip in enumerate(chips):
            copy(4 + j, (*chip, 1 - c), me).wait_recv()
        for cp in first + passed:
            cp.wait_send()
        mine.wait()

    return pl.pallas_call(
        body, name=name, out_shape=jax.ShapeDtypeStruct((N_DEV,) + shard.shape, shard.dtype),
        in_specs=[HBM_SPEC], out_specs=HBM_SPEC,
        scratch_shapes=[pltpu.SemaphoreType.DMA((7,)), pltpu.SemaphoreType.DMA((7,)), pltpu.SemaphoreType.DMA],
    )(shard)


def _exchange(src, routes, *, name):
    n_routes = len(routes)

    def body(s_ref, out_ref, send_sems, recv_sems):
        x, y, c = lax.axis_index("x"), lax.axis_index("y"), lax.axis_index("c")
        local, remote = [], []
        for k, (flip, block) in enumerate(routes):
            src_blk = s_ref.at[block(x, y, c)]
            if flip == 0:
                local.append(pltpu.make_async_copy(src_blk, out_ref.at[k], send_sems.at[k]))
            else:
                peer = (1 - x if flip & 4 else x, 1 - y if flip & 2 else y, 1 - c if flip & 1 else c)
                remote.append(pltpu.make_async_remote_copy(
                    src_ref=src_blk, dst_ref=out_ref.at[k], send_sem=send_sems.at[k], recv_sem=recv_sems.at[k],
                    device_id=peer, device_id_type=MESH))
        for cp in local + remote:
            cp.start()
        for cp in remote:
            cp.wait_recv()
        for cp in remote:
            cp.wait_send()
        for cp in local:
            cp.wait()

    return pl.pallas_call(
        body, name=name, out_shape=jax.ShapeDtypeStruct((n_routes,) + src.shape[1:], src.dtype),
        in_specs=[HBM_SPEC], out_specs=HBM_SPEC,
        scratch_shapes=[pltpu.SemaphoreType.DMA((n_routes,)), pltpu.SemaphoreType.DMA((n_routes,))],
    )(src)


def _reduce_scatter(send, *, name):
    def chip_block(k, other_core):
        return lambda x, y, c: (4 * (1 - x if k & 2 else x) + 2 * (1 - y if k & 1 else y)
                                + (1 - c if other_core else c))

    n_chips = 4
    pair = _exchange(send, [(1, chip_block(k, True)) for k in range(n_chips)], name=name + "_pair")
    rows, width = send.shape[1], send.shape[2]
    tr = _pick(rows, 256, 8)
    x, y, c = lax.axis_index("x"), lax.axis_index("y"), lax.axis_index("c")
    own_ids = jnp.stack([chip_block(k, False)(x, y, c) for k in range(n_chips)]).astype(jnp.int32)

    def add_body(ids_ref, *refs):
        own_refs, p_ref, o_ref = refs[:n_chips], refs[n_chips], refs[n_chips + 1]
        for k in range(n_chips):
            o_ref[k] = (own_refs[k][...].astype(F32) + p_ref[k].astype(F32)).astype(o_ref.dtype)

    own_spec = lambda k: pl.BlockSpec((None, tr, width), lambda i, ids: (ids[k], i, 0))
    chip_sums = pl.pallas_call(
        add_body, name=name + "_pair_sum",
        grid_spec=pltpu.PrefetchScalarGridSpec(
            num_scalar_prefetch=1, grid=(rows // tr,),
            in_specs=[own_spec(k) for k in range(n_chips)] + [pl.BlockSpec((n_chips, tr, width), lambda i, ids: (0, i, 0))],
            out_specs=pl.BlockSpec((n_chips, tr, width), lambda i, ids: (0, i, 0))),
        out_shape=jax.ShapeDtypeStruct((n_chips, rows, width), send.dtype), compiler_params=_params(("parallel",)),
    )(own_ids, *([send] * n_chips), pair)
    recv = _exchange(chip_sums, [(2 * k, (lambda kk: lambda x, y, c: kk)(k)) for k in range(1, n_chips)],
                     name=name + "_chips")

    def sum_body(q_ref, r_ref, o_ref):
        acc = q_ref[...].astype(F32)
        for k in range(n_chips - 1):
            acc = acc + r_ref[k].astype(F32)
        o_ref[...] = acc

    return pl.pallas_call(
        sum_body, name=name + "_sum", grid=(rows // tr,),
        in_specs=[pl.BlockSpec((None, tr, width), lambda i: (0, i, 0)),
                  pl.BlockSpec((n_chips - 1, tr, width), lambda i: (0, i, 0))],
        out_specs=pl.BlockSpec((tr, width), lambda i: (i, 0)), out_shape=_sds((rows, width)),
        compiler_params=_params(("parallel",)),
    )(chip_sums, recv)


def _adamw(w, g, m, v, *, name):
    rows, width = w.shape
    tr = _pick(rows, 256, 8)

    def body(w_ref, g_ref, m_ref, v_ref, d_ref, nm_ref, nv_ref):
        gg = g_ref[...]
        m_new = ADAM_B1 * m_ref[...] + (1.0 - ADAM_B1) * gg
        v_new = ADAM_B2 * v_ref[...] + (1.0 - ADAM_B2) * jnp.square(gg)
        m_hat = m_new / (1.0 - ADAM_B1 ** ADAM_STEP)
        v_hat = v_new / (1.0 - ADAM_B2 ** ADAM_STEP)
        d_ref[...] = -ADAM_LR * (m_hat / (jnp.sqrt(v_hat) + ADAM_EPS) + ADAM_WD * w_ref[...])
        nm_ref[...] = m_new
        nv_ref[...] = v_new

    spec = pl.BlockSpec((tr, width), lambda i: (i, 0))
    return pl.pallas_call(
        body, name=name, grid=(rows // tr,), in_specs=[spec] * 4, out_specs=[spec] * 3,
        out_shape=[_sds(w.shape)] * 3, compiler_params=_params(("parallel",)),
    )(w, g, m, v)


KINDS = ("gmlp", "s5", "mla", "gmlp")
LAYER_NAMES = {
    "gmlp": ("norm_g", "w_in", "ln_g", "ln_b", "w_s", "b_s", "w_out"),
    "s5": ("norm_g", "w_in", "a_re", "a_im", "log_step", "b_re", "b_im", "c_re", "c_im", "d_skip", "w_glu", "b_glu",
           "w_out"),
    "mla": ("norm_g", "w_in", "q_norm_g", "w_uq", "kv_norm_g", "w_ukv", "w_out"),
}
COL_SHARDED = ("w_in", "w_uq", "w_ukv")
ROW_SHARDED = ("w_out", "w_glu")
WEIGHT_NAMES = tuple("l%d_%s" % (i, n) for i, k in enumerate(KINDS) for n in LAYER_NAMES[k]) + ("final_norm_g",)


def _is_sharded(name):
    return name.split("_", 1)[1] in COL_SHARDED + ROW_SHARDED


def _flatten(arrs, pad_rows_to):
    parts, sizes = [], []
    for a in arrs:
        flat = a.reshape(-1)
        pad = (-flat.shape[0]) % FLAT_W
        if pad:
            flat = jnp.pad(flat, (0, pad))
        parts.append(flat)
        sizes.append(flat.shape[0] // FLAT_W)
    rows = sum(sizes)
    pad_rows = (-rows) % pad_rows_to
    if pad_rows:
        parts.append(jnp.zeros((pad_rows * FLAT_W,), arrs[0].dtype))
    return jnp.concatenate(parts).reshape(-1, FLAT_W), sizes


def _unflatten(flat, shapes, sizes):
    out, row = [], 0
    for shape, n_rows in zip(shapes, sizes):
        n = int(np.prod(shape))
        out.append(flat[row:row + n_rows].reshape(-1)[:n].reshape(shape))
        row += n_rows
    return out


def _full_from_gathered(blocks, name):
    if name.split("_", 1)[1] in COL_SHARDED:
        return blocks.transpose(1, 0, 2).reshape(blocks.shape[1], -1)
    return blocks.reshape(-1, blocks.shape[2])


def _shards_of(full, name):
    if name.split("_", 1)[1] in COL_SHARDED:
        return full.reshape(full.shape[0], N_DEV, -1).transpose(1, 0, 2)
    return full.reshape(N_DEV, -1, full.shape[1])


def _rope_tables(positions):
    inv_freq = ROPE_THETA ** (-jnp.arange(0, MLA_ROPE, 2, dtype=F32) / MLA_ROPE)
    ang = positions.astype(F32)[:, None] * inv_freq
    cos, sin = jnp.cos(ang), jnp.sin(ang)
    zero = jnp.zeros((positions.shape[0], LANES - MLA_ROPE), F32)
    return jnp.concatenate([cos, cos, zero], axis=1), jnp.concatenate([-sin, sin, zero], axis=1)


def _row(v):
    return v.reshape(1, -1)


def kernel(x, positions, l0_norm_g, l0_w_in, l0_ln_g, l0_ln_b, l0_w_s, l0_b_s, l0_w_out, l1_norm_g, l1_w_in, l1_a_re, l1_a_im, l1_log_step, l1_b_re, l1_b_im, l1_c_re, l1_c_im, l1_d_skip, l1_w_glu, l1_b_glu, l1_w_out, l2_norm_g, l2_w_in, l2_q_norm_g, l2_w_uq, l2_kv_norm_g, l2_w_ukv, l2_w_out, l3_norm_g, l3_w_in, l3_ln_g, l3_ln_b, l3_w_s, l3_b_s, l3_w_out, final_norm_g, loss_target, m_l0_norm_g, m_l0_w_in, m_l0_ln_g, m_l0_ln_b, m_l0_w_s, m_l0_b_s, m_l0_w_out, m_l1_norm_g, m_l1_w_in, m_l1_a_re, m_l1_a_im, m_l1_log_step, m_l1_b_re, m_l1_b_im, m_l1_c_re, m_l1_c_im, m_l1_d_skip, m_l1_w_glu, m_l1_b_glu, m_l1_w_out, m_l2_norm_g, m_l2_w_in, m_l2_q_norm_g, m_l2_w_uq, m_l2_kv_norm_g, m_l2_w_ukv, m_l2_w_out, m_l3_norm_g, m_l3_w_in, m_l3_ln_g, m_l3_ln_b, m_l3_w_s, m_l3_b_s, m_l3_w_out, m_final_norm_g, v_l0_norm_g, v_l0_w_in, v_l0_ln_g, v_l0_ln_b, v_l0_w_s, v_l0_b_s, v_l0_w_out, v_l1_norm_g, v_l1_w_in, v_l1_a_re, v_l1_a_im, v_l1_log_step, v_l1_b_re, v_l1_b_im, v_l1_c_re, v_l1_c_im, v_l1_d_skip, v_l1_w_glu, v_l1_b_glu, v_l1_w_out, v_l2_norm_g, v_l2_w_in, v_l2_q_norm_g, v_l2_w_uq, v_l2_kv_norm_g, v_l2_w_ukv, v_l2_w_out, v_l3_norm_g, v_l3_w_in, v_l3_ln_g, v_l3_ln_b, v_l3_w_s, v_l3_b_s, v_l3_w_out, v_final_norm_g):
    args = locals()
    weights = {n: args[n] for n in WEIGHT_NAMES}
    mom_m = {n: args["m_" + n] for n in WEIGHT_NAMES}
    mom_v = {n: args["v_" + n] for n in WEIGHT_NAMES}
    return _train_step(x, positions, loss_target, weights, mom_m, mom_v)


def _train_step(x, positions, loss_target, weights, mom_m, mom_v):
    big = [n for n in WEIGHT_NAMES if _is_sharded(n)]
    small = [n for n in WEIGHT_NAMES if not _is_sharded(n)]

    w_flat, big_sizes = _flatten([weights[n] for n in big], 8)
    gathered = _all_gather(w_flat.astype(BF16), name="weights_all_gather")
    full, row = {}, 0
    for n, n_rows in zip(big, big_sizes):
        blocks = gathered[:, row:row + n_rows].reshape((N_DEV,) + weights[n].shape)
        full[n] = _full_from_gathered(blocks, n)
        row += n_rows

    layers, ops_vjps = [], {}
    for i, kind in enumerate(KINDS):
        pre = "l%d_" % i
        p = {n: (full[pre + n] if _is_sharded(pre + n) else weights[pre + n]) for n in LAYER_NAMES[kind]}
        p["norm_g"] = _row(p["norm_g"])
        if kind == "gmlp":
            p["ln_g"], p["ln_b"], p["b_st"] = _row(p["ln_g"]), _row(p["ln_b"]), p["b_s"].T
        elif kind == "s5":
            p["d_skip"], p["b_glu"] = _row(p["d_skip"]), _row(p["b_glu"])
            ops, ops_vjps[i] = jax.vjp(_s5_operators, *[p[n] for n in ("a_re", "a_im", "log_step", "b_re", "b_im",
                                                                       "c_re", "c_im")])
            p["ops"] = tuple(o.astype(BF16) for o in ops[:3]) + ops[3:]
        else:
            heads = p["w_uq"].shape[1] // MLA_QK_DIM
            w_in = p["w_in"]
            split = MLA_Q_RANK + MLA_KV_RANK + MLA_ROPE
            p["w_in"] = jnp.concatenate([w_in[:, :split], jnp.zeros((w_in.shape[0], LANES - MLA_ROPE), w_in.dtype),
                                         w_in[:, split:]], axis=1)
            p["w_uq"] = jnp.pad(p["w_uq"].reshape(-1, heads, MLA_QK_DIM),
                                ((0, 0), (0, 0), (0, MLA_HEAD_PAD - MLA_QK_DIM))).reshape(-1, heads * MLA_HEAD_PAD)
            p["q_norm_g"], p["kv_norm_g"] = _row(p["q_norm_g"]), _row(p["kv_norm_g"])
        layers.append(p)
    rope = _rope_tables(positions[0])

    h = x[0]
    saved = []
    for kind, p in zip(KINDS, layers):
        if kind == "gmlp":
            h, s = _gmlp_fwd(h, p)
        elif kind == "s5":
            h, s = _s5_fwd(h, p)
        else:
            h, s = _mla_fwd(h, p, rope)
        saved.append(s)
    loss_local, dh, g_final = _loss_head(h, _row(weights["final_norm_g"]), loss_target[0])
    loss = lax.psum(loss_local, ("x", "y", "c"))

    grads = {"final_norm_g": g_final.reshape(-1)}
    for i in reversed(range(len(KINDS))):
        kind, p = KINDS[i], layers[i]
        if kind == "gmlp":
            dh, g = _gmlp_bwd(dh, p, saved[i])
        elif kind == "s5":
            dh, g = _s5_bwd(dh, p, saved[i], ops_vjps[i])
        else:
            dh, g = _mla_bwd(dh, p, saved[i], rope)
            heads = weights["l%d_w_uq" % i].shape[1] * N_DEV // MLA_QK_DIM
            split = MLA_Q_RANK + MLA_KV_RANK + MLA_ROPE
            g["w_in"] = jnp.concatenate([g["w_in"][:, :split], g["w_in"][:, MLA_Z0:]], axis=1)
            g["w_uq"] = g["w_uq"].reshape(-1, heads, MLA_HEAD_PAD)[:, :, :MLA_QK_DIM].reshape(-1, heads * MLA_QK_DIM)
        for n, val in g.items():
            name = "l%d_%s" % (i, n)
            grads[name] = val.reshape(weights[name].shape) if not _is_sharded(name) else val

    small_flat, small_sizes = _flatten([grads[n] for n in small], 8 * N_DEV)
    small_rows = small_flat.shape[0] // N_DEV
    send_parts = [_shards_of(grads[n], n).reshape(N_DEV, -1, FLAT_W) for n in big]
    send_parts.append(small_flat.reshape(N_DEV, small_rows, FLAT_W))
    send = jnp.concatenate(send_parts, axis=1)
    big_rows = send.shape[1] - small_rows
    pad_rows = (-send.shape[1]) % 8
    if pad_rows:
        send = jnp.pad(send, ((0, 0), (0, pad_rows), (0, 0)))
    reduced = _reduce_scatter(send.astype(BF16), name="grads")
    g_big_flat = reduced[:big_rows]
    g_small_all = _all_gather(reduced[big_rows:big_rows + small_rows], name="small_grads_all_gather")
    g_small_flat = g_small_all.reshape(-1, FLAT_W)

    def flat_of(tree, names, pad_to):
        return _flatten([tree[n] for n in names], pad_to)[0]

    rows_b = g_big_flat.shape[0]
    pad_b = (-rows_b) % 8
    if pad_b:
        g_big_flat = jnp.pad(g_big_flat, ((0, pad_b), (0, 0)))
    d_b, nm_b, nv_b = _adamw(flat_of(weights, big, 8), g_big_flat, flat_of(mom_m, big, 8), flat_of(mom_v, big, 8),
                             name="adamw_sharded")
    d_s, nm_s, nv_s = _adamw(flat_of(weights, small, 8 * N_DEV), g_small_flat, flat_of(mom_m, small, 8 * N_DEV),
                             flat_of(mom_v, small, 8 * N_DEV), name="adamw_replicated")

    big_shapes = [weights[n].shape for n in big]
    small_shapes = [weights[n].shape for n in small]
    outs = {}
    for prefix, fb, fs in (("grad_", g_big_flat, g_small_flat), ("delta_", d_b, d_s), ("new_m_", nm_b, nm_s),
                           ("new_v_", nv_b, nv_s)):
        for n, a in zip(big, _unflatten(fb, big_shapes, big_sizes)):
            outs[prefix + n] = a
        for n, a in zip(small, _unflatten(fs, small_shapes, small_sizes)):
            outs[prefix + n] = a
    result = [loss, dh[None]]
    for prefix in ("grad_", "delta_", "new_m_", "new_v_"):
        result += [outs[prefix + n] for n in WEIGHT_NAMES]
    return tuple(result)
```

```python
import functools
import math

import numpy as np
import jax
import jax.numpy as jnp
from jax import lax
from jax.experimental import pallas as pl
from jax.experimental.pallas import tpu as pltpu

F32 = jnp.float32
BF16 = jnp.bfloat16

NORM_EPS = 1e-6
GMLP_CHUNK = 128
S5_GROUP = 16
S5_STATE = 64
S5_T = 16
MLA_NOPE = 128
MLA_ROPE = 64
MLA_V = 128
MLA_QK_DIM = MLA_NOPE + MLA_ROPE
MLA_Q_RANK = 384
MLA_KV_RANK = 128
MLA_HEAD_PAD = 256
MLA_SCALE = MLA_QK_DIM ** -0.5
ROPE_THETA = 10000.0
NEG_INF = -1e30
ADAM_LR = 0.001
ADAM_B1 = 0.9
ADAM_B2 = 0.999
ADAM_EPS = 1e-08
ADAM_WD = 0.01
ADAM_STEP = 10

N_DEV = 8
LANES = 128
FLAT_W = 1024
VMEM_LIMIT = 56 * 1024 * 1024
MESH = pl.DeviceIdType.MESH


def _pick(dim, pref, align=LANES):
    t = (min(pref, dim) // align) * align
    while t >= align:
        if dim % t == 0:
            return t
        t -= align
    return dim


def _params(sem=None):
    return pltpu.CompilerParams(dimension_semantics=sem, vmem_limit_bytes=VMEM_LIMIT)


_DIMS = {"nn": (((1,), (0,)), ((), ())), "nt": (((1,), (1,)), ((), ())), "tn": (((0,), (0,)), ((), ()))}


def _mm(a, b, *, mode="nn", out_dtype=F32, add=None, name, tm=512, tn=1024, tk=2048):
    if mode == "nn":
        (m, k), (_, n) = a.shape, b.shape
    elif mode == "nt":
        (m, k), (n, _) = a.shape, b.shape
    else:
        (k, m), (_, n) = a.shape, b.shape
    tm, tn, tk = _pick(m, tm, 8), _pick(n, tn), _pick(k, tk)
    nk = k // tk
    dims = _DIMS[mode]

    def body(*refs):
        a_ref, b_ref = refs[:2]
        r_ref = refs[2] if add is not None else None
        o_ref = refs[3] if add is not None else refs[2]
        part = lax.dot_general(a_ref[...].astype(BF16), b_ref[...].astype(BF16), dims, preferred_element_type=F32)

        def finish(res):
            if add is not None:
                res = res + r_ref[...]
            o_ref[...] = res.astype(o_ref.dtype)

        if nk == 1:
            finish(part)
            return
        acc_ref = refs[-1]
        kk = pl.program_id(2)

        @pl.when(kk == 0)
        def _():
            acc_ref[...] = part

        @pl.when(kk > 0)
        def _():
            acc_ref[...] += part

        @pl.when(kk == nk - 1)
        def _():
            finish(acc_ref[...])

    a_spec = (pl.BlockSpec((tk, tm), lambda i, j, kk: (kk, i)) if mode == "tn"
              else pl.BlockSpec((tm, tk), lambda i, j, kk: (i, kk)))
    b_spec = (pl.BlockSpec((tn, tk), lambda i, j, kk: (j, kk)) if mode == "nt"
              else pl.BlockSpec((tk, tn), lambda i, j, kk: (kk, j)))
    in_specs = [a_spec, b_spec]
    args = [a, b]
    if add is not None:
        in_specs.append(pl.BlockSpec((tm, tn), lambda i, j, kk: (i, j)))
        args.append(add)
    return pl.pallas_call(
        body, name=name, grid=(m // tm, n // tn, nk),
        in_specs=in_specs, out_specs=pl.BlockSpec((tm, tn), lambda i, j, kk: (i, j)),
        out_shape=jax.ShapeDtypeStruct((m, n), out_dtype),
        scratch_shapes=[pltpu.VMEM((tm, tn), F32)] if nk > 1 else [],
        compiler_params=_params(("parallel", "parallel", "arbitrary")),
    )(*args)


def _rowwise(fn, tiled, full, out_tiled, out_acc, *, tm, name):
    rows = tiled[0].shape[0]
    tm = _pick(rows, tm, 8)
    nt, nf, no = len(tiled), len(full), len(out_tiled)

    def body(*refs):
        ins = [r[...] for r in refs[:nt + nf]]
        o_refs = refs[nt + nf:nt + nf + no]
        a_refs = refs[nt + nf + no:]
        outs = fn(*ins)
        if not isinstance(outs, (tuple, list)):
            outs = (outs,)
        for r, v in zip(o_refs, outs[:no]):
            r[...] = v.astype(r.dtype)
        if a_refs:
            @pl.when(pl.program_id(0) == 0)
            def _():
                for r in a_refs:
                    r[...] = jnp.zeros_like(r)

            for r, v in zip(a_refs, outs[no:]):
                r[...] += v.astype(r.dtype)

    def whole(shape):
        nd = len(shape)
        return pl.BlockSpec(tuple(shape), lambda i: (0,) * nd)

    in_specs = ([pl.BlockSpec((tm, t.shape[1]), lambda i: (i, 0)) for t in tiled]
                + [whole(f.shape) for f in full])
    out_specs = ([pl.BlockSpec((tm, o.shape[1]), lambda i: (i, 0)) for o in out_tiled]
                 + [whole(o.shape) for o in out_acc])
    outs = pl.pallas_call(
        body, name=name, grid=(rows // tm,), in_specs=in_specs, out_specs=out_specs,
        out_shape=list(out_tiled) + list(out_acc),
        compiler_params=_params(("arbitrary",)),
    )(*tiled, *full)
    return outs


def _sds(shape, dtype=F32):
    return jax.ShapeDtypeStruct(tuple(shape), dtype)


def _rms(x, g):
    return x * lax.rsqrt(jnp.mean(x * x, axis=-1, keepdims=True) + NORM_EPS) * g


def _layernorm(x, g, b):
    mu = jnp.mean(x, axis=-1, keepdims=True)
    xc = x - mu
    var = jnp.mean(xc * xc, axis=-1, keepdims=True)
    return xc * lax.rsqrt(var + NORM_EPS) * g + b


def _dot(a, b, mode="nn"):
    return lax.dot_general(a.astype(BF16), b.astype(BF16), _DIMS[mode], preferred_element_type=F32)


@jax.custom_vjp
def _bdot(a, b):
    return _dot(a, b)


def _bdot_fwd(a, b):
    return _bdot(a, b), (a, b)


def _bdot_bwd(res, ct):
    a, b = res
    return _dot(ct, b, "nt"), _dot(a, ct, "tn")


_bdot.defvjp(_bdot_fwd, _bdot_bwd)


def _rms_fwd(h, g, *, name):
    return _rowwise(lambda x, gg: _rms(x, gg), [h], [g], [_sds(h.shape, BF16)], [], tm=512, name=name)[0]


def _rms_bwd(h, g, d_hn, dh_out, *, name):
    def fn(x, ct, res, gg):
        _, vjp = jax.vjp(_rms, x, gg)
        dx, dg = vjp(ct)
        return res + dx, dg

    dh, dg = _rowwise(fn, [h, d_hn, dh_out], [g], [_sds(h.shape)], [_sds(g.shape)], tm=512, name=name)
    return dh, dg


def _gmlp_mid(uvz, ln_g, ln_b, w_s, b_st):
    di = ln_g.shape[1]
    ng, ck = w_s.shape[0], w_s.shape[1]
    dg = di // ng
    u = jax.nn.gelu(uvz[:, :di])
    v = _layernorm(jax.nn.gelu(uvz[:, di:2 * di]), ln_g, ln_b)
    z = uvz[:, 2 * di:]
    row = lax.broadcasted_iota(jnp.int32, (ck, ck), 0)
    col = lax.broadcasted_iota(jnp.int32, (ck, ck), 1)
    causal = col <= row
    blocks = []
    for c in range(uvz.shape[0] // ck):
        cols = []
        for g in range(ng):
            w = jnp.where(causal, w_s[g], 0.0)
            cols.append(_bdot(w, v[c * ck:(c + 1) * ck, g * dg:(g + 1) * dg]) + b_st[:, g:g + 1])
        blocks.append(jnp.concatenate(cols, axis=1))
    s = blocks[0] if len(blocks) == 1 else jnp.concatenate(blocks, axis=0)
    return u * s * jax.nn.silu(z)


def _gmlp_fwd(h, p):
    hn = _rms_fwd(h, p["norm_g"], name="gmlp_rms")
    uvz = _mm(hn, p["w_in"], name="gmlp_in")
    di = p["ln_g"].shape[1]
    gated = _rowwise(_gmlp_mid, [uvz], [p["ln_g"], p["ln_b"], p["w_s"], p["b_st"]],
                     [_sds((h.shape[0], di), BF16)], [], tm=256, name="gmlp_mid")[0]
    h_next = _mm(gated, p["w_out"], add=h, name="gmlp_out")
    return h_next, (h, hn, uvz, gated)


def _gmlp_bwd(dh_out, p, saved):
    h, hn, uvz, gated = saved
    d_gated = _mm(dh_out, p["w_out"], mode="nt", out_dtype=BF16, name="gmlp_dgated")
    g_w_out = _mm(gated, dh_out, mode="tn", name="gmlp_dwout")

    def fn(t, ct, ln_g, ln_b, w_s, b_st):
        _, vjp = jax.vjp(_gmlp_mid, t, ln_g, ln_b, w_s, b_st)
        return vjp(ct.astype(F32))

    d_uvz, g_ln_g, g_ln_b, g_w_s, g_b_st = _rowwise(
        fn, [uvz, d_gated], [p["ln_g"], p["ln_b"], p["w_s"], p["b_st"]],
        [_sds(uvz.shape, BF16)], [_sds(p["ln_g"].shape), _sds(p["ln_b"].shape), _sds(p["w_s"].shape),
                                  _sds(p["b_st"].shape)], tm=128, name="gmlp_mid_bwd")
    g_w_in = _mm(hn, d_uvz, mode="tn", name="gmlp_dwin")
    d_hn = _mm(d_uvz, p["w_in"], mode="nt", name="gmlp_dhn")
    dh_in, g_norm = _rms_bwd(h, p["norm_g"], d_hn, dh_out, name="gmlp_rms_bwd")
    grads = dict(norm_g=g_norm, w_in=g_w_in, ln_g=g_ln_g, ln_b=g_ln_b, w_s=g_w_s, b_s=g_b_st.T, w_out=g_w_out)
    return dh_in, grads


def _s5_operators(a_re, a_im, log_step, b_re, b_im, c_re, c_im):
    t_len = S5_T
    step = jnp.exp(log_step)[:, None]
    lr, li = a_re * step, a_im * step
    ks = jnp.arange(t_len + 1, dtype=F32)[:, None, None]
    mag = jnp.exp(ks * lr)
    pw_r, pw_i = mag * jnp.cos(ks * li), mag * jnp.sin(ks * li)
    nr, ni = pw_r[1] - 1.0, pw_i[1]
    den = a_re * a_re + a_im * a_im
    f_r, f_i = (nr * a_re + ni * a_im) / den, (ni * a_re - nr * a_im) / den
    bb_r = f_r[..., None] * b_re - f_i[..., None] * b_im
    bb_i = f_r[..., None] * b_im + f_i[..., None] * b_re
    hi = lax.Precision.HIGHEST
    cp_r = c_re[None] * pw_r[:, :, None, :] - c_im[None] * pw_i[:, :, None, :]
    cp_i = c_re[None] * pw_i[:, :, None, :] + c_im[None] * pw_r[:, :, None, :]
    n_g, n_h = a_re.shape[0], b_re.shape[2]
    lhs = jnp.concatenate([bb_r, -bb_i], axis=1)
    rhs = jnp.concatenate([cp_r[:t_len], cp_i[:t_len]], axis=3)
    rhs = rhs.transpose(1, 3, 0, 2).reshape(n_g, -1, t_len * n_h)
    kcat = jnp.einsum("gqi,gqn->gin", lhs, rhs, precision=hi)
    toep = jnp.stack([jnp.pad(kcat[:, :, :(t_len - s) * n_h], ((0, 0), (0, 0), (s * n_h, 0)))
                      for s in range(t_len)], axis=1).reshape(n_g, t_len * n_h, t_len * n_h)
    rev_r, rev_i = pw_r[t_len - 1::-1][:t_len], pw_i[t_len - 1::-1][:t_len]
    we_r = rev_r[..., None] * bb_r[None] - rev_i[..., None] * bb_i[None]
    we_i = rev_r[..., None] * bb_i[None] + rev_i[..., None] * bb_r[None]
    wend = jnp.concatenate([we_r, we_i], axis=2).transpose(1, 0, 3, 2).reshape(n_g, t_len * n_h, -1)
    wo = jnp.concatenate([cp_r[1:], -cp_i[1:]], axis=3)
    wout = wo.transpose(1, 3, 0, 2).reshape(n_g, -1, t_len * n_h)
    a_r, a_i = pw_r[t_len], pw_i[t_len]
    a1 = jnp.concatenate([a_r, a_r], axis=1)
    a2 = jnp.concatenate([-a_i, a_i], axis=1)
    return toep, wend, wout, a1, a2


def _group_call(body, ins, outs, *, gb, name):
    n_g = ins[0].shape[0]

    def spec(a):
        return pl.BlockSpec((gb,) + tuple(a.shape[1:]), lambda i: (i, 0, 0))

    return pl.pallas_call(
        body, name=name, grid=(n_g // gb,), in_specs=[spec(a) for a in ins],
        out_specs=[spec(o) for o in outs], out_shape=list(outs),
        compiler_params=_params(("parallel",)),
    )(*ins)


def _s5_states(u_g, wend, *, gb=8):
    def body(u_ref, w_ref, s_ref):
        for g in range(gb):
            s_ref[g] = _dot(u_ref[g], w_ref[g])

    n_g, n_c = u_g.shape[0], u_g.shape[1]
    return _group_call(body, [u_g, wend], [_sds((n_g, n_c, wend.shape[2]))], gb=gb, name="s5_states")[0]


def _s5_outputs(u_g, toep, xprev, wout, *, gb=8):
    def body(u_ref, t_ref, x_ref, w_ref, y_ref):
        for g in range(gb):
            y_ref[g] = _dot(u_ref[g], t_ref[g]) + _dot(x_ref[g], w_ref[g])

    return _group_call(body, [u_g, toep, xprev, wout], [_sds(u_g.shape)], gb=gb, name="s5_outputs")[0]


def _s5_outputs_bwd(u_g, d_y, xprev, wout, *, gb=8):
    def body(u_ref, dy_ref, x_ref, w_ref, dt_ref, dw_ref, dx_ref):
        for g in range(gb):
            dy = dy_ref[g]
            dt_ref[g] = _dot(u_ref[g], dy, "tn")
            dw_ref[g] = _dot(x_ref[g], dy, "tn")
            dx_ref[g] = _dot(dy, w_ref[g], "nt")

    n_g, n_c, n_k = u_g.shape
    return _group_call(body, [u_g, d_y, xprev, wout],
                       [_sds((n_g, n_k, n_k)), _sds(wout.shape), _sds(xprev.shape)], gb=gb, name="s5_outputs_bwd")


def _s5_inputs_bwd(u_g, d_y, d_s, toep, wend, *, gb=8):
    def body(u_ref, dy_ref, ds_ref, t_ref, w_ref, du_ref, dw_ref):
        for g in range(gb):
            ds = ds_ref[g]
            du_ref[g] = _dot(dy_ref[g], t_ref[g], "nt") + _dot(ds, w_ref[g], "nt")
            dw_ref[g] = _dot(u_ref[g], ds, "tn")

    return _group_call(body, [u_g, d_y, d_s, toep, wend], [_sds(u_g.shape), _sds(wend.shape)], gb=gb,
                       name="s5_inputs_bwd")


def _swap_halves(x):
    return pltpu.roll(x, x.shape[-1] // 2, axis=x.ndim - 1)


def _s5_scan(s_t, a1, a2, *, gb=32):
    n_c, n_g, n_p = s_t.shape
    gb = min(gb, n_g)

    def body(s_ref, a1_ref, a2_ref, x_ref):
        a1v, a2v = a1_ref[...], a2_ref[...]
        a2s = _swap_halves(a2v)

        def step(c, carry):
            x, xs = carry
            x_ref[c] = x
            s = s_ref[c]
            return x * a1v + xs * a2v + s, xs * a1v + x * a2s + _swap_halves(s)

        zero = jnp.zeros((gb, n_p), F32)
        lax.fori_loop(0, n_c, step, (zero, zero))

    return pl.pallas_call(
        body, name="s5_scan", grid=(n_g // gb,),
        in_specs=[pl.BlockSpec((n_c, gb, n_p), lambda i: (0, i, 0)), pl.BlockSpec((gb, n_p), lambda i: (i, 0)),
                  pl.BlockSpec((gb, n_p), lambda i: (i, 0))],
        out_specs=pl.BlockSpec((n_c, gb, n_p), lambda i: (0, i, 0)), out_shape=_sds(s_t.shape),
        compiler_params=_params(("parallel",)),
    )(s_t, a1, a2)


def _s5_scan_bwd(d_xprev_t, xprev_t, a1, a2, *, gb=32):
    n_c, n_g, n_p = xprev_t.shape
    gb = min(gb, n_g)

    def body(dx_ref, x_ref, a1_ref, a2_ref, ds_ref, p1_ref, p2_ref):
        a1v, a2v = a1_ref[...], a2_ref[...]
        a2s = _swap_halves(a2v)
        zero = jnp.zeros((gb, n_p), F32)
        ds_ref[n_c - 1] = zero

        def step(k, carry):
            gx_next, gs_next, p1, p2 = carry
            c = n_c - 2 - k
            xp = x_ref[c + 1]
            d = dx_ref[c + 1]
            gx = d + gx_next * a1v - gs_next * a2v
            gs = _swap_halves(d) + gs_next * a1v - gx_next * a2s
            ds_ref[c] = gx
            return gx, gs, p1 + gx_next * xp, p2 + gx_next * _swap_halves(xp)

        _, _, p1, p2 = lax.fori_loop(0, n_c - 1, step, (zero, zero, zero, zero))
        p1_ref[...] = p1
        p2_ref[...] = p2

    blk = pl.BlockSpec((n_c, gb, n_p), lambda i: (0, i, 0))
    vec = pl.BlockSpec((gb, n_p), lambda i: (i, 0))
    return pl.pallas_call(
        body, name="s5_scan_bwd", grid=(n_g // gb,), in_specs=[blk, blk, vec, vec],
        out_specs=[blk, vec, vec], out_shape=[_sds(xprev_t.shape), _sds(a1.shape), _sds(a1.shape)],
        compiler_params=_params(("parallel",)),
    )(d_xprev_t, xprev_t, a1, a2)


GROUPS_PER_TILE = LANES // S5_GROUP


def _to_groups(t, n_g):
    n_l = t.shape[0]
    n_c = n_l // S5_T
    gpt = min(GROUPS_PER_TILE, n_g)
    width = gpt * S5_GROUP

    def body(x_ref, o_ref):
        tr = [x_ref[pl.ds(s, n_c, stride=S5_T), :].T for s in range(S5_T)]
        for gl in range(gpt):
            rows = slice(gl * S5_GROUP, (gl + 1) * S5_GROUP)
            stacked = jnp.concatenate([tr[s][rows, :] for s in range(S5_T)], axis=0)
            o_ref[gl] = stacked.T.astype(o_ref.dtype)

    return pl.pallas_call(
        body, name="s5_to_groups", grid=(n_g // gpt,),
        in_specs=[pl.BlockSpec((n_l, width), lambda b: (0, b))],
        out_specs=pl.BlockSpec((gpt, n_c, S5_T * S5_GROUP), lambda b: (b, 0, 0)),
        out_shape=_sds((n_g, n_c, S5_T * S5_GROUP), BF16), compiler_params=_params(("parallel",)),
    )(t)


def _from_groups(t, n_l):
    n_g, n_c = t.shape[0], t.shape[1]
    gpt = min(GROUPS_PER_TILE, n_g)
    width = gpt * S5_GROUP

    def body(y_ref, o_ref):
        ytr = [y_ref[gl].T for gl in range(gpt)]
        for s in range(S5_T):
            rows = slice(s * S5_GROUP, (s + 1) * S5_GROUP)
            piece = jnp.concatenate([ytr[gl][rows, :] for gl in range(gpt)], axis=0)
            o_ref[pl.ds(s, n_c, stride=S5_T), :] = piece.T

    return pl.pallas_call(
        body, name="s5_from_groups", grid=(n_g // gpt,),
        in_specs=[pl.BlockSpec((gpt, n_c, S5_T * S5_GROUP), lambda b: (b, 0, 0))],
        out_specs=pl.BlockSpec((n_l, width), lambda b: (0, b)),
        out_shape=_sds((n_l, n_g * S5_GROUP)), compiler_params=_params(("parallel",)),
    )(t)


def _s5_act(ys, uz, d_skip):
    di = d_skip.shape[1]
    return jax.nn.gelu(ys + d_skip * uz[:, :di])


def _s5_gate(g1, glu_pre, uz, b_glu):
    di = b_glu.shape[1]
    return g1 * jax.nn.sigmoid(glu_pre + b_glu) * jax.nn.silu(uz[:, di:])


def _s5_fwd(h, p):
    n_l = h.shape[0]
    di = p["d_skip"].shape[1]
    n_g = di // S5_GROUP
    hn = _rms_fwd(h, p["norm_g"], name="s5_rms")
    uz = _mm(hn, p["w_in"], name="s5_in")
    toep, wend, wout, a1, a2 = p["ops"]
    u_g = _to_groups(uz, n_g)
    s = _s5_states(u_g, wend)
    xprev = _s5_scan(s.transpose(1, 0, 2), a1, a2).transpose(1, 0, 2)
    ys = _from_groups(_s5_outputs(u_g, toep, xprev, wout), n_l)
    g1 = _rowwise(_s5_act, [ys, uz], [p["d_skip"]], [_sds((n_l, di), BF16)], [], tm=512, name="s5_act")[0]
    glu_pre = _mm(g1, p["w_glu"], name="s5_glu")

    def gate(ys_t, pre_t, uz_t, d_skip, b_glu):
        return _s5_gate(_s5_act(ys_t, uz_t, d_skip), pre_t, uz_t, b_glu)

    gated = _rowwise(gate, [ys, glu_pre, uz], [p["d_skip"], p["b_glu"]], [_sds((n_l, di), BF16)], [],
                     tm=512, name="s5_gate")[0]
    h_next = _mm(gated, p["w_out"], add=h, name="s5_out")
    return h_next, (h, hn, uz, u_g, xprev, ys, g1, glu_pre, gated)


def _s5_bwd(dh_out, p, saved, ops_vjp):
    h, hn, uz, u_g, xprev, ys, g1, glu_pre, gated = saved
    n_l = h.shape[0]
    di = p["d_skip"].shape[1]
    n_g = di // S5_GROUP
    toep, wend, wout, a1, a2 = p["ops"]
    d_gated = _mm(dh_out, p["w_out"], mode="nt", out_dtype=BF16, name="s5_dgated")
    g_w_out = _mm(gated, dh_out, mode="tn", name="s5_dwout")

    def gate_bwd(ys_t, pre_t, uz_t, ct, d_skip, b_glu):
        g1_t = _s5_act(ys_t, uz_t, d_skip)
        _, vjp = jax.vjp(_s5_gate, g1_t, pre_t, uz_t, b_glu)
        d_g1, d_pre, d_uz, d_b = vjp(ct.astype(F32))
        return d_g1, d_pre, d_uz, d_b

    d_g1_direct, d_pre, d_uz_gate, g_b_glu = _rowwise(
        gate_bwd, [ys, glu_pre, uz, d_gated], [p["d_skip"], p["b_glu"]],
        [_sds((n_l, di)), _sds((n_l, di), BF16), _sds(uz.shape)], [_sds(p["b_glu"].shape)], tm=256, name="s5_gate_bwd")
    g_w_glu = _mm(g1, d_pre, mode="tn", name="s5_dwglu")
    d_g1 = _mm(d_pre, p["w_glu"], mode="nt", add=d_g1_direct, name="s5_dg1")

    def act_bwd(ys_t, uz_t, ct, d_uz_t, d_skip):
        _, vjp = jax.vjp(_s5_act, ys_t, uz_t, d_skip)
        d_ys, d_uz, d_d = vjp(ct)
        return d_ys, d_uz + d_uz_t, d_d

    d_ys, d_uz_part, g_d_skip = _rowwise(
        act_bwd, [ys, uz, d_g1, d_uz_gate], [p["d_skip"]], [_sds((n_l, di)), _sds(uz.shape)],
        [_sds(p["d_skip"].shape)], tm=256, name="s5_act_bwd")
    d_y = _to_groups(d_ys, n_g)
    d_toep, d_wout, d_xprev = _s5_outputs_bwd(u_g, d_y, xprev, wout)
    d_s_t, p1, p2 = _s5_scan_bwd(d_xprev.transpose(1, 0, 2), xprev.transpose(1, 0, 2), a1, a2)
    d_s = d_s_t.transpose(1, 0, 2)
    d_u_g, d_wend = _s5_inputs_bwd(u_g, d_y, d_s, toep, wend)
    d_u = _from_groups(d_u_g, n_l)
    d_uz = _rowwise(lambda part, du: jnp.concatenate([part[:, :di] + du, part[:, di:]], axis=1),
                    [d_uz_part, d_u], [], [_sds(uz.shape, BF16)], [], tm=512, name="s5_duz")[0]
    g_w_in = _mm(hn, d_uz, mode="tn", name="s5_dwin")
    d_hn = _mm(d_uz, p["w_in"], mode="nt", name="s5_dhn")
    dh_in, g_norm = _rms_bwd(h, p["norm_g"], d_hn, dh_out, name="s5_rms_bwd")
    g_ops = ops_vjp((d_toep, d_wend, d_wout, p1, p2))
    grads = dict(norm_g=g_norm, w_in=g_w_in, a_re=g_ops[0], a_im=g_ops[1], log_step=g_ops[2], b_re=g_ops[3],
                 b_im=g_ops[4], c_re=g_ops[5], c_im=g_ops[6], d_skip=g_d_skip, w_glu=g_w_glu, b_glu=g_b_glu,
                 w_out=g_w_out)
    return dh_in, grads


MLA_Z0 = MLA_Q_RANK + MLA_KV_RANK + LANES


def _rope_tile(t, cos_t, sin_t):
    q = LANES // 4
    lane = lax.broadcasted_iota(jnp.int32, t.shape, 1)
    swapped = jnp.where(lane < q, pltpu.roll(t, LANES - q, axis=1), pltpu.roll(t, q, axis=1))
    return t * cos_t + swapped * sin_t


def _mla_mid(proj, cos_t, sin_t, q_g, kv_g):
    cqn = _rms(proj[:, :MLA_Q_RANK], q_g)
    ckvn = _rms(proj[:, MLA_Q_RANK:MLA_Q_RANK + MLA_KV_RANK], kv_g)
    kr = _rope_tile(proj[:, MLA_Q_RANK + MLA_KV_RANK:MLA_Z0], cos_t, sin_t)
    return cqn, ckvn, kr


def _mla_rope_q(qp, cos_t, sin_t):
    parts = []
    for hd in range(qp.shape[1] // MLA_HEAD_PAD):
        base = hd * MLA_HEAD_PAD
        parts.append(qp[:, base:base + LANES])
        parts.append(_rope_tile(qp[:, base + LANES:base + MLA_HEAD_PAD], cos_t, sin_t))
    return jnp.concatenate(parts, axis=1)


def _mla_gate(o, proj):
    return o * jax.nn.silu(proj[:, MLA_Z0:])


LOG2E = math.log2(math.e)
SCORE_LOG2 = MLA_SCALE * LOG2E
FLASH_SPLIT = 2


def _causal_pairs(n_blk, kv_major):
    if kv_major:
        pairs = [(i, j) for j in range(n_blk) for i in range(j, n_blk)]
    else:
        pairs = [(i, j) for i in range(n_blk) for j in range(i + 1)]
    return (jnp.asarray([p[0] for p in pairs], jnp.int32), jnp.asarray([p[1] for p in pairs], jnp.int32))


def _raw_scores(q, kcat, row0, diagonal):
    s = _dot(q, kcat, "nt")
    if diagonal:
        qpos = row0 + lax.broadcasted_iota(jnp.int32, s.shape, 0)
        kpos = lax.broadcasted_iota(jnp.int32, s.shape, 1)
        s = jnp.where(kpos <= qpos, s, NEG_INF)
    return s


def _lanes(x, width):
    return jnp.tile(x, (1, width // LANES))


def _flash_fwd(qp, kv, kr, *, blk=512):
    n_l = qp.shape[0]
    heads = qp.shape[1] // MLA_HEAD_PAD
    blk = _pick(n_l, blk)
    n_blk = n_l // blk
    half = blk // FLASH_SPLIT
    qi, kj = _causal_pairs(n_blk, kv_major=False)

    def body(qi_ref, kj_ref, q_ref, kv_ref, kr_ref, o_ref, lse_ref, m_sc, l_sc, acc_sc):
        p = pl.program_id(1)
        i, j = qi_ref[p], kj_ref[p]

        @pl.when(j == 0)
        def _():
            m_sc[...] = jnp.full_like(m_sc, NEG_INF)
            l_sc[...] = jnp.zeros_like(l_sc)
            acc_sc[...] = jnp.zeros_like(acc_sc)

        def update(diagonal):
            kcat = jnp.concatenate([kv_ref[:, :LANES], kr_ref[...]], axis=1)
            v = kv_ref[:, LANES:]
            for r in range(FLASH_SPLIT):
                rows = slice(r * half, (r + 1) * half)
                s = _raw_scores(q_ref[rows, :], kcat, r * half, diagonal)
                m_old = m_sc[rows, :]
                m_new = jnp.maximum(m_old, jnp.max(s, axis=1, keepdims=True))
                alpha = jnp.exp2((m_old - m_new) * SCORE_LOG2)
                pr = jnp.exp2((s - _lanes(m_new, blk)) * SCORE_LOG2)
                l_sc[rows, :] = alpha * l_sc[rows, :] + jnp.sum(pr, axis=1, keepdims=True)
                acc_sc[rows, :] = alpha * acc_sc[rows, :] + _dot(pr, v)
                m_sc[rows, :] = m_new

        @pl.when(j < i)
        def _():
            update(False)

        @pl.when(j == i)
        def _():
            update(True)
            o_ref[...] = acc_sc[...] / l_sc[...]
            lse_ref[...] = m_sc[...] * MLA_SCALE + jnp.log(l_sc[...])

    grid_spec = pltpu.PrefetchScalarGridSpec(
        num_scalar_prefetch=2, grid=(heads, qi.shape[0]),
        in_specs=[pl.BlockSpec((blk, MLA_HEAD_PAD), lambda h, p, qi_r, kj_r: (qi_r[p], h)),
                  pl.BlockSpec((blk, MLA_HEAD_PAD), lambda h, p, qi_r, kj_r: (kj_r[p], h)),
                  pl.BlockSpec((blk, LANES), lambda h, p, qi_r, kj_r: (kj_r[p], 0))],
        out_specs=[pl.BlockSpec((blk, MLA_V), lambda h, p, qi_r, kj_r: (qi_r[p], h)),
                   pl.BlockSpec((None, blk, LANES), lambda h, p, qi_r, kj_r: (h, qi_r[p], 0))],
        scratch_shapes=[pltpu.VMEM((blk, LANES), F32), pltpu.VMEM((blk, LANES), F32), pltpu.VMEM((blk, MLA_V), F32)])
    return pl.pallas_call(
        body, name="mla_flash_fwd", grid_spec=grid_spec,
        out_shape=[_sds((n_l, heads * MLA_V)), _sds((heads, n_l, LANES))],
        compiler_params=_params(("parallel", "arbitrary")),
    )(qi, kj, qp, kv, kr)


def _flash_bwd(qp, kv, kr, d_o, lse, delta, *, blk=512):
    n_l = qp.shape[0]
    heads = qp.shape[1] // MLA_HEAD_PAD
    blk = _pick(n_l, blk)
    n_blk = n_l // blk
    half = blk // FLASH_SPLIT
    qi, kj = _causal_pairs(n_blk, kv_major=True)
    n_pairs = qi.shape[0]

    def body(qi_ref, kj_ref, q_ref, kv_ref, kr_ref, do_ref, lse_ref, dl_ref, dq_ref, dkv_ref, dkr_ref, dk_sc, dv_sc):
        h, p = pl.program_id(0), pl.program_id(1)
        i, j = qi_ref[p], kj_ref[p]

        @pl.when(p == 0)
        def _():
            dq_ref[...] = jnp.zeros_like(dq_ref)

        @pl.when(jnp.logical_and(p == 0, h == 0))
        def _():
            dkr_ref[...] = jnp.zeros_like(dkr_ref)

        @pl.when(i == j)
        def _():
            dk_sc[...] = jnp.zeros_like(dk_sc)
            dv_sc[...] = jnp.zeros_like(dv_sc)

        def update(diagonal):
            kcat = jnp.concatenate([kv_ref[:, :LANES], kr_ref[...]], axis=1)
            v = kv_ref[:, LANES:]
            for r in range(FLASH_SPLIT):
                rows = slice(r * half, (r + 1) * half)
                q_t, do_t = q_ref[rows, :], do_ref[rows, :]
                s = _raw_scores(q_t, kcat, r * half, diagonal)
                pr = jnp.exp2(s * SCORE_LOG2 - _lanes(lse_ref[rows, :] * LOG2E, blk))
                d_p = _dot(do_t, v, "nt")
                d_s = pr * (d_p - _lanes(dl_ref[rows, :], blk))
                dk_sc[...] += _dot(d_s, q_t, "tn")
                dv_sc[...] += _dot(pr, do_t, "tn")
                q_rows = pl.ds(pl.multiple_of(i * blk + r * half, half), half)
                dq_ref[q_rows, :] += _dot(d_s, kcat)

        @pl.when(i > j)
        def _():
            update(False)

        @pl.when(i == j)
        def _():
            update(True)

        @pl.when(i == n_blk - 1)
        def _():
            dk = dk_sc[...] * MLA_SCALE
            dkv_ref[:, :LANES] = dk[:, :LANES].astype(dkv_ref.dtype)
            dkv_ref[:, LANES:] = dv_sc[...].astype(dkv_ref.dtype)
            k_rows = pl.ds(pl.multiple_of(j * blk, blk), blk)
            dkr_ref[k_rows, :] += dk[:, LANES:]

        @pl.when(p == n_pairs - 1)
        def _():
            dq_ref[...] = dq_ref[...] * MLA_SCALE

    at_q = lambda h, p, qi_r, kj_r: (qi_r[p], h)
    at_kv = lambda h, p, qi_r, kj_r: (kj_r[p], h)
    grid_spec = pltpu.PrefetchScalarGridSpec(
        num_scalar_prefetch=2, grid=(heads, n_pairs),
        in_specs=[pl.BlockSpec((blk, MLA_HEAD_PAD), at_q),
                  pl.BlockSpec((blk, MLA_HEAD_PAD), at_kv),
                  pl.BlockSpec((blk, LANES), lambda h, p, qi_r, kj_r: (kj_r[p], 0)),
                  pl.BlockSpec((blk, MLA_V), at_q),
                  pl.BlockSpec((None, blk, LANES), lambda h, p, qi_r, kj_r: (h, qi_r[p], 0)),
                  pl.BlockSpec((blk, LANES), at_q)],
        out_specs=[pl.BlockSpec((n_l, MLA_HEAD_PAD), lambda h, p, qi_r, kj_r: (0, h)),
                   pl.BlockSpec((blk, MLA_HEAD_PAD), at_kv),
                   pl.BlockSpec((n_l, LANES), lambda h, p, qi_r, kj_r: (0, 0))],
        scratch_shapes=[pltpu.VMEM((blk, MLA_HEAD_PAD), F32), pltpu.VMEM((blk, MLA_V), F32)])
    return pl.pallas_call(
        body, name="mla_flash_bwd", grid_spec=grid_spec,
        out_shape=[_sds(qp.shape), _sds(kv.shape, BF16), _sds(kr.shape)],
        compiler_params=_params(("arbitrary", "arbitrary")),
    )(qi, kj, qp, kv, kr, d_o, lse, delta)


def _mla_fwd(h, p, rope):
    n_l = h.shape[0]
    cos_t, sin_t = rope
    hn = _rms_fwd(h, p["norm_g"], name="mla_rms")
    proj = _mm(hn, p["w_in"], name="mla_in", tn=896)
    cqn, ckvn, kr = _rowwise(_mla_mid, [proj, cos_t, sin_t], [p["q_norm_g"], p["kv_norm_g"]],
                             [_sds((n_l, MLA_Q_RANK), BF16), _sds((n_l, MLA_KV_RANK), BF16), _sds((n_l, LANES), BF16)],
                             [], tm=512, name="mla_mid")
    q_raw = _mm(cqn, p["w_uq"], name="mla_uq")
    qp = _rowwise(_mla_rope_q, [q_raw, cos_t, sin_t], [], [_sds(q_raw.shape, BF16)], [], tm=512, name="mla_rope_q")[0]
    kv = _mm(ckvn, p["w_ukv"], out_dtype=BF16, name="mla_ukv")
    o, lse = _flash_fwd(qp, kv, kr)
    gated = _rowwise(_mla_gate, [o, proj], [], [_sds(o.shape, BF16)], [], tm=512, name="mla_gate")[0]
    h_next = _mm(gated, p["w_out"], add=h, name="mla_out")
    return h_next, (h, hn, proj, cqn, ckvn, kr, qp, kv, o, lse, gated)


def _mla_bwd(dh_out, p, saved, rope):
    h, hn, proj, cqn, ckvn, kr, qp, kv, o, lse, gated = saved
    n_l = h.shape[0]
    cos_t, sin_t = rope
    d_gated = _mm(dh_out, p["w_out"], mode="nt", out_dtype=BF16, name="mla_dgated")
    g_w_out = _mm(gated, dh_out, mode="tn", name="mla_dwout")

    def gate_bwd(o_t, proj_t, ct):
        _, vjp = jax.vjp(lambda a, z: a * jax.nn.silu(z), o_t, proj_t[:, MLA_Z0:])
        d_o_t, d_z_t = vjp(ct.astype(F32))
        prod = d_o_t * o_t
        delta = jnp.concatenate(
            [jnp.broadcast_to(jnp.sum(prod[:, hd * MLA_V:(hd + 1) * MLA_V], axis=1, keepdims=True),
                              (prod.shape[0], MLA_V)) for hd in range(prod.shape[1] // MLA_V)], axis=1)
        return d_o_t, d_z_t, delta

    d_o, d_z, delta = _rowwise(gate_bwd, [o, proj, d_gated], [], [_sds(o.shape, BF16), _sds(o.shape), _sds(o.shape)],
                               [], tm=512, name="mla_gate_bwd")
    d_qp, d_kv, d_kr = _flash_bwd(qp, kv, kr, d_o, lse, delta)

    def rope_q_bwd(ct, c_t, s_t):
        return _mla_rope_q(ct, c_t, -s_t)

    d_q_raw = _rowwise(rope_q_bwd, [d_qp, cos_t, sin_t], [], [_sds(d_qp.shape, BF16)], [], tm=512,
                       name="mla_rope_q_bwd")[0]
    g_w_uq = _mm(cqn, d_q_raw, mode="tn", name="mla_dwuq")
    d_cqn = _mm(d_q_raw, p["w_uq"], mode="nt", name="mla_dcqn")
    g_w_ukv = _mm(ckvn, d_kv, mode="tn", name="mla_dwukv")
    d_ckvn = _mm(d_kv, p["w_ukv"], mode="nt", name="mla_dckvn")

    def mid_bwd(proj_t, c_t, s_t, d_cq, d_ckv, d_kr_t, d_z_t, q_g, kv_g):
        _, vjp_q = jax.vjp(_rms, proj_t[:, :MLA_Q_RANK], q_g)
        _, vjp_kv = jax.vjp(_rms, proj_t[:, MLA_Q_RANK:MLA_Q_RANK + MLA_KV_RANK], kv_g)
        d_q_in, d_qg = vjp_q(d_cq)
        d_kv_in, d_kvg = vjp_kv(d_ckv)
        d_kr_in = _rope_tile(d_kr_t, c_t, -s_t)
        return jnp.concatenate([d_q_in, d_kv_in, d_kr_in, d_z_t], axis=1), d_qg, d_kvg

    d_proj, g_q_norm, g_kv_norm = _rowwise(
        mid_bwd, [proj, cos_t, sin_t, d_cqn, d_ckvn, d_kr, d_z], [p["q_norm_g"], p["kv_norm_g"]],
        [_sds(proj.shape, BF16)], [_sds(p["q_norm_g"].shape), _sds(p["kv_norm_g"].shape)], tm=512, name="mla_mid_bwd")
    g_w_in = _mm(hn, d_proj, mode="tn", name="mla_dwin", tn=896)
    d_hn = _mm(d_proj, p["w_in"], mode="nt", name="mla_dhn", tk=896)
    dh_in, g_norm = _rms_bwd(h, p["norm_g"], d_hn, dh_out, name="mla_rms_bwd")
    grads = dict(norm_g=g_norm, w_in=g_w_in, q_norm_g=g_q_norm, w_uq=g_w_uq, kv_norm_g=g_kv_norm, w_ukv=g_w_ukv,
                 w_out=g_w_out)
    return dh_in, grads


def _loss_head(h, g, target):
    def fn(x, t, gg):
        def local(xx, g2):
            err = _rms(xx, g2) - t
            return 0.5 * jnp.sum(jnp.mean(err * err, axis=-1))

        val, (dx, dg) = jax.value_and_grad(local, argnums=(0, 1))(x, gg)
        return dx, jnp.full((1, LANES), val, F32), dg

    dh, loss, dg = _rowwise(fn, [h, target], [g], [_sds(h.shape)], [_sds((1, LANES)), _sds(g.shape)], tm=512,
                            name="loss_head")
    return loss[0, 0], dh, dg


HBM_SPEC = pl.BlockSpec(memory_space=pltpu.HBM)


def _all_gather(shard, *, name):
    def body(x_ref, out_ref, send_sems, recv_sems, local_sem):
        x, y, c = lax.axis_index("x"), lax.axis_index("y"), lax.axis_index("c")
        me, sibling = (x, y, c), (x, y, 1 - c)
        chips = [(1 - x, y), (x, 1 - y), (1 - x, 1 - y)]

        def rows(px, py, pc):
            return out_ref.at[4 * px + 2 * py + pc]

        def copy(k, block, to, src=None):
            return pltpu.make_async_remote_copy(
                src_ref=rows(*block) if src is None else src, dst_ref=rows(*block),
                send_sem=send_sems.at[k], recv_sem=recv_sems.at[k], device_id=to, device_id_type=MESH)

        mine = pltpu.make_async_copy(x_ref, rows(*me), local_sem)
        mine.start()
        first = [copy(0, me, sibling, src=x_ref)]
        first += [copy(1 + j, me, (*chip, c), src=x_ref) for j, chip in enumerate(chips)]
        for cp in first:
            cp.start()
        passed = [copy(4 + j, (*chip, c), sibling) for j, chip in enumerate(chips)]
        for j, chip in enumerate(chips):
            copy(1 + j, (*chip, c), me).wait_recv()
            passed[j].start()
        copy(0, sibling, me).wait_recv()
        for j, chip in enumerate(chips):
            copy(4 + j, (*chip, 1 - c), me).wait_recv()
        for cp in first + passed:
            cp.wait_send()
        mine.wait()

    return pl.pallas_call(
        body, name=name, out_shape=jax.ShapeDtypeStruct((N_DEV,) + shard.shape, shard.dtype),
        in_specs=[HBM_SPEC], out_specs=HBM_SPEC,
        scratch_shapes=[pltpu.SemaphoreType.DMA((7,)), pltpu.SemaphoreType.DMA((7,)), pltpu.SemaphoreType.DMA],
    )(shard)


def _exchange(src, routes, *, name):
    n_routes = len(routes)

    def body(s_ref, out_ref, send_sems, recv_sems):
        x, y, c = lax.axis_index("x"), lax.axis_index("y"), lax.axis_index("c")
        local, remote = [], []
        for k, (flip, block) in enumerate(routes):
            src_blk = s_ref.at[block(x, y, c)]
            if flip == 0:
                local.append(pltpu.make_async_copy(src_blk, out_ref.at[k], send_sems.at[k]))
            else:
                peer = (1 - x if flip & 4 else x, 1 - y if flip & 2 else y, 1 - c if flip & 1 else c)
                remote.append(pltpu.make_async_remote_copy(
                    src_ref=src_blk, dst_ref=out_ref.at[k], send_sem=send_sems.at[k], recv_sem=recv_sems.at[k],
                    device_id=peer, device_id_type=MESH))
        for cp in local + remote:
            cp.start()
        for cp in remote:
            cp.wait_recv()
        for cp in remote:
            cp.wait_send()
        for cp in local:
            cp.wait()

    return pl.pallas_call(
        body, name=name, out_shape=jax.ShapeDtypeStruct((n_routes,) + src.shape[1:], src.dtype),
        in_specs=[HBM_SPEC], out_specs=HBM_SPEC,
        scratch_shapes=[pltpu.SemaphoreType.DMA((n_routes,)), pltpu.SemaphoreType.DMA((n_routes,))],
    )(src)


def _reduce_scatter(send, *, name):
    def chip_block(k, other_core):
        return lambda x, y, c: (4 * (1 - x if k & 2 else x) + 2 * (1 - y if k & 1 else y)
                                + (1 - c if other_core else c))

    n_chips = 4
    pair = _exchange(send, [(1, chip_block(k, True)) for k in range(n_chips)], name=name + "_pair")
    rows, width = send.shape[1], send.shape[2]
    tr = _pick(rows, 256, 8)
    x, y, c = lax.axis_index("x"), lax.axis_index("y"), lax.axis_index("c")
    own_ids = jnp.stack([chip_block(k, False)(x, y, c) for k in range(n_chips)]).astype(jnp.int32)

    def add_body(ids_ref, *refs):
        own_refs, p_ref, o_ref = refs[:n_chips], refs[n_chips], refs[n_chips + 1]
        for k in range(n_chips):
            o_ref[k] = (own_refs[k][...].astype(F32) + p_ref[k].astype(F32)).astype(o_ref.dtype)

    own_spec = lambda k: pl.BlockSpec((None, tr, width), lambda i, ids: (ids[k], i, 0))
    chip_sums = pl.pallas_call(
        add_body, name=name + "_pair_sum",
        grid_spec=pltpu.PrefetchScalarGridSpec(
            num_scalar_prefetch=1, grid=(rows // tr,),
            in_specs=[own_spec(k) for k in range(n_chips)] + [pl.BlockSpec((n_chips, tr, width), lambda i, ids: (0, i, 0))],
            out_specs=pl.BlockSpec((n_chips, tr, width), lambda i, ids: (0, i, 0))),
        out_shape=jax.ShapeDtypeStruct((n_chips, rows, width), send.dtype), compiler_params=_params(("parallel",)),
    )(own_ids, *([send] * n_chips), pair)
    recv = _exchange(chip_sums, [(2 * k, (lambda kk: lambda x, y, c: kk)(k)) for k in range(1, n_chips)],
                     name=name + "_chips")

    def sum_body(q_ref, r_ref, o_ref):
        acc = q_ref[...].astype(F32)
        for k in range(n_chips - 1):
            acc = acc + r_ref[k].astype(F32)
        o_ref[...] = acc

    return pl.pallas_call(
        sum_body, name=name + "_sum", grid=(rows // tr,),
        in_specs=[pl.BlockSpec((None, tr, width), lambda i: (0, i, 0)),
                  pl.BlockSpec((n_chips - 1, tr, width), lambda i: (0, i, 0))],
        out_specs=pl.BlockSpec((tr, width), lambda i: (i, 0)), out_shape=_sds((rows, width)),
        compiler_params=_params(("parallel",)),
    )(chip_sums, recv)


def _adamw(w, g, m, v, *, name):
    def fn(ww, gg, mm, vv):
        m_new = ADAM_B1 * mm + (1.0 - ADAM_B1) * gg
        v_new = ADAM_B2 * vv + (1.0 - ADAM_B2) * jnp.square(gg)
        m_hat = m_new / (1.0 - ADAM_B1 ** ADAM_STEP)
        v_hat = v_new / (1.0 - ADAM_B2 ** ADAM_STEP)
        return -ADAM_LR * (m_hat / (jnp.sqrt(v_hat) + ADAM_EPS) + ADAM_WD * ww), m_new, v_new

    return _rowwise(fn, [w, g, m, v], [], [_sds(w.shape)] * 3, [], tm=256, name=name)


KINDS = ("gmlp", "s5", "mla", "gmlp")
LAYER_NAMES = {
    "gmlp": ("norm_g", "w_in", "ln_g", "ln_b", "w_s", "b_s", "w_out"),
    "s5": ("norm_g", "w_in", "a_re", "a_im", "log_step", "b_re", "b_im", "c_re", "c_im", "d_skip", "w_glu", "b_glu",
           "w_out"),
    "mla": ("norm_g", "w_in", "q_norm_g", "w_uq", "kv_norm_g", "w_ukv", "w_out"),
}
COL_SHARDED = ("w_in", "w_uq", "w_ukv")
ROW_SHARDED = ("w_out", "w_glu")
WEIGHT_NAMES = tuple("l%d_%s" % (i, n) for i, k in enumerate(KINDS) for n in LAYER_NAMES[k]) + ("final_norm_g",)


def _is_sharded(name):
    return name.split("_", 1)[1] in COL_SHARDED + ROW_SHARDED


def _flatten(arrs, pad_rows_to):
    parts, sizes = [], []
    for a in arrs:
        flat = a.reshape(-1)
        pad = (-flat.shape[0]) % FLAT_W
        if pad:
            flat = jnp.pad(flat, (0, pad))
        parts.append(flat)
        sizes.append(flat.shape[0] // FLAT_W)
    rows = sum(sizes)
    pad_rows = (-rows) % pad_rows_to
    if pad_rows:
        parts.append(jnp.zeros((pad_rows * FLAT_W,), arrs[0].dtype))
    return jnp.concatenate(parts).reshape(-1, FLAT_W), sizes


def _unflatten(flat, shapes, sizes):
    out, row = [], 0
    for shape, n_rows in zip(shapes, sizes):
        n = int(np.prod(shape))
        out.append(flat[row:row + n_rows].reshape(-1)[:n].reshape(shape))
        row += n_rows
    return out


def _full_from_gathered(blocks, name):
    if name.split("_", 1)[1] in COL_SHARDED:
        return blocks.transpose(1, 0, 2).reshape(blocks.shape[1], -1)
    return blocks.reshape(-1, blocks.shape[2])


def _shards_of(full, name):
    if name.split("_", 1)[1] in COL_SHARDED:
        return full.reshape(full.shape[0], N_DEV, -1).transpose(1, 0, 2)
    return full.reshape(N_DEV, -1, full.shape[1])


def _rope_tables(positions):
    inv_freq = ROPE_THETA ** (-jnp.arange(0, MLA_ROPE, 2, dtype=F32) / MLA_ROPE)
    ang = positions.astype(F32)[:, None] * inv_freq
    cos, sin = jnp.cos(ang), jnp.sin(ang)
    zero = jnp.zeros((positions.shape[0], LANES - MLA_ROPE), F32)
    return jnp.concatenate([cos, cos, zero], axis=1), jnp.concatenate([-sin, sin, zero], axis=1)


def _row(v):
    return v.reshape(1, -1)


def kernel(x, positions, l0_norm_g, l0_w_in, l0_ln_g, l0_ln_b, l0_w_s, l0_b_s, l0_w_out, l1_norm_g, l1_w_in, l1_a_re, l1_a_im, l1_log_step, l1_b_re, l1_b_im, l1_c_re, l1_c_im, l1_d_skip, l1_w_glu, l1_b_glu, l1_w_out, l2_norm_g, l2_w_in, l2_q_norm_g, l2_w_uq, l2_kv_norm_g, l2_w_ukv, l2_w_out, l3_norm_g, l3_w_in, l3_ln_g, l3_ln_b, l3_w_s, l3_b_s, l3_w_out, final_norm_g, loss_target, m_l0_norm_g, m_l0_w_in, m_l0_ln_g, m_l0_ln_b, m_l0_w_s, m_l0_b_s, m_l0_w_out, m_l1_norm_g, m_l1_w_in, m_l1_a_re, m_l1_a_im, m_l1_log_step, m_l1_b_re, m_l1_b_im, m_l1_c_re, m_l1_c_im, m_l1_d_skip, m_l1_w_glu, m_l1_b_glu, m_l1_w_out, m_l2_norm_g, m_l2_w_in, m_l2_q_norm_g, m_l2_w_uq, m_l2_kv_norm_g, m_l2_w_ukv, m_l2_w_out, m_l3_norm_g, m_l3_w_in, m_l3_ln_g, m_l3_ln_b, m_l3_w_s, m_l3_b_s, m_l3_w_out, m_final_norm_g, v_l0_norm_g, v_l0_w_in, v_l0_ln_g, v_l0_ln_b, v_l0_w_s, v_l0_b_s, v_l0_w_out, v_l1_norm_g, v_l1_w_in, v_l1_a_re, v_l1_a_im, v_l1_log_step, v_l1_b_re, v_l1_b_im, v_l1_c_re, v_l1_c_im, v_l1_d_skip, v_l1_w_glu, v_l1_b_glu, v_l1_w_out, v_l2_norm_g, v_l2_w_in, v_l2_q_norm_g, v_l2_w_uq, v_l2_kv_norm_g, v_l2_w_ukv, v_l2_w_out, v_l3_norm_g, v_l3_w_in, v_l3_ln_g, v_l3_ln_b, v_l3_w_s, v_l3_b_s, v_l3_w_out, v_final_norm_g):
    args = locals()
    weights = {n: args[n] for n in WEIGHT_NAMES}
    mom_m = {n: args["m_" + n] for n in WEIGHT_NAMES}
    mom_v = {n: args["v_" + n] for n in WEIGHT_NAMES}
    return _train_step(x, positions, loss_target, weights, mom_m, mom_v)


def _train_step(x, positions, loss_target, weights, mom_m, mom_v):
    big = [n for n in WEIGHT_NAMES if _is_sharded(n)]
    small = [n for n in WEIGHT_NAMES if not _is_sharded(n)]

    w_flat, big_sizes = _flatten([weights[n] for n in big], 8)
    gathered = _all_gather(w_flat.astype(BF16), name="weights_all_gather")
    full, row = {}, 0
    for n, n_rows in zip(big, big_sizes):
        blocks = gathered[:, row:row + n_rows].reshape((N_DEV,) + weights[n].shape)
        full[n] = _full_from_gathered(blocks, n)
        row += n_rows

    layers, ops_vjps = [], {}
    for i, kind in enumerate(KINDS):
        pre = "l%d_" % i
        p = {n: (full[pre + n] if _is_sharded(pre + n) else weights[pre + n]) for n in LAYER_NAMES[kind]}
        p["norm_g"] = _row(p["norm_g"])
        if kind == "gmlp":
            p["ln_g"], p["ln_b"], p["b_st"] = _row(p["ln_g"]), _row(p["ln_b"]), p["b_s"].T
        elif kind == "s5":
            p["d_skip"], p["b_glu"] = _row(p["d_skip"]), _row(p["b_glu"])
            ops, ops_vjps[i] = jax.vjp(_s5_operators, *[p[n] for n in ("a_re", "a_im", "log_step", "b_re", "b_im",
                                                                       "c_re", "c_im")])
            p["ops"] = tuple(o.astype(BF16) for o in ops[:3]) + ops[3:]
        else:
            heads = p["w_uq"].shape[1] // MLA_QK_DIM
            w_in = p["w_in"]
            split = MLA_Q_RANK + MLA_KV_RANK + MLA_ROPE
            p["w_in"] = jnp.concatenate([w_in[:, :split], jnp.zeros((w_in.shape[0], LANES - MLA_ROPE), w_in.dtype),
                                         w_in[:, split:]], axis=1)
            p["w_uq"] = jnp.pad(p["w_uq"].reshape(-1, heads, MLA_QK_DIM),
                                ((0, 0), (0, 0), (0, MLA_HEAD_PAD - MLA_QK_DIM))).reshape(-1, heads * MLA_HEAD_PAD)
            p["q_norm_g"], p["kv_norm_g"] = _row(p["q_norm_g"]), _row(p["kv_norm_g"])
        layers.append(p)
    rope = _rope_tables(positions[0])

    h = x[0]
    saved = []
    for kind, p in zip(KINDS, layers):
        if kind == "gmlp":
            h, s = _gmlp_fwd(h, p)
        elif kind == "s5":
            h, s = _s5_fwd(h, p)
        else:
            h, s = _mla_fwd(h, p, rope)
        saved.append(s)
    loss_local, dh, g_final = _loss_head(h, _row(weights["final_norm_g"]), loss_target[0])
    loss = lax.psum(loss_local, ("x", "y", "c"))

    grads = {"final_norm_g": g_final.reshape(-1)}
    for i in reversed(range(len(KINDS))):
        kind, p = KINDS[i], layers[i]
        if kind == "gmlp":
            dh, g = _gmlp_bwd(dh, p, saved[i])
        elif kind == "s5":
            dh, g = _s5_bwd(dh, p, saved[i], ops_vjps[i])
        else:
            dh, g = _mla_bwd(dh, p, saved[i], rope)
            heads = weights["l%d_w_uq" % i].shape[1] * N_DEV // MLA_QK_DIM
            split = MLA_Q_RANK + MLA_KV_RANK + MLA_ROPE
            g["w_in"] = jnp.concatenate([g["w_in"][:, :split], g["w_in"][:, MLA_Z0:]], axis=1)
            g["w_uq"] = g["w_uq"].reshape(-1, heads, MLA_HEAD_PAD)[:, :, :MLA_QK_DIM].reshape(-1, heads * MLA_QK_DIM)
        for n, val in g.items():
            name = "l%d_%s" % (i, n)
            grads[name] = val.reshape(weights[name].shape) if not _is_sharded(name) else val

    small_flat, small_sizes = _flatten([grads[n] for n in small], 8 * N_DEV)
    small_rows = small_flat.shape[0] // N_DEV
    send_parts = [_shards_of(grads[n], n).reshape(N_DEV, -1, FLAT_W) for n in big]
    send_parts.append(small_flat.reshape(N_DEV, small_rows, FLAT_W))
    send = jnp.concatenate(send_parts, axis=1)
    big_rows = send.shape[1] - small_rows
    pad_rows = (-send.shape[1]) % 8
    if pad_rows:
        send = jnp.pad(send, ((0, 0), (0, pad_rows), (0, 0)))
    reduced = _reduce_scatter(send.astype(BF16), name="grads")
    g_big_flat = reduced[:big_rows]
    g_small_all = _all_gather(reduced[big_rows:big_rows + small_rows], name="small_grads_all_gather")
    g_small_flat = g_small_all.reshape(-1, FLAT_W)

    def flat_of(tree, names, pad_to):
        return _flatten([tree[n] for n in names], pad_to)[0]

    outs = {}
    big_shapes = [weights[n].shape for n in big]
    for n, g_n in zip(big, _unflatten(g_big_flat, big_shapes, big_sizes)):
        outs["grad_" + n] = g_n
        outs["delta_" + n], outs["new_m_" + n], outs["new_v_" + n] = _adamw(
            weights[n], g_n, mom_m[n], mom_v[n], name="adamw_" + n)
    d_s, nm_s, nv_s = _adamw(flat_of(weights, small, 8 * N_DEV), g_small_flat, flat_of(mom_m, small, 8 * N_DEV),
                             flat_of(mom_v, small, 8 * N_DEV), name="adamw_replicated")
    small_shapes = [weights[n].shape for n in small]
    for prefix, fs in (("grad_", g_small_flat), ("delta_", d_s), ("new_m_", nm_s), ("new_v_", nv_s)):
        for n, a in zip(small, _unflatten(fs, small_shapes, small_sizes)):
            outs[prefix + n] = a
    result = [loss, dh[None]]
    for prefix in ("grad_", "delta_", "new_m_", "new_v_"):
        result += [outs[prefix + n] for n in WEIGHT_NAMES]
    return tuple(result)
```

```python
import functools
import math

import numpy as np
import jax
import jax.numpy as jnp
from jax import lax
from jax.experimental import pallas as pl
from jax.experimental.pallas import tpu as pltpu

F32 = jnp.float32
BF16 = jnp.bfloat16

NORM_EPS = 1e-6
GMLP_CHUNK = 128
S5_GROUP = 16
S5_STATE = 64
S5_T = 16
MLA_NOPE = 128
MLA_ROPE = 64
MLA_V = 128
MLA_QK_DIM = MLA_NOPE + MLA_ROPE
MLA_Q_RANK = 384
MLA_KV_RANK = 128
MLA_HEAD_PAD = 256
MLA_SCALE = MLA_QK_DIM ** -0.5
ROPE_THETA = 10000.0
NEG_INF = -1e30
ADAM_LR = 0.001
ADAM_B1 = 0.9
ADAM_B2 = 0.999
ADAM_EPS = 1e-08
ADAM_WD = 0.01
ADAM_STEP = 10

N_DEV = 8
LANES = 128
FLAT_W = 1024
VMEM_LIMIT = 56 * 1024 * 1024
MESH = pl.DeviceIdType.MESH


def _pick(dim, pref, align=LANES):
    t = (min(pref, dim) // align) * align
    while t >= align:
        if dim % t == 0:
            return t
        t -= align
    return dim


def _params(sem=None):
    return pltpu.CompilerParams(dimension_semantics=sem, vmem_limit_bytes=VMEM_LIMIT)


_DIMS = {"nn": (((1,), (0,)), ((), ())), "nt": (((1,), (1,)), ((), ())), "tn": (((0,), (0,)), ((), ()))}


def _mm(a, b, *, mode="nn", out_dtype=F32, add=None, name, tm=1024, tn=1024, tk=2048):
    if mode == "nn":
        (m, k), (_, n) = a.shape, b.shape
    elif mode == "nt":
        (m, k), (n, _) = a.shape, b.shape
    else:
        (k, m), (_, n) = a.shape, b.shape
    tm, tn, tk = _pick(m, tm, 8), _pick(n, tn), _pick(k, tk)
    nk = k // tk
    dims = _DIMS[mode]

    def body(*refs):
        a_ref, b_ref = refs[:2]
        r_ref = refs[2] if add is not None else None
        o_ref = refs[3] if add is not None else refs[2]
        part = lax.dot_general(a_ref[...].astype(BF16), b_ref[...].astype(BF16), dims, preferred_element_type=F32)

        def finish(res):
            if add is not None:
                res = res + r_ref[...]
            o_ref[...] = res.astype(o_ref.dtype)

        if nk == 1:
            finish(part)
            return
        acc_ref = refs[-1]
        kk = pl.program_id(2)

        @pl.when(kk == 0)
        def _():
            acc_ref[...] = part

        @pl.when(kk > 0)
        def _():
            acc_ref[...] += part

        @pl.when(kk == nk - 1)
        def _():
            finish(acc_ref[...])

    a_spec = (pl.BlockSpec((tk, tm), lambda i, j, kk: (kk, i)) if mode == "tn"
              else pl.BlockSpec((tm, tk), lambda i, j, kk: (i, kk)))
    b_spec = (pl.BlockSpec((tn, tk), lambda i, j, kk: (j, kk)) if mode == "nt"
              else pl.BlockSpec((tk, tn), lambda i, j, kk: (kk, j)))
    in_specs = [a_spec, b_spec]
    args = [a, b]
    if add is not None:
        in_specs.append(pl.BlockSpec((tm, tn), lambda i, j, kk: (i, j)))
        args.append(add)
    return pl.pallas_call(
        body, name=name, grid=(m // tm, n // tn, nk),
        in_specs=in_specs, out_specs=pl.BlockSpec((tm, tn), lambda i, j, kk: (i, j)),
        out_shape=jax.ShapeDtypeStruct((m, n), out_dtype),
        scratch_shapes=[pltpu.VMEM((tm, tn), F32)] if nk > 1 else [],
        compiler_params=_params(("parallel", "parallel", "arbitrary")),
    )(*args)


def _rowwise(fn, tiled, full, out_tiled, out_acc, *, tm, name):
    rows = tiled[0].shape[0]
    tm = _pick(rows, tm, 8)
    nt, nf, no = len(tiled), len(full), len(out_tiled)

    def body(*refs):
        ins = [r[...] for r in refs[:nt + nf]]
        o_refs = refs[nt + nf:nt + nf + no]
        a_refs = refs[nt + nf + no:]
        outs = fn(*ins)
        if not isinstance(outs, (tuple, list)):
            outs = (outs,)
        for r, v in zip(o_refs, outs[:no]):
            r[...] = v.astype(r.dtype)
        if a_refs:
            @pl.when(pl.program_id(0) == 0)
            def _():
                for r in a_refs:
                    r[...] = jnp.zeros_like(r)

            for r, v in zip(a_refs, outs[no:]):
                r[...] += v.astype(r.dtype)

    def whole(shape):
        nd = len(shape)
        return pl.BlockSpec(tuple(shape), lambda i: (0,) * nd)

    in_specs = ([pl.BlockSpec((tm, t.shape[1]), lambda i: (i, 0)) for t in tiled]
                + [whole(f.shape) for f in full])
    out_specs = ([pl.BlockSpec((tm, o.shape[1]), lambda i: (i, 0)) for o in out_tiled]
                 + [whole(o.shape) for o in out_acc])
    outs = pl.pallas_call(
        body, name=name, grid=(rows // tm,), in_specs=in_specs, out_specs=out_specs,
        out_shape=list(out_tiled) + list(out_acc),
        compiler_params=_params(("arbitrary",)),
    )(*tiled, *full)
    return outs


def _sds(shape, dtype=F32):
    return jax.ShapeDtypeStruct(tuple(shape), dtype)


def _rms(x, g):
    return x * lax.rsqrt(jnp.mean(x * x, axis=-1, keepdims=True) + NORM_EPS) * g


def _layernorm(x, g, b):
    mu = jnp.mean(x, axis=-1, keepdims=True)
    xc = x - mu
    var = jnp.mean(xc * xc, axis=-1, keepdims=True)
    return xc * lax.rsqrt(var + NORM_EPS) * g + b


def _dot(a, b, mode="nn"):
    return lax.dot_general(a.astype(BF16), b.astype(BF16), _DIMS[mode], preferred_element_type=F32)


@jax.custom_vjp
def _bdot(a, b):
    return _dot(a, b)


def _bdot_fwd(a, b):
    return _bdot(a, b), (a, b)


def _bdot_bwd(res, ct):
    a, b = res
    return _dot(ct, b, "nt"), _dot(a, ct, "tn")


_bdot.defvjp(_bdot_fwd, _bdot_bwd)


def _rms_fwd(h, g, *, name):
    return _rowwise(lambda x, gg: _rms(x, gg), [h], [g], [_sds(h.shape, BF16)], [], tm=512, name=name)[0]


def _rms_bwd(h, g, d_hn, dh_out, *, name):
    def fn(x, ct, res, gg):
        _, vjp = jax.vjp(_rms, x, gg)
        dx, dg = vjp(ct)
        return res + dx, dg

    dh, dg = _rowwise(fn, [h, d_hn, dh_out], [g], [_sds(h.shape)], [_sds(g.shape)], tm=512, name=name)
    return dh, dg


def _gmlp_mid(uvz, ln_g, ln_b, w_s, b_st):
    di = ln_g.shape[1]
    ng, ck = w_s.shape[0], w_s.shape[1]
    dg = di // ng
    u = jax.nn.gelu(uvz[:, :di])
    v = _layernorm(jax.nn.gelu(uvz[:, di:2 * di]), ln_g, ln_b)
    z = uvz[:, 2 * di:]
    row = lax.broadcasted_iota(jnp.int32, (ck, ck), 0)
    col = lax.broadcasted_iota(jnp.int32, (ck, ck), 1)
    causal = col <= row
    blocks = []
    for c in range(uvz.shape[0] // ck):
        cols = []
        for g in range(ng):
            w = jnp.where(causal, w_s[g], 0.0)
            cols.append(_bdot(w, v[c * ck:(c + 1) * ck, g * dg:(g + 1) * dg]) + b_st[:, g:g + 1])
        blocks.append(jnp.concatenate(cols, axis=1))
    s = blocks[0] if len(blocks) == 1 else jnp.concatenate(blocks, axis=0)
    return u * s * jax.nn.silu(z)


def _gmlp_fwd(h, p):
    hn = _rms_fwd(h, p["norm_g"], name="gmlp_rms")
    uvz = _mm(hn, p["w_in"], name="gmlp_in")
    di = p["ln_g"].shape[1]
    gated = _rowwise(_gmlp_mid, [uvz], [p["ln_g"], p["ln_b"], p["w_s"], p["b_st"]],
                     [_sds((h.shape[0], di), BF16)], [], tm=256, name="gmlp_mid")[0]
    h_next = _mm(gated, p["w_out"], add=h, name="gmlp_out")
    return h_next, (h, hn, uvz, gated)


def _gmlp_bwd(dh_out, p, saved):
    h, hn, uvz, gated = saved
    d_gated = _mm(dh_out, p["w_out"], mode="nt", out_dtype=BF16, name="gmlp_dgated")
    g_w_out = _mm(gated, dh_out, mode="tn", name="gmlp_dwout")

    def fn(t, ct, ln_g, ln_b, w_s, b_st):
        _, vjp = jax.vjp(_gmlp_mid, t, ln_g, ln_b, w_s, b_st)
        return vjp(ct.astype(F32))

    d_uvz, g_ln_g, g_ln_b, g_w_s, g_b_st = _rowwise(
        fn, [uvz, d_gated], [p["ln_g"], p["ln_b"], p["w_s"], p["b_st"]],
        [_sds(uvz.shape, BF16)], [_sds(p["ln_g"].shape), _sds(p["ln_b"].shape), _sds(p["w_s"].shape),
                                  _sds(p["b_st"].shape)], tm=128, name="gmlp_mid_bwd")
    g_w_in = _mm(hn, d_uvz, mode="tn", name="gmlp_dwin")
    d_hn = _mm(d_uvz, p["w_in"], mode="nt", name="gmlp_dhn")
    dh_in, g_norm = _rms_bwd(h, p["norm_g"], d_hn, dh_out, name="gmlp_rms_bwd")
    grads = dict(norm_g=g_norm, w_in=g_w_in, ln_g=g_ln_g, ln_b=g_ln_b, w_s=g_w_s, b_s=g_b_st.T, w_out=g_w_out)
    return dh_in, grads


def _s5_operators(a_re, a_im, log_step, b_re, b_im, c_re, c_im):
    t_len = S5_T
    step = jnp.exp(log_step)[:, None]
    lr, li = a_re * step, a_im * step
    ks = jnp.arange(t_len + 1, dtype=F32)[:, None, None]
    mag = jnp.exp(ks * lr)
    pw_r, pw_i = mag * jnp.cos(ks * li), mag * jnp.sin(ks * li)
    nr, ni = pw_r[1] - 1.0, pw_i[1]
    den = a_re * a_re + a_im * a_im
    f_r, f_i = (nr * a_re + ni * a_im) / den, (ni * a_re - nr * a_im) / den
    bb_r = f_r[..., None] * b_re - f_i[..., None] * b_im
    bb_i = f_r[..., None] * b_im + f_i[..., None] * b_re
    hi = lax.Precision.HIGHEST
    cp_r = c_re[None] * pw_r[:, :, None, :] - c_im[None] * pw_i[:, :, None, :]
    cp_i = c_re[None] * pw_i[:, :, None, :] + c_im[None] * pw_r[:, :, None, :]
    n_g, n_h = a_re.shape[0], b_re.shape[2]
    lhs = jnp.concatenate([bb_r, -bb_i], axis=1)
    rhs = jnp.concatenate([cp_r[:t_len], cp_i[:t_len]], axis=3)
    rhs = rhs.transpose(1, 3, 0, 2).reshape(n_g, -1, t_len * n_h)
    kcat = jnp.einsum("gqi,gqn->gin", lhs, rhs, precision=hi)
    toep = jnp.stack([jnp.pad(kcat[:, :, :(t_len - s) * n_h], ((0, 0), (0, 0), (s * n_h, 0)))
                      for s in range(t_len)], axis=1).reshape(n_g, t_len * n_h, t_len * n_h)
    rev_r, rev_i = pw_r[t_len - 1::-1][:t_len], pw_i[t_len - 1::-1][:t_len]
    we_r = rev_r[..., None] * bb_r[None] - rev_i[..., None] * bb_i[None]
    we_i = rev_r[..., None] * bb_i[None] + rev_i[..., None] * bb_r[None]
    wend = jnp.concatenate([we_r, we_i], axis=2).transpose(1, 0, 3, 2).reshape(n_g, t_len * n_h, -1)
    wo = jnp.concatenate([cp_r[1:], -cp_i[1:]], axis=3)
    wout = wo.transpose(1, 3, 0, 2).reshape(n_g, -1, t_len * n_h)
    a_r, a_i = pw_r[t_len], pw_i[t_len]
    a1 = jnp.concatenate([a_r, a_r], axis=1)
    a2 = jnp.concatenate([-a_i, a_i], axis=1)
    return toep, wend, wout, a1, a2


def _group_call(body, ins, outs, *, gb, name):
    n_g = ins[0].shape[0]

    def spec(a):
        return pl.BlockSpec((gb,) + tuple(a.shape[1:]), lambda i: (i, 0, 0))

    return pl.pallas_call(
        body, name=name, grid=(n_g // gb,), in_specs=[spec(a) for a in ins],
        out_specs=[spec(o) for o in outs], out_shape=list(outs),
        compiler_params=_params(("parallel",)),
    )(*ins)


def _s5_states(u_g, wend, *, gb=8):
    def body(u_ref, w_ref, s_ref):
        for g in range(gb):
            s_ref[g] = _dot(u_ref[g], w_ref[g])

    n_g, n_c = u_g.shape[0], u_g.shape[1]
    return _group_call(body, [u_g, wend], [_sds((n_g, n_c, wend.shape[2]))], gb=gb, name="s5_states")[0]


def _s5_outputs(u_g, toep, xprev, wout, *, gb=8):
    def body(u_ref, t_ref, x_ref, w_ref, y_ref):
        for g in range(gb):
            y_ref[g] = _dot(u_ref[g], t_ref[g]) + _dot(x_ref[g], w_ref[g])

    return _group_call(body, [u_g, toep, xprev, wout], [_sds(u_g.shape)], gb=gb, name="s5_outputs")[0]


def _s5_outputs_bwd(u_g, d_y, xprev, wout, *, gb=8):
    def body(u_ref, dy_ref, x_ref, w_ref, dt_ref, dw_ref, dx_ref):
        for g in range(gb):
            dy = dy_ref[g]
            dt_ref[g] = _dot(u_ref[g], dy, "tn")
            dw_ref[g] = _dot(x_ref[g], dy, "tn")
            dx_ref[g] = _dot(dy, w_ref[g], "nt")

    n_g, n_c, n_k = u_g.shape
    return _group_call(body, [u_g, d_y, xprev, wout],
                       [_sds((n_g, n_k, n_k)), _sds(wout.shape), _sds(xprev.shape)], gb=gb, name="s5_outputs_bwd")


def _s5_inputs_bwd(u_g, d_y, d_s, toep, wend, *, gb=8):
    def body(u_ref, dy_ref, ds_ref, t_ref, w_ref, du_ref, dw_ref):
        for g in range(gb):
            ds = ds_ref[g]
            du_ref[g] = _dot(dy_ref[g], t_ref[g], "nt") + _dot(ds, w_ref[g], "nt")
            dw_ref[g] = _dot(u_ref[g], ds, "tn")

    return _group_call(body, [u_g, d_y, d_s, toep, wend], [_sds(u_g.shape), _sds(wend.shape)], gb=gb,
                       name="s5_inputs_bwd")


def _swap_halves(x):
    return pltpu.roll(x, x.shape[-1] // 2, axis=x.ndim - 1)


def _s5_scan(s_t, a1, a2, *, gb=32):
    n_c, n_g, n_p = s_t.shape
    gb = min(gb, n_g)

    def body(s_ref, a1_ref, a2_ref, x_ref):
        a1v, a2v = a1_ref[...], a2_ref[...]
        a2s = _swap_halves(a2v)

        def step(c, carry):
            x, xs = carry
            x_ref[c] = x
            s = s_ref[c]
            return x * a1v + xs * a2v + s, xs * a1v + x * a2s + _swap_halves(s)

        zero = jnp.zeros((gb, n_p), F32)
        lax.fori_loop(0, n_c, step, (zero, zero))

    return pl.pallas_call(
        body, name="s5_scan", grid=(n_g // gb,),
        in_specs=[pl.BlockSpec((n_c, gb, n_p), lambda i: (0, i, 0)), pl.BlockSpec((gb, n_p), lambda i: (i, 0)),
                  pl.BlockSpec((gb, n_p), lambda i: (i, 0))],
        out_specs=pl.BlockSpec((n_c, gb, n_p), lambda i: (0, i, 0)), out_shape=_sds(s_t.shape),
        compiler_params=_params(("parallel",)),
    )(s_t, a1, a2)


def _s5_scan_bwd(d_xprev_t, xprev_t, a1, a2, *, gb=32):
    n_c, n_g, n_p = xprev_t.shape
    gb = min(gb, n_g)

    def body(dx_ref, x_ref, a1_ref, a2_ref, ds_ref, p1_ref, p2_ref):
        a1v, a2v = a1_ref[...], a2_ref[...]
        a2s = _swap_halves(a2v)
        zero = jnp.zeros((gb, n_p), F32)
        ds_ref[n_c - 1] = zero

        def step(k, carry):
            gx_next, gs_next, p1, p2 = carry
            c = n_c - 2 - k
            xp = x_ref[c + 1]
            d = dx_ref[c + 1]
            gx = d + gx_next * a1v - gs_next * a2v
            gs = _swap_halves(d) + gs_next * a1v - gx_next * a2s
            ds_ref[c] = gx
            return gx, gs, p1 + gx_next * xp, p2 + gx_next * _swap_halves(xp)

        _, _, p1, p2 = lax.fori_loop(0, n_c - 1, step, (zero, zero, zero, zero))
        p1_ref[...] = p1
        p2_ref[...] = p2

    blk = pl.BlockSpec((n_c, gb, n_p), lambda i: (0, i, 0))
    vec = pl.BlockSpec((gb, n_p), lambda i: (i, 0))
    return pl.pallas_call(
        body, name="s5_scan_bwd", grid=(n_g // gb,), in_specs=[blk, blk, vec, vec],
        out_specs=[blk, vec, vec], out_shape=[_sds(xprev_t.shape), _sds(a1.shape), _sds(a1.shape)],
        compiler_params=_params(("parallel",)),
    )(d_xprev_t, xprev_t, a1, a2)


GROUPS_PER_TILE = LANES // S5_GROUP


def _to_groups(t, n_g):
    n_l = t.shape[0]
    n_c = n_l // S5_T
    gpt = min(GROUPS_PER_TILE, n_g)
    width = gpt * S5_GROUP

    def body(x_ref, o_ref):
        tr = [x_ref[pl.ds(s, n_c, stride=S5_T), :].T for s in range(S5_T)]
        for gl in range(gpt):
            rows = slice(gl * S5_GROUP, (gl + 1) * S5_GROUP)
            stacked = jnp.concatenate([tr[s][rows, :] for s in range(S5_T)], axis=0)
            o_ref[gl] = stacked.T.astype(o_ref.dtype)

    return pl.pallas_call(
        body, name="s5_to_groups", grid=(n_g // gpt,),
        in_specs=[pl.BlockSpec((n_l, width), lambda b: (0, b))],
        out_specs=pl.BlockSpec((gpt, n_c, S5_T * S5_GROUP), lambda b: (b, 0, 0)),
        out_shape=_sds((n_g, n_c, S5_T * S5_GROUP), BF16), compiler_params=_params(("parallel",)),
    )(t)


def _from_groups(t, n_l):
    n_g, n_c = t.shape[0], t.shape[1]
    gpt = min(GROUPS_PER_TILE, n_g)
    width = gpt * S5_GROUP

    def body(y_ref, o_ref):
        ytr = [y_ref[gl].T for gl in range(gpt)]
        for s in range(S5_T):
            rows = slice(s * S5_GROUP, (s + 1) * S5_GROUP)
            piece = jnp.concatenate([ytr[gl][rows, :] for gl in range(gpt)], axis=0)
            o_ref[pl.ds(s, n_c, stride=S5_T), :] = piece.T

    return pl.pallas_call(
        body, name="s5_from_groups", grid=(n_g // gpt,),
        in_specs=[pl.BlockSpec((gpt, n_c, S5_T * S5_GROUP), lambda b: (b, 0, 0))],
        out_specs=pl.BlockSpec((n_l, width), lambda b: (0, b)),
        out_shape=_sds((n_l, n_g * S5_GROUP)), compiler_params=_params(("parallel",)),
    )(t)


def _s5_act(ys, uz, d_skip):
    di = d_skip.shape[1]
    return jax.nn.gelu(ys + d_skip * uz[:, :di])


def _s5_gate(g1, glu_pre, uz, b_glu):
    di = b_glu.shape[1]
    return g1 * jax.nn.sigmoid(glu_pre + b_glu) * jax.nn.silu(uz[:, di:])


def _s5_fwd(h, p):
    n_l = h.shape[0]
    di = p["d_skip"].shape[1]
    n_g = di // S5_GROUP
    hn = _rms_fwd(h, p["norm_g"], name="s5_rms")
    uz = _mm(hn, p["w_in"], name="s5_in")
    toep, wend, wout, a1, a2 = p["ops"]
    u_g = _to_groups(uz, n_g)
    s = _s5_states(u_g, wend)
    xprev = _s5_scan(s.transpose(1, 0, 2), a1, a2).transpose(1, 0, 2)
    ys = _from_groups(_s5_outputs(u_g, toep, xprev, wout), n_l)
    g1 = _rowwise(_s5_act, [ys, uz], [p["d_skip"]], [_sds((n_l, di), BF16)], [], tm=512, name="s5_act")[0]
    glu_pre = _mm(g1, p["w_glu"], name="s5_glu")

    def gate(ys_t, pre_t, uz_t, d_skip, b_glu):
        return _s5_gate(_s5_act(ys_t, uz_t, d_skip), pre_t, uz_t, b_glu)

    gated = _rowwise(gate, [ys, glu_pre, uz], [p["d_skip"], p["b_glu"]], [_sds((n_l, di), BF16)], [],
                     tm=512, name="s5_gate")[0]
    h_next = _mm(gated, p["w_out"], add=h, name="s5_out")
    return h_next, (h, hn, uz, u_g, xprev, ys, g1, glu_pre, gated)


def _s5_bwd(dh_out, p, saved, ops_vjp):
    h, hn, uz, u_g, xprev, ys, g1, glu_pre, gated = saved
    n_l = h.shape[0]
    di = p["d_skip"].shape[1]
    n_g = di // S5_GROUP
    toep, wend, wout, a1, a2 = p["ops"]
    d_gated = _mm(dh_out, p["w_out"], mode="nt", out_dtype=BF16, name="s5_dgated")
    g_w_out = _mm(gated, dh_out, mode="tn", name="s5_dwout")

    def gate_bwd(ys_t, pre_t, uz_t, ct, d_skip, b_glu):
        g1_t = _s5_act(ys_t, uz_t, d_skip)
        _, vjp = jax.vjp(_s5_gate, g1_t, pre_t, uz_t, b_glu)
        d_g1, d_pre, d_uz, d_b = vjp(ct.astype(F32))
        return d_g1, d_pre, d_uz, d_b

    d_g1_direct, d_pre, d_uz_gate, g_b_glu = _rowwise(
        gate_bwd, [ys, glu_pre, uz, d_gated], [p["d_skip"], p["b_glu"]],
        [_sds((n_l, di)), _sds((n_l, di), BF16), _sds(uz.shape)], [_sds(p["b_glu"].shape)], tm=256, name="s5_gate_bwd")
    g_w_glu = _mm(g1, d_pre, mode="tn", name="s5_dwglu")
    d_g1 = _mm(d_pre, p["w_glu"], mode="nt", add=d_g1_direct, name="s5_dg1")

    def act_bwd(ys_t, uz_t, ct, d_uz_t, d_skip):
        _, vjp = jax.vjp(_s5_act, ys_t, uz_t, d_skip)
        d_ys, d_uz, d_d = vjp(ct)
        return d_ys, d_uz + d_uz_t, d_d

    d_ys, d_uz_part, g_d_skip = _rowwise(
        act_bwd, [ys, uz, d_g1, d_uz_gate], [p["d_skip"]], [_sds((n_l, di)), _sds(uz.shape)],
        [_sds(p["d_skip"].shape)], tm=256, name="s5_act_bwd")
    d_y = _to_groups(d_ys, n_g)
    d_toep, d_wout, d_xprev = _s5_outputs_bwd(u_g, d_y, xprev, wout)
    d_s_t, p1, p2 = _s5_scan_bwd(d_xprev.transpose(1, 0, 2), xprev.transpose(1, 0, 2), a1, a2)
    d_s = d_s_t.transpose(1, 0, 2)
    d_u_g, d_wend = _s5_inputs_bwd(u_g, d_y, d_s, toep, wend)
    d_u = _from_groups(d_u_g, n_l)
    d_uz = _rowwise(lambda part, du: jnp.concatenate([part[:, :di] + du, part[:, di:]], axis=1),
                    [d_uz_part, d_u], [], [_sds(uz.shape, BF16)], [], tm=512, name="s5_duz")[0]
    g_w_in = _mm(hn, d_uz, mode="tn", name="s5_dwin")
    d_hn = _mm(d_uz, p["w_in"], mode="nt", name="s5_dhn")
    dh_in, g_norm = _rms_bwd(h, p["norm_g"], d_hn, dh_out, name="s5_rms_bwd")
    g_ops = ops_vjp((d_toep, d_wend, d_wout, p1, p2))
    grads = dict(norm_g=g_norm, w_in=g_w_in, a_re=g_ops[0], a_im=g_ops[1], log_step=g_ops[2], b_re=g_ops[3],
                 b_im=g_ops[4], c_re=g_ops[5], c_im=g_ops[6], d_skip=g_d_skip, w_glu=g_w_glu, b_glu=g_b_glu,
                 w_out=g_w_out)
    return dh_in, grads


MLA_Z0 = MLA_Q_RANK + MLA_KV_RANK + LANES


def _rope_tile(t, cos_t, sin_t):
    q = LANES // 4
    lane = lax.broadcasted_iota(jnp.int32, t.shape, 1)
    swapped = jnp.where(lane < q, pltpu.roll(t, LANES - q, axis=1), pltpu.roll(t, q, axis=1))
    return t * cos_t + swapped * sin_t


def _mla_mid(proj, cos_t, sin_t, q_g, kv_g):
    cqn = _rms(proj[:, :MLA_Q_RANK], q_g)
    ckvn = _rms(proj[:, MLA_Q_RANK:MLA_Q_RANK + MLA_KV_RANK], kv_g)
    kr = _rope_tile(proj[:, MLA_Q_RANK + MLA_KV_RANK:MLA_Z0], cos_t, sin_t)
    return cqn, ckvn, kr


def _mla_rope_q(qp, cos_t, sin_t):
    parts = []
    for hd in range(qp.shape[1] // MLA_HEAD_PAD):
        base = hd * MLA_HEAD_PAD
        parts.append(qp[:, base:base + LANES])
        parts.append(_rope_tile(qp[:, base + LANES:base + MLA_HEAD_PAD], cos_t, sin_t))
    return jnp.concatenate(parts, axis=1)


def _mla_gate(o, proj):
    return o * jax.nn.silu(proj[:, MLA_Z0:])


LOG2E = math.log2(math.e)
SCORE_LOG2 = MLA_SCALE * LOG2E
FLASH_SPLIT = 4


def _causal_pairs(n_blk, kv_major):
    if kv_major:
        pairs = [(i, j) for j in range(n_blk) for i in range(j, n_blk)]
    else:
        pairs = [(i, j) for i in range(n_blk) for j in range(i + 1)]
    return (jnp.asarray([p[0] for p in pairs], jnp.int32), jnp.asarray([p[1] for p in pairs], jnp.int32))


def _raw_scores(q, kcat, row0, diagonal):
    s = _dot(q, kcat, "nt")
    if diagonal:
        qpos = row0 + lax.broadcasted_iota(jnp.int32, s.shape, 0)
        kpos = lax.broadcasted_iota(jnp.int32, s.shape, 1)
        s = jnp.where(kpos <= qpos, s, NEG_INF)
    return s


def _lanes(x, width):
    return jnp.tile(x, (1, width // LANES))


def _flash_fwd(qp, kv, kr, *, blk=1024):
    n_l = qp.shape[0]
    heads = qp.shape[1] // MLA_HEAD_PAD
    blk = _pick(n_l, blk)
    n_blk = n_l // blk
    half = blk // FLASH_SPLIT
    qi, kj = _causal_pairs(n_blk, kv_major=False)

    def body(qi_ref, kj_ref, q_ref, kv_ref, kr_ref, o_ref, lse_ref, m_sc, l_sc, acc_sc):
        p = pl.program_id(1)
        i, j = qi_ref[p], kj_ref[p]

        @pl.when(j == 0)
        def _():
            m_sc[...] = jnp.full_like(m_sc, NEG_INF)
            l_sc[...] = jnp.zeros_like(l_sc)
            acc_sc[...] = jnp.zeros_like(acc_sc)

        def update(diagonal):
            kcat = jnp.concatenate([kv_ref[:, :LANES], kr_ref[...]], axis=1)
            v = kv_ref[:, LANES:]
            for r in range(FLASH_SPLIT):
                rows = slice(r * half, (r + 1) * half)
                s = _raw_scores(q_ref[rows, :], kcat, r * half, diagonal)
                m_old = m_sc[rows, :]
                m_new = jnp.maximum(m_old, jnp.max(s, axis=1, keepdims=True))
                alpha = jnp.exp2((m_old - m_new) * SCORE_LOG2)
                pr = jnp.exp2((s - _lanes(m_new, blk)) * SCORE_LOG2)
                l_sc[rows, :] = alpha * l_sc[rows, :] + jnp.sum(pr, axis=1, keepdims=True)
                acc_sc[rows, :] = alpha * acc_sc[rows, :] + _dot(pr, v)
                m_sc[rows, :] = m_new

        @pl.when(j < i)
        def _():
            update(False)

        @pl.when(j == i)
        def _():
            update(True)
            o_ref[...] = acc_sc[...] / l_sc[...]
            lse_ref[...] = m_sc[...] * MLA_SCALE + jnp.log(l_sc[...])

    grid_spec = pltpu.PrefetchScalarGridSpec(
        num_scalar_prefetch=2, grid=(heads, qi.shape[0]),
        in_specs=[pl.BlockSpec((blk, MLA_HEAD_PAD), lambda h, p, qi_r, kj_r: (qi_r[p], h)),
                  pl.BlockSpec((blk, MLA_HEAD_PAD), lambda h, p, qi_r, kj_r: (kj_r[p], h)),
                  pl.BlockSpec((blk, LANES), lambda h, p, qi_r, kj_r: (kj_r[p], 0))],
        out_specs=[pl.BlockSpec((blk, MLA_V), lambda h, p, qi_r, kj_r: (qi_r[p], h)),
                   pl.BlockSpec((None, blk, LANES), lambda h, p, qi_r, kj_r: (h, qi_r[p], 0))],
        scratch_shapes=[pltpu.VMEM((blk, LANES), F32), pltpu.VMEM((blk, LANES), F32), pltpu.VMEM((blk, MLA_V), F32)])
    return pl.pallas_call(
        body, name="mla_flash_fwd", grid_spec=grid_spec,
        out_shape=[_sds((n_l, heads * MLA_V)), _sds((heads, n_l, LANES))],
        compiler_params=_params(("parallel", "arbitrary")),
    )(qi, kj, qp, kv, kr)


def _flash_bwd(qp, kv, kr, d_o, lse, delta, *, blk=1024):
    n_l = qp.shape[0]
    heads = qp.shape[1] // MLA_HEAD_PAD
    blk = _pick(n_l, blk)
    n_blk = n_l // blk
    half = blk // FLASH_SPLIT
    qi, kj = _causal_pairs(n_blk, kv_major=True)
    n_pairs = qi.shape[0]

    def body(qi_ref, kj_ref, q_ref, kv_ref, kr_ref, do_ref, lse_ref, dl_ref, dq_ref, dkv_ref, dkr_ref, dk_sc, dv_sc):
        h, p = pl.program_id(0), pl.program_id(1)
        i, j = qi_ref[p], kj_ref[p]

        @pl.when(p == 0)
        def _():
            dq_ref[...] = jnp.zeros_like(dq_ref)

        @pl.when(jnp.logical_and(p == 0, h == 0))
        def _():
            dkr_ref[...] = jnp.zeros_like(dkr_ref)

        @pl.when(i == j)
        def _():
            dk_sc[...] = jnp.zeros_like(dk_sc)
            dv_sc[...] = jnp.zeros_like(dv_sc)

        def update(diagonal):
            kcat = jnp.concatenate([kv_ref[:, :LANES], kr_ref[...]], axis=1)
            v = kv_ref[:, LANES:]
            for r in range(FLASH_SPLIT):
                rows = slice(r * half, (r + 1) * half)
                q_t, do_t = q_ref[rows, :], do_ref[rows, :]
                s = _raw_scores(q_t, kcat, r * half, diagonal)
                pr = jnp.exp2(s * SCORE_LOG2 - _lanes(lse_ref[rows, :] * LOG2E, blk))
                d_p = _dot(do_t, v, "nt")
                d_s = pr * (d_p - _lanes(dl_ref[rows, :], blk))
                dk_sc[...] += _dot(d_s, q_t, "tn")
                dv_sc[...] += _dot(pr, do_t, "tn")
                q_rows = pl.ds(pl.multiple_of(i * blk + r * half, half), half)
                dq_ref[q_rows, :] += _dot(d_s, kcat)

        @pl.when(i > j)
        def _():
            update(False)

        @pl.when(i == j)
        def _():
            update(True)

        @pl.when(i == n_blk - 1)
        def _():
            dk = dk_sc[...] * MLA_SCALE
            dkv_ref[:, :LANES] = dk[:, :LANES].astype(dkv_ref.dtype)
            dkv_ref[:, LANES:] = dv_sc[...].astype(dkv_ref.dtype)
            k_rows = pl.ds(pl.multiple_of(j * blk, blk), blk)
            dkr_ref[k_rows, :] += dk[:, LANES:]

        @pl.when(p == n_pairs - 1)
        def _():
            dq_ref[...] = dq_ref[...] * MLA_SCALE

    at_q = lambda h, p, qi_r, kj_r: (qi_r[p], h)
    at_kv = lambda h, p, qi_r, kj_r: (kj_r[p], h)
    grid_spec = pltpu.PrefetchScalarGridSpec(
        num_scalar_prefetch=2, grid=(heads, n_pairs),
        in_specs=[pl.BlockSpec((blk, MLA_HEAD_PAD), at_q),
                  pl.BlockSpec((blk, MLA_HEAD_PAD), at_kv),
                  pl.BlockSpec((blk, LANES), lambda h, p, qi_r, kj_r: (kj_r[p], 0)),
                  pl.BlockSpec((blk, MLA_V), at_q),
                  pl.BlockSpec((None, blk, LANES), lambda h, p, qi_r, kj_r: (h, qi_r[p], 0)),
                  pl.BlockSpec((blk, LANES), at_q)],
        out_specs=[pl.BlockSpec((n_l, MLA_HEAD_PAD), lambda h, p, qi_r, kj_r: (0, h)),
                   pl.BlockSpec((blk, MLA_HEAD_PAD), at_kv),
                   pl.BlockSpec((n_l, LANES), lambda h, p, qi_r, kj_r: (0, 0))],
        scratch_shapes=[pltpu.VMEM((blk, MLA_HEAD_PAD), F32), pltpu.VMEM((blk, MLA_V), F32)])
    return pl.pallas_call(
        body, name="mla_flash_bwd", grid_spec=grid_spec,
        out_shape=[_sds(qp.shape), _sds(kv.shape, BF16), _sds(kr.shape)],
        compiler_params=_params(("arbitrary", "arbitrary")),
    )(qi, kj, qp, kv, kr, d_o, lse, delta)


def _mla_fwd(h, p, rope):
    n_l = h.shape[0]
    cos_t, sin_t = rope
    hn = _rms_fwd(h, p["norm_g"], name="mla_rms")
    proj = _mm(hn, p["w_in"], name="mla_in", tn=896)
    cqn, ckvn, kr = _rowwise(_mla_mid, [proj, cos_t, sin_t], [p["q_norm_g"], p["kv_norm_g"]],
                             [_sds((n_l, MLA_Q_RANK), BF16), _sds((n_l, MLA_KV_RANK), BF16), _sds((n_l, LANES), BF16)],
                             [], tm=512, name="mla_mid")
    q_raw = _mm(cqn, p["w_uq"], name="mla_uq")
    qp = _rowwise(_mla_rope_q, [q_raw, cos_t, sin_t], [], [_sds(q_raw.shape, BF16)], [], tm=512, name="mla_rope_q")[0]
    kv = _mm(ckvn, p["w_ukv"], out_dtype=BF16, name="mla_ukv")
    o, lse = _flash_fwd(qp, kv, kr)
    gated = _rowwise(_mla_gate, [o, proj], [], [_sds(o.shape, BF16)], [], tm=512, name="mla_gate")[0]
    h_next = _mm(gated, p["w_out"], add=h, name="mla_out")
    return h_next, (h, hn, proj, cqn, ckvn, kr, qp, kv, o, lse, gated)


def _mla_bwd(dh_out, p, saved, rope):
    h, hn, proj, cqn, ckvn, kr, qp, kv, o, lse, gated = saved
    n_l = h.shape[0]
    cos_t, sin_t = rope
    d_gated = _mm(dh_out, p["w_out"], mode="nt", out_dtype=BF16, name="mla_dgated")
    g_w_out = _mm(gated, dh_out, mode="tn", name="mla_dwout")

    def gate_bwd(o_t, proj_t, ct):
        _, vjp = jax.vjp(lambda a, z: a * jax.nn.silu(z), o_t, proj_t[:, MLA_Z0:])
        d_o_t, d_z_t = vjp(ct.astype(F32))
        prod = d_o_t * o_t
        delta = jnp.concatenate(
            [jnp.broadcast_to(jnp.sum(prod[:, hd * MLA_V:(hd + 1) * MLA_V], axis=1, keepdims=True),
                              (prod.shape[0], MLA_V)) for hd in range(prod.shape[1] // MLA_V)], axis=1)
        return d_o_t, d_z_t, delta

    d_o, d_z, delta = _rowwise(gate_bwd, [o, proj, d_gated], [], [_sds(o.shape, BF16), _sds(o.shape), _sds(o.shape)],
                               [], tm=512, name="mla_gate_bwd")
    d_qp, d_kv, d_kr = _flash_bwd(qp, kv, kr, d_o, lse, delta)

    def rope_q_bwd(ct, c_t, s_t):
        return _mla_rope_q(ct, c_t, -s_t)

    d_q_raw = _rowwise(rope_q_bwd, [d_qp, cos_t, sin_t], [], [_sds(d_qp.shape, BF16)], [], tm=512,
                       name="mla_rope_q_bwd")[0]
    g_w_uq = _mm(cqn, d_q_raw, mode="tn", name="mla_dwuq")
    d_cqn = _mm(d_q_raw, p["w_uq"], mode="nt", name="mla_dcqn")
    g_w_ukv = _mm(ckvn, d_kv, mode="tn", name="mla_dwukv")
    d_ckvn = _mm(d_kv, p["w_ukv"], mode="nt", name="mla_dckvn")

    def mid_bwd(proj_t, c_t, s_t, d_cq, d_ckv, d_kr_t, d_z_t, q_g, kv_g):
        _, vjp_q = jax.vjp(_rms, proj_t[:, :MLA_Q_RANK], q_g)
        _, vjp_kv = jax.vjp(_rms, proj_t[:, MLA_Q_RANK:MLA_Q_RANK + MLA_KV_RANK], kv_g)
        d_q_in, d_qg = vjp_q(d_cq)
        d_kv_in, d_kvg = vjp_kv(d_ckv)
        d_kr_in = _rope_tile(d_kr_t, c_t, -s_t)
        return jnp.concatenate([d_q_in, d_kv_in, d_kr_in, d_z_t], axis=1), d_qg, d_kvg

    d_proj, g_q_norm, g_kv_norm = _rowwise(
        mid_bwd, [proj, cos_t, sin_t, d_cqn, d_ckvn, d_kr, d_z], [p["q_norm_g"], p["kv_norm_g"]],
        [_sds(proj.shape, BF16)], [_sds(p["q_norm_g"].shape), _sds(p["kv_norm_g"].shape)], tm=512, name="mla_mid_bwd")
    g_w_in = _mm(hn, d_proj, mode="tn", name="mla_dwin", tn=896)
    d_hn = _mm(d_proj, p["w_in"], mode="nt", name="mla_dhn", tk=896)
    dh_in, g_norm = _rms_bwd(h, p["norm_g"], d_hn, dh_out, name="mla_rms_bwd")
    grads = dict(norm_g=g_norm, w_in=g_w_in, q_norm_g=g_q_norm, w_uq=g_w_uq, kv_norm_g=g_kv_norm, w_ukv=g_w_ukv,
                 w_out=g_w_out)
    return dh_in, grads


def _loss_head(h, g, target):
    def fn(x, t, gg):
        def local(xx, g2):
            err = _rms(xx, g2) - t
            return 0.5 * jnp.sum(jnp.mean(err * err, axis=-1))

        val, (dx, dg) = jax.value_and_grad(local, argnums=(0, 1))(x, gg)
        return dx, jnp.full((1, LANES), val, F32), dg

    dh, loss, dg = _rowwise(fn, [h, target], [g], [_sds(h.shape)], [_sds((1, LANES)), _sds(g.shape)], tm=512,
                            name="loss_head")
    return loss[0, 0], dh, dg


HBM_SPEC = pl.BlockSpec(memory_space=pltpu.HBM)


def _all_gather(shard, *, name):
    def body(x_ref, out_ref, send_sems, recv_sems, local_sem):
        x, y, c = lax.axis_index("x"), lax.axis_index("y"), lax.axis_index("c")
        me, sibling = (x, y, c), (x, y, 1 - c)
        chips = [(1 - x, y), (x, 1 - y), (1 - x, 1 - y)]

        def rows(px, py, pc):
            return out_ref.at[4 * px + 2 * py + pc]

        def copy(k, block, to, src=None):
            return pltpu.make_async_remote_copy(
                src_ref=rows(*block) if src is None else src, dst_ref=rows(*block),
                send_sem=send_sems.at[k], recv_sem=recv_sems.at[k], device_id=to, device_id_type=MESH)

        mine = pltpu.make_async_copy(x_ref, rows(*me), local_sem)
        mine.start()
        first = [copy(0, me, sibling, src=x_ref)]
        first += [copy(1 + j, me, (*chip, c), src=x_ref) for j, chip in enumerate(chips)]
        for cp in first:
            cp.start()
        passed = [copy(4 + j, (*chip, c), sibling) for j, chip in enumerate(chips)]
        for j, chip in enumerate(chips):
            copy(1 + j, (*chip, c), me).wait_recv()
            passed[j].start()
        copy(0, sibling, me).wait_recv()
        for j, chip in enumerate(chips):
            copy(4 + j, (*chip, 1 - c), me).wait_recv()
        for cp in first + passed:
            cp.wait_send()
        mine.wait()

    return pl.pallas_call(
        body, name=name, out_shape=jax.ShapeDtypeStruct((N_DEV,) + shard.shape, shard.dtype),
        in_specs=[HBM_SPEC], out_specs=HBM_SPEC,
        scratch_shapes=[pltpu.SemaphoreType.DMA((7,)), pltpu.SemaphoreType.DMA((7,)), pltpu.SemaphoreType.DMA],
    )(shard)


def _exchange(src, routes, *, name):
    n_routes = len(routes)

    def body(s_ref, out_ref, send_sems, recv_sems):
        x, y, c = lax.axis_index("x"), lax.axis_index("y"), lax.axis_index("c")
        local, remote = [], []
        for k, (flip, block) in enumerate(routes):
            src_blk = s_ref.at[block(x, y, c)]
            if flip == 0:
                local.append(pltpu.make_async_copy(src_blk, out_ref.at[k], send_sems.at[k]))
            else:
                peer = (1 - x if flip & 4 else x, 1 - y if flip & 2 else y, 1 - c if flip & 1 else c)
                remote.append(pltpu.make_async_remote_copy(
                    src_ref=src_blk, dst_ref=out_ref.at[k], send_sem=send_sems.at[k], recv_sem=recv_sems.at[k],
                    device_id=peer, device_id_type=MESH))
        for cp in local + remote:
            cp.start()
        for cp in remote:
            cp.wait_recv()
        for cp in remote:
            cp.wait_send()
        for cp in local:
            cp.wait()

    return pl.pallas_call(
        body, name=name, out_shape=jax.ShapeDtypeStruct((n_routes,) + src.shape[1:], src.dtype),
        in_specs=[HBM_SPEC], out_specs=HBM_SPEC,
        scratch_shapes=[pltpu.SemaphoreType.DMA((n_routes,)), pltpu.SemaphoreType.DMA((n_routes,))],
    )(src)


def _reduce_scatter(send, *, name):
    def chip_block(k, other_core):
        return lambda x, y, c: (4 * (1 - x if k & 2 else x) + 2 * (1 - y if k & 1 else y)
                                + (1 - c if other_core else c))

    n_chips = 4
    pair = _exchange(send, [(1, chip_block(k, True)) for k in range(n_chips)], name=name + "_pair")
    rows, width = send.shape[1], send.shape[2]
    tr = _pick(rows, 256, 8)
    x, y, c = lax.axis_index("x"), lax.axis_index("y"), lax.axis_index("c")
    own_ids = jnp.stack([chip_block(k, False)(x, y, c) for k in range(n_chips)]).astype(jnp.int32)

    def add_body(ids_ref, *refs):
        own_refs, p_ref, o_ref = refs[:n_chips], refs[n_chips], refs[n_chips + 1]
        for k in range(n_chips):
            o_ref[k] = (own_refs[k][...].astype(F32) + p_ref[k].astype(F32)).astype(o_ref.dtype)

    own_spec = lambda k: pl.BlockSpec((None, tr, width), lambda i, ids: (ids[k], i, 0))
    chip_sums = pl.pallas_call(
        add_body, name=name + "_pair_sum",
        grid_spec=pltpu.PrefetchScalarGridSpec(
            num_scalar_prefetch=1, grid=(rows // tr,),
            in_specs=[own_spec(k) for k in range(n_chips)] + [pl.BlockSpec((n_chips, tr, width), lambda i, ids: (0, i, 0))],
            out_specs=pl.BlockSpec((n_chips, tr, width), lambda i, ids: (0, i, 0))),
        out_shape=jax.ShapeDtypeStruct((n_chips, rows, width), send.dtype), compiler_params=_params(("parallel",)),
    )(own_ids, *([send] * n_chips), pair)
    recv = _exchange(chip_sums, [(2 * k, (lambda kk: lambda x, y, c: kk)(k)) for k in range(1, n_chips)],
                     name=name + "_chips")

    def sum_body(q_ref, r_ref, o_ref):
        acc = q_ref[...].astype(F32)
        for k in range(n_chips - 1):
            acc = acc + r_ref[k].astype(F32)
        o_ref[...] = acc

    return pl.pallas_call(
        sum_body, name=name + "_sum", grid=(rows // tr,),
        in_specs=[pl.BlockSpec((None, tr, width), lambda i: (0, i, 0)),
                  pl.BlockSpec((n_chips - 1, tr, width), lambda i: (0, i, 0))],
        out_specs=pl.BlockSpec((tr, width), lambda i: (i, 0)), out_shape=_sds((rows, width)),
        compiler_params=_params(("parallel",)),
    )(chip_sums, recv)


def _adamw(w, g, m, v, *, name):
    def fn(ww, gg, mm, vv):
        m_new = ADAM_B1 * mm + (1.0 - ADAM_B1) * gg
        v_new = ADAM_B2 * vv + (1.0 - ADAM_B2) * jnp.square(gg)
        m_hat = m_new / (1.0 - ADAM_B1 ** ADAM_STEP)
        v_hat = v_new / (1.0 - ADAM_B2 ** ADAM_STEP)
        return -ADAM_LR * (m_hat / (jnp.sqrt(v_hat) + ADAM_EPS) + ADAM_WD * ww), m_new, v_new

    return _rowwise(fn, [w, g, m, v], [], [_sds(w.shape)] * 3, [], tm=256, name=name)


KINDS = ("gmlp", "s5", "mla", "gmlp")
LAYER_NAMES = {
    "gmlp": ("norm_g", "w_in", "ln_g", "ln_b", "w_s", "b_s", "w_out"),
    "s5": ("norm_g", "w_in", "a_re", "a_im", "log_step", "b_re", "b_im", "c_re", "c_im", "d_skip", "w_glu", "b_glu",
           "w_out"),
    "mla": ("norm_g", "w_in", "q_norm_g", "w_uq", "kv_norm_g", "w_ukv", "w_out"),
}
COL_SHARDED = ("w_in", "w_uq", "w_ukv")
ROW_SHARDED = ("w_out", "w_glu")
WEIGHT_NAMES = tuple("l%d_%s" % (i, n) for i, k in enumerate(KINDS) for n in LAYER_NAMES[k]) + ("final_norm_g",)


def _is_sharded(name):
    return name.split("_", 1)[1] in COL_SHARDED + ROW_SHARDED


def _flatten(arrs, pad_rows_to):
    parts, sizes = [], []
    for a in arrs:
        flat = a.reshape(-1)
        pad = (-flat.shape[0]) % FLAT_W
        if pad:
            flat = jnp.pad(flat, (0, pad))
        parts.append(flat)
        sizes.append(flat.shape[0] // FLAT_W)
    rows = sum(sizes)
    pad_rows = (-rows) % pad_rows_to
    if pad_rows:
        parts.append(jnp.zeros((pad_rows * FLAT_W,), arrs[0].dtype))
    return jnp.concatenate(parts).reshape(-1, FLAT_W), sizes


def _unflatten(flat, shapes, sizes):
    out, row = [], 0
    for shape, n_rows in zip(shapes, sizes):
        n = int(np.prod(shape))
        out.append(flat[row:row + n_rows].reshape(-1)[:n].reshape(shape))
        row += n_rows
    return out


def _full_from_gathered(blocks, name):
    if name.split("_", 1)[1] in COL_SHARDED:
        return blocks.transpose(1, 0, 2).reshape(blocks.shape[1], -1)
    return blocks.reshape(-1, blocks.shape[2])


def _shards_of(full, name):
    if name.split("_", 1)[1] in COL_SHARDED:
        return full.reshape(full.shape[0], N_DEV, -1).transpose(1, 0, 2)
    return full.reshape(N_DEV, -1, full.shape[1])


def _rope_tables(positions):
    inv_freq = ROPE_THETA ** (-jnp.arange(0, MLA_ROPE, 2, dtype=F32) / MLA_ROPE)
    ang = positions.astype(F32)[:, None] * inv_freq
    cos, sin = jnp.cos(ang), jnp.sin(ang)
    zero = jnp.zeros((positions.shape[0], LANES - MLA_ROPE), F32)
    return jnp.concatenate([cos, cos, zero], axis=1), jnp.concatenate([-sin, sin, zero], axis=1)


def _row(v):
    return v.reshape(1, -1)


def kernel(x, positions, l0_norm_g, l0_w_in, l0_ln_g, l0_ln_b, l0_w_s, l0_b_s, l0_w_out, l1_norm_g, l1_w_in, l1_a_re, l1_a_im, l1_log_step, l1_b_re, l1_b_im, l1_c_re, l1_c_im, l1_d_skip, l1_w_glu, l1_b_glu, l1_w_out, l2_norm_g, l2_w_in, l2_q_norm_g, l2_w_uq, l2_kv_norm_g, l2_w_ukv, l2_w_out, l3_norm_g, l3_w_in, l3_ln_g, l3_ln_b, l3_w_s, l3_b_s, l3_w_out, final_norm_g, loss_target, m_l0_norm_g, m_l0_w_in, m_l0_ln_g, m_l0_ln_b, m_l0_w_s, m_l0_b_s, m_l0_w_out, m_l1_norm_g, m_l1_w_in, m_l1_a_re, m_l1_a_im, m_l1_log_step, m_l1_b_re, m_l1_b_im, m_l1_c_re, m_l1_c_im, m_l1_d_skip, m_l1_w_glu, m_l1_b_glu, m_l1_w_out, m_l2_norm_g, m_l2_w_in, m_l2_q_norm_g, m_l2_w_uq, m_l2_kv_norm_g, m_l2_w_ukv, m_l2_w_out, m_l3_norm_g, m_l3_w_in, m_l3_ln_g, m_l3_ln_b, m_l3_w_s, m_l3_b_s, m_l3_w_out, m_final_norm_g, v_l0_norm_g, v_l0_w_in, v_l0_ln_g, v_l0_ln_b, v_l0_w_s, v_l0_b_s, v_l0_w_out, v_l1_norm_g, v_l1_w_in, v_l1_a_re, v_l1_a_im, v_l1_log_step, v_l1_b_re, v_l1_b_im, v_l1_c_re, v_l1_c_im, v_l1_d_skip, v_l1_w_glu, v_l1_b_glu, v_l1_w_out, v_l2_norm_g, v_l2_w_in, v_l2_q_norm_g, v_l2_w_uq, v_l2_kv_norm_g, v_l2_w_ukv, v_l2_w_out, v_l3_norm_g, v_l3_w_in, v_l3_ln_g, v_l3_ln_b, v_l3_w_s, v_l3_b_s, v_l3_w_out, v_final_norm_g):
    args = locals()
    weights = {n: args[n] for n in WEIGHT_NAMES}
    mom_m = {n: args["m_" + n] for n in WEIGHT_NAMES}
    mom_v = {n: args["v_" + n] for n in WEIGHT_NAMES}
    return _train_step(x, positions, loss_target, weights, mom_m, mom_v)


def _train_step(x, positions, loss_target, weights, mom_m, mom_v):
    big = [n for n in WEIGHT_NAMES if _is_sharded(n)]
    small = [n for n in WEIGHT_NAMES if not _is_sharded(n)]

    w_flat, big_sizes = _flatten([weights[n] for n in big], 8)
    gathered = _all_gather(w_flat.astype(BF16), name="weights_all_gather")
    full, row = {}, 0
    for n, n_rows in zip(big, big_sizes):
        blocks = gathered[:, row:row + n_rows].reshape((N_DEV,) + weights[n].shape)
        full[n] = _full_from_gathered(blocks, n)
        row += n_rows

    layers, ops_vjps = [], {}
    for i, kind in enumerate(KINDS):
        pre = "l%d_" % i
        p = {n: (full[pre + n] if _is_sharded(pre + n) else weights[pre + n]) for n in LAYER_NAMES[kind]}
        p["norm_g"] = _row(p["norm_g"])
        if kind == "gmlp":
            p["ln_g"], p["ln_b"], p["b_st"] = _row(p["ln_g"]), _row(p["ln_b"]), p["b_s"].T
        elif kind == "s5":
            p["d_skip"], p["b_glu"] = _row(p["d_skip"]), _row(p["b_glu"])
            ops, ops_vjps[i] = jax.vjp(_s5_operators, *[p[n] for n in ("a_re", "a_im", "log_step", "b_re", "b_im",
                                                                       "c_re", "c_im")])
            p["ops"] = tuple(o.astype(BF16) for o in ops[:3]) + ops[3:]
        else:
            heads = p["w_uq"].shape[1] // MLA_QK_DIM
            w_in = p["w_in"]
            split = MLA_Q_RANK + MLA_KV_RANK + MLA_ROPE
            p["w_in"] = jnp.concatenate([w_in[:, :split], jnp.zeros((w_in.shape[0], LANES - MLA_ROPE), w_in.dtype),
                                         w_in[:, split:]], axis=1)
            p["w_uq"] = jnp.pad(p["w_uq"].reshape(-1, heads, MLA_QK_DIM),
                                ((0, 0), (0, 0), (0, MLA_HEAD_PAD - MLA_QK_DIM))).reshape(-1, heads * MLA_HEAD_PAD)
            p["q_norm_g"], p["kv_norm_g"] = _row(p["q_norm_g"]), _row(p["kv_norm_g"])
        layers.append(p)
    rope = _rope_tables(positions[0])

    h = x[0]
    saved = []
    for kind, p in zip(KINDS, layers):
        if kind == "gmlp":
            h, s = _gmlp_fwd(h, p)
        elif kind == "s5":
            h, s = _s5_fwd(h, p)
        else:
            h, s = _mla_fwd(h, p, rope)
        saved.append(s)
    loss_local, dh, g_final = _loss_head(h, _row(weights["final_norm_g"]), loss_target[0])
    loss = lax.psum(loss_local, ("x", "y", "c"))

    grads = {"final_norm_g": g_final.reshape(-1)}
    for i in reversed(range(len(KINDS))):
        kind, p = KINDS[i], layers[i]
        if kind == "gmlp":
            dh, g = _gmlp_bwd(dh, p, saved[i])
        elif kind == "s5":
            dh, g = _s5_bwd(dh, p, saved[i], ops_vjps[i])
        else:
            dh, g = _mla_bwd(dh, p, saved[i], rope)
            heads = weights["l%d_w_uq" % i].shape[1] * N_DEV // MLA_QK_DIM
            split = MLA_Q_RANK + MLA_KV_RANK + MLA_ROPE
            g["w_in"] = jnp.concatenate([g["w_in"][:, :split], g["w_in"][:, MLA_Z0:]], axis=1)
            g["w_uq"] = g["w_uq"].reshape(-1, heads, MLA_HEAD_PAD)[:, :, :MLA_QK_DIM].reshape(-1, heads * MLA_QK_DIM)
        for n, val in g.items():
            name = "l%d_%s" % (i, n)
            grads[name] = val.reshape(weights[name].shape) if not _is_sharded(name) else val

    small_flat, small_sizes = _flatten([grads[n] for n in small], 8 * N_DEV)
    small_rows = small_flat.shape[0] // N_DEV
    send_parts = [_shards_of(grads[n], n).reshape(N_DEV, -1, FLAT_W) for n in big]
    send_parts.append(small_flat.reshape(N_DEV, small_rows, FLAT_W))
    send = jnp.concatenate(send_parts, axis=1)
    big_rows = send.shape[1] - small_rows
    pad_rows = (-send.shape[1]) % 8
    if pad_rows:
        send = jnp.pad(send, ((0, 0), (0, pad_rows), (0, 0)))
    reduced = _reduce_scatter(send.astype(BF16), name="grads")
    g_big_flat = reduced[:big_rows]
    g_small_all = _all_gather(reduced[big_rows:big_rows + small_rows], name="small_grads_all_gather")
    g_small_flat = g_small_all.reshape(-1, FLAT_W)

    def flat_of(tree, names, pad_to):
        return _flatten([tree[n] for n in names], pad_to)[0]

    outs = {}
    big_shapes = [weights[n].shape for n in big]
    for n, g_n in zip(big, _unflatten(g_big_flat, big_shapes, big_sizes)):
        outs["grad_" + n] = g_n
        outs["delta_" + n], outs["new_m_" + n], outs["new_v_" + n] = _adamw(
            weights[n], g_n, mom_m[n], mom_v[n], name="adamw_" + n)
    d_s, nm_s, nv_s = _adamw(flat_of(weights, small, 8 * N_DEV), g_small_flat, flat_of(mom_m, small, 8 * N_DEV),
                             flat_of(mom_v, small, 8 * N_DEV), name="adamw_replicated")
    small_shapes = [weights[n].shape for n in small]
    for prefix, fs in (("grad_", g_small_flat), ("delta_", d_s), ("new_m_", nm_s), ("new_v_", nv_s)):
        for n, a in zip(small, _unflatten(fs, small_shapes, small_sizes)):
            outs[prefix + n] = a
    result = [loss, dh[None]]
    for prefix in ("grad_", "delta_", "new_m_", "new_v_"):
        result += [outs[prefix + n] for n in WEIGHT_NAMES]
    return tuple(result)
```

```python
import functools
import math

import numpy as np
import jax
import jax.numpy as jnp
from jax import lax
from jax.experimental import pallas as pl
from jax.experimental.pallas import tpu as pltpu

F32 = jnp.float32
BF16 = jnp.bfloat16

NORM_EPS = 1e-6
GMLP_CHUNK = 128
S5_GROUP = 16
S5_STATE = 64
S5_T = 16
MLA_NOPE = 128
MLA_ROPE = 64
MLA_V = 128
MLA_QK_DIM = MLA_NOPE + MLA_ROPE
MLA_Q_RANK = 384
MLA_KV_RANK = 128
MLA_HEAD_PAD = 256
MLA_SCALE = MLA_QK_DIM ** -0.5
ROPE_THETA = 10000.0
NEG_INF = -1e30
ADAM_LR = 0.001
ADAM_B1 = 0.9
ADAM_B2 = 0.999
ADAM_EPS = 1e-08
ADAM_WD = 0.01
ADAM_STEP = 10

N_DEV = 8
LANES = 128
FLAT_W = 1024
VMEM_LIMIT = 56 * 1024 * 1024
MESH = pl.DeviceIdType.MESH


def _pick(dim, pref, align=LANES):
    t = (min(pref, dim) // align) * align
    while t >= align:
        if dim % t == 0:
            return t
        t -= align
    return dim


def _params(sem=None):
    return pltpu.CompilerParams(dimension_semantics=sem, vmem_limit_bytes=VMEM_LIMIT)


_DIMS = {"nn": (((1,), (0,)), ((), ())), "nt": (((1,), (1,)), ((), ())), "tn": (((0,), (0,)), ((), ()))}


def _mm(a, b, *, mode="nn", out_dtype=F32, add=None, name, tm=1024, tn=1024, tk=2048):
    if mode == "nn":
        (m, k), (_, n) = a.shape, b.shape
    elif mode == "nt":
        (m, k), (n, _) = a.shape, b.shape
    else:
        (k, m), (_, n) = a.shape, b.shape
    tm, tn, tk = _pick(m, tm, 8), _pick(n, tn), _pick(k, tk)
    nk = k // tk
    dims = _DIMS[mode]

    def body(*refs):
        a_ref, b_ref = refs[:2]
        r_ref = refs[2] if add is not None else None
        o_ref = refs[3] if add is not None else refs[2]
        part = lax.dot_general(a_ref[...].astype(BF16), b_ref[...].astype(BF16), dims, preferred_element_type=F32)

        def finish(res):
            if add is not None:
                res = res + r_ref[...]
            o_ref[...] = res.astype(o_ref.dtype)

        if nk == 1:
            finish(part)
            return
        acc_ref = refs[-1]
        kk = pl.program_id(2)

        @pl.when(kk == 0)
        def _():
            acc_ref[...] = part

        @pl.when(kk > 0)
        def _():
            acc_ref[...] += part

        @pl.when(kk == nk - 1)
        def _():
            finish(acc_ref[...])

    a_spec = (pl.BlockSpec((tk, tm), lambda i, j, kk: (kk, i)) if mode == "tn"
              else pl.BlockSpec((tm, tk), lambda i, j, kk: (i, kk)))
    b_spec = (pl.BlockSpec((tn, tk), lambda i, j, kk: (j, kk)) if mode == "nt"
              else pl.BlockSpec((tk, tn), lambda i, j, kk: (kk, j)))
    in_specs = [a_spec, b_spec]
    args = [a, b]
    if add is not None:
        in_specs.append(pl.BlockSpec((tm, tn), lambda i, j, kk: (i, j)))
        args.append(add)
    return pl.pallas_call(
        body, name=name, grid=(m // tm, n // tn, nk),
        in_specs=in_specs, out_specs=pl.BlockSpec((tm, tn), lambda i, j, kk: (i, j)),
        out_shape=jax.ShapeDtypeStruct((m, n), out_dtype),
        scratch_shapes=[pltpu.VMEM((tm, tn), F32)] if nk > 1 else [],
        compiler_params=_params(("parallel", "parallel", "arbitrary")),
    )(*args)


def _rowwise(fn, tiled, full, out_tiled, out_acc, *, tm, name):
    rows = tiled[0].shape[0]
    tm = _pick(rows, tm, 8)
    nt, nf, no = len(tiled), len(full), len(out_tiled)

    def body(*refs):
        ins = [r[...] for r in refs[:nt + nf]]
        o_refs = refs[nt + nf:nt + nf + no]
        a_refs = refs[nt + nf + no:]
        outs = fn(*ins)
        if not isinstance(outs, (tuple, list)):
            outs = (outs,)
        for r, v in zip(o_refs, outs[:no]):
            r[...] = v.astype(r.dtype)
        if a_refs:
            @pl.when(pl.program_id(0) == 0)
            def _():
                for r in a_refs:
                    r[...] = jnp.zeros_like(r)

            for r, v in zip(a_refs, outs[no:]):
                r[...] += v.astype(r.dtype)

    def whole(shape):
        nd = len(shape)
        return pl.BlockSpec(tuple(shape), lambda i: (0,) * nd)

    in_specs = ([pl.BlockSpec((tm, t.shape[1]), lambda i: (i, 0)) for t in tiled]
                + [whole(f.shape) for f in full])
    out_specs = ([pl.BlockSpec((tm, o.shape[1]), lambda i: (i, 0)) for o in out_tiled]
                 + [whole(o.shape) for o in out_acc])
    outs = pl.pallas_call(
        body, name=name, grid=(rows // tm,), in_specs=in_specs, out_specs=out_specs,
        out_shape=list(out_tiled) + list(out_acc),
        compiler_params=_params(("arbitrary",)),
    )(*tiled, *full)
    return outs


def _sds(shape, dtype=F32):
    return jax.ShapeDtypeStruct(tuple(shape), dtype)


def _rms(x, g):
    return x * lax.rsqrt(jnp.mean(x * x, axis=-1, keepdims=True) + NORM_EPS) * g


def _layernorm(x, g, b):
    mu = jnp.mean(x, axis=-1, keepdims=True)
    xc = x - mu
    var = jnp.mean(xc * xc, axis=-1, keepdims=True)
    return xc * lax.rsqrt(var + NORM_EPS) * g + b


def _dot(a, b, mode="nn"):
    return lax.dot_general(a.astype(BF16), b.astype(BF16), _DIMS[mode], preferred_element_type=F32)


@jax.custom_vjp
def _bdot(a, b):
    return _dot(a, b)


def _bdot_fwd(a, b):
    return _bdot(a, b), (a, b)


def _bdot_bwd(res, ct):
    a, b = res
    return _dot(ct, b, "nt"), _dot(a, ct, "tn")


_bdot.defvjp(_bdot_fwd, _bdot_bwd)


def _rms_fwd(h, g, *, name):
    return _rowwise(lambda x, gg: _rms(x, gg), [h], [g], [_sds(h.shape, BF16)], [], tm=512, name=name)[0]


def _rms_bwd(h, g, d_hn, dh_out, *, name):
    def fn(x, ct, res, gg):
        _, vjp = jax.vjp(_rms, x, gg)
        dx, dg = vjp(ct)
        return res + dx, dg

    dh, dg = _rowwise(fn, [h, d_hn, dh_out], [g], [_sds(h.shape)], [_sds(g.shape)], tm=512, name=name)
    return dh, dg


def _gmlp_mid(uvz, ln_g, ln_b, w_s, b_st):
    di = ln_g.shape[1]
    ng, ck = w_s.shape[0], w_s.shape[1]
    dg = di // ng
    u = jax.nn.gelu(uvz[:, :di])
    v = _layernorm(jax.nn.gelu(uvz[:, di:2 * di]), ln_g, ln_b)
    z = uvz[:, 2 * di:]
    row = lax.broadcasted_iota(jnp.int32, (ck, ck), 0)
    col = lax.broadcasted_iota(jnp.int32, (ck, ck), 1)
    causal = col <= row
    blocks = []
    for c in range(uvz.shape[0] // ck):
        cols = []
        for g in range(ng):
            w = jnp.where(causal, w_s[g], 0.0)
            cols.append(_bdot(w, v[c * ck:(c + 1) * ck, g * dg:(g + 1) * dg]) + b_st[:, g:g + 1])
        blocks.append(jnp.concatenate(cols, axis=1))
    s = blocks[0] if len(blocks) == 1 else jnp.concatenate(blocks, axis=0)
    return u * s * jax.nn.silu(z)


def _gmlp_fwd(h, p):
    hn = _rms_fwd(h, p["norm_g"], name="gmlp_rms")
    uvz = _mm(hn, p["w_in"], name="gmlp_in")
    di = p["ln_g"].shape[1]
    gated = _rowwise(_gmlp_mid, [uvz], [p["ln_g"], p["ln_b"], p["w_s"], p["b_st"]],
                     [_sds((h.shape[0], di), BF16)], [], tm=256, name="gmlp_mid")[0]
    h_next = _mm(gated, p["w_out"], add=h, name="gmlp_out")
    return h_next, (h, hn, uvz, gated)


def _gmlp_bwd(dh_out, p, saved):
    h, hn, uvz, gated = saved
    d_gated = _mm(dh_out, p["w_out"], mode="nt", out_dtype=BF16, name="gmlp_dgated")
    g_w_out = _mm(gated, dh_out, mode="tn", name="gmlp_dwout")

    def fn(t, ct, ln_g, ln_b, w_s, b_st):
        _, vjp = jax.vjp(_gmlp_mid, t, ln_g, ln_b, w_s, b_st)
        return vjp(ct.astype(F32))

    d_uvz, g_ln_g, g_ln_b, g_w_s, g_b_st = _rowwise(
        fn, [uvz, d_gated], [p["ln_g"], p["ln_b"], p["w_s"], p["b_st"]],
        [_sds(uvz.shape, BF16)], [_sds(p["ln_g"].shape), _sds(p["ln_b"].shape), _sds(p["w_s"].shape),
                                  _sds(p["b_st"].shape)], tm=256, name="gmlp_mid_bwd")
    g_w_in = _mm(hn, d_uvz, mode="tn", name="gmlp_dwin")
    d_hn = _mm(d_uvz, p["w_in"], mode="nt", name="gmlp_dhn")
    dh_in, g_norm = _rms_bwd(h, p["norm_g"], d_hn, dh_out, name="gmlp_rms_bwd")
    grads = dict(norm_g=g_norm, w_in=g_w_in, ln_g=g_ln_g, ln_b=g_ln_b, w_s=g_w_s, b_s=g_b_st.T, w_out=g_w_out)
    return dh_in, grads


def _s5_operators(a_re, a_im, log_step, b_re, b_im, c_re, c_im):
    t_len = S5_T
    step = jnp.exp(log_step)[:, None]
    lr, li = a_re * step, a_im * step
    ks = jnp.arange(t_len + 1, dtype=F32)[:, None, None]
    mag = jnp.exp(ks * lr)
    pw_r, pw_i = mag * jnp.cos(ks * li), mag * jnp.sin(ks * li)
    nr, ni = pw_r[1] - 1.0, pw_i[1]
    den = a_re * a_re + a_im * a_im
    f_r, f_i = (nr * a_re + ni * a_im) / den, (ni * a_re - nr * a_im) / den
    bb_r = f_r[..., None] * b_re - f_i[..., None] * b_im
    bb_i = f_r[..., None] * b_im + f_i[..., None] * b_re
    hi = lax.Precision.HIGHEST
    cp_r = c_re[None] * pw_r[:, :, None, :] - c_im[None] * pw_i[:, :, None, :]
    cp_i = c_re[None] * pw_i[:, :, None, :] + c_im[None] * pw_r[:, :, None, :]
    n_g, n_h = a_re.shape[0], b_re.shape[2]
    lhs = jnp.concatenate([bb_r, -bb_i], axis=1)
    rhs = jnp.concatenate([cp_r[:t_len], cp_i[:t_len]], axis=3)
    rhs = rhs.transpose(1, 3, 0, 2).reshape(n_g, -1, t_len * n_h)
    kcat = jnp.einsum("gqi,gqn->gin", lhs, rhs, precision=hi)
    toep = jnp.stack([jnp.pad(kcat[:, :, :(t_len - s) * n_h], ((0, 0), (0, 0), (s * n_h, 0)))
                      for s in range(t_len)], axis=1).reshape(n_g, t_len * n_h, t_len * n_h)
    rev_r, rev_i = pw_r[t_len - 1::-1][:t_len], pw_i[t_len - 1::-1][:t_len]
    we_r = rev_r[..., None] * bb_r[None] - rev_i[..., None] * bb_i[None]
    we_i = rev_r[..., None] * bb_i[None] + rev_i[..., None] * bb_r[None]
    wend = jnp.concatenate([we_r, we_i], axis=2).transpose(1, 0, 3, 2).reshape(n_g, t_len * n_h, -1)
    wo = jnp.concatenate([cp_r[1:], -cp_i[1:]], axis=3)
    wout = wo.transpose(1, 3, 0, 2).reshape(n_g, -1, t_len * n_h)
    a_r, a_i = pw_r[t_len], pw_i[t_len]
    a1 = jnp.concatenate([a_r, a_r], axis=1)
    a2 = jnp.concatenate([-a_i, a_i], axis=1)
    return toep, wend, wout, a1, a2


def _group_call(body, ins, outs, *, gb, name):
    n_g = ins[0].shape[0]

    def spec(a):
        return pl.BlockSpec((gb,) + tuple(a.shape[1:]), lambda i: (i, 0, 0))

    return pl.pallas_call(
        body, name=name, grid=(n_g // gb,), in_specs=[spec(a) for a in ins],
        out_specs=[spec(o) for o in outs], out_shape=list(outs),
        compiler_params=_params(("parallel",)),
    )(*ins)


def _s5_states(u_g, wend, *, gb=8):
    def body(u_ref, w_ref, s_ref):
        for g in range(gb):
            s_ref[g] = _dot(u_ref[g], w_ref[g])

    n_g, n_c = u_g.shape[0], u_g.shape[1]
    return _group_call(body, [u_g, wend], [_sds((n_g, n_c, wend.shape[2]))], gb=gb, name="s5_states")[0]


def _s5_outputs(u_g, toep, xprev, wout, *, gb=8):
    def body(u_ref, t_ref, x_ref, w_ref, y_ref):
        for g in range(gb):
            y_ref[g] = _dot(u_ref[g], t_ref[g]) + _dot(x_ref[g], w_ref[g])

    return _group_call(body, [u_g, toep, xprev, wout], [_sds(u_g.shape)], gb=gb, name="s5_outputs")[0]


def _s5_outputs_bwd(u_g, d_y, xprev, wout, *, gb=8):
    def body(u_ref, dy_ref, x_ref, w_ref, dt_ref, dw_ref, dx_ref):
        for g in range(gb):
            dy = dy_ref[g]
            dt_ref[g] = _dot(u_ref[g], dy, "tn")
            dw_ref[g] = _dot(x_ref[g], dy, "tn")
            dx_ref[g] = _dot(dy, w_ref[g], "nt")

    n_g, n_c, n_k = u_g.shape
    return _group_call(body, [u_g, d_y, xprev, wout],
                       [_sds((n_g, n_k, n_k)), _sds(wout.shape), _sds(xprev.shape)], gb=gb, name="s5_outputs_bwd")


def _s5_inputs_bwd(u_g, d_y, d_s, toep, wend, *, gb=8):
    def body(u_ref, dy_ref, ds_ref, t_ref, w_ref, du_ref, dw_ref):
        for g in range(gb):
            ds = ds_ref[g]
            du_ref[g] = _dot(dy_ref[g], t_ref[g], "nt") + _dot(ds, w_ref[g], "nt")
            dw_ref[g] = _dot(u_ref[g], ds, "tn")

    return _group_call(body, [u_g, d_y, d_s, toep, wend], [_sds(u_g.shape), _sds(wend.shape)], gb=gb,
                       name="s5_inputs_bwd")


def _swap_halves(x):
    return pltpu.roll(x, x.shape[-1] // 2, axis=x.ndim - 1)


def _s5_scan(s_t, a1, a2, *, gb=32):
    n_c, n_g, n_p = s_t.shape
    gb = min(gb, n_g)

    def body(s_ref, a1_ref, a2_ref, x_ref):
        a1v, a2v = a1_ref[...], a2_ref[...]
        a2s = _swap_halves(a2v)

        def step(c, carry):
            x, xs = carry
            x_ref[c] = x
            s = s_ref[c]
            return x * a1v + xs * a2v + s, xs * a1v + x * a2s + _swap_halves(s)

        zero = jnp.zeros((gb, n_p), F32)
        lax.fori_loop(0, n_c, step, (zero, zero), unroll=4 if n_c % 4 == 0 else 1)

    return pl.pallas_call(
        body, name="s5_scan", grid=(n_g // gb,),
        in_specs=[pl.BlockSpec((n_c, gb, n_p), lambda i: (0, i, 0)), pl.BlockSpec((gb, n_p), lambda i: (i, 0)),
                  pl.BlockSpec((gb, n_p), lambda i: (i, 0))],
        out_specs=pl.BlockSpec((n_c, gb, n_p), lambda i: (0, i, 0)), out_shape=_sds(s_t.shape),
        compiler_params=_params(("parallel",)),
    )(s_t, a1, a2)


def _s5_scan_bwd(d_xprev_t, xprev_t, a1, a2, *, gb=32):
    n_c, n_g, n_p = xprev_t.shape
    gb = min(gb, n_g)

    def body(dx_ref, x_ref, a1_ref, a2_ref, ds_ref, p1_ref, p2_ref):
        a1v, a2v = a1_ref[...], a2_ref[...]
        a2s = _swap_halves(a2v)
        zero = jnp.zeros((gb, n_p), F32)
        ds_ref[n_c - 1] = zero

        def step(k, carry):
            gx_next, gs_next, p1, p2 = carry
            c = n_c - 2 - k
            xp = x_ref[c + 1]
            d = dx_ref[c + 1]
            gx = d + gx_next * a1v - gs_next * a2v
            gs = _swap_halves(d) + gs_next * a1v - gx_next * a2s
            ds_ref[c] = gx
            return gx, gs, p1 + gx_next * xp, p2 + gx_next * _swap_halves(xp)

        _, _, p1, p2 = lax.fori_loop(0, n_c - 1, step, (zero, zero, zero, zero),
                                     unroll=5 if (n_c - 1) % 5 == 0 else 1)
        p1_ref[...] = p1
        p2_ref[...] = p2

    blk = pl.BlockSpec((n_c, gb, n_p), lambda i: (0, i, 0))
    vec = pl.BlockSpec((gb, n_p), lambda i: (i, 0))
    return pl.pallas_call(
        body, name="s5_scan_bwd", grid=(n_g // gb,), in_specs=[blk, blk, vec, vec],
        out_specs=[blk, vec, vec], out_shape=[_sds(xprev_t.shape), _sds(a1.shape), _sds(a1.shape)],
        compiler_params=_params(("parallel",)),
    )(d_xprev_t, xprev_t, a1, a2)


GROUPS_PER_TILE = LANES // S5_GROUP


def _to_groups(t, n_g):
    n_l = t.shape[0]
    n_c = n_l // S5_T
    gpt = min(GROUPS_PER_TILE, n_g)
    width = gpt * S5_GROUP

    def body(x_ref, o_ref):
        tr = [x_ref[pl.ds(s, n_c, stride=S5_T), :].T for s in range(S5_T)]
        for gl in range(gpt):
            rows = slice(gl * S5_GROUP, (gl + 1) * S5_GROUP)
            stacked = jnp.concatenate([tr[s][rows, :] for s in range(S5_T)], axis=0)
            o_ref[gl] = stacked.T.astype(o_ref.dtype)

    return pl.pallas_call(
        body, name="s5_to_groups", grid=(n_g // gpt,),
        in_specs=[pl.BlockSpec((n_l, width), lambda b: (0, b))],
        out_specs=pl.BlockSpec((gpt, n_c, S5_T * S5_GROUP), lambda b: (b, 0, 0)),
        out_shape=_sds((n_g, n_c, S5_T * S5_GROUP), BF16), compiler_params=_params(("parallel",)),
    )(t)


def _from_groups(t, n_l):
    n_g, n_c = t.shape[0], t.shape[1]
    gpt = min(GROUPS_PER_TILE, n_g)
    width = gpt * S5_GROUP

    def body(y_ref, o_ref):
        ytr = [y_ref[gl].T for gl in range(gpt)]
        for s in range(S5_T):
            rows = slice(s * S5_GROUP, (s + 1) * S5_GROUP)
            piece = jnp.concatenate([ytr[gl][rows, :] for gl in range(gpt)], axis=0)
            o_ref[pl.ds(s, n_c, stride=S5_T), :] = piece.T

    return pl.pallas_call(
        body, name="s5_from_groups", grid=(n_g // gpt,),
        in_specs=[pl.BlockSpec((gpt, n_c, S5_T * S5_GROUP), lambda b: (b, 0, 0))],
        out_specs=pl.BlockSpec((n_l, width), lambda b: (0, b)),
        out_shape=_sds((n_l, n_g * S5_GROUP)), compiler_params=_params(("parallel",)),
    )(t)


def _s5_act(ys, uz, d_skip):
    di = d_skip.shape[1]
    return jax.nn.gelu(ys + d_skip * uz[:, :di])


def _s5_gate(g1, glu_pre, uz, b_glu):
    di = b_glu.shape[1]
    return g1 * jax.nn.sigmoid(glu_pre + b_glu) * jax.nn.silu(uz[:, di:])


def _s5_fwd(h, p):
    n_l = h.shape[0]
    di = p["d_skip"].shape[1]
    n_g = di // S5_GROUP
    hn = _rms_fwd(h, p["norm_g"], name="s5_rms")
    uz = _mm(hn, p["w_in"], name="s5_in")
    toep, wend, wout, a1, a2 = p["ops"]
    u_g = _to_groups(uz, n_g)
    s = _s5_states(u_g, wend)
    xprev = _s5_scan(s.transpose(1, 0, 2), a1, a2).transpose(1, 0, 2)
    ys = _from_groups(_s5_outputs(u_g, toep, xprev, wout), n_l)
    g1 = _rowwise(_s5_act, [ys, uz], [p["d_skip"]], [_sds((n_l, di), BF16)], [], tm=512, name="s5_act")[0]
    glu_pre = _mm(g1, p["w_glu"], name="s5_glu")

    def gate(ys_t, pre_t, uz_t, d_skip, b_glu):
        return _s5_gate(_s5_act(ys_t, uz_t, d_skip), pre_t, uz_t, b_glu)

    gated = _rowwise(gate, [ys, glu_pre, uz], [p["d_skip"], p["b_glu"]], [_sds((n_l, di), BF16)], [],
                     tm=512, name="s5_gate")[0]
    h_next = _mm(gated, p["w_out"], add=h, name="s5_out")
    return h_next, (h, hn, uz, u_g, xprev, ys, g1, glu_pre, gated)


def _s5_bwd(dh_out, p, saved, ops_vjp):
    h, hn, uz, u_g, xprev, ys, g1, glu_pre, gated = saved
    n_l = h.shape[0]
    di = p["d_skip"].shape[1]
    n_g = di // S5_GROUP
    toep, wend, wout, a1, a2 = p["ops"]
    d_gated = _mm(dh_out, p["w_out"], mode="nt", out_dtype=BF16, name="s5_dgated")
    g_w_out = _mm(gated, dh_out, mode="tn", name="s5_dwout")

    def gate_bwd(ys_t, pre_t, uz_t, ct, d_skip, b_glu):
        g1_t = _s5_act(ys_t, uz_t, d_skip)
        _, vjp = jax.vjp(_s5_gate, g1_t, pre_t, uz_t, b_glu)
        d_g1, d_pre, d_uz, d_b = vjp(ct.astype(F32))
        return d_g1, d_pre, d_uz, d_b

    d_g1_direct, d_pre, d_uz_gate, g_b_glu = _rowwise(
        gate_bwd, [ys, glu_pre, uz, d_gated], [p["d_skip"], p["b_glu"]],
        [_sds((n_l, di)), _sds((n_l, di), BF16), _sds(uz.shape)], [_sds(p["b_glu"].shape)], tm=256, name="s5_gate_bwd")
    g_w_glu = _mm(g1, d_pre, mode="tn", name="s5_dwglu")
    d_g1 = _mm(d_pre, p["w_glu"], mode="nt", add=d_g1_direct, name="s5_dg1")

    def act_bwd(ys_t, uz_t, ct, d_uz_t, d_skip):
        _, vjp = jax.vjp(_s5_act, ys_t, uz_t, d_skip)
        d_ys, d_uz, d_d = vjp(ct)
        return d_ys, d_uz + d_uz_t, d_d

    d_ys, d_uz_part, g_d_skip = _rowwise(
        act_bwd, [ys, uz, d_g1, d_uz_gate], [p["d_skip"]], [_sds((n_l, di)), _sds(uz.shape)],
        [_sds(p["d_skip"].shape)], tm=256, name="s5_act_bwd")
    d_y = _to_groups(d_ys, n_g)
    d_toep, d_wout, d_xprev = _s5_outputs_bwd(u_g, d_y, xprev, wout)
    d_s_t, p1, p2 = _s5_scan_bwd(d_xprev.transpose(1, 0, 2), xprev.transpose(1, 0, 2), a1, a2)
    d_s = d_s_t.transpose(1, 0, 2)
    d_u_g, d_wend = _s5_inputs_bwd(u_g, d_y, d_s, toep, wend)
    d_u = _from_groups(d_u_g, n_l)
    d_uz = _rowwise(lambda part, du: jnp.concatenate([part[:, :di] + du, part[:, di:]], axis=1),
                    [d_uz_part, d_u], [], [_sds(uz.shape, BF16)], [], tm=512, name="s5_duz")[0]
    g_w_in = _mm(hn, d_uz, mode="tn", name="s5_dwin")
    d_hn = _mm(d_uz, p["w_in"], mode="nt", name="s5_dhn")
    dh_in, g_norm = _rms_bwd(h, p["norm_g"], d_hn, dh_out, name="s5_rms_bwd")
    g_ops = ops_vjp((d_toep, d_wend, d_wout, p1, p2))
    grads = dict(norm_g=g_norm, w_in=g_w_in, a_re=g_ops[0], a_im=g_ops[1], log_step=g_ops[2], b_re=g_ops[3],
                 b_im=g_ops[4], c_re=g_ops[5], c_im=g_ops[6], d_skip=g_d_skip, w_glu=g_w_glu, b_glu=g_b_glu,
                 w_out=g_w_out)
    return dh_in, grads


MLA_Z0 = MLA_Q_RANK + MLA_KV_RANK + LANES


def _rope_tile(t, cos_t, sin_t):
    q = LANES // 4
    lane = lax.broadcasted_iota(jnp.int32, t.shape, 1)
    swapped = jnp.where(lane < q, pltpu.roll(t, LANES - q, axis=1), pltpu.roll(t, q, axis=1))
    return t * cos_t + swapped * sin_t


def _mla_mid(proj, cos_t, sin_t, q_g, kv_g):
    cqn = _rms(proj[:, :MLA_Q_RANK], q_g)
    ckvn = _rms(proj[:, MLA_Q_RANK:MLA_Q_RANK + MLA_KV_RANK], kv_g)
    kr = _rope_tile(proj[:, MLA_Q_RANK + MLA_KV_RANK:MLA_Z0], cos_t, sin_t)
    return cqn, ckvn, kr


def _mla_rope_q(qp, cos_t, sin_t):
    parts = []
    for hd in range(qp.shape[1] // MLA_HEAD_PAD):
        base = hd * MLA_HEAD_PAD
        parts.append(qp[:, base:base + LANES])
        parts.append(_rope_tile(qp[:, base + LANES:base + MLA_HEAD_PAD], cos_t, sin_t))
    return jnp.concatenate(parts, axis=1)


def _mla_gate(o, proj):
    return o * jax.nn.silu(proj[:, MLA_Z0:])


LOG2E = math.log2(math.e)
SCORE_LOG2 = MLA_SCALE * LOG2E
FLASH_SPLIT = 4


def _causal_pairs(n_blk, kv_major):
    if kv_major:
        pairs = [(i, j) for j in range(n_blk) for i in range(j, n_blk)]
    else:
        pairs = [(i, j) for i in range(n_blk) for j in range(i + 1)]
    return (jnp.asarray([p[0] for p in pairs], jnp.int32), jnp.asarray([p[1] for p in pairs], jnp.int32))


def _raw_scores(q, kcat, row0, diagonal):
    s = _dot(q, kcat, "nt")
    if diagonal:
        qpos = row0 + lax.broadcasted_iota(jnp.int32, s.shape, 0)
        kpos = lax.broadcasted_iota(jnp.int32, s.shape, 1)
        s = jnp.where(kpos <= qpos, s, NEG_INF)
    return s


def _lanes(x, width):
    return jnp.tile(x, (1, width // LANES))


def _flash_fwd(qp, kv, kr, *, blk=1024):
    n_l = qp.shape[0]
    heads = qp.shape[1] // MLA_HEAD_PAD
    blk = _pick(n_l, blk)
    n_blk = n_l // blk
    half = blk // FLASH_SPLIT
    qi, kj = _causal_pairs(n_blk, kv_major=False)

    def body(qi_ref, kj_ref, q_ref, kv_ref, kr_ref, o_ref, lse_ref, m_sc, l_sc, acc_sc):
        p = pl.program_id(1)
        i, j = qi_ref[p], kj_ref[p]

        @pl.when(j == 0)
        def _():
            m_sc[...] = jnp.full_like(m_sc, NEG_INF)
            l_sc[...] = jnp.zeros_like(l_sc)
            acc_sc[...] = jnp.zeros_like(acc_sc)

        def update(diagonal):
            kcat = jnp.concatenate([kv_ref[:, :LANES], kr_ref[...]], axis=1)
            v = kv_ref[:, LANES:]
            for r in range(FLASH_SPLIT):
                rows = slice(r * half, (r + 1) * half)
                s = _raw_scores(q_ref[rows, :], kcat, r * half, diagonal)
                m_old = m_sc[rows, :]
                m_new = jnp.maximum(m_old, jnp.max(s, axis=1, keepdims=True))
                alpha = jnp.exp2((m_old - m_new) * SCORE_LOG2)
                pr = jnp.exp2((s - _lanes(m_new, blk)) * SCORE_LOG2)
                l_sc[rows, :] = alpha * l_sc[rows, :] + jnp.sum(pr, axis=1, keepdims=True)
                acc_sc[rows, :] = alpha * acc_sc[rows, :] + _dot(pr, v)
                m_sc[rows, :] = m_new

        @pl.when(j < i)
        def _():
            update(False)

        @pl.when(j == i)
        def _():
            update(True)
            o_ref[...] = acc_sc[...] / l_sc[...]
            lse_ref[...] = m_sc[...] * MLA_SCALE + jnp.log(l_sc[...])

    grid_spec = pltpu.PrefetchScalarGridSpec(
        num_scalar_prefetch=2, grid=(heads, qi.shape[0]),
        in_specs=[pl.BlockSpec((blk, MLA_HEAD_PAD), lambda h, p, qi_r, kj_r: (qi_r[p], h)),
                  pl.BlockSpec((blk, MLA_HEAD_PAD), lambda h, p, qi_r, kj_r: (kj_r[p], h)),
                  pl.BlockSpec((blk, LANES), lambda h, p, qi_r, kj_r: (kj_r[p], 0))],
        out_specs=[pl.BlockSpec((blk, MLA_V), lambda h, p, qi_r, kj_r: (qi_r[p], h)),
                   pl.BlockSpec((None, blk, LANES), lambda h, p, qi_r, kj_r: (h, qi_r[p], 0))],
        scratch_shapes=[pltpu.VMEM((blk, LANES), F32), pltpu.VMEM((blk, LANES), F32), pltpu.VMEM((blk, MLA_V), F32)])
    return pl.pallas_call(
        body, name="mla_flash_fwd", grid_spec=grid_spec,
        out_shape=[_sds((n_l, heads * MLA_V)), _sds((heads, n_l, LANES))],
        compiler_params=_params(("parallel", "arbitrary")),
    )(qi, kj, qp, kv, kr)


def _flash_bwd(qp, kv, kr, d_o, lse, delta, *, blk=1024):
    n_l = qp.shape[0]
    heads = qp.shape[1] // MLA_HEAD_PAD
    blk = _pick(n_l, blk)
    n_blk = n_l // blk
    half = blk // FLASH_SPLIT
    qi, kj = _causal_pairs(n_blk, kv_major=True)
    n_pairs = qi.shape[0]

    def body(qi_ref, kj_ref, q_ref, kv_ref, kr_ref, do_ref, lse_ref, dl_ref, dq_ref, dkv_ref, dkr_ref, dk_sc, dv_sc):
        h, p = pl.program_id(0), pl.program_id(1)
        i, j = qi_ref[p], kj_ref[p]

        @pl.when(p == 0)
        def _():
            dq_ref[...] = jnp.zeros_like(dq_ref)

        @pl.when(jnp.logical_and(p == 0, h == 0))
        def _():
            dkr_ref[...] = jnp.zeros_like(dkr_ref)

        @pl.when(i == j)
        def _():
            dk_sc[...] = jnp.zeros_like(dk_sc)
            dv_sc[...] = jnp.zeros_like(dv_sc)

        def update(diagonal):
            kcat = jnp.concatenate([kv_ref[:, :LANES], kr_ref[...]], axis=1)
            v = kv_ref[:, LANES:]
            for r in range(FLASH_SPLIT):
                rows = slice(r * half, (r + 1) * half)
                q_t, do_t = q_ref[rows, :], do_ref[rows, :]
                s = _raw_scores(q_t, kcat, r * half, diagonal)
                pr = jnp.exp2(s * SCORE_LOG2 - _lanes(lse_ref[rows, :] * LOG2E, blk))
                d_p = _dot(do_t, v, "nt")
                d_s = pr * (d_p - _lanes(dl_ref[rows, :], blk))
                dk_sc[...] += _dot(d_s, q_t, "tn")
                dv_sc[...] += _dot(pr, do_t, "tn")
                q_rows = pl.ds(pl.multiple_of(i * blk + r * half, half), half)
                dq_ref[q_rows, :] += _dot(d_s, kcat)

        @pl.when(i > j)
        def _():
            update(False)

        @pl.when(i == j)
        def _():
            update(True)

        @pl.when(i == n_blk - 1)
        def _():
            dk = dk_sc[...] * MLA_SCALE
            dkv_ref[:, :LANES] = dk[:, :LANES].astype(dkv_ref.dtype)
            dkv_ref[:, LANES:] = dv_sc[...].astype(dkv_ref.dtype)
            k_rows = pl.ds(pl.multiple_of(j * blk, blk), blk)
            dkr_ref[k_rows, :] += dk[:, LANES:]

        @pl.when(p == n_pairs - 1)
        def _():
            dq_ref[...] = dq_ref[...] * MLA_SCALE

    at_q = lambda h, p, qi_r, kj_r: (qi_r[p], h)
    at_kv = lambda h, p, qi_r, kj_r: (kj_r[p], h)
    grid_spec = pltpu.PrefetchScalarGridSpec(
        num_scalar_prefetch=2, grid=(heads, n_pairs),
        in_specs=[pl.BlockSpec((blk, MLA_HEAD_PAD), at_q),
                  pl.BlockSpec((blk, MLA_HEAD_PAD), at_kv),
                  pl.BlockSpec((blk, LANES), lambda h, p, qi_r, kj_r: (kj_r[p], 0)),
                  pl.BlockSpec((blk, MLA_V), at_q),
                  pl.BlockSpec((None, blk, LANES), lambda h, p, qi_r, kj_r: (h, qi_r[p], 0)),
                  pl.BlockSpec((blk, LANES), at_q)],
        out_specs=[pl.BlockSpec((n_l, MLA_HEAD_PAD), lambda h, p, qi_r, kj_r: (0, h)),
                   pl.BlockSpec((blk, MLA_HEAD_PAD), at_kv),
                   pl.BlockSpec((n_l, LANES), lambda h, p, qi_r, kj_r: (0, 0))],
        scratch_shapes=[pltpu.VMEM((blk, MLA_HEAD_PAD), F32), pltpu.VMEM((blk, MLA_V), F32)])
    return pl.pallas_call(
        body, name="mla_flash_bwd", grid_spec=grid_spec,
        out_shape=[_sds(qp.shape), _sds(kv.shape, BF16), _sds(kr.shape)],
        compiler_params=_params(("arbitrary", "arbitrary")),
    )(qi, kj, qp, kv, kr, d_o, lse, delta)


def _mla_fwd(h, p, rope):
    n_l = h.shape[0]
    cos_t, sin_t = rope
    hn = _rms_fwd(h, p["norm_g"], name="mla_rms")
    proj = _mm(hn, p["w_in"], name="mla_in", tn=896)
    cqn, ckvn, kr = _rowwise(_mla_mid, [proj, cos_t, sin_t], [p["q_norm_g"], p["kv_norm_g"]],
                             [_sds((n_l, MLA_Q_RANK), BF16), _sds((n_l, MLA_KV_RANK), BF16), _sds((n_l, LANES), BF16)],
                             [], tm=512, name="mla_mid")
    q_raw = _mm(cqn, p["w_uq"], name="mla_uq")
    qp = _rowwise(_mla_rope_q, [q_raw, cos_t, sin_t], [], [_sds(q_raw.shape, BF16)], [], tm=512, name="mla_rope_q")[0]
    kv = _mm(ckvn, p["w_ukv"], out_dtype=BF16, name="mla_ukv")
    o, lse = _flash_fwd(qp, kv, kr)
    gated = _rowwise(_mla_gate, [o, proj], [], [_sds(o.shape, BF16)], [], tm=512, name="mla_gate")[0]
    h_next = _mm(gated, p["w_out"], add=h, name="mla_out")
    return h_next, (h, hn, proj, cqn, ckvn, kr, qp, kv, o, lse, gated)


def _mla_bwd(dh_out, p, saved, rope):
    h, hn, proj, cqn, ckvn, kr, qp, kv, o, lse, gated = saved
    n_l = h.shape[0]
    cos_t, sin_t = rope
    d_gated = _mm(dh_out, p["w_out"], mode="nt", out_dtype=BF16, name="mla_dgated")
    g_w_out = _mm(gated, dh_out, mode="tn", name="mla_dwout")

    def gate_bwd(o_t, proj_t, ct):
        _, vjp = jax.vjp(lambda a, z: a * jax.nn.silu(z), o_t, proj_t[:, MLA_Z0:])
        d_o_t, d_z_t = vjp(ct.astype(F32))
        prod = d_o_t * o_t
        delta = jnp.concatenate(
            [jnp.broadcast_to(jnp.sum(prod[:, hd * MLA_V:(hd + 1) * MLA_V], axis=1, keepdims=True),
                              (prod.shape[0], MLA_V)) for hd in range(prod.shape[1] // MLA_V)], axis=1)
        return d_o_t, d_z_t, delta

    d_o, d_z, delta = _rowwise(gate_bwd, [o, proj, d_gated], [], [_sds(o.shape, BF16), _sds(o.shape), _sds(o.shape)],
                               [], tm=512, name="mla_gate_bwd")
    d_qp, d_kv, d_kr = _flash_bwd(qp, kv, kr, d_o, lse, delta)

    def rope_q_bwd(ct, c_t, s_t):
        return _mla_rope_q(ct, c_t, -s_t)

    d_q_raw = _rowwise(rope_q_bwd, [d_qp, cos_t, sin_t], [], [_sds(d_qp.shape, BF16)], [], tm=512,
                       name="mla_rope_q_bwd")[0]
    g_w_uq = _mm(cqn, d_q_raw, mode="tn", name="mla_dwuq")
    d_cqn = _mm(d_q_raw, p["w_uq"], mode="nt", name="mla_dcqn")
    g_w_ukv = _mm(ckvn, d_kv, mode="tn", name="mla_dwukv")
    d_ckvn = _mm(d_kv, p["w_ukv"], mode="nt", name="mla_dckvn")

    def mid_bwd(proj_t, c_t, s_t, d_cq, d_ckv, d_kr_t, d_z_t, q_g, kv_g):
        _, vjp_q = jax.vjp(_rms, proj_t[:, :MLA_Q_RANK], q_g)
        _, vjp_kv = jax.vjp(_rms, proj_t[:, MLA_Q_RANK:MLA_Q_RANK + MLA_KV_RANK], kv_g)
        d_q_in, d_qg = vjp_q(d_cq)
        d_kv_in, d_kvg = vjp_kv(d_ckv)
        d_kr_in = _rope_tile(d_kr_t, c_t, -s_t)
        return jnp.concatenate([d_q_in, d_kv_in, d_kr_in, d_z_t], axis=1), d_qg, d_kvg

    d_proj, g_q_norm, g_kv_norm = _rowwise(
        mid_bwd, [proj, cos_t, sin_t, d_cqn, d_ckvn, d_kr, d_z], [p["q_norm_g"], p["kv_norm_g"]],
        [_sds(proj.shape, BF16)], [_sds(p["q_norm_g"].shape), _sds(p["kv_norm_g"].shape)], tm=512, name="mla_mid_bwd")
    g_w_in = _mm(hn, d_proj, mode="tn", name="mla_dwin", tn=896)
    d_hn = _mm(d_proj, p["w_in"], mode="nt", name="mla_dhn", tk=896)
    dh_in, g_norm = _rms_bwd(h, p["norm_g"], d_hn, dh_out, name="mla_rms_bwd")
    grads = dict(norm_g=g_norm, w_in=g_w_in, q_norm_g=g_q_norm, w_uq=g_w_uq, kv_norm_g=g_kv_norm, w_ukv=g_w_ukv,
                 w_out=g_w_out)
    return dh_in, grads


def _loss_head(h, g, target):
    def fn(x, t, gg):
        def local(xx, g2):
            err = _rms(xx, g2) - t
            return 0.5 * jnp.sum(jnp.mean(err * err, axis=-1))

        val, (dx, dg) = jax.value_and_grad(local, argnums=(0, 1))(x, gg)
        return dx, jnp.full((1, LANES), val, F32), dg

    dh, loss, dg = _rowwise(fn, [h, target], [g], [_sds(h.shape)], [_sds((1, LANES)), _sds(g.shape)], tm=512,
                            name="loss_head")
    return loss[0, 0], dh, dg


HBM_SPEC = pl.BlockSpec(memory_space=pltpu.HBM)


def _all_gather(shard, *, name):
    def body(x_ref, out_ref, send_sems, recv_sems, local_sem):
        x, y, c = lax.axis_index("x"), lax.axis_index("y"), lax.axis_index("c")
        me, sibling = (x, y, c), (x, y, 1 - c)
        chips = [(1 - x, y), (x, 1 - y), (1 - x, 1 - y)]

        def rows(px, py, pc):
            return out_ref.at[4 * px + 2 * py + pc]

        def copy(k, block, to, src=None):
            return pltpu.make_async_remote_copy(
                src_ref=rows(*block) if src is None else src, dst_ref=rows(*block),
                send_sem=send_sems.at[k], recv_sem=recv_sems.at[k], device_id=to, device_id_type=MESH)

        mine = pltpu.make_async_copy(x_ref, rows(*me), local_sem)
        mine.start()
        first = [copy(0, me, sibling, src=x_ref)]
        first += [copy(1 + j, me, (*chip, c), src=x_ref) for j, chip in enumerate(chips)]
        for cp in first:
            cp.start()
        passed = [copy(4 + j, (*chip, c), sibling) for j, chip in enumerate(chips)]
        for j, chip in enumerate(chips):
            copy(1 + j, (*chip, c), me).wait_recv()
            passed[j].start()
        copy(0, sibling, me).wait_recv()
        for j, chip in enumerate(chips):
            copy(4 + j, (*chip, 1 - c), me).wait_recv()
        for cp in first + passed:
            cp.wait_send()
        mine.wait()

    return pl.pallas_call(
        body, name=name, out_shape=jax.ShapeDtypeStruct((N_DEV,) + shard.shape, shard.dtype),
        in_specs=[HBM_SPEC], out_specs=HBM_SPEC,
        scratch_shapes=[pltpu.SemaphoreType.DMA((7,)), pltpu.SemaphoreType.DMA((7,)), pltpu.SemaphoreType.DMA],
    )(shard)


def _exchange(src, routes, *, name):
    n_routes = len(routes)

    def body(s_ref, out_ref, send_sems, recv_sems):
        x, y, c = lax.axis_index("x"), lax.axis_index("y"), lax.axis_index("c")
        local, remote = [], []
        for k, (flip, block) in enumerate(routes):
            src_blk = s_ref.at[block(x, y, c)]
            if flip == 0:
                local.append(pltpu.make_async_copy(src_blk, out_ref.at[k], send_sems.at[k]))
            else:
                peer = (1 - x if flip & 4 else x, 1 - y if flip & 2 else y, 1 - c if flip & 1 else c)
                remote.append(pltpu.make_async_remote_copy(
                    src_ref=src_blk, dst_ref=out_ref.at[k], send_sem=send_sems.at[k], recv_sem=recv_sems.at[k],
                    device_id=peer, device_id_type=MESH))
        for cp in local + remote:
            cp.start()
        for cp in remote:
            cp.wait_recv()
        for cp in remote:
            cp.wait_send()
        for cp in local:
            cp.wait()

    return pl.pallas_call(
        body, name=name, out_shape=jax.ShapeDtypeStruct((n_routes,) + src.shape[1:], src.dtype),
        in_specs=[HBM_SPEC], out_specs=HBM_SPEC,
        scratch_shapes=[pltpu.SemaphoreType.DMA((n_routes,)), pltpu.SemaphoreType.DMA((n_routes,))],
    )(src)


def _reduce_scatter(send, *, name):
    def chip_block(k, other_core):
        return lambda x, y, c: (4 * (1 - x if k & 2 else x) + 2 * (1 - y if k & 1 else y)
                                + (1 - c if other_core else c))

    n_chips = 4
    pair = _exchange(send, [(1, chip_block(k, True)) for k in range(n_chips)], name=name + "_pair")
    rows, width = send.shape[1], send.shape[2]
    tr = _pick(rows, 256, 8)
    x, y, c = lax.axis_index("x"), lax.axis_index("y"), lax.axis_index("c")
    own_ids = jnp.stack([chip_block(k, False)(x, y, c) for k in range(n_chips)]).astype(jnp.int32)

    def add_body(ids_ref, *refs):
        own_refs, p_ref, o_ref = refs[:n_chips], refs[n_chips], refs[n_chips + 1]
        for k in range(n_chips):
            o_ref[k] = (own_refs[k][...].astype(F32) + p_ref[k].astype(F32)).astype(o_ref.dtype)

    own_spec = lambda k: pl.BlockSpec((None, tr, width), lambda i, ids: (ids[k], i, 0))
    chip_sums = pl.pallas_call(
        add_body, name=name + "_pair_sum",
        grid_spec=pltpu.PrefetchScalarGridSpec(
            num_scalar_prefetch=1, grid=(rows // tr,),
            in_specs=[own_spec(k) for k in range(n_chips)] + [pl.BlockSpec((n_chips, tr, width), lambda i, ids: (0, i, 0))],
            out_specs=pl.BlockSpec((n_chips, tr, width), lambda i, ids: (0, i, 0))),
        out_shape=jax.ShapeDtypeStruct((n_chips, rows, width), send.dtype), compiler_params=_params(("parallel",)),
    )(own_ids, *([send] * n_chips), pair)
    recv = _exchange(chip_sums, [(2 * k, (lambda kk: lambda x, y, c: kk)(k)) for k in range(1, n_chips)],
                     name=name + "_chips")

    def sum_body(q_ref, r_ref, o_ref):
        acc = q_ref[...].astype(F32)
        for k in range(n_chips - 1):
            acc = acc + r_ref[k].astype(F32)
        o_ref[...] = acc

    return pl.pallas_call(
        sum_body, name=name + "_sum", grid=(rows // tr,),
        in_specs=[pl.BlockSpec((None, tr, width), lambda i: (0, i, 0)),
                  pl.BlockSpec((n_chips - 1, tr, width), lambda i: (0, i, 0))],
        out_specs=pl.BlockSpec((tr, width), lambda i: (i, 0)), out_shape=_sds((rows, width)),
        compiler_params=_params(("parallel",)),
    )(chip_sums, recv)


def _adamw(w, g, m, v, *, name):
    def fn(ww, gg, mm, vv):
        m_new = ADAM_B1 * mm + (1.0 - ADAM_B1) * gg
        v_new = ADAM_B2 * vv + (1.0 - ADAM_B2) * jnp.square(gg)
        m_hat = m_new / (1.0 - ADAM_B1 ** ADAM_STEP)
        v_hat = v_new / (1.0 - ADAM_B2 ** ADAM_STEP)
        return -ADAM_LR * (m_hat / (jnp.sqrt(v_hat) + ADAM_EPS) + ADAM_WD * ww), m_new, v_new

    return _rowwise(fn, [w, g, m, v], [], [_sds(w.shape)] * 3, [], tm=256, name=name)


KINDS = ("gmlp", "s5", "mla", "gmlp")
LAYER_NAMES = {
    "gmlp": ("norm_g", "w_in", "ln_g", "ln_b", "w_s", "b_s", "w_out"),
    "s5": ("norm_g", "w_in", "a_re", "a_im", "log_step", "b_re", "b_im", "c_re", "c_im", "d_skip", "w_glu", "b_glu",
           "w_out"),
    "mla": ("norm_g", "w_in", "q_norm_g", "w_uq", "kv_norm_g", "w_ukv", "w_out"),
}
COL_SHARDED = ("w_in", "w_uq", "w_ukv")
ROW_SHARDED = ("w_out", "w_glu")
WEIGHT_NAMES = tuple("l%d_%s" % (i, n) for i, k in enumerate(KINDS) for n in LAYER_NAMES[k]) + ("final_norm_g",)


def _is_sharded(name):
    return name.split("_", 1)[1] in COL_SHARDED + ROW_SHARDED


def _flatten(arrs, pad_rows_to):
    parts, sizes = [], []
    for a in arrs:
        flat = a.reshape(-1)
        pad = (-flat.shape[0]) % FLAT_W
        if pad:
            flat = jnp.pad(flat, (0, pad))
        parts.append(flat)
        sizes.append(flat.shape[0] // FLAT_W)
    rows = sum(sizes)
    pad_rows = (-rows) % pad_rows_to
    if pad_rows:
        parts.append(jnp.zeros((pad_rows * FLAT_W,), arrs[0].dtype))
    return jnp.concatenate(parts).reshape(-1, FLAT_W), sizes


def _unflatten(flat, shapes, sizes):
    out, row = [], 0
    for shape, n_rows in zip(shapes, sizes):
        n = int(np.prod(shape))
        out.append(flat[row:row + n_rows].reshape(-1)[:n].reshape(shape))
        row += n_rows
    return out


def _full_from_gathered(blocks, name):
    if name.split("_", 1)[1] in COL_SHARDED:
        return blocks.transpose(1, 0, 2).reshape(blocks.shape[1], -1)
    return blocks.reshape(-1, blocks.shape[2])


def _shards_of(full, name):
    if name.split("_", 1)[1] in COL_SHARDED:
        return full.reshape(full.shape[0], N_DEV, -1).transpose(1, 0, 2)
    return full.reshape(N_DEV, -1, full.shape[1])


def _rope_tables(positions):
    inv_freq = ROPE_THETA ** (-jnp.arange(0, MLA_ROPE, 2, dtype=F32) / MLA_ROPE)
    ang = positions.astype(F32)[:, None] * inv_freq
    cos, sin = jnp.cos(ang), jnp.sin(ang)
    zero = jnp.zeros((positions.shape[0], LANES - MLA_ROPE), F32)
    return jnp.concatenate([cos, cos, zero], axis=1), jnp.concatenate([-sin, sin, zero], axis=1)


def _row(v):
    return v.reshape(1, -1)


def kernel(x, positions, l0_norm_g, l0_w_in, l0_ln_g, l0_ln_b, l0_w_s, l0_b_s, l0_w_out, l1_norm_g, l1_w_in, l1_a_re, l1_a_im, l1_log_step, l1_b_re, l1_b_im, l1_c_re, l1_c_im, l1_d_skip, l1_w_glu, l1_b_glu, l1_w_out, l2_norm_g, l2_w_in, l2_q_norm_g, l2_w_uq, l2_kv_norm_g, l2_w_ukv, l2_w_out, l3_norm_g, l3_w_in, l3_ln_g, l3_ln_b, l3_w_s, l3_b_s, l3_w_out, final_norm_g, loss_target, m_l0_norm_g, m_l0_w_in, m_l0_ln_g, m_l0_ln_b, m_l0_w_s, m_l0_b_s, m_l0_w_out, m_l1_norm_g, m_l1_w_in, m_l1_a_re, m_l1_a_im, m_l1_log_step, m_l1_b_re, m_l1_b_im, m_l1_c_re, m_l1_c_im, m_l1_d_skip, m_l1_w_glu, m_l1_b_glu, m_l1_w_out, m_l2_norm_g, m_l2_w_in, m_l2_q_norm_g, m_l2_w_uq, m_l2_kv_norm_g, m_l2_w_ukv, m_l2_w_out, m_l3_norm_g, m_l3_w_in, m_l3_ln_g, m_l3_ln_b, m_l3_w_s, m_l3_b_s, m_l3_w_out, m_final_norm_g, v_l0_norm_g, v_l0_w_in, v_l0_ln_g, v_l0_ln_b, v_l0_w_s, v_l0_b_s, v_l0_w_out, v_l1_norm_g, v_l1_w_in, v_l1_a_re, v_l1_a_im, v_l1_log_step, v_l1_b_re, v_l1_b_im, v_l1_c_re, v_l1_c_im, v_l1_d_skip, v_l1_w_glu, v_l1_b_glu, v_l1_w_out, v_l2_norm_g, v_l2_w_in, v_l2_q_norm_g, v_l2_w_uq, v_l2_kv_norm_g, v_l2_w_ukv, v_l2_w_out, v_l3_norm_g, v_l3_w_in, v_l3_ln_g, v_l3_ln_b, v_l3_w_s, v_l3_b_s, v_l3_w_out, v_final_norm_g):
    args = locals()
    weights = {n: args[n] for n in WEIGHT_NAMES}
    mom_m = {n: args["m_" + n] for n in WEIGHT_NAMES}
    mom_v = {n: args["v_" + n] for n in WEIGHT_NAMES}
    return _train_step(x, positions, loss_target, weights, mom_m, mom_v)


def _train_step(x, positions, loss_target, weights, mom_m, mom_v):
    big = [n for n in WEIGHT_NAMES if _is_sharded(n)]
    small = [n for n in WEIGHT_NAMES if not _is_sharded(n)]

    w_flat, big_sizes = _flatten([weights[n] for n in big], 8)
    gathered = _all_gather(w_flat.astype(BF16), name="weights_all_gather")
    full, row = {}, 0
    for n, n_rows in zip(big, big_sizes):
        blocks = gathered[:, row:row + n_rows].reshape((N_DEV,) + weights[n].shape)
        full[n] = _full_from_gathered(blocks, n)
        row += n_rows

    layers, ops_vjps = [], {}
    for i, kind in enumerate(KINDS):
        pre = "l%d_" % i
        p = {n: (full[pre + n] if _is_sharded(pre + n) else weights[pre + n]) for n in LAYER_NAMES[kind]}
        p["norm_g"] = _row(p["norm_g"])
        if kind == "gmlp":
            p["ln_g"], p["ln_b"], p["b_st"] = _row(p["ln_g"]), _row(p["ln_b"]), p["b_s"].T
        elif kind == "s5":
            p["d_skip"], p["b_glu"] = _row(p["d_skip"]), _row(p["b_glu"])
            ops, ops_vjps[i] = jax.vjp(_s5_operators, *[p[n] for n in ("a_re", "a_im", "log_step", "b_re", "b_im",
                                                                       "c_re", "c_im")])
            p["ops"] = tuple(o.astype(BF16) for o in ops[:3]) + ops[3:]
        else:
            heads = p["w_uq"].shape[1] // MLA_QK_DIM
            w_in = p["w_in"]
            split = MLA_Q_RANK + MLA_KV_RANK + MLA_ROPE
            p["w_in"] = jnp.concatenate([w_in[:, :split], jnp.zeros((w_in.shape[0], LANES - MLA_ROPE), w_in.dtype),
                                         w_in[:, split:]], axis=1)
            p["w_uq"] = jnp.pad(p["w_uq"].reshape(-1, heads, MLA_QK_DIM),
                                ((0, 0), (0, 0), (0, MLA_HEAD_PAD - MLA_QK_DIM))).reshape(-1, heads * MLA_HEAD_PAD)
            p["q_norm_g"], p["kv_norm_g"] = _row(p["q_norm_g"]), _row(p["kv_norm_g"])
        layers.append(p)
    rope = _rope_tables(positions[0])

    h = x[0]
    saved = []
    for kind, p in zip(KINDS, layers):
        if kind == "gmlp":
            h, s = _gmlp_fwd(h, p)
        elif kind == "s5":
            h, s = _s5_fwd(h, p)
        else:
            h, s = _mla_fwd(h, p, rope)
        saved.append(s)
    loss_local, dh, g_final = _loss_head(h, _row(weights["final_norm_g"]), loss_target[0])
    loss = lax.psum(loss_local, ("x", "y", "c"))

    grads = {"final_norm_g": g_final.reshape(-1)}
    for i in reversed(range(len(KINDS))):
        kind, p = KINDS[i], layers[i]
        if kind == "gmlp":
            dh, g = _gmlp_bwd(dh, p, saved[i])
        elif kind == "s5":
            dh, g = _s5_bwd(dh, p, saved[i], ops_vjps[i])
        else:
            dh, g = _mla_bwd(dh, p, saved[i], rope)
            heads = weights["l%d_w_uq" % i].shape[1] * N_DEV // MLA_QK_DIM
            split = MLA_Q_RANK + MLA_KV_RANK + MLA_ROPE
            g["w_in"] = jnp.concatenate([g["w_in"][:, :split], g["w_in"][:, MLA_Z0:]], axis=1)
            g["w_uq"] = g["w_uq"].reshape(-1, heads, MLA_HEAD_PAD)[:, :, :MLA_QK_DIM].reshape(-1, heads * MLA_QK_DIM)
        for n, val in g.items():
            name = "l%d_%s" % (i, n)
            grads[name] = val.reshape(weights[name].shape) if not _is_sharded(name) else val

    small_flat, small_sizes = _flatten([grads[n] for n in small], 8 * N_DEV)
    small_rows = small_flat.shape[0] // N_DEV
    send_parts = [_shards_of(grads[n], n).reshape(N_DEV, -1, FLAT_W) for n in big]
    send_parts.append(small_flat.reshape(N_DEV, small_rows, FLAT_W))
    send = jnp.concatenate(send_parts, axis=1)
    big_rows = send.shape[1] - small_rows
    pad_rows = (-send.shape[1]) % 8
    if pad_rows:
        send = jnp.pad(send, ((0, 0), (0, pad_rows), (0, 0)))
    reduced = _reduce_scatter(send.astype(BF16), name="grads")
    g_big_flat = reduced[:big_rows]
    g_small_all = _all_gather(reduced[big_rows:big_rows + small_rows], name="small_grads_all_gather")
    g_small_flat = g_small_all.reshape(-1, FLAT_W)

    def flat_of(tree, names, pad_to):
        return _flatten([tree[n] for n in names], pad_to)[0]

    outs = {}
    big_shapes = [weights[n].shape for n in big]
    for n, g_n in zip(big, _unflatten(g_big_flat, big_shapes, big_sizes)):
        outs["grad_" + n] = g_n
        outs["delta_" + n], outs["new_m_" + n], outs["new_v_" + n] = _adamw(
            weights[n], g_n, mom_m[n], mom_v[n], name="adamw_" + n)
    d_s, nm_s, nv_s = _adamw(flat_of(weights, small, 8 * N_DEV), g_small_flat, flat_of(mom_m, small, 8 * N_DEV),
                             flat_of(mom_v, small, 8 * N_DEV), name="adamw_replicated")
    small_shapes = [weights[n].shape for n in small]
    for prefix, fs in (("grad_", g_small_flat), ("delta_", d_s), ("new_m_", nm_s), ("new_v_", nv_s)):
        for n, a in zip(small, _unflatten(fs, small_shapes, small_sizes)):
            outs[prefix + n] = a
    result = [loss, dh[None]]
    for prefix in ("grad_", "delta_", "new_m_", "new_v_"):
        result += [outs[prefix + n] for n in WEIGHT_NAMES]
    return tuple(result)
```

```python
import functools
import math

import numpy as np
import jax
import jax.numpy as jnp
from jax import lax
from jax.experimental import pallas as pl
from jax.experimental.pallas import tpu as pltpu

F32 = jnp.float32
BF16 = jnp.bfloat16

NORM_EPS = 1e-6
GMLP_CHUNK = 128
S5_GROUP = 16
S5_STATE = 64
S5_T = 16
MLA_NOPE = 128
MLA_ROPE = 64
MLA_V = 128
MLA_QK_DIM = MLA_NOPE + MLA_ROPE
MLA_Q_RANK = 384
MLA_KV_RANK = 128
MLA_HEAD_PAD = 256
MLA_SCALE = MLA_QK_DIM ** -0.5
ROPE_THETA = 10000.0
NEG_INF = -1e30
ADAM_LR = 0.001
ADAM_B1 = 0.9
ADAM_B2 = 0.999
ADAM_EPS = 1e-08
ADAM_WD = 0.01
ADAM_STEP = 10

N_DEV = 8
LANES = 128
FLAT_W = 1024
VMEM_LIMIT = 56 * 1024 * 1024
MESH = pl.DeviceIdType.MESH


def _pick(dim, pref, align=LANES):
    t = (min(pref, dim) // align) * align
    while t >= align:
        if dim % t == 0:
            return t
        t -= align
    return dim


def _params(sem=None):
    return pltpu.CompilerParams(dimension_semantics=sem, vmem_limit_bytes=VMEM_LIMIT)


_DIMS = {"nn": (((1,), (0,)), ((), ())), "nt": (((1,), (1,)), ((), ())), "tn": (((0,), (0,)), ((), ()))}


def _mm(a, b, *, mode="nn", out_dtype=F32, add=None, name, tm=1024, tn=1024, tk=2048):
    if mode == "nn":
        (m, k), (_, n) = a.shape, b.shape
    elif mode == "nt":
        (m, k), (n, _) = a.shape, b.shape
    else:
        (k, m), (_, n) = a.shape, b.shape
    tm, tn, tk = _pick(m, tm, 8), _pick(n, tn), _pick(k, tk)
    nk = k // tk
    dims = _DIMS[mode]

    def body(*refs):
        a_ref, b_ref = refs[:2]
        r_ref = refs[2] if add is not None else None
        o_ref = refs[3] if add is not None else refs[2]
        part = lax.dot_general(a_ref[...].astype(BF16), b_ref[...].astype(BF16), dims, preferred_element_type=F32)

        def finish(res):
            if add is not None:
                res = res + r_ref[...]
            o_ref[...] = res.astype(o_ref.dtype)

        if nk == 1:
            finish(part)
            return
        acc_ref = refs[-1]
        kk = pl.program_id(2)

        @pl.when(kk == 0)
        def _():
            acc_ref[...] = part

        @pl.when(kk > 0)
        def _():
            acc_ref[...] += part

        @pl.when(kk == nk - 1)
        def _():
            finish(acc_ref[...])

    a_spec = (pl.BlockSpec((tk, tm), lambda i, j, kk: (kk, i)) if mode == "tn"
              else pl.BlockSpec((tm, tk), lambda i, j, kk: (i, kk)))
    b_spec = (pl.BlockSpec((tn, tk), lambda i, j, kk: (j, kk)) if mode == "nt"
              else pl.BlockSpec((tk, tn), lambda i, j, kk: (kk, j)))
    in_specs = [a_spec, b_spec]
    args = [a, b]
    if add is not None:
        in_specs.append(pl.BlockSpec((tm, tn), lambda i, j, kk: (i, j)))
        args.append(add)
    return pl.pallas_call(
        body, name=name, grid=(m // tm, n // tn, nk),
        in_specs=in_specs, out_specs=pl.BlockSpec((tm, tn), lambda i, j, kk: (i, j)),
        out_shape=jax.ShapeDtypeStruct((m, n), out_dtype),
        scratch_shapes=[pltpu.VMEM((tm, tn), F32)] if nk > 1 else [],
        compiler_params=_params(("parallel", "parallel", "arbitrary")),
    )(*args)


def _rowwise(fn, tiled, full, out_tiled, out_acc, *, tm, name):
    rows = tiled[0].shape[0]
    tm = _pick(rows, tm, 8)
    nt, nf, no = len(tiled), len(full), len(out_tiled)

    def body(*refs):
        ins = [r[...] for r in refs[:nt + nf]]
        o_refs = refs[nt + nf:nt + nf + no]
        a_refs = refs[nt + nf + no:]
        outs = fn(*ins)
        if not isinstance(outs, (tuple, list)):
            outs = (outs,)
        for r, v in zip(o_refs, outs[:no]):
            r[...] = v.astype(r.dtype)
        if a_refs:
            @pl.when(pl.program_id(0) == 0)
            def _():
                for r in a_refs:
                    r[...] = jnp.zeros_like(r)

            for r, v in zip(a_refs, outs[no:]):
                r[...] += v.astype(r.dtype)

    def whole(shape):
        nd = len(shape)
        return pl.BlockSpec(tuple(shape), lambda i: (0,) * nd)

    in_specs = ([pl.BlockSpec((tm, t.shape[1]), lambda i: (i, 0)) for t in tiled]
                + [whole(f.shape) for f in full])
    out_specs = ([pl.BlockSpec((tm, o.shape[1]), lambda i: (i, 0)) for o in out_tiled]
                 + [whole(o.shape) for o in out_acc])
    outs = pl.pallas_call(
        body, name=name, grid=(rows // tm,), in_specs=in_specs, out_specs=out_specs,
        out_shape=list(out_tiled) + list(out_acc),
        compiler_params=_params(("arbitrary",)),
    )(*tiled, *full)
    return outs


def _sds(shape, dtype=F32):
    return jax.ShapeDtypeStruct(tuple(shape), dtype)


def _rms(x, g):
    return x * lax.rsqrt(jnp.mean(x * x, axis=-1, keepdims=True) + NORM_EPS) * g


def _layernorm(x, g, b):
    mu = jnp.mean(x, axis=-1, keepdims=True)
    xc = x - mu
    var = jnp.mean(xc * xc, axis=-1, keepdims=True)
    return xc * lax.rsqrt(var + NORM_EPS) * g + b


def _dot(a, b, mode="nn"):
    return lax.dot_general(a.astype(BF16), b.astype(BF16), _DIMS[mode], preferred_element_type=F32)


@jax.custom_vjp
def _bdot(a, b):
    return _dot(a, b)


def _bdot_fwd(a, b):
    return _bdot(a, b), (a, b)


def _bdot_bwd(res, ct):
    a, b = res
    return _dot(ct, b, "nt"), _dot(a, ct, "tn")


_bdot.defvjp(_bdot_fwd, _bdot_bwd)


def _rms_fwd(h, g, *, name):
    return _rowwise(lambda x, gg: _rms(x, gg), [h], [g], [_sds(h.shape, BF16)], [], tm=512, name=name)[0]


def _rms_bwd(h, g, d_hn, dh_out, *, name):
    def fn(x, ct, res, gg):
        _, vjp = jax.vjp(_rms, x, gg)
        dx, dg = vjp(ct)
        return res + dx, dg

    dh, dg = _rowwise(fn, [h, d_hn, dh_out], [g], [_sds(h.shape)], [_sds(g.shape)], tm=512, name=name)
    return dh, dg


def _gmlp_mid(uvz, ln_g, ln_b, w_s, b_st):
    di = ln_g.shape[1]
    ng, ck = w_s.shape[0], w_s.shape[1]
    dg = di // ng
    u = jax.nn.gelu(uvz[:, :di])
    v = _layernorm(jax.nn.gelu(uvz[:, di:2 * di]), ln_g, ln_b)
    z = uvz[:, 2 * di:]
    row = lax.broadcasted_iota(jnp.int32, (ck, ck), 0)
    col = lax.broadcasted_iota(jnp.int32, (ck, ck), 1)
    causal = col <= row
    blocks = []
    for c in range(uvz.shape[0] // ck):
        cols = []
        for g in range(ng):
            w = jnp.where(causal, w_s[g], 0.0)
            cols.append(_bdot(w, v[c * ck:(c + 1) * ck, g * dg:(g + 1) * dg]) + b_st[:, g:g + 1])
        blocks.append(jnp.concatenate(cols, axis=1))
    s = blocks[0] if len(blocks) == 1 else jnp.concatenate(blocks, axis=0)
    return u * s * jax.nn.silu(z)


def _gmlp_fwd(h, p):
    hn = _rms_fwd(h, p["norm_g"], name="gmlp_rms")
    uvz = _mm(hn, p["w_in"], name="gmlp_in")
    di = p["ln_g"].shape[1]
    gated = _rowwise(_gmlp_mid, [uvz], [p["ln_g"], p["ln_b"], p["w_s"], p["b_st"]],
                     [_sds((h.shape[0], di), BF16)], [], tm=256, name="gmlp_mid")[0]
    h_next = _mm(gated, p["w_out"], add=h, name="gmlp_out")
    return h_next, (h, hn, uvz, gated)


def _gmlp_bwd(dh_out, p, saved):
    h, hn, uvz, gated = saved
    d_gated = _mm(dh_out, p["w_out"], mode="nt", out_dtype=BF16, name="gmlp_dgated")
    g_w_out = _mm(gated, dh_out, mode="tn", name="gmlp_dwout")

    def fn(t, ct, ln_g, ln_b, w_s, b_st):
        _, vjp = jax.vjp(_gmlp_mid, t, ln_g, ln_b, w_s, b_st)
        return vjp(ct.astype(F32))

    d_uvz, g_ln_g, g_ln_b, g_w_s, g_b_st = _rowwise(
        fn, [uvz, d_gated], [p["ln_g"], p["ln_b"], p["w_s"], p["b_st"]],
        [_sds(uvz.shape, BF16)], [_sds(p["ln_g"].shape), _sds(p["ln_b"].shape), _sds(p["w_s"].shape),
                                  _sds(p["b_st"].shape)], tm=256, name="gmlp_mid_bwd")
    g_w_in = _mm(hn, d_uvz, mode="tn", name="gmlp_dwin")
    d_hn = _mm(d_uvz, p["w_in"], mode="nt", name="gmlp_dhn")
    dh_in, g_norm = _rms_bwd(h, p["norm_g"], d_hn, dh_out, name="gmlp_rms_bwd")
    grads = dict(norm_g=g_norm, w_in=g_w_in, ln_g=g_ln_g, ln_b=g_ln_b, w_s=g_w_s, b_s=g_b_st.T, w_out=g_w_out)
    return dh_in, grads


def _s5_operators(a_re, a_im, log_step, b_re, b_im, c_re, c_im):
    t_len = S5_T
    step = jnp.exp(log_step)[:, None]
    lr, li = a_re * step, a_im * step
    ks = jnp.arange(t_len + 1, dtype=F32)[:, None, None]
    mag = jnp.exp(ks * lr)
    pw_r, pw_i = mag * jnp.cos(ks * li), mag * jnp.sin(ks * li)
    nr, ni = pw_r[1] - 1.0, pw_i[1]
    den = a_re * a_re + a_im * a_im
    f_r, f_i = (nr * a_re + ni * a_im) / den, (ni * a_re - nr * a_im) / den
    bb_r = f_r[..., None] * b_re - f_i[..., None] * b_im
    bb_i = f_r[..., None] * b_im + f_i[..., None] * b_re
    hi = lax.Precision.HIGHEST
    cp_r = c_re[None] * pw_r[:, :, None, :] - c_im[None] * pw_i[:, :, None, :]
    cp_i = c_re[None] * pw_i[:, :, None, :] + c_im[None] * pw_r[:, :, None, :]
    n_g, n_h = a_re.shape[0], b_re.shape[2]
    lhs = jnp.concatenate([bb_r, -bb_i], axis=1)
    rhs = jnp.concatenate([cp_r[:t_len], cp_i[:t_len]], axis=3)
    rhs = rhs.transpose(1, 3, 0, 2).reshape(n_g, -1, t_len * n_h)
    kcat = jnp.einsum("gqi,gqn->gin", lhs, rhs, precision=hi)
    toep = jnp.stack([jnp.pad(kcat[:, :, :(t_len - s) * n_h], ((0, 0), (0, 0), (s * n_h, 0)))
                      for s in range(t_len)], axis=1).reshape(n_g, t_len * n_h, t_len * n_h)
    rev_r, rev_i = pw_r[t_len - 1::-1][:t_len], pw_i[t_len - 1::-1][:t_len]
    we_r = rev_r[..., None] * bb_r[None] - rev_i[..., None] * bb_i[None]
    we_i = rev_r[..., None] * bb_i[None] + rev_i[..., None] * bb_r[None]
    wend = jnp.concatenate([we_r, we_i], axis=2).transpose(1, 0, 3, 2).reshape(n_g, t_len * n_h, -1)
    wo = jnp.concatenate([cp_r[1:], -cp_i[1:]], axis=3)
    wout = wo.transpose(1, 3, 0, 2).reshape(n_g, -1, t_len * n_h)
    a_r, a_i = pw_r[t_len], pw_i[t_len]
    a1 = jnp.concatenate([a_r, a_r], axis=1)
    a2 = jnp.concatenate([-a_i, a_i], axis=1)
    return toep, wend, wout, a1, a2


def _group_call(body, ins, outs, *, gb, name):
    n_g = ins[0].shape[0]

    def spec(a):
        return pl.BlockSpec((gb,) + tuple(a.shape[1:]), lambda i: (i, 0, 0))

    return pl.pallas_call(
        body, name=name, grid=(n_g // gb,), in_specs=[spec(a) for a in ins],
        out_specs=[spec(o) for o in outs], out_shape=list(outs),
        compiler_params=_params(("parallel",)),
    )(*ins)


def _s5_states(u_g, wend, *, gb=8):
    def body(u_ref, w_ref, s_ref):
        for g in range(gb):
            s_ref[g] = _dot(u_ref[g], w_ref[g])

    n_g, n_c = u_g.shape[0], u_g.shape[1]
    return _group_call(body, [u_g, wend], [_sds((n_g, n_c, wend.shape[2]))], gb=gb, name="s5_states")[0]


def _s5_outputs(u_g, toep, xprev, wout, *, gb=8):
    def body(u_ref, t_ref, x_ref, w_ref, y_ref):
        for g in range(gb):
            y_ref[g] = _dot(u_ref[g], t_ref[g]) + _dot(x_ref[g], w_ref[g])

    return _group_call(body, [u_g, toep, xprev, wout], [_sds(u_g.shape)], gb=gb, name="s5_outputs")[0]


def _s5_outputs_bwd(u_g, d_y, xprev, wout, *, gb=8):
    def body(u_ref, dy_ref, x_ref, w_ref, dt_ref, dw_ref, dx_ref):
        for g in range(gb):
            dy = dy_ref[g]
            dt_ref[g] = _dot(u_ref[g], dy, "tn")
            dw_ref[g] = _dot(x_ref[g], dy, "tn")
            dx_ref[g] = _dot(dy, w_ref[g], "nt")

    n_g, n_c, n_k = u_g.shape
    return _group_call(body, [u_g, d_y, xprev, wout],
                       [_sds((n_g, n_k, n_k)), _sds(wout.shape), _sds(xprev.shape)], gb=gb, name="s5_outputs_bwd")


def _s5_inputs_bwd(u_g, d_y, d_s, toep, wend, *, gb=8):
    def body(u_ref, dy_ref, ds_ref, t_ref, w_ref, du_ref, dw_ref):
        for g in range(gb):
            ds = ds_ref[g]
            du_ref[g] = _dot(dy_ref[g], t_ref[g], "nt") + _dot(ds, w_ref[g], "nt")
            dw_ref[g] = _dot(u_ref[g], ds, "tn")

    return _group_call(body, [u_g, d_y, d_s, toep, wend], [_sds(u_g.shape), _sds(wend.shape)], gb=gb,
                       name="s5_inputs_bwd")


def _swap_halves(x):
    return pltpu.roll(x, x.shape[-1] // 2, axis=x.ndim - 1)


def _s5_scan(s_t, a1, a2, *, gb=16):
    n_c, n_g, n_p = s_t.shape
    gb = min(gb, n_g)

    def body(s_ref, a1_ref, a2_ref, x_ref):
        a1v, a2v = a1_ref[...], a2_ref[...]
        a2s = _swap_halves(a2v)

        def step(c, carry):
            x, xs = carry
            x_ref[c] = x
            s = s_ref[c]
            return x * a1v + xs * a2v + s, xs * a1v + x * a2s + _swap_halves(s)

        zero = jnp.zeros((gb, n_p), F32)
        lax.fori_loop(0, n_c, step, (zero, zero), unroll=4 if n_c % 4 == 0 else 1)

    return pl.pallas_call(
        body, name="s5_scan", grid=(n_g // gb,),
        in_specs=[pl.BlockSpec((n_c, gb, n_p), lambda i: (0, i, 0)), pl.BlockSpec((gb, n_p), lambda i: (i, 0)),
                  pl.BlockSpec((gb, n_p), lambda i: (i, 0))],
        out_specs=pl.BlockSpec((n_c, gb, n_p), lambda i: (0, i, 0)), out_shape=_sds(s_t.shape),
        compiler_params=_params(("parallel",)),
    )(s_t, a1, a2)


def _s5_scan_bwd(d_xprev_t, xprev_t, a1, a2, *, gb=16):
    n_c, n_g, n_p = xprev_t.shape
    gb = min(gb, n_g)

    def body(dx_ref, x_ref, a1_ref, a2_ref, ds_ref, p1_ref, p2_ref):
        a1v, a2v = a1_ref[...], a2_ref[...]
        a2s = _swap_halves(a2v)
        zero = jnp.zeros((gb, n_p), F32)
        ds_ref[n_c - 1] = zero

        def step(k, carry):
            gx_next, gs_next, p1, p2 = carry
            c = n_c - 2 - k
            xp = x_ref[c + 1]
            d = dx_ref[c + 1]
            gx = d + gx_next * a1v - gs_next * a2v
            gs = _swap_halves(d) + gs_next * a1v - gx_next * a2s
            ds_ref[c] = gx
            return gx, gs, p1 + gx_next * xp, p2 + gx_next * _swap_halves(xp)

        _, _, p1, p2 = lax.fori_loop(0, n_c - 1, step, (zero, zero, zero, zero),
                                     unroll=5 if (n_c - 1) % 5 == 0 else 1)
        p1_ref[...] = p1
        p2_ref[...] = p2

    blk = pl.BlockSpec((n_c, gb, n_p), lambda i: (0, i, 0))
    vec = pl.BlockSpec((gb, n_p), lambda i: (i, 0))
    return pl.pallas_call(
        body, name="s5_scan_bwd", grid=(n_g // gb,), in_specs=[blk, blk, vec, vec],
        out_specs=[blk, vec, vec], out_shape=[_sds(xprev_t.shape), _sds(a1.shape), _sds(a1.shape)],
        compiler_params=_params(("parallel",)),
    )(d_xprev_t, xprev_t, a1, a2)


GROUPS_PER_TILE = LANES // S5_GROUP


def _to_groups(t, n_g):
    n_l = t.shape[0]
    n_c = n_l // S5_T
    gpt = min(GROUPS_PER_TILE, n_g)
    width = gpt * S5_GROUP

    def body(x_ref, o_ref):
        tr = [x_ref[pl.ds(s, n_c, stride=S5_T), :].T for s in range(S5_T)]
        for gl in range(gpt):
            rows = slice(gl * S5_GROUP, (gl + 1) * S5_GROUP)
            stacked = jnp.concatenate([tr[s][rows, :] for s in range(S5_T)], axis=0)
            o_ref[gl] = stacked.T.astype(o_ref.dtype)

    return pl.pallas_call(
        body, name="s5_to_groups", grid=(n_g // gpt,),
        in_specs=[pl.BlockSpec((n_l, width), lambda b: (0, b))],
        out_specs=pl.BlockSpec((gpt, n_c, S5_T * S5_GROUP), lambda b: (b, 0, 0)),
        out_shape=_sds((n_g, n_c, S5_T * S5_GROUP), BF16), compiler_params=_params(("parallel",)),
    )(t)


def _from_groups(t, n_l):
    n_g, n_c = t.shape[0], t.shape[1]
    gpt = min(GROUPS_PER_TILE, n_g)
    width = gpt * S5_GROUP

    def body(y_ref, o_ref):
        ytr = [y_ref[gl].T for gl in range(gpt)]
        for s in range(S5_T):
            rows = slice(s * S5_GROUP, (s + 1) * S5_GROUP)
            piece = jnp.concatenate([ytr[gl][rows, :] for gl in range(gpt)], axis=0)
            o_ref[pl.ds(s, n_c, stride=S5_T), :] = piece.T

    return pl.pallas_call(
        body, name="s5_from_groups", grid=(n_g // gpt,),
        in_specs=[pl.BlockSpec((gpt, n_c, S5_T * S5_GROUP), lambda b: (b, 0, 0))],
        out_specs=pl.BlockSpec((n_l, width), lambda b: (0, b)),
        out_shape=_sds((n_l, n_g * S5_GROUP)), compiler_params=_params(("parallel",)),
    )(t)


def _s5_act(ys, uz, d_skip):
    di = d_skip.shape[1]
    return jax.nn.gelu(ys + d_skip * uz[:, :di])


def _s5_gate(g1, glu_pre, uz, b_glu):
    di = b_glu.shape[1]
    return g1 * jax.nn.sigmoid(glu_pre + b_glu) * jax.nn.silu(uz[:, di:])


def _s5_fwd(h, p):
    n_l = h.shape[0]
    di = p["d_skip"].shape[1]
    n_g = di // S5_GROUP
    hn = _rms_fwd(h, p["norm_g"], name="s5_rms")
    uz = _mm(hn, p["w_in"], name="s5_in")
    toep, wend, wout, a1, a2 = p["ops"]
    u_g = _to_groups(uz, n_g)
    s = _s5_states(u_g, wend)
    xprev = _s5_scan(s.transpose(1, 0, 2), a1, a2).transpose(1, 0, 2)
    ys = _from_groups(_s5_outputs(u_g, toep, xprev, wout), n_l)
    g1 = _rowwise(_s5_act, [ys, uz], [p["d_skip"]], [_sds((n_l, di), BF16)], [], tm=512, name="s5_act")[0]
    glu_pre = _mm(g1, p["w_glu"], name="s5_glu")

    def gate(ys_t, pre_t, uz_t, d_skip, b_glu):
        return _s5_gate(_s5_act(ys_t, uz_t, d_skip), pre_t, uz_t, b_glu)

    gated = _rowwise(gate, [ys, glu_pre, uz], [p["d_skip"], p["b_glu"]], [_sds((n_l, di), BF16)], [],
                     tm=512, name="s5_gate")[0]
    h_next = _mm(gated, p["w_out"], add=h, name="s5_out")
    return h_next, (h, hn, uz, u_g, xprev, ys, g1, glu_pre, gated)


def _s5_bwd(dh_out, p, saved, ops_vjp):
    h, hn, uz, u_g, xprev, ys, g1, glu_pre, gated = saved
    n_l = h.shape[0]
    di = p["d_skip"].shape[1]
    n_g = di // S5_GROUP
    toep, wend, wout, a1, a2 = p["ops"]
    d_gated = _mm(dh_out, p["w_out"], mode="nt", out_dtype=BF16, name="s5_dgated")
    g_w_out = _mm(gated, dh_out, mode="tn", name="s5_dwout")

    def gate_bwd(ys_t, pre_t, uz_t, ct, d_skip, b_glu):
        g1_t = _s5_act(ys_t, uz_t, d_skip)
        _, vjp = jax.vjp(_s5_gate, g1_t, pre_t, uz_t, b_glu)
        d_g1, d_pre, d_uz, d_b = vjp(ct.astype(F32))
        return d_g1, d_pre, d_uz, d_b

    d_g1_direct, d_pre, d_uz_gate, g_b_glu = _rowwise(
        gate_bwd, [ys, glu_pre, uz, d_gated], [p["d_skip"], p["b_glu"]],
        [_sds((n_l, di)), _sds((n_l, di), BF16), _sds(uz.shape)], [_sds(p["b_glu"].shape)], tm=256, name="s5_gate_bwd")
    g_w_glu = _mm(g1, d_pre, mode="tn", name="s5_dwglu")
    d_g1 = _mm(d_pre, p["w_glu"], mode="nt", add=d_g1_direct, name="s5_dg1")

    def act_bwd(ys_t, uz_t, ct, d_uz_t, d_skip):
        _, vjp = jax.vjp(_s5_act, ys_t, uz_t, d_skip)
        d_ys, d_uz, d_d = vjp(ct)
        return d_ys, d_uz + d_uz_t, d_d

    d_ys, d_uz_part, g_d_skip = _rowwise(
        act_bwd, [ys, uz, d_g1, d_uz_gate], [p["d_skip"]], [_sds((n_l, di)), _sds(uz.shape)],
        [_sds(p["d_skip"].shape)], tm=256, name="s5_act_bwd")
    d_y = _to_groups(d_ys, n_g)
    d_toep, d_wout, d_xprev = _s5_outputs_bwd(u_g, d_y, xprev, wout)
    d_s_t, p1, p2 = _s5_scan_bwd(d_xprev.transpose(1, 0, 2), xprev.transpose(1, 0, 2), a1, a2)
    d_s = d_s_t.transpose(1, 0, 2)
    d_u_g, d_wend = _s5_inputs_bwd(u_g, d_y, d_s, toep, wend)
    d_u = _from_groups(d_u_g, n_l)
    d_uz = _rowwise(lambda part, du: jnp.concatenate([part[:, :di] + du, part[:, di:]], axis=1),
                    [d_uz_part, d_u], [], [_sds(uz.shape, BF16)], [], tm=512, name="s5_duz")[0]
    g_w_in = _mm(hn, d_uz, mode="tn", name="s5_dwin")
    d_hn = _mm(d_uz, p["w_in"], mode="nt", name="s5_dhn")
    dh_in, g_norm = _rms_bwd(h, p["norm_g"], d_hn, dh_out, name="s5_rms_bwd")
    g_ops = ops_vjp((d_toep, d_wend, d_wout, p1, p2))
    grads = dict(norm_g=g_norm, w_in=g_w_in, a_re=g_ops[0], a_im=g_ops[1], log_step=g_ops[2], b_re=g_ops[3],
                 b_im=g_ops[4], c_re=g_ops[5], c_im=g_ops[6], d_skip=g_d_skip, w_glu=g_w_glu, b_glu=g_b_glu,
                 w_out=g_w_out)
    return dh_in, grads


MLA_Z0 = MLA_Q_RANK + MLA_KV_RANK + LANES


def _rope_tile(t, cos_t, sin_t):
    q = LANES // 4
    lane = lax.broadcasted_iota(jnp.int32, t.shape, 1)
    swapped = jnp.where(lane < q, pltpu.roll(t, LANES - q, axis=1), pltpu.roll(t, q, axis=1))
    return t * cos_t + swapped * sin_t


def _mla_mid(proj, cos_t, sin_t, q_g, kv_g):
    cqn = _rms(proj[:, :MLA_Q_RANK], q_g)
    ckvn = _rms(proj[:, MLA_Q_RANK:MLA_Q_RANK + MLA_KV_RANK], kv_g)
    kr = _rope_tile(proj[:, MLA_Q_RANK + MLA_KV_RANK:MLA_Z0], cos_t, sin_t)
    return cqn, ckvn, kr


def _mla_rope_q(qp, cos_t, sin_t):
    parts = []
    for hd in range(qp.shape[1] // MLA_HEAD_PAD):
        base = hd * MLA_HEAD_PAD
        parts.append(qp[:, base:base + LANES])
        parts.append(_rope_tile(qp[:, base + LANES:base + MLA_HEAD_PAD], cos_t, sin_t))
    return jnp.concatenate(parts, axis=1)


def _mla_gate(o, proj):
    return o * jax.nn.silu(proj[:, MLA_Z0:])


LOG2E = math.log2(math.e)
SCORE_LOG2 = MLA_SCALE * LOG2E
FLASH_SPLIT = 4


def _causal_pairs(n_blk, kv_major):
    if kv_major:
        pairs = [(i, j) for j in range(n_blk) for i in range(j, n_blk)]
    else:
        pairs = [(i, j) for i in range(n_blk) for j in range(i + 1)]
    return (jnp.asarray([p[0] for p in pairs], jnp.int32), jnp.asarray([p[1] for p in pairs], jnp.int32))


def _raw_scores(q, kcat, row0, diagonal):
    s = _dot(q, kcat, "nt")
    if diagonal:
        qpos = row0 + lax.broadcasted_iota(jnp.int32, s.shape, 0)
        kpos = lax.broadcasted_iota(jnp.int32, s.shape, 1)
        s = jnp.where(kpos <= qpos, s, NEG_INF)
    return s


def _lanes(x, width):
    return jnp.tile(x, (1, width // LANES))


def _flash_fwd(qp, kv, kr, *, blk=1024):
    n_l = qp.shape[0]
    heads = qp.shape[1] // MLA_HEAD_PAD
    blk = _pick(n_l, blk)
    n_blk = n_l // blk
    half = blk // FLASH_SPLIT
    qi, kj = _causal_pairs(n_blk, kv_major=False)

    def body(qi_ref, kj_ref, q_ref, kv_ref, kr_ref, o_ref, lse_ref, m_sc, l_sc, acc_sc):
        p = pl.program_id(1)
        i, j = qi_ref[p], kj_ref[p]

        @pl.when(j == 0)
        def _():
            m_sc[...] = jnp.full_like(m_sc, NEG_INF)
            l_sc[...] = jnp.zeros_like(l_sc)
            acc_sc[...] = jnp.zeros_like(acc_sc)

        def update(diagonal):
            kcat = jnp.concatenate([kv_ref[:, :LANES], kr_ref[...]], axis=1)
            v = kv_ref[:, LANES:]
            for r in range(FLASH_SPLIT):
                rows = slice(r * half, (r + 1) * half)
                s = _raw_scores(q_ref[rows, :], kcat, r * half, diagonal)
                m_old = m_sc[rows, :]
                m_new = jnp.maximum(m_old, jnp.max(s, axis=1, keepdims=True))
                alpha = jnp.exp2((m_old - m_new) * SCORE_LOG2)
                pr = jnp.exp2((s - _lanes(m_new, blk)) * SCORE_LOG2)
                l_sc[rows, :] = alpha * l_sc[rows, :] + jnp.sum(pr, axis=1, keepdims=True)
                acc_sc[rows, :] = alpha * acc_sc[rows, :] + _dot(pr, v)
                m_sc[rows, :] = m_new

        @pl.when(j < i)
        def _():
            update(False)

        @pl.when(j == i)
        def _():
            update(True)
            o_ref[...] = acc_sc[...] / l_sc[...]
            lse_ref[...] = m_sc[...] * MLA_SCALE + jnp.log(l_sc[...])

    grid_spec = pltpu.PrefetchScalarGridSpec(
        num_scalar_prefetch=2, grid=(heads, qi.shape[0]),
        in_specs=[pl.BlockSpec((blk, MLA_HEAD_PAD), lambda h, p, qi_r, kj_r: (qi_r[p], h)),
                  pl.BlockSpec((blk, MLA_HEAD_PAD), lambda h, p, qi_r, kj_r: (kj_r[p], h)),
                  pl.BlockSpec((blk, LANES), lambda h, p, qi_r, kj_r: (kj_r[p], 0))],
        out_specs=[pl.BlockSpec((blk, MLA_V), lambda h, p, qi_r, kj_r: (qi_r[p], h)),
                   pl.BlockSpec((None, blk, LANES), lambda h, p, qi_r, kj_r: (h, qi_r[p], 0))],
        scratch_shapes=[pltpu.VMEM((blk, LANES), F32), pltpu.VMEM((blk, LANES), F32), pltpu.VMEM((blk, MLA_V), F32)])
    return pl.pallas_call(
        body, name="mla_flash_fwd", grid_spec=grid_spec,
        out_shape=[_sds((n_l, heads * MLA_V)), _sds((heads, n_l, LANES))],
        compiler_params=_params(("parallel", "arbitrary")),
    )(qi, kj, qp, kv, kr)


def _flash_bwd(qp, kv, kr, d_o, lse, delta, *, blk=1024):
    n_l = qp.shape[0]
    heads = qp.shape[1] // MLA_HEAD_PAD
    blk = _pick(n_l, blk)
    n_blk = n_l // blk
    half = blk // FLASH_SPLIT
    qi, kj = _causal_pairs(n_blk, kv_major=True)
    n_pairs = qi.shape[0]

    def body(qi_ref, kj_ref, q_ref, kv_ref, kr_ref, do_ref, lse_ref, dl_ref, dq_ref, dkv_ref, dkr_ref, dk_sc, dv_sc):
        h, p = pl.program_id(0), pl.program_id(1)
        i, j = qi_ref[p], kj_ref[p]

        @pl.when(p == 0)
        def _():
            dq_ref[...] = jnp.zeros_like(dq_ref)

        @pl.when(jnp.logical_and(p == 0, h == 0))
        def _():
            dkr_ref[...] = jnp.zeros_like(dkr_ref)

        @pl.when(i == j)
        def _():
            dk_sc[...] = jnp.zeros_like(dk_sc)
            dv_sc[...] = jnp.zeros_like(dv_sc)

        def update(diagonal):
            kcat = jnp.concatenate([kv_ref[:, :LANES], kr_ref[...]], axis=1)
            v = kv_ref[:, LANES:]
            for r in range(FLASH_SPLIT):
                rows = slice(r * half, (r + 1) * half)
                q_t, do_t = q_ref[rows, :], do_ref[rows, :]
                s = _raw_scores(q_t, kcat, r * half, diagonal)
                pr = jnp.exp2(s * SCORE_LOG2 - _lanes(lse_ref[rows, :] * LOG2E, blk))
                d_p = _dot(do_t, v, "nt")
                d_s = pr * (d_p - _lanes(dl_ref[rows, :], blk))
                dk_sc[...] += _dot(d_s, q_t, "tn")
                dv_sc[...] += _dot(pr, do_t, "tn")
                q_rows = pl.ds(pl.multiple_of(i * blk + r * half, half), half)
                dq_ref[q_rows, :] += _dot(d_s, kcat)

        @pl.when(i > j)
        def _():
            update(False)

        @pl.when(i == j)
        def _():
            update(True)

        @pl.when(i == n_blk - 1)
        def _():
            dk = dk_sc[...] * MLA_SCALE
            dkv_ref[:, :LANES] = dk[:, :LANES].astype(dkv_ref.dtype)
            dkv_ref[:, LANES:] = dv_sc[...].astype(dkv_ref.dtype)
            k_rows = pl.ds(pl.multiple_of(j * blk, blk), blk)
            dkr_ref[k_rows, :] += dk[:, LANES:]

        @pl.when(p == n_pairs - 1)
        def _():
            dq_ref[...] = dq_ref[...] * MLA_SCALE

    at_q = lambda h, p, qi_r, kj_r: (qi_r[p], h)
    at_kv = lambda h, p, qi_r, kj_r: (kj_r[p], h)
    grid_spec = pltpu.PrefetchScalarGridSpec(
        num_scalar_prefetch=2, grid=(heads, n_pairs),
        in_specs=[pl.BlockSpec((blk, MLA_HEAD_PAD), at_q),
                  pl.BlockSpec((blk, MLA_HEAD_PAD), at_kv),
                  pl.BlockSpec((blk, LANES), lambda h, p, qi_r, kj_r: (kj_r[p], 0)),
                  pl.BlockSpec((blk, MLA_V), at_q),
                  pl.BlockSpec((None, blk, LANES), lambda h, p, qi_r, kj_r: (h, qi_r[p], 0)),
                  pl.BlockSpec((blk, LANES), at_q)],
        out_specs=[pl.BlockSpec((n_l, MLA_HEAD_PAD), lambda h, p, qi_r, kj_r: (0, h)),
                   pl.BlockSpec((blk, MLA_HEAD_PAD), at_kv),
                   pl.BlockSpec((n_l, LANES), lambda h, p, qi_r, kj_r: (0, 0))],
        scratch_shapes=[pltpu.VMEM((blk, MLA_HEAD_PAD), F32), pltpu.VMEM((blk, MLA_V), F32)])
    return pl.pallas_call(
        body, name="mla_flash_bwd", grid_spec=grid_spec,
        out_shape=[_sds(qp.shape), _sds(kv.shape, BF16), _sds(kr.shape)],
        compiler_params=_params(("arbitrary", "arbitrary")),
    )(qi, kj, qp, kv, kr, d_o, lse, delta)


def _mla_fwd(h, p, rope):
    n_l = h.shape[0]
    cos_t, sin_t = rope
    hn = _rms_fwd(h, p["norm_g"], name="mla_rms")
    proj = _mm(hn, p["w_in"], name="mla_in", tn=896)
    cqn, ckvn, kr = _rowwise(_mla_mid, [proj, cos_t, sin_t], [p["q_norm_g"], p["kv_norm_g"]],
                             [_sds((n_l, MLA_Q_RANK), BF16), _sds((n_l, MLA_KV_RANK), BF16), _sds((n_l, LANES), BF16)],
                             [], tm=512, name="mla_mid")
    q_raw = _mm(cqn, p["w_uq"], name="mla_uq")
    qp = _rowwise(_mla_rope_q, [q_raw, cos_t, sin_t], [], [_sds(q_raw.shape, BF16)], [], tm=512, name="mla_rope_q")[0]
    kv = _mm(ckvn, p["w_ukv"], out_dtype=BF16, name="mla_ukv")
    o, lse = _flash_fwd(qp, kv, kr)
    gated = _rowwise(_mla_gate, [o, proj], [], [_sds(o.shape, BF16)], [], tm=512, name="mla_gate")[0]
    h_next = _mm(gated, p["w_out"], add=h, name="mla_out")
    return h_next, (h, hn, proj, cqn, ckvn, kr, qp, kv, o, lse, gated)


def _mla_bwd(dh_out, p, saved, rope):
    h, hn, proj, cqn, ckvn, kr, qp, kv, o, lse, gated = saved
    n_l = h.shape[0]
    cos_t, sin_t = rope
    d_gated = _mm(dh_out, p["w_out"], mode="nt", out_dtype=BF16, name="mla_dgated")
    g_w_out = _mm(gated, dh_out, mode="tn", name="mla_dwout")

    def gate_bwd(o_t, proj_t, ct):
        _, vjp = jax.vjp(lambda a, z: a * jax.nn.silu(z), o_t, proj_t[:, MLA_Z0:])
        d_o_t, d_z_t = vjp(ct.astype(F32))
        prod = d_o_t * o_t
        delta = jnp.concatenate(
            [jnp.broadcast_to(jnp.sum(prod[:, hd * MLA_V:(hd + 1) * MLA_V], axis=1, keepdims=True),
                              (prod.shape[0], MLA_V)) for hd in range(prod.shape[1] // MLA_V)], axis=1)
        return d_o_t, d_z_t, delta

    d_o, d_z, delta = _rowwise(gate_bwd, [o, proj, d_gated], [], [_sds(o.shape, BF16), _sds(o.shape), _sds(o.shape)],
                               [], tm=512, name="mla_gate_bwd")
    d_qp, d_kv, d_kr = _flash_bwd(qp, kv, kr, d_o, lse, delta)

    def rope_q_bwd(ct, c_t, s_t):
        return _mla_rope_q(ct, c_t, -s_t)

    d_q_raw = _rowwise(rope_q_bwd, [d_qp, cos_t, sin_t], [], [_sds(d_qp.shape, BF16)], [], tm=512,
                       name="mla_rope_q_bwd")[0]
    g_w_uq = _mm(cqn, d_q_raw, mode="tn", name="mla_dwuq")
    d_cqn = _mm(d_q_raw, p["w_uq"], mode="nt", name="mla_dcqn")
    g_w_ukv = _mm(ckvn, d_kv, mode="tn", name="mla_dwukv")
    d_ckvn = _mm(d_kv, p["w_ukv"], mode="nt", name="mla_dckvn")

    def mid_bwd(proj_t, c_t, s_t, d_cq, d_ckv, d_kr_t, d_z_t, q_g, kv_g):
        _, vjp_q = jax.vjp(_rms, proj_t[:, :MLA_Q_RANK], q_g)
        _, vjp_kv = jax.vjp(_rms, proj_t[:, MLA_Q_RANK:MLA_Q_RANK + MLA_KV_RANK], kv_g)
        d_q_in, d_qg = vjp_q(d_cq)
        d_kv_in, d_kvg = vjp_kv(d_ckv)
        d_kr_in = _rope_tile(d_kr_t, c_t, -s_t)
        return jnp.concatenate([d_q_in, d_kv_in, d_kr_in, d_z_t], axis=1), d_qg, d_kvg

    d_proj, g_q_norm, g_kv_norm = _rowwise(
        mid_bwd, [proj, cos_t, sin_t, d_cqn, d_ckvn, d_kr, d_z], [p["q_norm_g"], p["kv_norm_g"]],
        [_sds(proj.shape, BF16)], [_sds(p["q_norm_g"].shape), _sds(p["kv_norm_g"].shape)], tm=512, name="mla_mid_bwd")
    g_w_in = _mm(hn, d_proj, mode="tn", name="mla_dwin", tn=896)
    d_hn = _mm(d_proj, p["w_in"], mode="nt", name="mla_dhn", tk=896)
    dh_in, g_norm = _rms_bwd(h, p["norm_g"], d_hn, dh_out, name="mla_rms_bwd")
    grads = dict(norm_g=g_norm, w_in=g_w_in, q_norm_g=g_q_norm, w_uq=g_w_uq, kv_norm_g=g_kv_norm, w_ukv=g_w_ukv,
                 w_out=g_w_out)
    return dh_in, grads


def _loss_head(h, g, target):
    def fn(x, t, gg):
        def local(xx, g2):
            err = _rms(xx, g2) - t
            return 0.5 * jnp.sum(jnp.mean(err * err, axis=-1))

        val, (dx, dg) = jax.value_and_grad(local, argnums=(0, 1))(x, gg)
        return dx, jnp.full((1, LANES), val, F32), dg

    dh, loss, dg = _rowwise(fn, [h, target], [g], [_sds(h.shape)], [_sds((1, LANES)), _sds(g.shape)], tm=512,
                            name="loss_head")
    return loss[0, 0], dh, dg


HBM_SPEC = pl.BlockSpec(memory_space=pltpu.HBM)


AG_COPIES = 7


def _all_gather(shards, *, name):
    n_arr = len(shards)

    def body(*refs):
        x_refs, out_refs = refs[:n_arr], refs[n_arr:2 * n_arr]
        send_sems, recv_sems, local_sems = refs[2 * n_arr:]
        x, y, c = lax.axis_index("x"), lax.axis_index("y"), lax.axis_index("c")
        me, sibling = (x, y, c), (x, y, 1 - c)
        chips = [(1 - x, y), (x, 1 - y), (1 - x, 1 - y)]

        def block_of(t, px, py, pc):
            return out_refs[t].at[4 * px + 2 * py + pc]

        def copy(t, k, block, to, src=None):
            return pltpu.make_async_remote_copy(
                src_ref=block_of(t, *block) if src is None else src, dst_ref=block_of(t, *block),
                send_sem=send_sems.at[t * AG_COPIES + k], recv_sem=recv_sems.at[t * AG_COPIES + k],
                device_id=to, device_id_type=MESH)

        mine = [pltpu.make_async_copy(x_refs[t], block_of(t, *me), local_sems.at[t]) for t in range(n_arr)]
        for cp in mine:
            cp.start()
        first = []
        for t in range(n_arr):
            first.append(copy(t, 0, me, sibling, src=x_refs[t]))
            first += [copy(t, 1 + j, me, (*chip, c), src=x_refs[t]) for j, chip in enumerate(chips)]
        for cp in first:
            cp.start()
        passed = []
        for j, chip in enumerate(chips):
            for t in range(n_arr):
                copy(t, 1 + j, (*chip, c), me).wait_recv()
                passed.append(copy(t, 4 + j, (*chip, c), sibling))
                passed[-1].start()
        for t in range(n_arr):
            copy(t, 0, sibling, me).wait_recv()
        for j, chip in enumerate(chips):
            for t in range(n_arr):
                copy(t, 4 + j, (*chip, 1 - c), me).wait_recv()
        for cp in first + passed:
            cp.wait_send()
        for cp in mine:
            cp.wait()

    return pl.pallas_call(
        body, name=name, out_shape=[jax.ShapeDtypeStruct((N_DEV,) + s.shape, s.dtype) for s in shards],
        in_specs=[HBM_SPEC] * n_arr, out_specs=[HBM_SPEC] * n_arr,
        scratch_shapes=[pltpu.SemaphoreType.DMA((n_arr * AG_COPIES,)), pltpu.SemaphoreType.DMA((n_arr * AG_COPIES,)),
                        pltpu.SemaphoreType.DMA((n_arr,))],
    )(*shards)


def _exchange(src, routes, *, name):
    n_routes = len(routes)

    def body(s_ref, out_ref, send_sems, recv_sems):
        x, y, c = lax.axis_index("x"), lax.axis_index("y"), lax.axis_index("c")
        local, remote = [], []
        for k, (flip, block) in enumerate(routes):
            src_blk = s_ref.at[block(x, y, c)]
            if flip == 0:
                local.append(pltpu.make_async_copy(src_blk, out_ref.at[k], send_sems.at[k]))
            else:
                peer = (1 - x if flip & 4 else x, 1 - y if flip & 2 else y, 1 - c if flip & 1 else c)
                remote.append(pltpu.make_async_remote_copy(
                    src_ref=src_blk, dst_ref=out_ref.at[k], send_sem=send_sems.at[k], recv_sem=recv_sems.at[k],
                    device_id=peer, device_id_type=MESH))
        for cp in local + remote:
            cp.start()
        for cp in remote:
            cp.wait_recv()
        for cp in remote:
            cp.wait_send()
        for cp in local:
            cp.wait()

    return pl.pallas_call(
        body, name=name, out_shape=jax.ShapeDtypeStruct((n_routes,) + src.shape[1:], src.dtype),
        in_specs=[HBM_SPEC], out_specs=HBM_SPEC,
        scratch_shapes=[pltpu.SemaphoreType.DMA((n_routes,)), pltpu.SemaphoreType.DMA((n_routes,))],
    )(src)


def _reduce_scatter(send, *, name):
    def chip_block(k, other_core):
        return lambda x, y, c: (4 * (1 - x if k & 2 else x) + 2 * (1 - y if k & 1 else y)
                                + (1 - c if other_core else c))

    n_chips = 4
    pair = _exchange(send, [(1, chip_block(k, True)) for k in range(n_chips)], name=name + "_pair")
    rows, width = send.shape[1], send.shape[2]
    tr = _pick(rows, 256, 8)
    x, y, c = lax.axis_index("x"), lax.axis_index("y"), lax.axis_index("c")
    own_ids = jnp.stack([chip_block(k, False)(x, y, c) for k in range(n_chips)]).astype(jnp.int32)

    def add_body(ids_ref, *refs):
        own_refs, p_ref, o_ref = refs[:n_chips], refs[n_chips], refs[n_chips + 1]
        for k in range(n_chips):
            o_ref[k] = (own_refs[k][...].astype(F32) + p_ref[k].astype(F32)).astype(o_ref.dtype)

    own_spec = lambda k: pl.BlockSpec((None, tr, width), lambda i, ids: (ids[k], i, 0))
    chip_sums = pl.pallas_call(
        add_body, name=name + "_pair_sum",
        grid_spec=pltpu.PrefetchScalarGridSpec(
            num_scalar_prefetch=1, grid=(rows // tr,),
            in_specs=[own_spec(k) for k in range(n_chips)] + [pl.BlockSpec((n_chips, tr, width), lambda i, ids: (0, i, 0))],
            out_specs=pl.BlockSpec((n_chips, tr, width), lambda i, ids: (0, i, 0))),
        out_shape=jax.ShapeDtypeStruct((n_chips, rows, width), send.dtype), compiler_params=_params(("parallel",)),
    )(own_ids, *([send] * n_chips), pair)
    recv = _exchange(chip_sums, [(2 * k, (lambda kk: lambda x, y, c: kk)(k)) for k in range(1, n_chips)],
                     name=name + "_chips")

    def sum_body(q_ref, r_ref, o_ref):
        acc = q_ref[...].astype(F32)
        for k in range(n_chips - 1):
            acc = acc + r_ref[k].astype(F32)
        o_ref[...] = acc

    return pl.pallas_call(
        sum_body, name=name + "_sum", grid=(rows // tr,),
        in_specs=[pl.BlockSpec((None, tr, width), lambda i: (0, i, 0)),
                  pl.BlockSpec((n_chips - 1, tr, width), lambda i: (0, i, 0))],
        out_specs=pl.BlockSpec((tr, width), lambda i: (i, 0)), out_shape=_sds((rows, width)),
        compiler_params=_params(("parallel",)),
    )(chip_sums, recv)


def _adamw(w, g, m, v, *, name):
    def fn(ww, gg, mm, vv):
        m_new = ADAM_B1 * mm + (1.0 - ADAM_B1) * gg
        v_new = ADAM_B2 * vv + (1.0 - ADAM_B2) * jnp.square(gg)
        m_hat = m_new / (1.0 - ADAM_B1 ** ADAM_STEP)
        v_hat = v_new / (1.0 - ADAM_B2 ** ADAM_STEP)
        return -ADAM_LR * (m_hat / (jnp.sqrt(v_hat) + ADAM_EPS) + ADAM_WD * ww), m_new, v_new

    return _rowwise(fn, [w, g, m, v], [], [_sds(w.shape)] * 3, [], tm=256, name=name)


KINDS = ("gmlp", "s5", "mla", "gmlp")
LAYER_NAMES = {
    "gmlp": ("norm_g", "w_in", "ln_g", "ln_b", "w_s", "b_s", "w_out"),
    "s5": ("norm_g", "w_in", "a_re", "a_im", "log_step", "b_re", "b_im", "c_re", "c_im", "d_skip", "w_glu", "b_glu",
           "w_out"),
    "mla": ("norm_g", "w_in", "q_norm_g", "w_uq", "kv_norm_g", "w_ukv", "w_out"),
}
COL_SHARDED = ("w_in", "w_uq", "w_ukv")
ROW_SHARDED = ("w_out", "w_glu")
WEIGHT_NAMES = tuple("l%d_%s" % (i, n) for i, k in enumerate(KINDS) for n in LAYER_NAMES[k]) + ("final_norm_g",)


def _is_sharded(name):
    return name.split("_", 1)[1] in COL_SHARDED + ROW_SHARDED


def _flatten(arrs, pad_rows_to):
    parts, sizes = [], []
    for a in arrs:
        flat = a.reshape(-1)
        pad = (-flat.shape[0]) % FLAT_W
        if pad:
            flat = jnp.pad(flat, (0, pad))
        parts.append(flat)
        sizes.append(flat.shape[0] // FLAT_W)
    rows = sum(sizes)
    pad_rows = (-rows) % pad_rows_to
    if pad_rows:
        parts.append(jnp.zeros((pad_rows * FLAT_W,), arrs[0].dtype))
    return jnp.concatenate(parts).reshape(-1, FLAT_W), sizes


def _unflatten(flat, shapes, sizes):
    out, row = [], 0
    for shape, n_rows in zip(shapes, sizes):
        n = int(np.prod(shape))
        out.append(flat[row:row + n_rows].reshape(-1)[:n].reshape(shape))
        row += n_rows
    return out


def _full_from_gathered(blocks, name):
    if name.split("_", 1)[1] in COL_SHARDED:
        return blocks.transpose(1, 0, 2).reshape(blocks.shape[1], -1)
    return blocks.reshape(-1, blocks.shape[2])


def _shards_of(full, name):
    if name.split("_", 1)[1] in COL_SHARDED:
        return full.reshape(full.shape[0], N_DEV, -1).transpose(1, 0, 2)
    return full.reshape(N_DEV, -1, full.shape[1])


def _rope_tables(positions):
    inv_freq = ROPE_THETA ** (-jnp.arange(0, MLA_ROPE, 2, dtype=F32) / MLA_ROPE)
    ang = positions.astype(F32)[:, None] * inv_freq
    cos, sin = jnp.cos(ang), jnp.sin(ang)
    zero = jnp.zeros((positions.shape[0], LANES - MLA_ROPE), F32)
    return jnp.concatenate([cos, cos, zero], axis=1), jnp.concatenate([-sin, sin, zero], axis=1)


def _row(v):
    return v.reshape(1, -1)


def kernel(x, positions, l0_norm_g, l0_w_in, l0_ln_g, l0_ln_b, l0_w_s, l0_b_s, l0_w_out, l1_norm_g, l1_w_in, l1_a_re, l1_a_im, l1_log_step, l1_b_re, l1_b_im, l1_c_re, l1_c_im, l1_d_skip, l1_w_glu, l1_b_glu, l1_w_out, l2_norm_g, l2_w_in, l2_q_norm_g, l2_w_uq, l2_kv_norm_g, l2_w_ukv, l2_w_out, l3_norm_g, l3_w_in, l3_ln_g, l3_ln_b, l3_w_s, l3_b_s, l3_w_out, final_norm_g, loss_target, m_l0_norm_g, m_l0_w_in, m_l0_ln_g, m_l0_ln_b, m_l0_w_s, m_l0_b_s, m_l0_w_out, m_l1_norm_g, m_l1_w_in, m_l1_a_re, m_l1_a_im, m_l1_log_step, m_l1_b_re, m_l1_b_im, m_l1_c_re, m_l1_c_im, m_l1_d_skip, m_l1_w_glu, m_l1_b_glu, m_l1_w_out, m_l2_norm_g, m_l2_w_in, m_l2_q_norm_g, m_l2_w_uq, m_l2_kv_norm_g, m_l2_w_ukv, m_l2_w_out, m_l3_norm_g, m_l3_w_in, m_l3_ln_g, m_l3_ln_b, m_l3_w_s, m_l3_b_s, m_l3_w_out, m_final_norm_g, v_l0_norm_g, v_l0_w_in, v_l0_ln_g, v_l0_ln_b, v_l0_w_s, v_l0_b_s, v_l0_w_out, v_l1_norm_g, v_l1_w_in, v_l1_a_re, v_l1_a_im, v_l1_log_step, v_l1_b_re, v_l1_b_im, v_l1_c_re, v_l1_c_im, v_l1_d_skip, v_l1_w_glu, v_l1_b_glu, v_l1_w_out, v_l2_norm_g, v_l2_w_in, v_l2_q_norm_g, v_l2_w_uq, v_l2_kv_norm_g, v_l2_w_ukv, v_l2_w_out, v_l3_norm_g, v_l3_w_in, v_l3_ln_g, v_l3_ln_b, v_l3_w_s, v_l3_b_s, v_l3_w_out, v_final_norm_g):
    args = locals()
    weights = {n: args[n] for n in WEIGHT_NAMES}
    mom_m = {n: args["m_" + n] for n in WEIGHT_NAMES}
    mom_v = {n: args["v_" + n] for n in WEIGHT_NAMES}
    return _train_step(x, positions, loss_target, weights, mom_m, mom_v)


def _train_step(x, positions, loss_target, weights, mom_m, mom_v):
    big = [n for n in WEIGHT_NAMES if _is_sharded(n)]
    small = [n for n in WEIGHT_NAMES if not _is_sharded(n)]

    big_sizes = [weights[n].size // FLAT_W for n in big]
    gathered = _all_gather([weights[n].astype(BF16) for n in big], name="weights_all_gather")
    full = {n: _full_from_gathered(blocks, n) for n, blocks in zip(big, gathered)}

    layers, ops_vjps = [], {}
    for i, kind in enumerate(KINDS):
        pre = "l%d_" % i
        p = {n: (full[pre + n] if _is_sharded(pre + n) else weights[pre + n]) for n in LAYER_NAMES[kind]}
        p["norm_g"] = _row(p["norm_g"])
        if kind == "gmlp":
            p["ln_g"], p["ln_b"], p["b_st"] = _row(p["ln_g"]), _row(p["ln_b"]), p["b_s"].T
        elif kind == "s5":
            p["d_skip"], p["b_glu"] = _row(p["d_skip"]), _row(p["b_glu"])
            ops, ops_vjps[i] = jax.vjp(_s5_operators, *[p[n] for n in ("a_re", "a_im", "log_step", "b_re", "b_im",
                                                                       "c_re", "c_im")])
            p["ops"] = tuple(o.astype(BF16) for o in ops[:3]) + ops[3:]
        else:
            heads = p["w_uq"].shape[1] // MLA_QK_DIM
            w_in = p["w_in"]
            split = MLA_Q_RANK + MLA_KV_RANK + MLA_ROPE
            p["w_in"] = jnp.concatenate([w_in[:, :split], jnp.zeros((w_in.shape[0], LANES - MLA_ROPE), w_in.dtype),
                                         w_in[:, split:]], axis=1)
            p["w_uq"] = jnp.pad(p["w_uq"].reshape(-1, heads, MLA_QK_DIM),
                                ((0, 0), (0, 0), (0, MLA_HEAD_PAD - MLA_QK_DIM))).reshape(-1, heads * MLA_HEAD_PAD)
            p["q_norm_g"], p["kv_norm_g"] = _row(p["q_norm_g"]), _row(p["kv_norm_g"])
        layers.append(p)
    rope = _rope_tables(positions[0])

    h = x[0]
    saved = []
    for kind, p in zip(KINDS, layers):
        if kind == "gmlp":
            h, s = _gmlp_fwd(h, p)
        elif kind == "s5":
            h, s = _s5_fwd(h, p)
        else:
            h, s = _mla_fwd(h, p, rope)
        saved.append(s)
    loss_local, dh, g_final = _loss_head(h, _row(weights["final_norm_g"]), loss_target[0])
    loss = lax.psum(loss_local, ("x", "y", "c"))

    grads = {"final_norm_g": g_final.reshape(-1)}
    for i in reversed(range(len(KINDS))):
        kind, p = KINDS[i], layers[i]
        if kind == "gmlp":
            dh, g = _gmlp_bwd(dh, p, saved[i])
        elif kind == "s5":
            dh, g = _s5_bwd(dh, p, saved[i], ops_vjps[i])
        else:
            dh, g = _mla_bwd(dh, p, saved[i], rope)
            heads = weights["l%d_w_uq" % i].shape[1] * N_DEV // MLA_QK_DIM
            split = MLA_Q_RANK + MLA_KV_RANK + MLA_ROPE
            g["w_in"] = jnp.concatenate([g["w_in"][:, :split], g["w_in"][:, MLA_Z0:]], axis=1)
            g["w_uq"] = g["w_uq"].reshape(-1, heads, MLA_HEAD_PAD)[:, :, :MLA_QK_DIM].reshape(-1, heads * MLA_QK_DIM)
        for n, val in g.items():
            name = "l%d_%s" % (i, n)
            grads[name] = val.reshape(weights[name].shape) if not _is_sharded(name) else val

    small_flat, small_sizes = _flatten([grads[n] for n in small], 8 * N_DEV)
    small_rows = small_flat.shape[0] // N_DEV
    send_parts = [_shards_of(grads[n], n).reshape(N_DEV, -1, FLAT_W) for n in big]
    send_parts.append(small_flat.reshape(N_DEV, small_rows, FLAT_W))
    send = jnp.concatenate(send_parts, axis=1)
    big_rows = send.shape[1] - small_rows
    pad_rows = (-send.shape[1]) % 8
    if pad_rows:
        send = jnp.pad(send, ((0, 0), (0, pad_rows), (0, 0)))
    reduced = _reduce_scatter(send.astype(BF16), name="grads")
    g_big_flat = reduced[:big_rows]
    g_small_all = _all_gather([reduced[big_rows:big_rows + small_rows]], name="small_grads_all_gather")[0]
    g_small_flat = g_small_all.reshape(-1, FLAT_W)

    def flat_of(tree, names, pad_to):
        return _flatten([tree[n] for n in names], pad_to)[0]

    outs = {}
    big_shapes = [weights[n].shape for n in big]
    for n, g_n in zip(big, _unflatten(g_big_flat, big_shapes, big_sizes)):
        outs["grad_" + n] = g_n
        outs["delta_" + n], outs["new_m_" + n], outs["new_v_" + n] = _adamw(
            weights[n], g_n, mom_m[n], mom_v[n], name="adamw_" + n)
    d_s, nm_s, nv_s = _adamw(flat_of(weights, small, 8 * N_DEV), g_small_flat, flat_of(mom_m, small, 8 * N_DEV),
                             flat_of(mom_v, small, 8 * N_DEV), name="adamw_replicated")
    small_shapes = [weights[n].shape for n in small]
    for prefix, fs in (("grad_", g_small_flat), ("delta_", d_s), ("new_m_", nm_s), ("new_v_", nv_s)):
        for n, a in zip(small, _unflatten(fs, small_shapes, small_sizes)):
            outs[prefix + n] = a
    result = [loss, dh[None]]
    for prefix in ("grad_", "delta_", "new_m_", "new_v_"):
        result += [outs[prefix + n] for n in WEIGHT_NAMES]
    return tuple(result)
```

```python
import functools
import math

import numpy as np
import jax
import jax.numpy as jnp
from jax import lax
from jax.experimental import pallas as pl
from jax.experimental.pallas import tpu as pltpu

F32 = jnp.float32
BF16 = jnp.bfloat16

NORM_EPS = 1e-6
GMLP_CHUNK = 128
S5_GROUP = 16
S5_STATE = 64
S5_T = 16
MLA_NOPE = 128
MLA_ROPE = 64
MLA_V = 128
MLA_QK_DIM = MLA_NOPE + MLA_ROPE
MLA_Q_RANK = 384
MLA_KV_RANK = 128
MLA_HEAD_PAD = 256
MLA_SCALE = MLA_QK_DIM ** -0.5
ROPE_THETA = 10000.0
NEG_INF = -1e30
ADAM_LR = 0.001
ADAM_B1 = 0.9
ADAM_B2 = 0.999
ADAM_EPS = 1e-08
ADAM_WD = 0.01
ADAM_STEP = 10

N_DEV = 8
LANES = 128
FLAT_W = 1024
VMEM_LIMIT = 56 * 1024 * 1024
MESH = pl.DeviceIdType.MESH


def _pick(dim, pref, align=LANES):
    t = (min(pref, dim) // align) * align
    while t >= align:
        if dim % t == 0:
            return t
        t -= align
    return dim


def _params(sem=None):
    return pltpu.CompilerParams(dimension_semantics=sem, vmem_limit_bytes=VMEM_LIMIT)


_DIMS = {"nn": (((1,), (0,)), ((), ())), "nt": (((1,), (1,)), ((), ())), "tn": (((0,), (0,)), ((), ()))}


def _mm(a, b, *, mode="nn", out_dtype=F32, add=None, name, tm=1024, tn=1024, tk=2048):
    if mode == "nn":
        (m, k), (_, n) = a.shape, b.shape
    elif mode == "nt":
        (m, k), (n, _) = a.shape, b.shape
    else:
        (k, m), (_, n) = a.shape, b.shape
    tm, tn, tk = _pick(m, tm, 8), _pick(n, tn), _pick(k, tk)
    nk = k // tk
    dims = _DIMS[mode]

    def body(*refs):
        a_ref, b_ref = refs[:2]
        r_ref = refs[2] if add is not None else None
        o_ref = refs[3] if add is not None else refs[2]
        part = lax.dot_general(a_ref[...].astype(BF16), b_ref[...].astype(BF16), dims, preferred_element_type=F32)

        def finish(res):
            if add is not None:
                res = res + r_ref[...]
            o_ref[...] = res.astype(o_ref.dtype)

        if nk == 1:
            finish(part)
            return
        acc_ref = refs[-1]
        kk = pl.program_id(2)

        @pl.when(kk == 0)
        def _():
            acc_ref[...] = part

        @pl.when(kk > 0)
        def _():
            acc_ref[...] += part

        @pl.when(kk == nk - 1)
        def _():
            finish(acc_ref[...])

    a_spec = (pl.BlockSpec((tk, tm), lambda i, j, kk: (kk, i)) if mode == "tn"
              else pl.BlockSpec((tm, tk), lambda i, j, kk: (i, kk)))
    b_spec = (pl.BlockSpec((tn, tk), lambda i, j, kk: (j, kk)) if mode == "nt"
              else pl.BlockSpec((tk, tn), lambda i, j, kk: (kk, j)))
    in_specs = [a_spec, b_spec]
    args = [a, b]
    if add is not None:
        in_specs.append(pl.BlockSpec((tm, tn), lambda i, j, kk: (i, j)))
        args.append(add)
    return pl.pallas_call(
        body, name=name, grid=(m // tm, n // tn, nk),
        in_specs=in_specs, out_specs=pl.BlockSpec((tm, tn), lambda i, j, kk: (i, j)),
        out_shape=jax.ShapeDtypeStruct((m, n), out_dtype),
        scratch_shapes=[pltpu.VMEM((tm, tn), F32)] if nk > 1 else [],
        compiler_params=_params(("parallel", "parallel", "arbitrary")),
    )(*args)


def _rowwise(fn, tiled, full, out_tiled, out_acc, *, tm, name):
    rows = tiled[0].shape[0]
    tm = _pick(rows, tm, 8)
    nt, nf, no = len(tiled), len(full), len(out_tiled)

    def body(*refs):
        ins = [r[...] for r in refs[:nt + nf]]
        o_refs = refs[nt + nf:nt + nf + no]
        a_refs = refs[nt + nf + no:]
        outs = fn(*ins)
        if not isinstance(outs, (tuple, list)):
            outs = (outs,)
        for r, v in zip(o_refs, outs[:no]):
            r[...] = v.astype(r.dtype)
        if a_refs:
            @pl.when(pl.program_id(0) == 0)
            def _():
                for r in a_refs:
                    r[...] = jnp.zeros_like(r)

            for r, v in zip(a_refs, outs[no:]):
                r[...] += v.astype(r.dtype)

    def whole(shape):
        nd = len(shape)
        return pl.BlockSpec(tuple(shape), lambda i: (0,) * nd)

    in_specs = ([pl.BlockSpec((tm, t.shape[1]), lambda i: (i, 0)) for t in tiled]
                + [whole(f.shape) for f in full])
    out_specs = ([pl.BlockSpec((tm, o.shape[1]), lambda i: (i, 0)) for o in out_tiled]
                 + [whole(o.shape) for o in out_acc])
    outs = pl.pallas_call(
        body, name=name, grid=(rows // tm,), in_specs=in_specs, out_specs=out_specs,
        out_shape=list(out_tiled) + list(out_acc),
        compiler_params=_params(("arbitrary",)),
    )(*tiled, *full)
    return outs


def _sds(shape, dtype=F32):
    return jax.ShapeDtypeStruct(tuple(shape), dtype)


def _rms(x, g):
    return x * lax.rsqrt(jnp.mean(x * x, axis=-1, keepdims=True) + NORM_EPS) * g


def _layernorm(x, g, b):
    mu = jnp.mean(x, axis=-1, keepdims=True)
    xc = x - mu
    var = jnp.mean(xc * xc, axis=-1, keepdims=True)
    return xc * lax.rsqrt(var + NORM_EPS) * g + b


def _dot(a, b, mode="nn"):
    return lax.dot_general(a.astype(BF16), b.astype(BF16), _DIMS[mode], preferred_element_type=F32)


@jax.custom_vjp
def _bdot(a, b):
    return _dot(a, b)


def _bdot_fwd(a, b):
    return _bdot(a, b), (a, b)


def _bdot_bwd(res, ct):
    a, b = res
    return _dot(ct, b, "nt"), _dot(a, ct, "tn")


_bdot.defvjp(_bdot_fwd, _bdot_bwd)


def _rms_fwd(h, g, *, name):
    return _rowwise(lambda x, gg: _rms(x, gg), [h], [g], [_sds(h.shape, BF16)], [], tm=512, name=name)[0]


def _rms_bwd(h, g, d_hn, dh_out, *, name):
    def fn(x, ct, res, gg):
        _, vjp = jax.vjp(_rms, x, gg)
        dx, dg = vjp(ct)
        return res + dx, dg

    dh, dg = _rowwise(fn, [h, d_hn, dh_out], [g], [_sds(h.shape)], [_sds(g.shape)], tm=512, name=name)
    return dh, dg


def _gmlp_mid(uvz, ln_g, ln_b, w_s, b_st):
    di = ln_g.shape[1]
    ng, ck = w_s.shape[0], w_s.shape[1]
    dg = di // ng
    u = jax.nn.gelu(uvz[:, :di])
    v = _layernorm(jax.nn.gelu(uvz[:, di:2 * di]), ln_g, ln_b)
    z = uvz[:, 2 * di:]
    row = lax.broadcasted_iota(jnp.int32, (ck, ck), 0)
    col = lax.broadcasted_iota(jnp.int32, (ck, ck), 1)
    causal = col <= row
    blocks = []
    for c in range(uvz.shape[0] // ck):
        cols = []
        for g in range(ng):
            w = jnp.where(causal, w_s[g], 0.0)
            cols.append(_bdot(w, v[c * ck:(c + 1) * ck, g * dg:(g + 1) * dg]) + b_st[:, g:g + 1])
        blocks.append(jnp.concatenate(cols, axis=1))
    s = blocks[0] if len(blocks) == 1 else jnp.concatenate(blocks, axis=0)
    return u * s * jax.nn.silu(z)


def _gmlp_fwd(h, p):
    hn = _rms_fwd(h, p["norm_g"], name="gmlp_rms")
    uvz = _mm(hn, p["w_in"], name="gmlp_in")
    di = p["ln_g"].shape[1]
    gated = _rowwise(_gmlp_mid, [uvz], [p["ln_g"], p["ln_b"], p["w_s"], p["b_st"]],
                     [_sds((h.shape[0], di), BF16)], [], tm=256, name="gmlp_mid")[0]
    h_next = _mm(gated, p["w_out"], add=h, name="gmlp_out")
    return h_next, (h, hn, uvz, gated)


def _gmlp_bwd(dh_out, p, saved):
    h, hn, uvz, gated = saved
    d_gated = _mm(dh_out, p["w_out"], mode="nt", out_dtype=BF16, name="gmlp_dgated")
    g_w_out = _mm(gated, dh_out, mode="tn", name="gmlp_dwout")

    def fn(t, ct, ln_g, ln_b, w_s, b_st):
        _, vjp = jax.vjp(_gmlp_mid, t, ln_g, ln_b, w_s, b_st)
        return vjp(ct.astype(F32))

    d_uvz, g_ln_g, g_ln_b, g_w_s, g_b_st = _rowwise(
        fn, [uvz, d_gated], [p["ln_g"], p["ln_b"], p["w_s"], p["b_st"]],
        [_sds(uvz.shape, BF16)], [_sds(p["ln_g"].shape), _sds(p["ln_b"].shape), _sds(p["w_s"].shape),
                                  _sds(p["b_st"].shape)], tm=256, name="gmlp_mid_bwd")
    g_w_in = _mm(hn, d_uvz, mode="tn", name="gmlp_dwin")
    d_hn = _mm(d_uvz, p["w_in"], mode="nt", name="gmlp_dhn")
    dh_in, g_norm = _rms_bwd(h, p["norm_g"], d_hn, dh_out, name="gmlp_rms_bwd")
    grads = dict(norm_g=g_norm, w_in=g_w_in, ln_g=g_ln_g, ln_b=g_ln_b, w_s=g_w_s, b_s=g_b_st.T, w_out=g_w_out)
    return dh_in, grads


def _s5_operators(a_re, a_im, log_step, b_re, b_im, c_re, c_im):
    t_len = S5_T
    step = jnp.exp(log_step)[:, None]
    lr, li = a_re * step, a_im * step
    ks = jnp.arange(t_len + 1, dtype=F32)[:, None, None]
    mag = jnp.exp(ks * lr)
    pw_r, pw_i = mag * jnp.cos(ks * li), mag * jnp.sin(ks * li)
    nr, ni = pw_r[1] - 1.0, pw_i[1]
    den = a_re * a_re + a_im * a_im
    f_r, f_i = (nr * a_re + ni * a_im) / den, (ni * a_re - nr * a_im) / den
    bb_r = f_r[..., None] * b_re - f_i[..., None] * b_im
    bb_i = f_r[..., None] * b_im + f_i[..., None] * b_re
    hi = lax.Precision.HIGHEST
    cp_r = c_re[None] * pw_r[:, :, None, :] - c_im[None] * pw_i[:, :, None, :]
    cp_i = c_re[None] * pw_i[:, :, None, :] + c_im[None] * pw_r[:, :, None, :]
    n_g, n_h = a_re.shape[0], b_re.shape[2]
    lhs = jnp.concatenate([bb_r, -bb_i], axis=1)
    rhs = jnp.concatenate([cp_r[:t_len], cp_i[:t_len]], axis=3)
    rhs = rhs.transpose(1, 3, 0, 2).reshape(n_g, -1, t_len * n_h)
    kcat = jnp.einsum("gqi,gqn->gin", lhs, rhs, precision=hi)
    toep = jnp.stack([jnp.pad(kcat[:, :, :(t_len - s) * n_h], ((0, 0), (0, 0), (s * n_h, 0)))
                      for s in range(t_len)], axis=1).reshape(n_g, t_len * n_h, t_len * n_h)
    rev_r, rev_i = pw_r[t_len - 1::-1][:t_len], pw_i[t_len - 1::-1][:t_len]
    we_r = rev_r[..., None] * bb_r[None] - rev_i[..., None] * bb_i[None]
    we_i = rev_r[..., None] * bb_i[None] + rev_i[..., None] * bb_r[None]
    wend = jnp.concatenate([we_r, we_i], axis=2).transpose(1, 0, 3, 2).reshape(n_g, t_len * n_h, -1)
    wo = jnp.concatenate([cp_r[1:], -cp_i[1:]], axis=3)
    wout = wo.transpose(1, 3, 0, 2).reshape(n_g, -1, t_len * n_h)
    a_r, a_i = pw_r[t_len], pw_i[t_len]
    a1 = jnp.concatenate([a_r, a_r], axis=1)
    a2 = jnp.concatenate([-a_i, a_i], axis=1)
    return toep, wend, wout, a1, a2


def _group_call(body, ins, outs, *, gb, name):
    n_g = ins[0].shape[0]

    def spec(a):
        return pl.BlockSpec((gb,) + tuple(a.shape[1:]), lambda i: (i, 0, 0))

    return pl.pallas_call(
        body, name=name, grid=(n_g // gb,), in_specs=[spec(a) for a in ins],
        out_specs=[spec(o) for o in outs], out_shape=list(outs),
        compiler_params=_params(("parallel",)),
    )(*ins)


def _s5_states(u_g, wend, *, gb=8):
    def body(u_ref, w_ref, s_ref):
        for g in range(gb):
            s_ref[g] = _dot(u_ref[g], w_ref[g])

    n_g, n_c = u_g.shape[0], u_g.shape[1]
    return _group_call(body, [u_g, wend], [_sds((n_g, n_c, wend.shape[2]))], gb=gb, name="s5_states")[0]


def _s5_outputs(u_g, toep, xprev, wout, *, gb=8):
    def body(u_ref, t_ref, x_ref, w_ref, y_ref):
        for g in range(gb):
            y_ref[g] = _dot(u_ref[g], t_ref[g]) + _dot(x_ref[g], w_ref[g])

    return _group_call(body, [u_g, toep, xprev, wout], [_sds(u_g.shape)], gb=gb, name="s5_outputs")[0]


def _s5_outputs_bwd(u_g, d_y, xprev, wout, *, gb=8):
    def body(u_ref, dy_ref, x_ref, w_ref, dt_ref, dw_ref, dx_ref):
        for g in range(gb):
            dy = dy_ref[g]
            dt_ref[g] = _dot(u_ref[g], dy, "tn")
            dw_ref[g] = _dot(x_ref[g], dy, "tn")
            dx_ref[g] = _dot(dy, w_ref[g], "nt")

    n_g, n_c, n_k = u_g.shape
    return _group_call(body, [u_g, d_y, xprev, wout],
                       [_sds((n_g, n_k, n_k)), _sds(wout.shape), _sds(xprev.shape)], gb=gb, name="s5_outputs_bwd")


def _s5_inputs_bwd(u_g, d_y, d_s, toep, wend, *, gb=8):
    def body(u_ref, dy_ref, ds_ref, t_ref, w_ref, du_ref, dw_ref):
        for g in range(gb):
            ds = ds_ref[g]
            du_ref[g] = _dot(dy_ref[g], t_ref[g], "nt") + _dot(ds, w_ref[g], "nt")
            dw_ref[g] = _dot(u_ref[g], ds, "tn")

    return _group_call(body, [u_g, d_y, d_s, toep, wend], [_sds(u_g.shape), _sds(wend.shape)], gb=gb,
                       name="s5_inputs_bwd")


def _swap_halves(x):
    return pltpu.roll(x, x.shape[-1] // 2, axis=x.ndim - 1)


def _s5_scan(s_t, a1, a2, *, gb=64):
    n_c, n_g, n_p = s_t.shape
    gb = min(gb, n_g)

    def body(s_ref, a1_ref, a2_ref, x_ref):
        a1v, a2v = a1_ref[...], a2_ref[...]
        a2s = _swap_halves(a2v)

        def step(c, carry):
            x, xs = carry
            x_ref[c] = x
            s = s_ref[c]
            return x * a1v + xs * a2v + s, xs * a1v + x * a2s + _swap_halves(s)

        zero = jnp.zeros((gb, n_p), F32)
        lax.fori_loop(0, n_c, step, (zero, zero), unroll=4 if n_c % 4 == 0 else 1)

    return pl.pallas_call(
        body, name="s5_scan", grid=(n_g // gb,),
        in_specs=[pl.BlockSpec((n_c, gb, n_p), lambda i: (0, i, 0)), pl.BlockSpec((gb, n_p), lambda i: (i, 0)),
                  pl.BlockSpec((gb, n_p), lambda i: (i, 0))],
        out_specs=pl.BlockSpec((n_c, gb, n_p), lambda i: (0, i, 0)), out_shape=_sds(s_t.shape),
        compiler_params=_params(("parallel",)),
    )(s_t, a1, a2)


def _s5_scan_bwd(d_xprev_t, xprev_t, a1, a2, *, gb=32):
    n_c, n_g, n_p = xprev_t.shape
    gb = min(gb, n_g)

    def body(dx_ref, x_ref, a1_ref, a2_ref, ds_ref, p1_ref, p2_ref):
        a1v, a2v = a1_ref[...], a2_ref[...]
        a2s = _swap_halves(a2v)
        zero = jnp.zeros((gb, n_p), F32)
        ds_ref[n_c - 1] = zero

        def step(k, carry):
            gx_next, gs_next, p1, p2 = carry
            c = n_c - 2 - k
            xp = x_ref[c + 1]
            d = dx_ref[c + 1]
            gx = d + gx_next * a1v - gs_next * a2v
            gs = _swap_halves(d) + gs_next * a1v - gx_next * a2s
            ds_ref[c] = gx
            return gx, gs, p1 + gx_next * xp, p2 + gx_next * _swap_halves(xp)

        _, _, p1, p2 = lax.fori_loop(0, n_c - 1, step, (zero, zero, zero, zero),
                                     unroll=5 if (n_c - 1) % 5 == 0 else 1)
        p1_ref[...] = p1
        p2_ref[...] = p2

    blk = pl.BlockSpec((n_c, gb, n_p), lambda i: (0, i, 0))
    vec = pl.BlockSpec((gb, n_p), lambda i: (i, 0))
    return pl.pallas_call(
        body, name="s5_scan_bwd", grid=(n_g // gb,), in_specs=[blk, blk, vec, vec],
        out_specs=[blk, vec, vec], out_shape=[_sds(xprev_t.shape), _sds(a1.shape), _sds(a1.shape)],
        compiler_params=_params(("parallel",)),
    )(d_xprev_t, xprev_t, a1, a2)


GROUPS_PER_TILE = LANES // S5_GROUP


def _to_groups(t, n_g):
    n_l = t.shape[0]
    n_c = n_l // S5_T
    gpt = min(GROUPS_PER_TILE, n_g)
    width = gpt * S5_GROUP

    def body(x_ref, o_ref):
        tr = [x_ref[pl.ds(s, n_c, stride=S5_T), :].T for s in range(S5_T)]
        for gl in range(gpt):
            rows = slice(gl * S5_GROUP, (gl + 1) * S5_GROUP)
            stacked = jnp.concatenate([tr[s][rows, :] for s in range(S5_T)], axis=0)
            o_ref[gl] = stacked.T.astype(o_ref.dtype)

    return pl.pallas_call(
        body, name="s5_to_groups", grid=(n_g // gpt,),
        in_specs=[pl.BlockSpec((n_l, width), lambda b: (0, b))],
        out_specs=pl.BlockSpec((gpt, n_c, S5_T * S5_GROUP), lambda b: (b, 0, 0)),
        out_shape=_sds((n_g, n_c, S5_T * S5_GROUP), BF16), compiler_params=_params(("parallel",)),
    )(t)


def _from_groups(t, n_l):
    n_g, n_c = t.shape[0], t.shape[1]
    gpt = min(GROUPS_PER_TILE, n_g)
    width = gpt * S5_GROUP

    def body(y_ref, o_ref):
        ytr = [y_ref[gl].T for gl in range(gpt)]
        for s in range(S5_T):
            rows = slice(s * S5_GROUP, (s + 1) * S5_GROUP)
            piece = jnp.concatenate([ytr[gl][rows, :] for gl in range(gpt)], axis=0)
            o_ref[pl.ds(s, n_c, stride=S5_T), :] = piece.T

    return pl.pallas_call(
        body, name="s5_from_groups", grid=(n_g // gpt,),
        in_specs=[pl.BlockSpec((gpt, n_c, S5_T * S5_GROUP), lambda b: (b, 0, 0))],
        out_specs=pl.BlockSpec((n_l, width), lambda b: (0, b)),
        out_shape=_sds((n_l, n_g * S5_GROUP)), compiler_params=_params(("parallel",)),
    )(t)


def _s5_act(ys, uz, d_skip):
    di = d_skip.shape[1]
    return jax.nn.gelu(ys + d_skip * uz[:, :di])


def _s5_gate(g1, glu_pre, uz, b_glu):
    di = b_glu.shape[1]
    return g1 * jax.nn.sigmoid(glu_pre + b_glu) * jax.nn.silu(uz[:, di:])


def _s5_fwd(h, p):
    n_l = h.shape[0]
    di = p["d_skip"].shape[1]
    n_g = di // S5_GROUP
    hn = _rms_fwd(h, p["norm_g"], name="s5_rms")
    uz = _mm(hn, p["w_in"], name="s5_in")
    toep, wend, wout, a1, a2 = p["ops"]
    u_g = _to_groups(uz, n_g)
    s = _s5_states(u_g, wend)
    xprev = _s5_scan(s.transpose(1, 0, 2), a1, a2).transpose(1, 0, 2)
    ys = _from_groups(_s5_outputs(u_g, toep, xprev, wout), n_l)
    g1 = _rowwise(_s5_act, [ys, uz], [p["d_skip"]], [_sds((n_l, di), BF16)], [], tm=512, name="s5_act")[0]
    glu_pre = _mm(g1, p["w_glu"], name="s5_glu")

    def gate(ys_t, pre_t, uz_t, d_skip, b_glu):
        return _s5_gate(_s5_act(ys_t, uz_t, d_skip), pre_t, uz_t, b_glu)

    gated = _rowwise(gate, [ys, glu_pre, uz], [p["d_skip"], p["b_glu"]], [_sds((n_l, di), BF16)], [],
                     tm=512, name="s5_gate")[0]
    h_next = _mm(gated, p["w_out"], add=h, name="s5_out")
    return h_next, (h, hn, uz, u_g, xprev, ys, g1, glu_pre, gated)


def _s5_bwd(dh_out, p, saved, ops_vjp):
    h, hn, uz, u_g, xprev, ys, g1, glu_pre, gated = saved
    n_l = h.shape[0]
    di = p["d_skip"].shape[1]
    n_g = di // S5_GROUP
    toep, wend, wout, a1, a2 = p["ops"]
    d_gated = _mm(dh_out, p["w_out"], mode="nt", out_dtype=BF16, name="s5_dgated")
    g_w_out = _mm(gated, dh_out, mode="tn", name="s5_dwout")

    def gate_bwd(ys_t, pre_t, uz_t, ct, d_skip, b_glu):
        g1_t = _s5_act(ys_t, uz_t, d_skip)
        _, vjp = jax.vjp(_s5_gate, g1_t, pre_t, uz_t, b_glu)
        d_g1, d_pre, d_uz, d_b = vjp(ct.astype(F32))
        return d_g1, d_pre, d_uz, d_b

    d_g1_direct, d_pre, d_uz_gate, g_b_glu = _rowwise(
        gate_bwd, [ys, glu_pre, uz, d_gated], [p["d_skip"], p["b_glu"]],
        [_sds((n_l, di)), _sds((n_l, di), BF16), _sds(uz.shape)], [_sds(p["b_glu"].shape)], tm=256, name="s5_gate_bwd")
    g_w_glu = _mm(g1, d_pre, mode="tn", name="s5_dwglu")
    d_g1 = _mm(d_pre, p["w_glu"], mode="nt", add=d_g1_direct, name="s5_dg1")

    def act_bwd(ys_t, uz_t, ct, d_uz_t, d_skip):
        _, vjp = jax.vjp(_s5_act, ys_t, uz_t, d_skip)
        d_ys, d_uz, d_d = vjp(ct)
        return d_ys, d_uz + d_uz_t, d_d

    d_ys, d_uz_part, g_d_skip = _rowwise(
        act_bwd, [ys, uz, d_g1, d_uz_gate], [p["d_skip"]], [_sds((n_l, di)), _sds(uz.shape)],
        [_sds(p["d_skip"].shape)], tm=256, name="s5_act_bwd")
    d_y = _to_groups(d_ys, n_g)
    d_toep, d_wout, d_xprev = _s5_outputs_bwd(u_g, d_y, xprev, wout)
    d_s_t, p1, p2 = _s5_scan_bwd(d_xprev.transpose(1, 0, 2), xprev.transpose(1, 0, 2), a1, a2)
    d_s = d_s_t.transpose(1, 0, 2)
    d_u_g, d_wend = _s5_inputs_bwd(u_g, d_y, d_s, toep, wend)
    d_u = _from_groups(d_u_g, n_l)
    d_uz = _rowwise(lambda part, du: jnp.concatenate([part[:, :di] + du, part[:, di:]], axis=1),
                    [d_uz_part, d_u], [], [_sds(uz.shape, BF16)], [], tm=512, name="s5_duz")[0]
    g_w_in = _mm(hn, d_uz, mode="tn", name="s5_dwin")
    d_hn = _mm(d_uz, p["w_in"], mode="nt", name="s5_dhn")
    dh_in, g_norm = _rms_bwd(h, p["norm_g"], d_hn, dh_out, name="s5_rms_bwd")
    g_ops = ops_vjp((d_toep, d_wend, d_wout, p1, p2))
    grads = dict(norm_g=g_norm, w_in=g_w_in, a_re=g_ops[0], a_im=g_ops[1], log_step=g_ops[2], b_re=g_ops[3],
                 b_im=g_ops[4], c_re=g_ops[5], c_im=g_ops[6], d_skip=g_d_skip, w_glu=g_w_glu, b_glu=g_b_glu,
                 w_out=g_w_out)
    return dh_in, grads


MLA_Z0 = MLA_Q_RANK + MLA_KV_RANK + LANES


def _rope_tile(t, cos_t, sin_t):
    q = LANES // 4
    lane = lax.broadcasted_iota(jnp.int32, t.shape, 1)
    swapped = jnp.where(lane < q, pltpu.roll(t, LANES - q, axis=1), pltpu.roll(t, q, axis=1))
    return t * cos_t + swapped * sin_t


def _mla_mid(proj, cos_t, sin_t, q_g, kv_g):
    cqn = _rms(proj[:, :MLA_Q_RANK], q_g)
    ckvn = _rms(proj[:, MLA_Q_RANK:MLA_Q_RANK + MLA_KV_RANK], kv_g)
    kr = _rope_tile(proj[:, MLA_Q_RANK + MLA_KV_RANK:MLA_Z0], cos_t, sin_t)
    return cqn, ckvn, kr


def _mla_rope_q(qp, cos_t, sin_t):
    parts = []
    for hd in range(qp.shape[1] // MLA_HEAD_PAD):
        base = hd * MLA_HEAD_PAD
        parts.append(qp[:, base:base + LANES])
        parts.append(_rope_tile(qp[:, base + LANES:base + MLA_HEAD_PAD], cos_t, sin_t))
    return jnp.concatenate(parts, axis=1)


def _mla_gate(o, proj):
    return o * jax.nn.silu(proj[:, MLA_Z0:])


LOG2E = math.log2(math.e)
SCORE_LOG2 = MLA_SCALE * LOG2E
FLASH_SPLIT = 4


def _causal_pairs(n_blk, kv_major):
    if kv_major:
        pairs = [(i, j) for j in range(n_blk) for i in range(j, n_blk)]
    else:
        pairs = [(i, j) for i in range(n_blk) for j in range(i + 1)]
    return (jnp.asarray([p[0] for p in pairs], jnp.int32), jnp.asarray([p[1] for p in pairs], jnp.int32))


def _raw_scores(q, kcat, row0, diagonal):
    s = _dot(q, kcat, "nt")
    if diagonal:
        qpos = row0 + lax.broadcasted_iota(jnp.int32, s.shape, 0)
        kpos = lax.broadcasted_iota(jnp.int32, s.shape, 1)
        s = jnp.where(kpos <= qpos, s, NEG_INF)
    return s


def _lanes(x, width):
    return jnp.tile(x, (1, width // LANES))


def _flash_fwd(qp, kv, kr, *, blk=1024):
    n_l = qp.shape[0]
    heads = qp.shape[1] // MLA_HEAD_PAD
    blk = _pick(n_l, blk)
    n_blk = n_l // blk
    half = blk // FLASH_SPLIT
    qi, kj = _causal_pairs(n_blk, kv_major=False)

    def body(qi_ref, kj_ref, q_ref, kv_ref, kr_ref, o_ref, lse_ref, m_sc, l_sc, acc_sc):
        p = pl.program_id(1)
        i, j = qi_ref[p], kj_ref[p]

        @pl.when(j == 0)
        def _():
            m_sc[...] = jnp.full_like(m_sc, NEG_INF)
            l_sc[...] = jnp.zeros_like(l_sc)
            acc_sc[...] = jnp.zeros_like(acc_sc)

        def update(diagonal):
            kcat = jnp.concatenate([kv_ref[:, :LANES], kr_ref[...]], axis=1)
            v = kv_ref[:, LANES:]
            for r in range(FLASH_SPLIT):
                rows = slice(r * half, (r + 1) * half)
                s = _raw_scores(q_ref[rows, :], kcat, r * half, diagonal)
                m_old = m_sc[rows, :]
                m_new = jnp.maximum(m_old, jnp.max(s, axis=1, keepdims=True))
                alpha = jnp.exp2((m_old - m_new) * SCORE_LOG2)
                pr = jnp.exp2((s - _lanes(m_new, blk)) * SCORE_LOG2)
                l_sc[rows, :] = alpha * l_sc[rows, :] + jnp.sum(pr, axis=1, keepdims=True)
                acc_sc[rows, :] = alpha * acc_sc[rows, :] + _dot(pr, v)
                m_sc[rows, :] = m_new

        @pl.when(j < i)
        def _():
            update(False)

        @pl.when(j == i)
        def _():
            update(True)
            o_ref[...] = acc_sc[...] / l_sc[...]
            lse_ref[...] = m_sc[...] * MLA_SCALE + jnp.log(l_sc[...])

    grid_spec = pltpu.PrefetchScalarGridSpec(
        num_scalar_prefetch=2, grid=(heads, qi.shape[0]),
        in_specs=[pl.BlockSpec((blk, MLA_HEAD_PAD), lambda h, p, qi_r, kj_r: (qi_r[p], h)),
                  pl.BlockSpec((blk, MLA_HEAD_PAD), lambda h, p, qi_r, kj_r: (kj_r[p], h)),
                  pl.BlockSpec((blk, LANES), lambda h, p, qi_r, kj_r: (kj_r[p], 0))],
        out_specs=[pl.BlockSpec((blk, MLA_V), lambda h, p, qi_r, kj_r: (qi_r[p], h)),
                   pl.BlockSpec((None, blk, LANES), lambda h, p, qi_r, kj_r: (h, qi_r[p], 0))],
        scratch_shapes=[pltpu.VMEM((blk, LANES), F32), pltpu.VMEM((blk, LANES), F32), pltpu.VMEM((blk, MLA_V), F32)])
    return pl.pallas_call(
        body, name="mla_flash_fwd", grid_spec=grid_spec,
        out_shape=[_sds((n_l, heads * MLA_V)), _sds((heads, n_l, LANES))],
        compiler_params=_params(("parallel", "arbitrary")),
    )(qi, kj, qp, kv, kr)


def _flash_bwd(qp, kv, kr, d_o, lse, delta, *, blk=1024):
    n_l = qp.shape[0]
    heads = qp.shape[1] // MLA_HEAD_PAD
    blk = _pick(n_l, blk)
    n_blk = n_l // blk
    half = blk // FLASH_SPLIT
    qi, kj = _causal_pairs(n_blk, kv_major=True)
    n_pairs = qi.shape[0]

    def body(qi_ref, kj_ref, q_ref, kv_ref, kr_ref, do_ref, lse_ref, dl_ref, dq_ref, dkv_ref, dkr_ref, dk_sc, dv_sc):
        h, p = pl.program_id(0), pl.program_id(1)
        i, j = qi_ref[p], kj_ref[p]

        @pl.when(p == 0)
        def _():
            dq_ref[...] = jnp.zeros_like(dq_ref)

        @pl.when(jnp.logical_and(p == 0, h == 0))
        def _():
            dkr_ref[...] = jnp.zeros_like(dkr_ref)

        @pl.when(i == j)
        def _():
            dk_sc[...] = jnp.zeros_like(dk_sc)
            dv_sc[...] = jnp.zeros_like(dv_sc)

        def update(diagonal):
            kcat = jnp.concatenate([kv_ref[:, :LANES], kr_ref[...]], axis=1)
            v = kv_ref[:, LANES:]
            for r in range(FLASH_SPLIT):
                rows = slice(r * half, (r + 1) * half)
                q_t, do_t = q_ref[rows, :], do_ref[rows, :]
                s = _raw_scores(q_t, kcat, r * half, diagonal)
                pr = jnp.exp2(s * SCORE_LOG2 - _lanes(lse_ref[rows, :] * LOG2E, blk))
                d_p = _dot(do_t, v, "nt")
                d_s = pr * (d_p - _lanes(dl_ref[rows, :], blk))
                dk_sc[...] += _dot(d_s, q_t, "tn")
                dv_sc[...] += _dot(pr, do_t, "tn")
                q_rows = pl.ds(pl.multiple_of(i * blk + r * half, half), half)
                dq_ref[q_rows, :] += _dot(d_s, kcat)

        @pl.when(i > j)
        def _():
            update(False)

        @pl.when(i == j)
        def _():
            update(True)

        @pl.when(i == n_blk - 1)
        def _():
            dk = dk_sc[...] * MLA_SCALE
            dkv_ref[:, :LANES] = dk[:, :LANES].astype(dkv_ref.dtype)
            dkv_ref[:, LANES:] = dv_sc[...].astype(dkv_ref.dtype)
            k_rows = pl.ds(pl.multiple_of(j * blk, blk), blk)
            dkr_ref[k_rows, :] += dk[:, LANES:]

        @pl.when(p == n_pairs - 1)
        def _():
            dq_ref[...] = dq_ref[...] * MLA_SCALE

    at_q = lambda h, p, qi_r, kj_r: (qi_r[p], h)
    at_kv = lambda h, p, qi_r, kj_r: (kj_r[p], h)
    grid_spec = pltpu.PrefetchScalarGridSpec(
        num_scalar_prefetch=2, grid=(heads, n_pairs),
        in_specs=[pl.BlockSpec((blk, MLA_HEAD_PAD), at_q),
                  pl.BlockSpec((blk, MLA_HEAD_PAD), at_kv),
                  pl.BlockSpec((blk, LANES), lambda h, p, qi_r, kj_r: (kj_r[p], 0)),
                  pl.BlockSpec((blk, MLA_V), at_q),
                  pl.BlockSpec((None, blk, LANES), lambda h, p, qi_r, kj_r: (h, qi_r[p], 0)),
                  pl.BlockSpec((blk, LANES), at_q)],
        out_specs=[pl.BlockSpec((n_l, MLA_HEAD_PAD), lambda h, p, qi_r, kj_r: (0, h)),
                   pl.BlockSpec((blk, MLA_HEAD_PAD), at_kv),
                   pl.BlockSpec((n_l, LANES), lambda h, p, qi_r, kj_r: (0, 0))],
        scratch_shapes=[pltpu.VMEM((blk, MLA_HEAD_PAD), F32), pltpu.VMEM((blk, MLA_V), F32)])
    return pl.pallas_call(
        body, name="mla_flash_bwd", grid_spec=grid_spec,
        out_shape=[_sds(qp.shape), _sds(kv.shape, BF16), _sds(kr.shape)],
        compiler_params=_params(("arbitrary", "arbitrary")),
    )(qi, kj, qp, kv, kr, d_o, lse, delta)


def _mla_fwd(h, p, rope):
    n_l = h.shape[0]
    cos_t, sin_t = rope
    hn = _rms_fwd(h, p["norm_g"], name="mla_rms")
    proj = _mm(hn, p["w_in"], name="mla_in", tn=896)
    cqn, ckvn, kr = _rowwise(_mla_mid, [proj, cos_t, sin_t], [p["q_norm_g"], p["kv_norm_g"]],
                             [_sds((n_l, MLA_Q_RANK), BF16), _sds((n_l, MLA_KV_RANK), BF16), _sds((n_l, LANES), BF16)],
                             [], tm=512, name="mla_mid")
    q_raw = _mm(cqn, p["w_uq"], name="mla_uq")
    qp = _rowwise(_mla_rope_q, [q_raw, cos_t, sin_t], [], [_sds(q_raw.shape, BF16)], [], tm=512, name="mla_rope_q")[0]
    kv = _mm(ckvn, p["w_ukv"], out_dtype=BF16, name="mla_ukv")
    o, lse = _flash_fwd(qp, kv, kr)
    gated = _rowwise(_mla_gate, [o, proj], [], [_sds(o.shape, BF16)], [], tm=512, name="mla_gate")[0]
    h_next = _mm(gated, p["w_out"], add=h, name="mla_out")
    return h_next, (h, hn, proj, cqn, ckvn, kr, qp, kv, o, lse, gated)


def _mla_bwd(dh_out, p, saved, rope):
    h, hn, proj, cqn, ckvn, kr, qp, kv, o, lse, gated = saved
    n_l = h.shape[0]
    cos_t, sin_t = rope
    d_gated = _mm(dh_out, p["w_out"], mode="nt", out_dtype=BF16, name="mla_dgated")
    g_w_out = _mm(gated, dh_out, mode="tn", name="mla_dwout")

    def gate_bwd(o_t, proj_t, ct):
        _, vjp = jax.vjp(lambda a, z: a * jax.nn.silu(z), o_t, proj_t[:, MLA_Z0:])
        d_o_t, d_z_t = vjp(ct.astype(F32))
        prod = d_o_t * o_t
        delta = jnp.concatenate(
            [jnp.broadcast_to(jnp.sum(prod[:, hd * MLA_V:(hd + 1) * MLA_V], axis=1, keepdims=True),
                              (prod.shape[0], MLA_V)) for hd in range(prod.shape[1] // MLA_V)], axis=1)
        return d_o_t, d_z_t, delta

    d_o, d_z, delta = _rowwise(gate_bwd, [o, proj, d_gated], [], [_sds(o.shape, BF16), _sds(o.shape), _sds(o.shape)],
                               [], tm=512, name="mla_gate_bwd")
    d_qp, d_kv, d_kr = _flash_bwd(qp, kv, kr, d_o, lse, delta)

    def rope_q_bwd(ct, c_t, s_t):
        return _mla_rope_q(ct, c_t, -s_t)

    d_q_raw = _rowwise(rope_q_bwd, [d_qp, cos_t, sin_t], [], [_sds(d_qp.shape, BF16)], [], tm=512,
                       name="mla_rope_q_bwd")[0]
    g_w_uq = _mm(cqn, d_q_raw, mode="tn", name="mla_dwuq")
    d_cqn = _mm(d_q_raw, p["w_uq"], mode="nt", name="mla_dcqn")
    g_w_ukv = _mm(ckvn, d_kv, mode="tn", name="mla_dwukv")
    d_ckvn = _mm(d_kv, p["w_ukv"], mode="nt", name="mla_dckvn")

    def mid_bwd(proj_t, c_t, s_t, d_cq, d_ckv, d_kr_t, d_z_t, q_g, kv_g):
        _, vjp_q = jax.vjp(_rms, proj_t[:, :MLA_Q_RANK], q_g)
        _, vjp_kv = jax.vjp(_rms, proj_t[:, MLA_Q_RANK:MLA_Q_RANK + MLA_KV_RANK], kv_g)
        d_q_in, d_qg = vjp_q(d_cq)
        d_kv_in, d_kvg = vjp_kv(d_ckv)
        d_kr_in = _rope_tile(d_kr_t, c_t, -s_t)
        return jnp.concatenate([d_q_in, d_kv_in, d_kr_in, d_z_t], axis=1), d_qg, d_kvg

    d_proj, g_q_norm, g_kv_norm = _rowwise(
        mid_bwd, [proj, cos_t, sin_t, d_cqn, d_ckvn, d_kr, d_z], [p["q_norm_g"], p["kv_norm_g"]],
        [_sds(proj.shape, BF16)], [_sds(p["q_norm_g"].shape), _sds(p["kv_norm_g"].shape)], tm=512, name="mla_mid_bwd")
    g_w_in = _mm(hn, d_proj, mode="tn", name="mla_dwin", tn=896)
    d_hn = _mm(d_proj, p["w_in"], mode="nt", name="mla_dhn", tk=896)
    dh_in, g_norm = _rms_bwd(h, p["norm_g"], d_hn, dh_out, name="mla_rms_bwd")
    grads = dict(norm_g=g_norm, w_in=g_w_in, q_norm_g=g_q_norm, w_uq=g_w_uq, kv_norm_g=g_kv_norm, w_ukv=g_w_ukv,
                 w_out=g_w_out)
    return dh_in, grads


def _loss_head(h, g, target):
    def fn(x, t, gg):
        def local(xx, g2):
            err = _rms(xx, g2) - t
            return 0.5 * jnp.sum(jnp.mean(err * err, axis=-1))

        val, (dx, dg) = jax.value_and_grad(local, argnums=(0, 1))(x, gg)
        return dx, jnp.full((1, LANES), val, F32), dg

    dh, loss, dg = _rowwise(fn, [h, target], [g], [_sds(h.shape)], [_sds((1, LANES)), _sds(g.shape)], tm=512,
                            name="loss_head")
    return loss[0, 0], dh, dg


HBM_SPEC = pl.BlockSpec(memory_space=pltpu.HBM)


AG_COPIES = 7


def _all_gather(shards, *, name):
    n_arr = len(shards)

    def body(*refs):
        x_refs, out_refs = refs[:n_arr], refs[n_arr:2 * n_arr]
        send_sems, recv_sems, local_sems = refs[2 * n_arr:]
        x, y, c = lax.axis_index("x"), lax.axis_index("y"), lax.axis_index("c")
        me, sibling = (x, y, c), (x, y, 1 - c)
        chips = [(1 - x, y), (x, 1 - y), (1 - x, 1 - y)]

        def block_of(t, px, py, pc):
            return out_refs[t].at[4 * px + 2 * py + pc]

        def copy(t, k, block, to, src=None):
            return pltpu.make_async_remote_copy(
                src_ref=block_of(t, *block) if src is None else src, dst_ref=block_of(t, *block),
                send_sem=send_sems.at[t * AG_COPIES + k], recv_sem=recv_sems.at[t * AG_COPIES + k],
                device_id=to, device_id_type=MESH)

        mine = [pltpu.make_async_copy(x_refs[t], block_of(t, *me), local_sems.at[t]) for t in range(n_arr)]
        for cp in mine:
            cp.start()
        first = []
        for t in range(n_arr):
            first.append(copy(t, 0, me, sibling, src=x_refs[t]))
            first += [copy(t, 1 + j, me, (*chip, c), src=x_refs[t]) for j, chip in enumerate(chips)]
        for cp in first:
            cp.start()
        passed = []
        for j, chip in enumerate(chips):
            for t in range(n_arr):
                copy(t, 1 + j, (*chip, c), me).wait_recv()
                passed.append(copy(t, 4 + j, (*chip, c), sibling))
                passed[-1].start()
        for t in range(n_arr):
            copy(t, 0, sibling, me).wait_recv()
        for j, chip in enumerate(chips):
            for t in range(n_arr):
                copy(t, 4 + j, (*chip, 1 - c), me).wait_recv()
        for cp in first + passed:
            cp.wait_send()
        for cp in mine:
            cp.wait()

    return pl.pallas_call(
        body, name=name, out_shape=[jax.ShapeDtypeStruct((N_DEV,) + s.shape, s.dtype) for s in shards],
        in_specs=[HBM_SPEC] * n_arr, out_specs=[HBM_SPEC] * n_arr,
        scratch_shapes=[pltpu.SemaphoreType.DMA((n_arr * AG_COPIES,)), pltpu.SemaphoreType.DMA((n_arr * AG_COPIES,)),
                        pltpu.SemaphoreType.DMA((n_arr,))],
    )(*shards)


def _exchange(src, routes, *, name):
    n_routes = len(routes)

    def body(s_ref, out_ref, send_sems, recv_sems):
        x, y, c = lax.axis_index("x"), lax.axis_index("y"), lax.axis_index("c")
        local, remote = [], []
        for k, (flip, block) in enumerate(routes):
            src_blk = s_ref.at[block(x, y, c)]
            if flip == 0:
                local.append(pltpu.make_async_copy(src_blk, out_ref.at[k], send_sems.at[k]))
            else:
                peer = (1 - x if flip & 4 else x, 1 - y if flip & 2 else y, 1 - c if flip & 1 else c)
                remote.append(pltpu.make_async_remote_copy(
                    src_ref=src_blk, dst_ref=out_ref.at[k], send_sem=send_sems.at[k], recv_sem=recv_sems.at[k],
                    device_id=peer, device_id_type=MESH))
        for cp in local + remote:
            cp.start()
        for cp in remote:
            cp.wait_recv()
        for cp in remote:
            cp.wait_send()
        for cp in local:
            cp.wait()

    return pl.pallas_call(
        body, name=name, out_shape=jax.ShapeDtypeStruct((n_routes,) + src.shape[1:], src.dtype),
        in_specs=[HBM_SPEC], out_specs=HBM_SPEC,
        scratch_shapes=[pltpu.SemaphoreType.DMA((n_routes,)), pltpu.SemaphoreType.DMA((n_routes,))],
    )(src)


def _reduce_scatter(send, *, name):
    def chip_block(k, other_core):
        return lambda x, y, c: (4 * (1 - x if k & 2 else x) + 2 * (1 - y if k & 1 else y)
                                + (1 - c if other_core else c))

    n_chips = 4
    pair = _exchange(send, [(1, chip_block(k, True)) for k in range(n_chips)], name=name + "_pair")
    rows, width = send.shape[1], send.shape[2]
    tr = _pick(rows, 256, 8)
    x, y, c = lax.axis_index("x"), lax.axis_index("y"), lax.axis_index("c")
    own_ids = jnp.stack([chip_block(k, False)(x, y, c) for k in range(n_chips)]).astype(jnp.int32)

    def add_body(ids_ref, *refs):
        own_refs, p_ref, o_ref = refs[:n_chips], refs[n_chips], refs[n_chips + 1]
        for k in range(n_chips):
            o_ref[k] = (own_refs[k][...].astype(F32) + p_ref[k].astype(F32)).astype(o_ref.dtype)

    own_spec = lambda k: pl.BlockSpec((None, tr, width), lambda i, ids: (ids[k], i, 0))
    chip_sums = pl.pallas_call(
        add_body, name=name + "_pair_sum",
        grid_spec=pltpu.PrefetchScalarGridSpec(
            num_scalar_prefetch=1, grid=(rows // tr,),
            in_specs=[own_spec(k) for k in range(n_chips)] + [pl.BlockSpec((n_chips, tr, width), lambda i, ids: (0, i, 0))],
            out_specs=pl.BlockSpec((n_chips, tr, width), lambda i, ids: (0, i, 0))),
        out_shape=jax.ShapeDtypeStruct((n_chips, rows, width), send.dtype), compiler_params=_params(("parallel",)),
    )(own_ids, *([send] * n_chips), pair)
    recv = _exchange(chip_sums, [(2 * k, (lambda kk: lambda x, y, c: kk)(k)) for k in range(1, n_chips)],
                     name=name + "_chips")

    def sum_body(q_ref, r_ref, o_ref):
        acc = q_ref[...].astype(F32)
        for k in range(n_chips - 1):
            acc = acc + r_ref[k].astype(F32)
        o_ref[...] = acc

    return pl.pallas_call(
        sum_body, name=name + "_sum", grid=(rows // tr,),
        in_specs=[pl.BlockSpec((None, tr, width), lambda i: (0, i, 0)),
                  pl.BlockSpec((n_chips - 1, tr, width), lambda i: (0, i, 0))],
        out_specs=pl.BlockSpec((tr, width), lambda i: (i, 0)), out_shape=_sds((rows, width)),
        compiler_params=_params(("parallel",)),
    )(chip_sums, recv)


def _adamw(w, g, m, v, *, name):
    def fn(ww, gg, mm, vv):
        m_new = ADAM_B1 * mm + (1.0 - ADAM_B1) * gg
        v_new = ADAM_B2 * vv + (1.0 - ADAM_B2) * jnp.square(gg)
        m_hat = m_new / (1.0 - ADAM_B1 ** ADAM_STEP)
        v_hat = v_new / (1.0 - ADAM_B2 ** ADAM_STEP)
        return -ADAM_LR * (m_hat / (jnp.sqrt(v_hat) + ADAM_EPS) + ADAM_WD * ww), m_new, v_new

    return _rowwise(fn, [w, g, m, v], [], [_sds(w.shape)] * 3, [], tm=256, name=name)


KINDS = ("gmlp", "s5", "mla", "gmlp")
LAYER_NAMES = {
    "gmlp": ("norm_g", "w_in", "ln_g", "ln_b", "w_s", "b_s", "w_out"),
    "s5": ("norm_g", "w_in", "a_re", "a_im", "log_step", "b_re", "b_im", "c_re", "c_im", "d_skip", "w_glu", "b_glu",
           "w_out"),
    "mla": ("norm_g", "w_in", "q_norm_g", "w_uq", "kv_norm_g", "w_ukv", "w_out"),
}
COL_SHARDED = ("w_in", "w_uq", "w_ukv")
ROW_SHARDED = ("w_out", "w_glu")
WEIGHT_NAMES = tuple("l%d_%s" % (i, n) for i, k in enumerate(KINDS) for n in LAYER_NAMES[k]) + ("final_norm_g",)


def _is_sharded(name):
    return name.split("_", 1)[1] in COL_SHARDED + ROW_SHARDED


def _flatten(arrs, pad_rows_to):
    parts, sizes = [], []
    for a in arrs:
        flat = a.reshape(-1)
        pad = (-flat.shape[0]) % FLAT_W
        if pad:
            flat = jnp.pad(flat, (0, pad))
        parts.append(flat)
        sizes.append(flat.shape[0] // FLAT_W)
    rows = sum(sizes)
    pad_rows = (-rows) % pad_rows_to
    if pad_rows:
        parts.append(jnp.zeros((pad_rows * FLAT_W,), arrs[0].dtype))
    return jnp.concatenate(parts).reshape(-1, FLAT_W), sizes


def _unflatten(flat, shapes, sizes):
    out, row = [], 0
    for shape, n_rows in zip(shapes, sizes):
        n = int(np.prod(shape))
        out.append(flat[row:row + n_rows].reshape(-1)[:n].reshape(shape))
        row += n_rows
    return out


def _full_from_gathered(blocks, name):
    if name.split("_", 1)[1] in COL_SHARDED:
        return blocks.transpose(1, 0, 2).reshape(blocks.shape[1], -1)
    return blocks.reshape(-1, blocks.shape[2])


def _shards_of(full, name):
    if name.split("_", 1)[1] in COL_SHARDED:
        return full.reshape(full.shape[0], N_DEV, -1).transpose(1, 0, 2)
    return full.reshape(N_DEV, -1, full.shape[1])


def _rope_tables(positions):
    inv_freq = ROPE_THETA ** (-jnp.arange(0, MLA_ROPE, 2, dtype=F32) / MLA_ROPE)
    ang = positions.astype(F32)[:, None] * inv_freq
    cos, sin = jnp.cos(ang), jnp.sin(ang)
    zero = jnp.zeros((positions.shape[0], LANES - MLA_ROPE), F32)
    return jnp.concatenate([cos, cos, zero], axis=1), jnp.concatenate([-sin, sin, zero], axis=1)


def _row(v):
    return v.reshape(1, -1)


def kernel(x, positions, l0_norm_g, l0_w_in, l0_ln_g, l0_ln_b, l0_w_s, l0_b_s, l0_w_out, l1_norm_g, l1_w_in, l1_a_re, l1_a_im, l1_log_step, l1_b_re, l1_b_im, l1_c_re, l1_c_im, l1_d_skip, l1_w_glu, l1_b_glu, l1_w_out, l2_norm_g, l2_w_in, l2_q_norm_g, l2_w_uq, l2_kv_norm_g, l2_w_ukv, l2_w_out, l3_norm_g, l3_w_in, l3_ln_g, l3_ln_b, l3_w_s, l3_b_s, l3_w_out, final_norm_g, loss_target, m_l0_norm_g, m_l0_w_in, m_l0_ln_g, m_l0_ln_b, m_l0_w_s, m_l0_b_s, m_l0_w_out, m_l1_norm_g, m_l1_w_in, m_l1_a_re, m_l1_a_im, m_l1_log_step, m_l1_b_re, m_l1_b_im, m_l1_c_re, m_l1_c_im, m_l1_d_skip, m_l1_w_glu, m_l1_b_glu, m_l1_w_out, m_l2_norm_g, m_l2_w_in, m_l2_q_norm_g, m_l2_w_uq, m_l2_kv_norm_g, m_l2_w_ukv, m_l2_w_out, m_l3_norm_g, m_l3_w_in, m_l3_ln_g, m_l3_ln_b, m_l3_w_s, m_l3_b_s, m_l3_w_out, m_final_norm_g, v_l0_norm_g, v_l0_w_in, v_l0_ln_g, v_l0_ln_b, v_l0_w_s, v_l0_b_s, v_l0_w_out, v_l1_norm_g, v_l1_w_in, v_l1_a_re, v_l1_a_im, v_l1_log_step, v_l1_b_re, v_l1_b_im, v_l1_c_re, v_l1_c_im, v_l1_d_skip, v_l1_w_glu, v_l1_b_glu, v_l1_w_out, v_l2_norm_g, v_l2_w_in, v_l2_q_norm_g, v_l2_w_uq, v_l2_kv_norm_g, v_l2_w_ukv, v_l2_w_out, v_l3_norm_g, v_l3_w_in, v_l3_ln_g, v_l3_ln_b, v_l3_w_s, v_l3_b_s, v_l3_w_out, v_final_norm_g):
    args = locals()
    weights = {n: args[n] for n in WEIGHT_NAMES}
    mom_m = {n: args["m_" + n] for n in WEIGHT_NAMES}
    mom_v = {n: args["v_" + n] for n in WEIGHT_NAMES}
    return _train_step(x, positions, loss_target, weights, mom_m, mom_v)


def _train_step(x, positions, loss_target, weights, mom_m, mom_v):
    big = [n for n in WEIGHT_NAMES if _is_sharded(n)]
    small = [n for n in WEIGHT_NAMES if not _is_sharded(n)]

    big_sizes = [weights[n].size // FLAT_W for n in big]
    gathered = _all_gather([weights[n].astype(BF16) for n in big], name="weights_all_gather")
    full = {n: _full_from_gathered(blocks, n) for n, blocks in zip(big, gathered)}

    layers, ops_vjps = [], {}
    for i, kind in enumerate(KINDS):
        pre = "l%d_" % i
        p = {n: (full[pre + n] if _is_sharded(pre + n) else weights[pre + n]) for n in LAYER_NAMES[kind]}
        p["norm_g"] = _row(p["norm_g"])
        if kind == "gmlp":
            p["ln_g"], p["ln_b"], p["b_st"] = _row(p["ln_g"]), _row(p["ln_b"]), p["b_s"].T
        elif kind == "s5":
            p["d_skip"], p["b_glu"] = _row(p["d_skip"]), _row(p["b_glu"])
            ops, ops_vjps[i] = jax.vjp(_s5_operators, *[p[n] for n in ("a_re", "a_im", "log_step", "b_re", "b_im",
                                                                       "c_re", "c_im")])
            p["ops"] = tuple(o.astype(BF16) for o in ops[:3]) + ops[3:]
        else:
            heads = p["w_uq"].shape[1] // MLA_QK_DIM
            w_in = p["w_in"]
            split = MLA_Q_RANK + MLA_KV_RANK + MLA_ROPE
            p["w_in"] = jnp.concatenate([w_in[:, :split], jnp.zeros((w_in.shape[0], LANES - MLA_ROPE), w_in.dtype),
                                         w_in[:, split:]], axis=1)
            p["w_uq"] = jnp.pad(p["w_uq"].reshape(-1, heads, MLA_QK_DIM),
                                ((0, 0), (0, 0), (0, MLA_HEAD_PAD - MLA_QK_DIM))).reshape(-1, heads * MLA_HEAD_PAD)
            p["q_norm_g"], p["kv_norm_g"] = _row(p["q_norm_g"]), _row(p["kv_norm_g"])
        layers.append(p)
    rope = _rope_tables(positions[0])

    h = x[0]
    saved = []
    for kind, p in zip(KINDS, layers):
        if kind == "gmlp":
            h, s = _gmlp_fwd(h, p)
        elif kind == "s5":
            h, s = _s5_fwd(h, p)
        else:
            h, s = _mla_fwd(h, p, rope)
        saved.append(s)
    loss_local, dh, g_final = _loss_head(h, _row(weights["final_norm_g"]), loss_target[0])
    loss = lax.psum(loss_local, ("x", "y", "c"))

    grads = {"final_norm_g": g_final.reshape(-1)}
    for i in reversed(range(len(KINDS))):
        kind, p = KINDS[i], layers[i]
        if kind == "gmlp":
            dh, g = _gmlp_bwd(dh, p, saved[i])
        elif kind == "s5":
            dh, g = _s5_bwd(dh, p, saved[i], ops_vjps[i])
        else:
            dh, g = _mla_bwd(dh, p, saved[i], rope)
            heads = weights["l%d_w_uq" % i].shape[1] * N_DEV // MLA_QK_DIM
            split = MLA_Q_RANK + MLA_KV_RANK + MLA_ROPE
            g["w_in"] = jnp.concatenate([g["w_in"][:, :split], g["w_in"][:, MLA_Z0:]], axis=1)
            g["w_uq"] = g["w_uq"].reshape(-1, heads, MLA_HEAD_PAD)[:, :, :MLA_QK_DIM].reshape(-1, heads * MLA_QK_DIM)
        for n, val in g.items():
            name = "l%d_%s" % (i, n)
            grads[name] = val.reshape(weights[name].shape) if not _is_sharded(name) else val

    small_flat, small_sizes = _flatten([grads[n] for n in small], 8 * N_DEV)
    small_rows = small_flat.shape[0] // N_DEV
    send_parts = [_shards_of(grads[n], n).reshape(N_DEV, -1, FLAT_W) for n in big]
    send_parts.append(small_flat.reshape(N_DEV, small_rows, FLAT_W))
    send = jnp.concatenate(send_parts, axis=1)
    big_rows = send.shape[1] - small_rows
    pad_rows = (-send.shape[1]) % 8
    if pad_rows:
        send = jnp.pad(send, ((0, 0), (0, pad_rows), (0, 0)))
    reduced = _reduce_scatter(send.astype(BF16), name="grads")
    g_big_flat = reduced[:big_rows]
    g_small_all = _all_gather([reduced[big_rows:big_rows + small_rows]], name="small_grads_all_gather")[0]
    g_small_flat = g_small_all.reshape(-1, FLAT_W)

    def flat_of(tree, names, pad_to):
        return _flatten([tree[n] for n in names], pad_to)[0]

    outs = {}
    big_shapes = [weights[n].shape for n in big]
    for n, g_n in zip(big, _unflatten(g_big_flat, big_shapes, big_sizes)):
        outs["grad_" + n] = g_n
        outs["delta_" + n], outs["new_m_" + n], outs["new_v_" + n] = _adamw(
            weights[n], g_n, mom_m[n], mom_v[n], name="adamw_" + n)
    d_s, nm_s, nv_s = _adamw(flat_of(weights, small, 8 * N_DEV), g_small_flat, flat_of(mom_m, small, 8 * N_DEV),
                             flat_of(mom_v, small, 8 * N_DEV), name="adamw_replicated")
    small_shapes = [weights[n].shape for n in small]
    for prefix, fs in (("grad_", g_small_flat), ("delta_", d_s), ("new_m_", nm_s), ("new_v_", nv_s)):
        for n, a in zip(small, _unflatten(fs, small_shapes, small_sizes)):
            outs[prefix + n] = a
    result = [loss, dh[None]]
    for prefix in ("grad_", "delta_", "new_m_", "new_v_"):
        result += [outs[prefix + n] for n in WEIGHT_NAMES]
    return tuple(result)
```

```python
import functools
import math

import numpy as np
import jax
import jax.numpy as jnp
from jax import lax
from jax.experimental import pallas as pl
from jax.experimental.pallas import tpu as pltpu

F32 = jnp.float32
BF16 = jnp.bfloat16

NORM_EPS = 1e-6
GMLP_CHUNK = 128
S5_GROUP = 16
S5_STATE = 64
S5_T = 16
MLA_NOPE = 128
MLA_ROPE = 64
MLA_V = 128
MLA_QK_DIM = MLA_NOPE + MLA_ROPE
MLA_Q_RANK = 384
MLA_KV_RANK = 128
MLA_HEAD_PAD = 256
MLA_SCALE = MLA_QK_DIM ** -0.5
ROPE_THETA = 10000.0
NEG_INF = -1e30
ADAM_LR = 0.001
ADAM_B1 = 0.9
ADAM_B2 = 0.999
ADAM_EPS = 1e-08
ADAM_WD = 0.01
ADAM_STEP = 10

N_DEV = 8
LANES = 128
FLAT_W = 1024
VMEM_LIMIT = 56 * 1024 * 1024
MESH = pl.DeviceIdType.MESH


def _pick(dim, pref, align=LANES):
    t = (min(pref, dim) // align) * align
    while t >= align:
        if dim % t == 0:
            return t
        t -= align
    return dim


def _params(sem=None):
    return pltpu.CompilerParams(dimension_semantics=sem, vmem_limit_bytes=VMEM_LIMIT)


_DIMS = {"nn": (((1,), (0,)), ((), ())), "nt": (((1,), (1,)), ((), ())), "tn": (((0,), (0,)), ((), ()))}


def _mm(a, b, *, mode="nn", out_dtype=F32, add=None, name, tm=1024, tn=1024, tk=2048):
    if mode == "nn":
        (m, k), (_, n) = a.shape, b.shape
    elif mode == "nt":
        (m, k), (n, _) = a.shape, b.shape
    else:
        (k, m), (_, n) = a.shape, b.shape
    tm, tn, tk = _pick(m, tm, 8), _pick(n, tn), _pick(k, tk)
    nk = k // tk
    dims = _DIMS[mode]

    def body(*refs):
        a_ref, b_ref = refs[:2]
        r_ref = refs[2] if add is not None else None
        o_ref = refs[3] if add is not None else refs[2]
        part = lax.dot_general(a_ref[...].astype(BF16), b_ref[...].astype(BF16), dims, preferred_element_type=F32)

        def finish(res):
            if add is not None:
                res = res + r_ref[...]
            o_ref[...] = res.astype(o_ref.dtype)

        if nk == 1:
            finish(part)
            return
        acc_ref = refs[-1]
        kk = pl.program_id(2)

        @pl.when(kk == 0)
        def _():
            acc_ref[...] = part

        @pl.when(kk > 0)
        def _():
            acc_ref[...] += part

        @pl.when(kk == nk - 1)
        def _():
            finish(acc_ref[...])

    a_spec = (pl.BlockSpec((tk, tm), lambda i, j, kk: (kk, i)) if mode == "tn"
              else pl.BlockSpec((tm, tk), lambda i, j, kk: (i, kk)))
    b_spec = (pl.BlockSpec((tn, tk), lambda i, j, kk: (j, kk)) if mode == "nt"
              else pl.BlockSpec((tk, tn), lambda i, j, kk: (kk, j)))
    in_specs = [a_spec, b_spec]
    args = [a, b]
    if add is not None:
        in_specs.append(pl.BlockSpec((tm, tn), lambda i, j, kk: (i, j)))
        args.append(add)
    return pl.pallas_call(
        body, name=name, grid=(m // tm, n // tn, nk),
        in_specs=in_specs, out_specs=pl.BlockSpec((tm, tn), lambda i, j, kk: (i, j)),
        out_shape=jax.ShapeDtypeStruct((m, n), out_dtype),
        scratch_shapes=[pltpu.VMEM((tm, tn), F32)] if nk > 1 else [],
        compiler_params=_params(("parallel", "parallel", "arbitrary")),
    )(*args)


def _rowwise(fn, tiled, full, out_tiled, out_acc, *, tm, name):
    rows = tiled[0].shape[0]
    tm = _pick(rows, tm, 8)
    nt, nf, no = len(tiled), len(full), len(out_tiled)

    def body(*refs):
        ins = [r[...] for r in refs[:nt + nf]]
        o_refs = refs[nt + nf:nt + nf + no]
        a_refs = refs[nt + nf + no:]
        outs = fn(*ins)
        if not isinstance(outs, (tuple, list)):
            outs = (outs,)
        for r, v in zip(o_refs, outs[:no]):
            r[...] = v.astype(r.dtype)
        if a_refs:
            @pl.when(pl.program_id(0) == 0)
            def _():
                for r in a_refs:
                    r[...] = jnp.zeros_like(r)

            for r, v in zip(a_refs, outs[no:]):
                r[...] += v.astype(r.dtype)

    def whole(shape):
        nd = len(shape)
        return pl.BlockSpec(tuple(shape), lambda i: (0,) * nd)

    in_specs = ([pl.BlockSpec((tm, t.shape[1]), lambda i: (i, 0)) for t in tiled]
                + [whole(f.shape) for f in full])
    out_specs = ([pl.BlockSpec((tm, o.shape[1]), lambda i: (i, 0)) for o in out_tiled]
                 + [whole(o.shape) for o in out_acc])
    outs = pl.pallas_call(
        body, name=name, grid=(rows // tm,), in_specs=in_specs, out_specs=out_specs,
        out_shape=list(out_tiled) + list(out_acc),
        compiler_params=_params(("arbitrary",)),
    )(*tiled, *full)
    return outs


def _sds(shape, dtype=F32):
    return jax.ShapeDtypeStruct(tuple(shape), dtype)


def _rms(x, g):
    return x * lax.rsqrt(jnp.mean(x * x, axis=-1, keepdims=True) + NORM_EPS) * g


def _layernorm(x, g, b):
    mu = jnp.mean(x, axis=-1, keepdims=True)
    xc = x - mu
    var = jnp.mean(xc * xc, axis=-1, keepdims=True)
    return xc * lax.rsqrt(var + NORM_EPS) * g + b


def _dot(a, b, mode="nn"):
    return lax.dot_general(a.astype(BF16), b.astype(BF16), _DIMS[mode], preferred_element_type=F32)


@jax.custom_vjp
def _bdot(a, b):
    return _dot(a, b)


def _bdot_fwd(a, b):
    return _bdot(a, b), (a, b)


def _bdot_bwd(res, ct):
    a, b = res
    return _dot(ct, b, "nt"), _dot(a, ct, "tn")


_bdot.defvjp(_bdot_fwd, _bdot_bwd)


def _rms_fwd(h, g, *, name):
    return _rowwise(lambda x, gg: _rms(x, gg), [h], [g], [_sds(h.shape, BF16)], [], tm=512, name=name)[0]


def _rms_bwd(h, g, d_hn, dh_out, *, name):
    def fn(x, ct, res, gg):
        _, vjp = jax.vjp(_rms, x, gg)
        dx, dg = vjp(ct)
        return res + dx, dg

    dh, dg = _rowwise(fn, [h, d_hn, dh_out], [g], [_sds(h.shape)], [_sds(g.shape)], tm=512, name=name)
    return dh, dg


def _gmlp_mid(uvz, ln_g, ln_b, w_s, b_st):
    di = ln_g.shape[1]
    ng, ck = w_s.shape[0], w_s.shape[1]
    dg = di // ng
    u = jax.nn.gelu(uvz[:, :di])
    v = _layernorm(jax.nn.gelu(uvz[:, di:2 * di]), ln_g, ln_b)
    z = uvz[:, 2 * di:]
    row = lax.broadcasted_iota(jnp.int32, (ck, ck), 0)
    col = lax.broadcasted_iota(jnp.int32, (ck, ck), 1)
    causal = col <= row
    blocks = []
    for c in range(uvz.shape[0] // ck):
        cols = []
        for g in range(ng):
            w = jnp.where(causal, w_s[g], 0.0)
            cols.append(_bdot(w, v[c * ck:(c + 1) * ck, g * dg:(g + 1) * dg]) + b_st[:, g:g + 1])
        blocks.append(jnp.concatenate(cols, axis=1))
    s = blocks[0] if len(blocks) == 1 else jnp.concatenate(blocks, axis=0)
    return u * s * jax.nn.silu(z)


def _gmlp_fwd(h, p):
    hn = _rms_fwd(h, p["norm_g"], name="gmlp_rms")
    uvz = _mm(hn, p["w_in"], name="gmlp_in")
    di = p["ln_g"].shape[1]
    gated = _rowwise(_gmlp_mid, [uvz], [p["ln_g"], p["ln_b"], p["w_s"], p["b_st"]],
                     [_sds((h.shape[0], di), BF16)], [], tm=256, name="gmlp_mid")[0]
    h_next = _mm(gated, p["w_out"], add=h, name="gmlp_out")
    return h_next, (h, hn, uvz, gated)


def _gmlp_bwd(dh_out, p, saved):
    h, hn, uvz, gated = saved
    d_gated = _mm(dh_out, p["w_out"], mode="nt", out_dtype=BF16, name="gmlp_dgated")
    g_w_out = _mm(gated, dh_out, mode="tn", name="gmlp_dwout")

    def fn(t, ct, ln_g, ln_b, w_s, b_st):
        _, vjp = jax.vjp(_gmlp_mid, t, ln_g, ln_b, w_s, b_st)
        return vjp(ct.astype(F32))

    d_uvz, g_ln_g, g_ln_b, g_w_s, g_b_st = _rowwise(
        fn, [uvz, d_gated], [p["ln_g"], p["ln_b"], p["w_s"], p["b_st"]],
        [_sds(uvz.shape, BF16)], [_sds(p["ln_g"].shape), _sds(p["ln_b"].shape), _sds(p["w_s"].shape),
                                  _sds(p["b_st"].shape)], tm=256, name="gmlp_mid_bwd")
    g_w_in = _mm(hn, d_uvz, mode="tn", name="gmlp_dwin")
    d_hn = _mm(d_uvz, p["w_in"], mode="nt", name="gmlp_dhn")
    dh_in, g_norm = _rms_bwd(h, p["norm_g"], d_hn, dh_out, name="gmlp_rms_bwd")
    grads = dict(norm_g=g_norm, w_in=g_w_in, ln_g=g_ln_g, ln_b=g_ln_b, w_s=g_w_s, b_s=g_b_st.T, w_out=g_w_out)
    return dh_in, grads


def _s5_operators(a_re, a_im, log_step, b_re, b_im, c_re, c_im):
    t_len = S5_T
    step = jnp.exp(log_step)[:, None]
    lr, li = a_re * step, a_im * step
    ks = jnp.arange(t_len + 1, dtype=F32)[:, None, None]
    mag = jnp.exp(ks * lr)
    pw_r, pw_i = mag * jnp.cos(ks * li), mag * jnp.sin(ks * li)
    nr, ni = pw_r[1] - 1.0, pw_i[1]
    den = a_re * a_re + a_im * a_im
    f_r, f_i = (nr * a_re + ni * a_im) / den, (ni * a_re - nr * a_im) / den
    bb_r = f_r[..., None] * b_re - f_i[..., None] * b_im
    bb_i = f_r[..., None] * b_im + f_i[..., None] * b_re
    hi = lax.Precision.HIGHEST
    cp_r = c_re[None] * pw_r[:, :, None, :] - c_im[None] * pw_i[:, :, None, :]
    cp_i = c_re[None] * pw_i[:, :, None, :] + c_im[None] * pw_r[:, :, None, :]
    n_g, n_h = a_re.shape[0], b_re.shape[2]
    lhs = jnp.concatenate([bb_r, -bb_i], axis=1)
    rhs = jnp.concatenate([cp_r[:t_len], cp_i[:t_len]], axis=3)
    rhs = rhs.transpose(1, 3, 0, 2).reshape(n_g, -1, t_len * n_h)
    kcat = jnp.einsum("gqi,gqn->gin", lhs, rhs, precision=hi)
    toep = jnp.stack([jnp.pad(kcat[:, :, :(t_len - s) * n_h], ((0, 0), (0, 0), (s * n_h, 0)))
                      for s in range(t_len)], axis=1).reshape(n_g, t_len * n_h, t_len * n_h)
    rev_r, rev_i = pw_r[t_len - 1::-1][:t_len], pw_i[t_len - 1::-1][:t_len]
    we_r = rev_r[..., None] * bb_r[None] - rev_i[..., None] * bb_i[None]
    we_i = rev_r[..., None] * bb_i[None] + rev_i[..., None] * bb_r[None]
    wend = jnp.concatenate([we_r, we_i], axis=2).transpose(1, 0, 3, 2).reshape(n_g, t_len * n_h, -1)
    wo = jnp.concatenate([cp_r[1:], -cp_i[1:]], axis=3)
    wout = wo.transpose(1, 3, 0, 2).reshape(n_g, -1, t_len * n_h)
    a_r, a_i = pw_r[t_len], pw_i[t_len]
    a1 = jnp.concatenate([a_r, a_r], axis=1)
    a2 = jnp.concatenate([-a_i, a_i], axis=1)
    return toep, wend, wout, a1, a2


def _group_call(body, ins, outs, *, gb, name):
    n_g = ins[0].shape[0]

    def spec(a):
        return pl.BlockSpec((gb,) + tuple(a.shape[1:]), lambda i: (i, 0, 0))

    return pl.pallas_call(
        body, name=name, grid=(n_g // gb,), in_specs=[spec(a) for a in ins],
        out_specs=[spec(o) for o in outs], out_shape=list(outs),
        compiler_params=_params(("parallel",)),
    )(*ins)


def _s5_states(u_g, wend, *, gb=8):
    def body(u_ref, w_ref, s_ref):
        for g in range(gb):
            s_ref[g] = _dot(u_ref[g], w_ref[g])

    n_g, n_c = u_g.shape[0], u_g.shape[1]
    return _group_call(body, [u_g, wend], [_sds((n_g, n_c, wend.shape[2]))], gb=gb, name="s5_states")[0]


def _s5_outputs(u_g, toep, xprev, wout, *, gb=8):
    def body(u_ref, t_ref, x_ref, w_ref, y_ref):
        for g in range(gb):
            y_ref[g] = _dot(u_ref[g], t_ref[g]) + _dot(x_ref[g], w_ref[g])

    return _group_call(body, [u_g, toep, xprev, wout], [_sds(u_g.shape)], gb=gb, name="s5_outputs")[0]


def _s5_outputs_bwd(u_g, d_y, xprev, wout, *, gb=8):
    def body(u_ref, dy_ref, x_ref, w_ref, dt_ref, dw_ref, dx_ref):
        for g in range(gb):
            dy = dy_ref[g]
            dt_ref[g] = _dot(u_ref[g], dy, "tn")
            dw_ref[g] = _dot(x_ref[g], dy, "tn")
            dx_ref[g] = _dot(dy, w_ref[g], "nt")

    n_g, n_c, n_k = u_g.shape
    return _group_call(body, [u_g, d_y, xprev, wout],
                       [_sds((n_g, n_k, n_k)), _sds(wout.shape), _sds(xprev.shape)], gb=gb, name="s5_outputs_bwd")


def _s5_inputs_bwd(u_g, d_y, d_s, toep, wend, *, gb=8):
    def body(u_ref, dy_ref, ds_ref, t_ref, w_ref, du_ref, dw_ref):
        for g in range(gb):
            ds = ds_ref[g]
            du_ref[g] = _dot(dy_ref[g], t_ref[g], "nt") + _dot(ds, w_ref[g], "nt")
            dw_ref[g] = _dot(u_ref[g], ds, "tn")

    return _group_call(body, [u_g, d_y, d_s, toep, wend], [_sds(u_g.shape), _sds(wend.shape)], gb=gb,
                       name="s5_inputs_bwd")


def _swap_halves(x):
    return pltpu.roll(x, x.shape[-1] // 2, axis=x.ndim - 1)


def _s5_scan(s_t, a1, a2, *, gb=64):
    n_c, n_g, n_p = s_t.shape
    gb = min(gb, n_g)

    def body(s_ref, a1_ref, a2_ref, x_ref):
        a1v, a2v = a1_ref[...], a2_ref[...]
        a2s = _swap_halves(a2v)

        def step(c, carry):
            x, xs = carry
            x_ref[c] = x
            s = s_ref[c]
            return x * a1v + xs * a2v + s, xs * a1v + x * a2s + _swap_halves(s)

        zero = jnp.zeros((gb, n_p), F32)
        lax.fori_loop(0, n_c, step, (zero, zero), unroll=4 if n_c % 4 == 0 else 1)

    return pl.pallas_call(
        body, name="s5_scan", grid=(n_g // gb,),
        in_specs=[pl.BlockSpec((n_c, gb, n_p), lambda i: (0, i, 0)), pl.BlockSpec((gb, n_p), lambda i: (i, 0)),
                  pl.BlockSpec((gb, n_p), lambda i: (i, 0))],
        out_specs=pl.BlockSpec((n_c, gb, n_p), lambda i: (0, i, 0)), out_shape=_sds(s_t.shape),
        compiler_params=_params(("parallel",)),
    )(s_t, a1, a2)


def _s5_scan_bwd(d_xprev_t, xprev_t, a1, a2, *, gb=64):
    n_c, n_g, n_p = xprev_t.shape
    gb = min(gb, n_g)

    def body(dx_ref, x_ref, a1_ref, a2_ref, ds_ref, p1_ref, p2_ref):
        a1v, a2v = a1_ref[...], a2_ref[...]
        a2s = _swap_halves(a2v)
        zero = jnp.zeros((gb, n_p), F32)
        ds_ref[n_c - 1] = zero

        def step(k, carry):
            gx_next, gs_next, p1, p2 = carry
            c = n_c - 2 - k
            xp = x_ref[c + 1]
            d = dx_ref[c + 1]
            gx = d + gx_next * a1v - gs_next * a2v
            gs = _swap_halves(d) + gs_next * a1v - gx_next * a2s
            ds_ref[c] = gx
            return gx, gs, p1 + gx_next * xp, p2 + gx_next * _swap_halves(xp)

        _, _, p1, p2 = lax.fori_loop(0, n_c - 1, step, (zero, zero, zero, zero),
                                     unroll=5 if (n_c - 1) % 5 == 0 else 1)
        p1_ref[...] = p1
        p2_ref[...] = p2

    blk = pl.BlockSpec((n_c, gb, n_p), lambda i: (0, i, 0))
    vec = pl.BlockSpec((gb, n_p), lambda i: (i, 0))
    return pl.pallas_call(
        body, name="s5_scan_bwd", grid=(n_g // gb,), in_specs=[blk, blk, vec, vec],
        out_specs=[blk, vec, vec], out_shape=[_sds(xprev_t.shape), _sds(a1.shape), _sds(a1.shape)],
        compiler_params=_params(("parallel",)),
    )(d_xprev_t, xprev_t, a1, a2)


GROUPS_PER_TILE = LANES // S5_GROUP


def _to_groups(t, n_g):
    n_l = t.shape[0]
    n_c = n_l // S5_T
    gpt = min(GROUPS_PER_TILE, n_g)
    width = gpt * S5_GROUP

    def body(x_ref, o_ref):
        tr = [x_ref[pl.ds(s, n_c, stride=S5_T), :].T for s in range(S5_T)]
        for gl in range(gpt):
            rows = slice(gl * S5_GROUP, (gl + 1) * S5_GROUP)
            stacked = jnp.concatenate([tr[s][rows, :] for s in range(S5_T)], axis=0)
            o_ref[gl] = stacked.T.astype(o_ref.dtype)

    return pl.pallas_call(
        body, name="s5_to_groups", grid=(n_g // gpt,),
        in_specs=[pl.BlockSpec((n_l, width), lambda b: (0, b))],
        out_specs=pl.BlockSpec((gpt, n_c, S5_T * S5_GROUP), lambda b: (b, 0, 0)),
        out_shape=_sds((n_g, n_c, S5_T * S5_GROUP), BF16), compiler_params=_params(("parallel",)),
    )(t)


def _from_groups(t, n_l):
    n_g, n_c = t.shape[0], t.shape[1]
    gpt = min(GROUPS_PER_TILE, n_g)
    width = gpt * S5_GROUP

    def body(y_ref, o_ref):
        ytr = [y_ref[gl].T for gl in range(gpt)]
        for s in range(S5_T):
            rows = slice(s * S5_GROUP, (s + 1) * S5_GROUP)
            piece = jnp.concatenate([ytr[gl][rows, :] for gl in range(gpt)], axis=0)
            o_ref[pl.ds(s, n_c, stride=S5_T), :] = piece.T

    return pl.pallas_call(
        body, name="s5_from_groups", grid=(n_g // gpt,),
        in_specs=[pl.BlockSpec((gpt, n_c, S5_T * S5_GROUP), lambda b: (b, 0, 0))],
        out_specs=pl.BlockSpec((n_l, width), lambda b: (0, b)),
        out_shape=_sds((n_l, n_g * S5_GROUP)), compiler_params=_params(("parallel",)),
    )(t)


def _s5_act(ys, uz, d_skip):
    di = d_skip.shape[1]
    return jax.nn.gelu(ys + d_skip * uz[:, :di])


def _s5_gate(g1, glu_pre, uz, b_glu):
    di = b_glu.shape[1]
    return g1 * jax.nn.sigmoid(glu_pre + b_glu) * jax.nn.silu(uz[:, di:])


def _s5_fwd(h, p):
    n_l = h.shape[0]
    di = p["d_skip"].shape[1]
    n_g = di // S5_GROUP
    hn = _rms_fwd(h, p["norm_g"], name="s5_rms")
    uz = _mm(hn, p["w_in"], name="s5_in")
    toep, wend, wout, a1, a2 = p["ops"]
    u_g = _to_groups(uz, n_g)
    s = _s5_states(u_g, wend)
    xprev = _s5_scan(s.transpose(1, 0, 2), a1, a2).transpose(1, 0, 2)
    ys = _from_groups(_s5_outputs(u_g, toep, xprev, wout), n_l)
    g1 = _rowwise(_s5_act, [ys, uz], [p["d_skip"]], [_sds((n_l, di), BF16)], [], tm=512, name="s5_act")[0]
    glu_pre = _mm(g1, p["w_glu"], name="s5_glu")

    def gate(ys_t, pre_t, uz_t, d_skip, b_glu):
        return _s5_gate(_s5_act(ys_t, uz_t, d_skip), pre_t, uz_t, b_glu)

    gated = _rowwise(gate, [ys, glu_pre, uz], [p["d_skip"], p["b_glu"]], [_sds((n_l, di), BF16)], [],
                     tm=512, name="s5_gate")[0]
    h_next = _mm(gated, p["w_out"], add=h, name="s5_out")
    return h_next, (h, hn, uz, u_g, xprev, ys, g1, glu_pre, gated)


def _s5_bwd(dh_out, p, saved, ops_vjp):
    h, hn, uz, u_g, xprev, ys, g1, glu_pre, gated = saved
    n_l = h.shape[0]
    di = p["d_skip"].shape[1]
    n_g = di // S5_GROUP
    toep, wend, wout, a1, a2 = p["ops"]
    d_gated = _mm(dh_out, p["w_out"], mode="nt", out_dtype=BF16, name="s5_dgated")
    g_w_out = _mm(gated, dh_out, mode="tn", name="s5_dwout")

    def gate_bwd(ys_t, pre_t, uz_t, ct, d_skip, b_glu):
        g1_t = _s5_act(ys_t, uz_t, d_skip)
        _, vjp = jax.vjp(_s5_gate, g1_t, pre_t, uz_t, b_glu)
        d_g1, d_pre, d_uz, d_b = vjp(ct.astype(F32))
        return d_g1, d_pre, d_uz, d_b

    d_g1_direct, d_pre, d_uz_gate, g_b_glu = _rowwise(
        gate_bwd, [ys, glu_pre, uz, d_gated], [p["d_skip"], p["b_glu"]],
        [_sds((n_l, di)), _sds((n_l, di), BF16), _sds(uz.shape)], [_sds(p["b_glu"].shape)], tm=256, name="s5_gate_bwd")
    g_w_glu = _mm(g1, d_pre, mode="tn", name="s5_dwglu")
    d_g1 = _mm(d_pre, p["w_glu"], mode="nt", add=d_g1_direct, name="s5_dg1")

    def act_bwd(ys_t, uz_t, ct, d_uz_t, d_skip):
        _, vjp = jax.vjp(_s5_act, ys_t, uz_t, d_skip)
        d_ys, d_uz, d_d = vjp(ct)
        return d_ys, d_uz + d_uz_t, d_d

    d_ys, d_uz_part, g_d_skip = _rowwise(
        act_bwd, [ys, uz, d_g1, d_uz_gate], [p["d_skip"]], [_sds((n_l, di)), _sds(uz.shape)],
        [_sds(p["d_skip"].shape)], tm=256, name="s5_act_bwd")
    d_y = _to_groups(d_ys, n_g)
    d_toep, d_wout, d_xprev = _s5_outputs_bwd(u_g, d_y, xprev, wout)
    d_s_t, p1, p2 = _s5_scan_bwd(d_xprev.transpose(1, 0, 2), xprev.transpose(1, 0, 2), a1, a2)
    d_s = d_s_t.transpose(1, 0, 2)
    d_u_g, d_wend = _s5_inputs_bwd(u_g, d_y, d_s, toep, wend)
    d_u = _from_groups(d_u_g, n_l)
    d_uz = _rowwise(lambda part, du: jnp.concatenate([part[:, :di] + du, part[:, di:]], axis=1),
                    [d_uz_part, d_u], [], [_sds(uz.shape, BF16)], [], tm=512, name="s5_duz")[0]
    g_w_in = _mm(hn, d_uz, mode="tn", name="s5_dwin")
    d_hn = _mm(d_uz, p["w_in"], mode="nt", name="s5_dhn")
    dh_in, g_norm = _rms_bwd(h, p["norm_g"], d_hn, dh_out, name="s5_rms_bwd")
    g_ops = ops_vjp((d_toep, d_wend, d_wout, p1, p2))
    grads = dict(norm_g=g_norm, w_in=g_w_in, a_re=g_ops[0], a_im=g_ops[1], log_step=g_ops[2], b_re=g_ops[3],
                 b_im=g_ops[4], c_re=g_ops[5], c_im=g_ops[6], d_skip=g_d_skip, w_glu=g_w_glu, b_glu=g_b_glu,
                 w_out=g_w_out)
    return dh_in, grads


MLA_Z0 = MLA_Q_RANK + MLA_KV_RANK + LANES


def _rope_tile(t, cos_t, sin_t):
    q = LANES // 4
    lane = lax.broadcasted_iota(jnp.int32, t.shape, 1)
    swapped = jnp.where(lane < q, pltpu.roll(t, LANES - q, axis=1), pltpu.roll(t, q, axis=1))
    return t * cos_t + swapped * sin_t


def _mla_mid(proj, cos_t, sin_t, q_g, kv_g):
    cqn = _rms(proj[:, :MLA_Q_RANK], q_g)
    ckvn = _rms(proj[:, MLA_Q_RANK:MLA_Q_RANK + MLA_KV_RANK], kv_g)
    kr = _rope_tile(proj[:, MLA_Q_RANK + MLA_KV_RANK:MLA_Z0], cos_t, sin_t)
    return cqn, ckvn, kr


def _mla_rope_q(qp, cos_t, sin_t):
    parts = []
    for hd in range(qp.shape[1] // MLA_HEAD_PAD):
        base = hd * MLA_HEAD_PAD
        parts.append(qp[:, base:base + LANES])
        parts.append(_rope_tile(qp[:, base + LANES:base + MLA_HEAD_PAD], cos_t, sin_t))
    return jnp.concatenate(parts, axis=1)


def _mla_gate(o, proj):
    return o * jax.nn.silu(proj[:, MLA_Z0:])


LOG2E = math.log2(math.e)
SCORE_LOG2 = MLA_SCALE * LOG2E
FLASH_SPLIT = 4


def _causal_pairs(n_blk, kv_major):
    if kv_major:
        pairs = [(i, j) for j in range(n_blk) for i in range(j, n_blk)]
    else:
        pairs = [(i, j) for i in range(n_blk) for j in range(i + 1)]
    return (jnp.asarray([p[0] for p in pairs], jnp.int32), jnp.asarray([p[1] for p in pairs], jnp.int32))


def _raw_scores(q, kcat, row0, diagonal):
    s = _dot(q, kcat, "nt")
    if diagonal:
        qpos = row0 + lax.broadcasted_iota(jnp.int32, s.shape, 0)
        kpos = lax.broadcasted_iota(jnp.int32, s.shape, 1)
        s = jnp.where(kpos <= qpos, s, NEG_INF)
    return s


def _lanes(x, width):
    return jnp.tile(x, (1, width // LANES))


def _flash_fwd(qp, kv, kr, *, blk=1024):
    n_l = qp.shape[0]
    heads = qp.shape[1] // MLA_HEAD_PAD
    blk = _pick(n_l, blk)
    n_blk = n_l // blk
    half = blk // FLASH_SPLIT
    qi, kj = _causal_pairs(n_blk, kv_major=False)

    def body(qi_ref, kj_ref, q_ref, kv_ref, kr_ref, o_ref, lse_ref, m_sc, l_sc, acc_sc):
        p = pl.program_id(1)
        i, j = qi_ref[p], kj_ref[p]

        @pl.when(j == 0)
        def _():
            m_sc[...] = jnp.full_like(m_sc, NEG_INF)
            l_sc[...] = jnp.zeros_like(l_sc)
            acc_sc[...] = jnp.zeros_like(acc_sc)

        def update(diagonal):
            kcat = jnp.concatenate([kv_ref[:, :LANES], kr_ref[...]], axis=1)
            v = kv_ref[:, LANES:]
            for r in range(FLASH_SPLIT):
                rows = slice(r * half, (r + 1) * half)
                s = _raw_scores(q_ref[rows, :], kcat, r * half, diagonal)
                m_old = m_sc[rows, :]
                m_new = jnp.maximum(m_old, jnp.max(s, axis=1, keepdims=True))
                alpha = jnp.exp2((m_old - m_new) * SCORE_LOG2)
                pr = jnp.exp2((s - _lanes(m_new, blk)) * SCORE_LOG2)
                l_sc[rows, :] = alpha * l_sc[rows, :] + jnp.sum(pr, axis=1, keepdims=True)
                acc_sc[rows, :] = alpha * acc_sc[rows, :] + _dot(pr, v)
                m_sc[rows, :] = m_new

        @pl.when(j < i)
        def _():
            update(False)

        @pl.when(j == i)
        def _():
            update(True)
            o_ref[...] = acc_sc[...] / l_sc[...]
            lse_ref[...] = m_sc[...] * MLA_SCALE + jnp.log(l_sc[...])

    grid_spec = pltpu.PrefetchScalarGridSpec(
        num_scalar_prefetch=2, grid=(heads, qi.shape[0]),
        in_specs=[pl.BlockSpec((blk, MLA_HEAD_PAD), lambda h, p, qi_r, kj_r: (qi_r[p], h)),
                  pl.BlockSpec((blk, MLA_HEAD_PAD), lambda h, p, qi_r, kj_r: (kj_r[p], h)),
                  pl.BlockSpec((blk, LANES), lambda h, p, qi_r, kj_r: (kj_r[p], 0))],
        out_specs=[pl.BlockSpec((blk, MLA_V), lambda h, p, qi_r, kj_r: (qi_r[p], h)),
                   pl.BlockSpec((None, blk, LANES), lambda h, p, qi_r, kj_r: (h, qi_r[p], 0))],
        scratch_shapes=[pltpu.VMEM((blk, LANES), F32), pltpu.VMEM((blk, LANES), F32), pltpu.VMEM((blk, MLA_V), F32)])
    return pl.pallas_call(
        body, name="mla_flash_fwd", grid_spec=grid_spec,
        out_shape=[_sds((n_l, heads * MLA_V)), _sds((heads, n_l, LANES))],
        compiler_params=_params(("parallel", "arbitrary")),
    )(qi, kj, qp, kv, kr)


def _flash_bwd(qp, kv, kr, d_o, lse, delta, *, blk=1024):
    n_l = qp.shape[0]
    heads = qp.shape[1] // MLA_HEAD_PAD
    blk = _pick(n_l, blk)
    n_blk = n_l // blk
    half = blk // FLASH_SPLIT
    qi, kj = _causal_pairs(n_blk, kv_major=True)
    n_pairs = qi.shape[0]

    def body(qi_ref, kj_ref, q_ref, kv_ref, kr_ref, do_ref, lse_ref, dl_ref, dq_ref, dkv_ref, dkr_ref, dk_sc, dv_sc):
        h, p = pl.program_id(0), pl.program_id(1)
        i, j = qi_ref[p], kj_ref[p]

        @pl.when(p == 0)
        def _():
            dq_ref[...] = jnp.zeros_like(dq_ref)

        @pl.when(jnp.logical_and(p == 0, h == 0))
        def _():
            dkr_ref[...] = jnp.zeros_like(dkr_ref)

        @pl.when(i == j)
        def _():
            dk_sc[...] = jnp.zeros_like(dk_sc)
            dv_sc[...] = jnp.zeros_like(dv_sc)

        def update(diagonal):
            kcat = jnp.concatenate([kv_ref[:, :LANES], kr_ref[...]], axis=1)
            v = kv_ref[:, LANES:]
            for r in range(FLASH_SPLIT):
                rows = slice(r * half, (r + 1) * half)
                q_t, do_t = q_ref[rows, :], do_ref[rows, :]
                s = _raw_scores(q_t, kcat, r * half, diagonal)
                pr = jnp.exp2(s * SCORE_LOG2 - _lanes(lse_ref[rows, :] * LOG2E, blk))
                d_p = _dot(do_t, v, "nt")
                d_s = pr * (d_p - _lanes(dl_ref[rows, :], blk))
                dk_sc[...] += _dot(d_s, q_t, "tn")
                dv_sc[...] += _dot(pr, do_t, "tn")
                q_rows = pl.ds(pl.multiple_of(i * blk + r * half, half), half)
                dq_ref[q_rows, :] += _dot(d_s, kcat)

        @pl.when(i > j)
        def _():
            update(False)

        @pl.when(i == j)
        def _():
            update(True)

        @pl.when(i == n_blk - 1)
        def _():
            dk = dk_sc[...] * MLA_SCALE
            dkv_ref[:, :LANES] = dk[:, :LANES].astype(dkv_ref.dtype)
            dkv_ref[:, LANES:] = dv_sc[...].astype(dkv_ref.dtype)
            k_rows = pl.ds(pl.multiple_of(j * blk, blk), blk)
            dkr_ref[k_rows, :] += dk[:, LANES:]

        @pl.when(p == n_pairs - 1)
        def _():
            dq_ref[...] = dq_ref[...] * MLA_SCALE

    at_q = lambda h, p, qi_r, kj_r: (qi_r[p], h)
    at_kv = lambda h, p, qi_r, kj_r: (kj_r[p], h)
    grid_spec = pltpu.PrefetchScalarGridSpec(
        num_scalar_prefetch=2, grid=(heads, n_pairs),
        in_specs=[pl.BlockSpec((blk, MLA_HEAD_PAD), at_q),
                  pl.BlockSpec((blk, MLA_HEAD_PAD), at_kv),
                  pl.BlockSpec((blk, LANES), lambda h, p, qi_r, kj_r: (kj_r[p], 0)),
                  pl.BlockSpec((blk, MLA_V), at_q),
                  pl.BlockSpec((None, blk, LANES), lambda h, p, qi_r, kj_r: (h, qi_r[p], 0)),
                  pl.BlockSpec((blk, LANES), at_q)],
        out_specs=[pl.BlockSpec((n_l, MLA_HEAD_PAD), lambda h, p, qi_r, kj_r: (0, h)),
                   pl.BlockSpec((blk, MLA_HEAD_PAD), at_kv),
                   pl.BlockSpec((n_l, LANES), lambda h, p, qi_r, kj_r: (0, 0))],
        scratch_shapes=[pltpu.VMEM((blk, MLA_HEAD_PAD), F32), pltpu.VMEM((blk, MLA_V), F32)])
    return pl.pallas_call(
        body, name="mla_flash_bwd", grid_spec=grid_spec,
        out_shape=[_sds(qp.shape), _sds(kv.shape, BF16), _sds(kr.shape)],
        compiler_params=_params(("arbitrary", "arbitrary")),
    )(qi, kj, qp, kv, kr, d_o, lse, delta)


def _mla_fwd(h, p, rope):
    n_l = h.shape[0]
    cos_t, sin_t = rope
    hn = _rms_fwd(h, p["norm_g"], name="mla_rms")
    proj = _mm(hn, p["w_in"], name="mla_in", tn=896)
    cqn, ckvn, kr = _rowwise(_mla_mid, [proj, cos_t, sin_t], [p["q_norm_g"], p["kv_norm_g"]],
                             [_sds((n_l, MLA_Q_RANK), BF16), _sds((n_l, MLA_KV_RANK), BF16), _sds((n_l, LANES), BF16)],
                             [], tm=512, name="mla_mid")
    q_raw = _mm(cqn, p["w_uq"], name="mla_uq")
    qp = _rowwise(_mla_rope_q, [q_raw, cos_t, sin_t], [], [_sds(q_raw.shape, BF16)], [], tm=512, name="mla_rope_q")[0]
    kv = _mm(ckvn, p["w_ukv"], out_dtype=BF16, name="mla_ukv")
    o, lse = _flash_fwd(qp, kv, kr)
    gated = _rowwise(_mla_gate, [o, proj], [], [_sds(o.shape, BF16)], [], tm=512, name="mla_gate")[0]
    h_next = _mm(gated, p["w_out"], add=h, name="mla_out")
    return h_next, (h, hn, proj, cqn, ckvn, kr, qp, kv, o, lse, gated)


def _mla_bwd(dh_out, p, saved, rope):
    h, hn, proj, cqn, ckvn, kr, qp, kv, o, lse, gated = saved
    n_l = h.shape[0]
    cos_t, sin_t = rope
    d_gated = _mm(dh_out, p["w_out"], mode="nt", out_dtype=BF16, name="mla_dgated")
    g_w_out = _mm(gated, dh_out, mode="tn", name="mla_dwout")

    def gate_bwd(o_t, proj_t, ct):
        _, vjp = jax.vjp(lambda a, z: a * jax.nn.silu(z), o_t, proj_t[:, MLA_Z0:])
        d_o_t, d_z_t = vjp(ct.astype(F32))
        prod = d_o_t * o_t
        delta = jnp.concatenate(
            [jnp.broadcast_to(jnp.sum(prod[:, hd * MLA_V:(hd + 1) * MLA_V], axis=1, keepdims=True),
                              (prod.shape[0], MLA_V)) for hd in range(prod.shape[1] // MLA_V)], axis=1)
        return d_o_t, d_z_t, delta

    d_o, d_z, delta = _rowwise(gate_bwd, [o, proj, d_gated], [], [_sds(o.shape, BF16), _sds(o.shape), _sds(o.shape)],
                               [], tm=512, name="mla_gate_bwd")
    d_qp, d_kv, d_kr = _flash_bwd(qp, kv, kr, d_o, lse, delta)

    def rope_q_bwd(ct, c_t, s_t):
        return _mla_rope_q(ct, c_t, -s_t)

    d_q_raw = _rowwise(rope_q_bwd, [d_qp, cos_t, sin_t], [], [_sds(d_qp.shape, BF16)], [], tm=512,
                       name="mla_rope_q_bwd")[0]
    g_w_uq = _mm(cqn, d_q_raw, mode="tn", name="mla_dwuq")
    d_cqn = _mm(d_q_raw, p["w_uq"], mode="nt", name="mla_dcqn")
    g_w_ukv = _mm(ckvn, d_kv, mode="tn", name="mla_dwukv")
    d_ckvn = _mm(d_kv, p["w_ukv"], mode="nt", name="mla_dckvn")

    def mid_bwd(proj_t, c_t, s_t, d_cq, d_ckv, d_kr_t, d_z_t, q_g, kv_g):
        _, vjp_q = jax.vjp(_rms, proj_t[:, :MLA_Q_RANK], q_g)
        _, vjp_kv = jax.vjp(_rms, proj_t[:, MLA_Q_RANK:MLA_Q_RANK + MLA_KV_RANK], kv_g)
        d_q_in, d_qg = vjp_q(d_cq)
        d_kv_in, d_kvg = vjp_kv(d_ckv)
        d_kr_in = _rope_tile(d_kr_t, c_t, -s_t)
        return jnp.concatenate([d_q_in, d_kv_in, d_kr_in, d_z_t], axis=1), d_qg, d_kvg

    d_proj, g_q_norm, g_kv_norm = _rowwise(
        mid_bwd, [proj, cos_t, sin_t, d_cqn, d_ckvn, d_kr, d_z], [p["q_norm_g"], p["kv_norm_g"]],
        [_sds(proj.shape, BF16)], [_sds(p["q_norm_g"].shape), _sds(p["kv_norm_g"].shape)], tm=512, name="mla_mid_bwd")
    g_w_in = _mm(hn, d_proj, mode="tn", name="mla_dwin", tn=896)
    d_hn = _mm(d_proj, p["w_in"], mode="nt", name="mla_dhn", tk=896)
    dh_in, g_norm = _rms_bwd(h, p["norm_g"], d_hn, dh_out, name="mla_rms_bwd")
    grads = dict(norm_g=g_norm, w_in=g_w_in, q_norm_g=g_q_norm, w_uq=g_w_uq, kv_norm_g=g_kv_norm, w_ukv=g_w_ukv,
                 w_out=g_w_out)
    return dh_in, grads


def _loss_head(h, g, target):
    def fn(x, t, gg):
        def local(xx, g2):
            err = _rms(xx, g2) - t
            return 0.5 * jnp.sum(jnp.mean(err * err, axis=-1))

        val, (dx, dg) = jax.value_and_grad(local, argnums=(0, 1))(x, gg)
        return dx, jnp.full((1, LANES), val, F32), dg

    dh, loss, dg = _rowwise(fn, [h, target], [g], [_sds(h.shape)], [_sds((1, LANES)), _sds(g.shape)], tm=512,
                            name="loss_head")
    return loss[0, 0], dh, dg


HBM_SPEC = pl.BlockSpec(memory_space=pltpu.HBM)


AG_COPIES = 7


def _all_gather(shards, *, name):
    n_arr = len(shards)

    def body(*refs):
        x_refs, out_refs = refs[:n_arr], refs[n_arr:2 * n_arr]
        send_sems, recv_sems, local_sems = refs[2 * n_arr:]
        x, y, c = lax.axis_index("x"), lax.axis_index("y"), lax.axis_index("c")
        me, sibling = (x, y, c), (x, y, 1 - c)
        chips = [(1 - x, y), (x, 1 - y), (1 - x, 1 - y)]

        def block_of(t, px, py, pc):
            return out_refs[t].at[4 * px + 2 * py + pc]

        def copy(t, k, block, to, src=None):
            return pltpu.make_async_remote_copy(
                src_ref=block_of(t, *block) if src is None else src, dst_ref=block_of(t, *block),
                send_sem=send_sems.at[t * AG_COPIES + k], recv_sem=recv_sems.at[t * AG_COPIES + k],
                device_id=to, device_id_type=MESH)

        mine = [pltpu.make_async_copy(x_refs[t], block_of(t, *me), local_sems.at[t]) for t in range(n_arr)]
        for cp in mine:
            cp.start()
        first = []
        for t in range(n_arr):
            first.append(copy(t, 0, me, sibling, src=x_refs[t]))
            first += [copy(t, 1 + j, me, (*chip, c), src=x_refs[t]) for j, chip in enumerate(chips)]
        for cp in first:
            cp.start()
        passed = []
        for j, chip in enumerate(chips):
            for t in range(n_arr):
                copy(t, 1 + j, (*chip, c), me).wait_recv()
                passed.append(copy(t, 4 + j, (*chip, c), sibling))
                passed[-1].start()
        for t in range(n_arr):
            copy(t, 0, sibling, me).wait_recv()
        for j, chip in enumerate(chips):
            for t in range(n_arr):
                copy(t, 4 + j, (*chip, 1 - c), me).wait_recv()
        for cp in first + passed:
            cp.wait_send()
        for cp in mine:
            cp.wait()

    return pl.pallas_call(
        body, name=name, out_shape=[jax.ShapeDtypeStruct((N_DEV,) + s.shape, s.dtype) for s in shards],
        in_specs=[HBM_SPEC] * n_arr, out_specs=[HBM_SPEC] * n_arr,
        scratch_shapes=[pltpu.SemaphoreType.DMA((n_arr * AG_COPIES,)), pltpu.SemaphoreType.DMA((n_arr * AG_COPIES,)),
                        pltpu.SemaphoreType.DMA((n_arr,))],
    )(*shards)


def _exchange(src, routes, *, name):
    n_routes = len(routes)

    def body(s_ref, out_ref, send_sems, recv_sems):
        x, y, c = lax.axis_index("x"), lax.axis_index("y"), lax.axis_index("c")
        local, remote = [], []
        for k, (flip, block) in enumerate(routes):
            src_blk = s_ref.at[block(x, y, c)]
            if flip == 0:
                local.append(pltpu.make_async_copy(src_blk, out_ref.at[k], send_sems.at[k]))
            else:
                peer = (1 - x if flip & 4 else x, 1 - y if flip & 2 else y, 1 - c if flip & 1 else c)
                remote.append(pltpu.make_async_remote_copy(
                    src_ref=src_blk, dst_ref=out_ref.at[k], send_sem=send_sems.at[k], recv_sem=recv_sems.at[k],
                    device_id=peer, device_id_type=MESH))
        for cp in local + remote:
            cp.start()
        for cp in remote:
            cp.wait_recv()
        for cp in remote:
            cp.wait_send()
        for cp in local:
            cp.wait()

    return pl.pallas_call(
        body, name=name, out_shape=jax.ShapeDtypeStruct((n_routes,) + src.shape[1:], src.dtype),
        in_specs=[HBM_SPEC], out_specs=HBM_SPEC,
        scratch_shapes=[pltpu.SemaphoreType.DMA((n_routes,)), pltpu.SemaphoreType.DMA((n_routes,))],
    )(src)


def _reduce_scatter(send, *, name):
    def chip_block(k, other_core):
        return lambda x, y, c: (4 * (1 - x if k & 2 else x) + 2 * (1 - y if k & 1 else y)
                                + (1 - c if other_core else c))

    n_chips = 4
    pair = _exchange(send, [(1, chip_block(k, True)) for k in range(n_chips)], name=name + "_pair")
    rows, width = send.shape[1], send.shape[2]
    tr = _pick(rows, 256, 8)
    x, y, c = lax.axis_index("x"), lax.axis_index("y"), lax.axis_index("c")
    own_ids = jnp.stack([chip_block(k, False)(x, y, c) for k in range(n_chips)]).astype(jnp.int32)

    def add_body(ids_ref, *refs):
        own_refs, p_ref, o_ref = refs[:n_chips], refs[n_chips], refs[n_chips + 1]
        for k in range(n_chips):
            o_ref[k] = (own_refs[k][...].astype(F32) + p_ref[k].astype(F32)).astype(o_ref.dtype)

    own_spec = lambda k: pl.BlockSpec((None, tr, width), lambda i, ids: (ids[k], i, 0))
    chip_sums = pl.pallas_call(
        add_body, name=name + "_pair_sum",
        grid_spec=pltpu.PrefetchScalarGridSpec(
            num_scalar_prefetch=1, grid=(rows // tr,),
            in_specs=[own_spec(k) for k in range(n_chips)] + [pl.BlockSpec((n_chips, tr, width), lambda i, ids: (0, i, 0))],
            out_specs=pl.BlockSpec((n_chips, tr, width), lambda i, ids: (0, i, 0))),
        out_shape=jax.ShapeDtypeStruct((n_chips, rows, width), send.dtype), compiler_params=_params(("parallel",)),
    )(own_ids, *([send] * n_chips), pair)
    recv = _exchange(chip_sums, [(2 * k, (lambda kk: lambda x, y, c: kk)(k)) for k in range(1, n_chips)],
                     name=name + "_chips")

    def sum_body(q_ref, r_ref, o_ref):
        acc = q_ref[...].astype(F32)
        for k in range(n_chips - 1):
            acc = acc + r_ref[k].astype(F32)
        o_ref[...] = acc

    return pl.pallas_call(
        sum_body, name=name + "_sum", grid=(rows // tr,),
        in_specs=[pl.BlockSpec((None, tr, width), lambda i: (0, i, 0)),
                  pl.BlockSpec((n_chips - 1, tr, width), lambda i: (0, i, 0))],
        out_specs=pl.BlockSpec((tr, width), lambda i: (i, 0)), out_shape=_sds((rows, width)),
        compiler_params=_params(("parallel",)),
    )(chip_sums, recv)


def _adamw(w, g, m, v, *, name):
    def fn(ww, gg, mm, vv):
        m_new = ADAM_B1 * mm + (1.0 - ADAM_B1) * gg
        v_new = ADAM_B2 * vv + (1.0 - ADAM_B2) * jnp.square(gg)
        m_hat = m_new / (1.0 - ADAM_B1 ** ADAM_STEP)
        v_hat = v_new / (1.0 - ADAM_B2 ** ADAM_STEP)
        return -ADAM_LR * (m_hat / (jnp.sqrt(v_hat) + ADAM_EPS) + ADAM_WD * ww), m_new, v_new

    return _rowwise(fn, [w, g, m, v], [], [_sds(w.shape)] * 3, [], tm=256, name=name)


KINDS = ("gmlp", "s5", "mla", "gmlp")
LAYER_NAMES = {
    "gmlp": ("norm_g", "w_in", "ln_g", "ln_b", "w_s", "b_s", "w_out"),
    "s5": ("norm_g", "w_in", "a_re", "a_im", "log_step", "b_re", "b_im", "c_re", "c_im", "d_skip", "w_glu", "b_glu",
           "w_out"),
    "mla": ("norm_g", "w_in", "q_norm_g", "w_uq", "kv_norm_g", "w_ukv", "w_out"),
}
COL_SHARDED = ("w_in", "w_uq", "w_ukv")
ROW_SHARDED = ("w_out", "w_glu")
WEIGHT_NAMES = tuple("l%d_%s" % (i, n) for i, k in enumerate(KINDS) for n in LAYER_NAMES[k]) + ("final_norm_g",)


def _is_sharded(name):
    return name.split("_", 1)[1] in COL_SHARDED + ROW_SHARDED


def _flatten(arrs, pad_rows_to):
    parts, sizes = [], []
    for a in arrs:
        flat = a.reshape(-1)
        pad = (-flat.shape[0]) % FLAT_W
        if pad:
            flat = jnp.pad(flat, (0, pad))
        parts.append(flat)
        sizes.append(flat.shape[0] // FLAT_W)
    rows = sum(sizes)
    pad_rows = (-rows) % pad_rows_to
    if pad_rows:
        parts.append(jnp.zeros((pad_rows * FLAT_W,), arrs[0].dtype))
    return jnp.concatenate(parts).reshape(-1, FLAT_W), sizes


def _unflatten(flat, shapes, sizes):
    out, row = [], 0
    for shape, n_rows in zip(shapes, sizes):
        n = int(np.prod(shape))
        out.append(flat[row:row + n_rows].reshape(-1)[:n].reshape(shape))
        row += n_rows
    return out


def _full_from_gathered(blocks, name):
    if name.split("_", 1)[1] in COL_SHARDED:
        return blocks.transpose(1, 0, 2).reshape(blocks.shape[1], -1)
    return blocks.reshape(-1, blocks.shape[2])


def _shards_of(full, name):
    if name.split("_", 1)[1] in COL_SHARDED:
        return full.reshape(full.shape[0], N_DEV, -1).transpose(1, 0, 2)
    return full.reshape(N_DEV, -1, full.shape[1])


def _rope_tables(positions):
    inv_freq = ROPE_THETA ** (-jnp.arange(0, MLA_ROPE, 2, dtype=F32) / MLA_ROPE)
    ang = positions.astype(F32)[:, None] * inv_freq
    cos, sin = jnp.cos(ang), jnp.sin(ang)
    zero = jnp.zeros((positions.shape[0], LANES - MLA_ROPE), F32)
    return jnp.concatenate([cos, cos, zero], axis=1), jnp.concatenate([-sin, sin, zero], axis=1)


def _row(v):
    return v.reshape(1, -1)


def kernel(x, positions, l0_norm_g, l0_w_in, l0_ln_g, l0_ln_b, l0_w_s, l0_b_s, l0_w_out, l1_norm_g, l1_w_in, l1_a_re, l1_a_im, l1_log_step, l1_b_re, l1_b_im, l1_c_re, l1_c_im, l1_d_skip, l1_w_glu, l1_b_glu, l1_w_out, l2_norm_g, l2_w_in, l2_q_norm_g, l2_w_uq, l2_kv_norm_g, l2_w_ukv, l2_w_out, l3_norm_g, l3_w_in, l3_ln_g, l3_ln_b, l3_w_s, l3_b_s, l3_w_out, final_norm_g, loss_target, m_l0_norm_g, m_l0_w_in, m_l0_ln_g, m_l0_ln_b, m_l0_w_s, m_l0_b_s, m_l0_w_out, m_l1_norm_g, m_l1_w_in, m_l1_a_re, m_l1_a_im, m_l1_log_step, m_l1_b_re, m_l1_b_im, m_l1_c_re, m_l1_c_im, m_l1_d_skip, m_l1_w_glu, m_l1_b_glu, m_l1_w_out, m_l2_norm_g, m_l2_w_in, m_l2_q_norm_g, m_l2_w_uq, m_l2_kv_norm_g, m_l2_w_ukv, m_l2_w_out, m_l3_norm_g, m_l3_w_in, m_l3_ln_g, m_l3_ln_b, m_l3_w_s, m_l3_b_s, m_l3_w_out, m_final_norm_g, v_l0_norm_g, v_l0_w_in, v_l0_ln_g, v_l0_ln_b, v_l0_w_s, v_l0_b_s, v_l0_w_out, v_l1_norm_g, v_l1_w_in, v_l1_a_re, v_l1_a_im, v_l1_log_step, v_l1_b_re, v_l1_b_im, v_l1_c_re, v_l1_c_im, v_l1_d_skip, v_l1_w_glu, v_l1_b_glu, v_l1_w_out, v_l2_norm_g, v_l2_w_in, v_l2_q_norm_g, v_l2_w_uq, v_l2_kv_norm_g, v_l2_w_ukv, v_l2_w_out, v_l3_norm_g, v_l3_w_in, v_l3_ln_g, v_l3_ln_b, v_l3_w_s, v_l3_b_s, v_l3_w_out, v_final_norm_g):
    args = locals()
    weights = {n: args[n] for n in WEIGHT_NAMES}
    mom_m = {n: args["m_" + n] for n in WEIGHT_NAMES}
    mom_v = {n: args["v_" + n] for n in WEIGHT_NAMES}
    return _train_step(x, positions, loss_target, weights, mom_m, mom_v)


def _train_step(x, positions, loss_target, weights, mom_m, mom_v):
    big = [n for n in WEIGHT_NAMES if _is_sharded(n)]
    small = [n for n in WEIGHT_NAMES if not _is_sharded(n)]

    big_sizes = [weights[n].size // FLAT_W for n in big]
    gathered = _all_gather([weights[n].astype(BF16) for n in big], name="weights_all_gather")
    full = {n: _full_from_gathered(blocks, n) for n, blocks in zip(big, gathered)}

    layers, ops_vjps = [], {}
    for i, kind in enumerate(KINDS):
        pre = "l%d_" % i
        p = {n: (full[pre + n] if _is_sharded(pre + n) else weights[pre + n]) for n in LAYER_NAMES[kind]}
        p["norm_g"] = _row(p["norm_g"])
        if kind == "gmlp":
            p["ln_g"], p["ln_b"], p["b_st"] = _row(p["ln_g"]), _row(p["ln_b"]), p["b_s"].T
        elif kind == "s5":
            p["d_skip"], p["b_glu"] = _row(p["d_skip"]), _row(p["b_glu"])
            ops, ops_vjps[i] = jax.vjp(_s5_operators, *[p[n] for n in ("a_re", "a_im", "log_step", "b_re", "b_im",
                                                                       "c_re", "c_im")])
            p["ops"] = tuple(o.astype(BF16) for o in ops[:3]) + ops[3:]
        else:
            heads = p["w_uq"].shape[1] // MLA_QK_DIM
            w_in = p["w_in"]
            split = MLA_Q_RANK + MLA_KV_RANK + MLA_ROPE
            p["w_in"] = jnp.concatenate([w_in[:, :split], jnp.zeros((w_in.shape[0], LANES - MLA_ROPE), w_in.dtype),
                                         w_in[:, split:]], axis=1)
            p["w_uq"] = jnp.pad(p["w_uq"].reshape(-1, heads, MLA_QK_DIM),
                                ((0, 0), (0, 0), (0, MLA_HEAD_PAD - MLA_QK_DIM))).reshape(-1, heads * MLA_HEAD_PAD)
            p["q_norm_g"], p["kv_norm_g"] = _row(p["q_norm_g"]), _row(p["kv_norm_g"])
        layers.append(p)
    rope = _rope_tables(positions[0])

    h = x[0]
    saved = []
    for kind, p in zip(KINDS, layers):
        if kind == "gmlp":
            h, s = _gmlp_fwd(h, p)
        elif kind == "s5":
            h, s = _s5_fwd(h, p)
        else:
            h, s = _mla_fwd(h, p, rope)
        saved.append(s)
    loss_local, dh, g_final = _loss_head(h, _row(weights["final_norm_g"]), loss_target[0])
    loss = lax.psum(loss_local, ("x", "y", "c"))

    grads = {"final_norm_g": g_final.reshape(-1)}
    for i in reversed(range(len(KINDS))):
        kind, p = KINDS[i], layers[i]
        if kind == "gmlp":
            dh, g = _gmlp_bwd(dh, p, saved[i])
        elif kind == "s5":
            dh, g = _s5_bwd(dh, p, saved[i], ops_vjps[i])
        else:
            dh, g = _mla_bwd(dh, p, saved[i], rope)
            heads = weights["l%d_w_uq" % i].shape[1] * N_DEV // MLA_QK_DIM
            split = MLA_Q_RANK + MLA_KV_RANK + MLA_ROPE
            g["w_in"] = jnp.concatenate([g["w_in"][:, :split], g["w_in"][:, MLA_Z0:]], axis=1)
            g["w_uq"] = g["w_uq"].reshape(-1, heads, MLA_HEAD_PAD)[:, :, :MLA_QK_DIM].reshape(-1, heads * MLA_QK_DIM)
        for n, val in g.items():
            name = "l%d_%s" % (i, n)
            grads[name] = val.reshape(weights[name].shape) if not _is_sharded(name) else val

    small_flat, small_sizes = _flatten([grads[n] for n in small], 8 * N_DEV)
    small_rows = small_flat.shape[0] // N_DEV
    send_parts = [_shards_of(grads[n], n).reshape(N_DEV, -1, FLAT_W) for n in big]
    send_parts.append(small_flat.reshape(N_DEV, small_rows, FLAT_W))
    send = jnp.concatenate(send_parts, axis=1)
    big_rows = send.shape[1] - small_rows
    pad_rows = (-send.shape[1]) % 8
    if pad_rows:
        send = jnp.pad(send, ((0, 0), (0, pad_rows), (0, 0)))
    reduced = _reduce_scatter(send.astype(BF16), name="grads")
    g_big_flat = reduced[:big_rows]
    g_small_all = _all_gather([reduced[big_rows:big_rows + small_rows]], name="small_grads_all_gather")[0]
    g_small_flat = g_small_all.reshape(-1, FLAT_W)

    def flat_of(tree, names, pad_to):
        return _flatten([tree[n] for n in names], pad_to)[0]

    outs = {}
    big_shapes = [weights[n].shape for n in big]
    for n, g_n in zip(big, _unflatten(g_big_flat, big_shapes, big_sizes)):
        outs["grad_" + n] = g_n
        outs["delta_" + n], outs["new_m_" + n], outs["new_v_" + n] = _adamw(
            weights[n], g_n, mom_m[n], mom_v[n], name="adamw_" + n)
    d_s, nm_s, nv_s = _adamw(flat_of(weights, small, 8 * N_DEV), g_small_flat, flat_of(mom_m, small, 8 * N_DEV),
                             flat_of(mom_v, small, 8 * N_DEV), name="adamw_replicated")
    small_shapes = [weights[n].shape for n in small]
    for prefix, fs in (("grad_", g_small_flat), ("delta_", d_s), ("new_m_", nm_s), ("new_v_", nv_s)):
        for n, a in zip(small, _unflatten(fs, small_shapes, small_sizes)):
            outs[prefix + n] = a
    result = [loss, dh[None]]
    for prefix in ("grad_", "delta_", "new_m_", "new_v_"):
        result += [outs[prefix + n] for n in WEIGHT_NAMES]
    return tuple(result)
```

```python
import functools
import math

import numpy as np
import jax
import jax.numpy as jnp
from jax import lax
from jax.experimental import pallas as pl
from jax.experimental.pallas import tpu as pltpu

F32 = jnp.float32
BF16 = jnp.bfloat16

NORM_EPS = 1e-6
GMLP_CHUNK = 128
S5_GROUP = 16
S5_STATE = 64
S5_T = 16
MLA_NOPE = 128
MLA_ROPE = 64
MLA_V = 128
MLA_QK_DIM = MLA_NOPE + MLA_ROPE
MLA_Q_RANK = 384
MLA_KV_RANK = 128
MLA_HEAD_PAD = 256
MLA_SCALE = MLA_QK_DIM ** -0.5
ROPE_THETA = 10000.0
NEG_INF = -1e30
ADAM_LR = 0.001
ADAM_B1 = 0.9
ADAM_B2 = 0.999
ADAM_EPS = 1e-08
ADAM_WD = 0.01
ADAM_STEP = 10

N_DEV = 8
LANES = 128
FLAT_W = 1024
VMEM_LIMIT = 56 * 1024 * 1024
MESH = pl.DeviceIdType.MESH


def _pick(dim, pref, align=LANES):
    t = (min(pref, dim) // align) * align
    while t >= align:
        if dim % t == 0:
            return t
        t -= align
    return dim


def _params(sem=None):
    return pltpu.CompilerParams(dimension_semantics=sem, vmem_limit_bytes=VMEM_LIMIT)


_DIMS = {"nn": (((1,), (0,)), ((), ())), "nt": (((1,), (1,)), ((), ())), "tn": (((0,), (0,)), ((), ()))}


def _mm(a, b, *, mode="nn", out_dtype=F32, add=None, name, tm=1024, tn=1024, tk=2048):
    if mode == "nn":
        (m, k), (_, n) = a.shape, b.shape
    elif mode == "nt":
        (m, k), (n, _) = a.shape, b.shape
    else:
        (k, m), (_, n) = a.shape, b.shape
    tm, tn, tk = _pick(m, tm, 8), _pick(n, tn), _pick(k, tk)
    nk = k // tk
    dims = _DIMS[mode]

    def body(*refs):
        a_ref, b_ref = refs[:2]
        r_ref = refs[2] if add is not None else None
        o_ref = refs[3] if add is not None else refs[2]
        part = lax.dot_general(a_ref[...].astype(BF16), b_ref[...].astype(BF16), dims, preferred_element_type=F32)

        def finish(res):
            if add is not None:
                res = res + r_ref[...]
            o_ref[...] = res.astype(o_ref.dtype)

        if nk == 1:
            finish(part)
            return
        acc_ref = refs[-1]
        kk = pl.program_id(2)

        @pl.when(kk == 0)
        def _():
            acc_ref[...] = part

        @pl.when(kk > 0)
        def _():
            acc_ref[...] += part

        @pl.when(kk == nk - 1)
        def _():
            finish(acc_ref[...])

    a_spec = (pl.BlockSpec((tk, tm), lambda i, j, kk: (kk, i)) if mode == "tn"
              else pl.BlockSpec((tm, tk), lambda i, j, kk: (i, kk)))
    b_spec = (pl.BlockSpec((tn, tk), lambda i, j, kk: (j, kk)) if mode == "nt"
              else pl.BlockSpec((tk, tn), lambda i, j, kk: (kk, j)))
    in_specs = [a_spec, b_spec]
    args = [a, b]
    if add is not None:
        in_specs.append(pl.BlockSpec((tm, tn), lambda i, j, kk: (i, j)))
        args.append(add)
    return pl.pallas_call(
        body, name=name, grid=(m // tm, n // tn, nk),
        in_specs=in_specs, out_specs=pl.BlockSpec((tm, tn), lambda i, j, kk: (i, j)),
        out_shape=jax.ShapeDtypeStruct((m, n), out_dtype),
        scratch_shapes=[pltpu.VMEM((tm, tn), F32)] if nk > 1 else [],
        compiler_params=_params(("parallel", "parallel", "arbitrary")),
    )(*args)


def _rowwise(fn, tiled, full, out_tiled, out_acc, *, tm, name):
    rows = tiled[0].shape[0]
    tm = _pick(rows, tm, 8)
    nt, nf, no = len(tiled), len(full), len(out_tiled)

    def body(*refs):
        ins = [r[...] for r in refs[:nt + nf]]
        o_refs = refs[nt + nf:nt + nf + no]
        a_refs = refs[nt + nf + no:]
        outs = fn(*ins)
        if not isinstance(outs, (tuple, list)):
            outs = (outs,)
        for r, v in zip(o_refs, outs[:no]):
            r[...] = v.astype(r.dtype)
        if a_refs:
            @pl.when(pl.program_id(0) == 0)
            def _():
                for r in a_refs:
                    r[...] = jnp.zeros_like(r)

            for r, v in zip(a_refs, outs[no:]):
                r[...] += v.astype(r.dtype)

    def whole(shape):
        nd = len(shape)
        return pl.BlockSpec(tuple(shape), lambda i: (0,) * nd)

    in_specs = ([pl.BlockSpec((tm, t.shape[1]), lambda i: (i, 0)) for t in tiled]
                + [whole(f.shape) for f in full])
    out_specs = ([pl.BlockSpec((tm, o.shape[1]), lambda i: (i, 0)) for o in out_tiled]
                 + [whole(o.shape) for o in out_acc])
    outs = pl.pallas_call(
        body, name=name, grid=(rows // tm,), in_specs=in_specs, out_specs=out_specs,
        out_shape=list(out_tiled) + list(out_acc),
        compiler_params=_params(("arbitrary",)),
    )(*tiled, *full)
    return outs


def _sds(shape, dtype=F32):
    return jax.ShapeDtypeStruct(tuple(shape), dtype)


def _rms(x, g):
    return x * lax.rsqrt(jnp.mean(x * x, axis=-1, keepdims=True) + NORM_EPS) * g


def _layernorm(x, g, b):
    mu = jnp.mean(x, axis=-1, keepdims=True)
    xc = x - mu
    var = jnp.mean(xc * xc, axis=-1, keepdims=True)
    return xc * lax.rsqrt(var + NORM_EPS) * g + b


def _dot(a, b, mode="nn"):
    return lax.dot_general(a.astype(BF16), b.astype(BF16), _DIMS[mode], preferred_element_type=F32)


@jax.custom_vjp
def _bdot(a, b):
    return _dot(a, b)


def _bdot_fwd(a, b):
    return _bdot(a, b), (a, b)


def _bdot_bwd(res, ct):
    a, b = res
    return _dot(ct, b, "nt"), _dot(a, ct, "tn")


_bdot.defvjp(_bdot_fwd, _bdot_bwd)


def _rms_fwd(h, g, *, name):
    return _rowwise(lambda x, gg: _rms(x, gg), [h], [g], [_sds(h.shape, BF16)], [], tm=512, name=name)[0]


def _rms_bwd(h, g, d_hn, dh_out, *, name):
    def fn(x, ct, res, gg):
        _, vjp = jax.vjp(_rms, x, gg)
        dx, dg = vjp(ct)
        return res + dx, dg

    dh, dg = _rowwise(fn, [h, d_hn, dh_out], [g], [_sds(h.shape)], [_sds(g.shape)], tm=512, name=name)
    return dh, dg


def _gmlp_mid(uvz, ln_g, ln_b, w_s, b_st):
    di = ln_g.shape[1]
    ng, ck = w_s.shape[0], w_s.shape[1]
    dg = di // ng
    u = jax.nn.gelu(uvz[:, :di])
    v = _layernorm(jax.nn.gelu(uvz[:, di:2 * di]), ln_g, ln_b)
    z = uvz[:, 2 * di:]
    row = lax.broadcasted_iota(jnp.int32, (ck, ck), 0)
    col = lax.broadcasted_iota(jnp.int32, (ck, ck), 1)
    causal = col <= row
    blocks = []
    for c in range(uvz.shape[0] // ck):
        cols = []
        for g in range(ng):
            w = jnp.where(causal, w_s[g], 0.0)
            cols.append(_bdot(w, v[c * ck:(c + 1) * ck, g * dg:(g + 1) * dg]) + b_st[:, g:g + 1])
        blocks.append(jnp.concatenate(cols, axis=1))
    s = blocks[0] if len(blocks) == 1 else jnp.concatenate(blocks, axis=0)
    return u * s * jax.nn.silu(z)


def _gmlp_fwd(h, p):
    hn = _rms_fwd(h, p["norm_g"], name="gmlp_rms")
    uvz = _mm(hn, p["w_in"], name="gmlp_in")
    di = p["ln_g"].shape[1]
    gated = _rowwise(_gmlp_mid, [uvz], [p["ln_g"], p["ln_b"], p["w_s"], p["b_st"]],
                     [_sds((h.shape[0], di), BF16)], [], tm=256, name="gmlp_mid")[0]
    h_next = _mm(gated, p["w_out"], add=h, name="gmlp_out")
    return h_next, (h, hn, uvz, gated)


def _gmlp_bwd(dh_out, p, saved):
    h, hn, uvz, gated = saved
    d_gated = _mm(dh_out, p["w_out"], mode="nt", out_dtype=BF16, name="gmlp_dgated")
    g_w_out = _mm(gated, dh_out, mode="tn", name="gmlp_dwout")

    def fn(t, ct, ln_g, ln_b, w_s, b_st):
        _, vjp = jax.vjp(_gmlp_mid, t, ln_g, ln_b, w_s, b_st)
        return vjp(ct.astype(F32))

    d_uvz, g_ln_g, g_ln_b, g_w_s, g_b_st = _rowwise(
        fn, [uvz, d_gated], [p["ln_g"], p["ln_b"], p["w_s"], p["b_st"]],
        [_sds(uvz.shape, BF16)], [_sds(p["ln_g"].shape), _sds(p["ln_b"].shape), _sds(p["w_s"].shape),
                                  _sds(p["b_st"].shape)], tm=256, name="gmlp_mid_bwd")
    g_w_in = _mm(hn, d_uvz, mode="tn", name="gmlp_dwin")
    d_hn = _mm(d_uvz, p["w_in"], mode="nt", name="gmlp_dhn")
    dh_in, g_norm = _rms_bwd(h, p["norm_g"], d_hn, dh_out, name="gmlp_rms_bwd")
    grads = dict(norm_g=g_norm, w_in=g_w_in, ln_g=g_ln_g, ln_b=g_ln_b, w_s=g_w_s, b_s=g_b_st.T, w_out=g_w_out)
    return dh_in, grads


def _s5_operators(a_re, a_im, log_step, b_re, b_im, c_re, c_im):
    t_len = S5_T
    step = jnp.exp(log_step)[:, None]
    lr, li = a_re * step, a_im * step
    ks = jnp.arange(t_len + 1, dtype=F32)[:, None, None]
    mag = jnp.exp(ks * lr)
    pw_r, pw_i = mag * jnp.cos(ks * li), mag * jnp.sin(ks * li)
    nr, ni = pw_r[1] - 1.0, pw_i[1]
    den = a_re * a_re + a_im * a_im
    f_r, f_i = (nr * a_re + ni * a_im) / den, (ni * a_re - nr * a_im) / den
    bb_r = f_r[..., None] * b_re - f_i[..., None] * b_im
    bb_i = f_r[..., None] * b_im + f_i[..., None] * b_re
    hi = lax.Precision.HIGHEST
    cp_r = c_re[None] * pw_r[:, :, None, :] - c_im[None] * pw_i[:, :, None, :]
    cp_i = c_re[None] * pw_i[:, :, None, :] + c_im[None] * pw_r[:, :, None, :]
    n_g, n_h = a_re.shape[0], b_re.shape[2]
    lhs = jnp.concatenate([bb_r, -bb_i], axis=1)
    rhs = jnp.concatenate([cp_r[:t_len], cp_i[:t_len]], axis=3)
    rhs = rhs.transpose(1, 3, 0, 2).reshape(n_g, -1, t_len * n_h)
    kcat = jnp.einsum("gqi,gqn->gin", lhs, rhs, precision=hi)
    toep = jnp.stack([jnp.pad(kcat[:, :, :(t_len - s) * n_h], ((0, 0), (0, 0), (s * n_h, 0)))
                      for s in range(t_len)], axis=1).reshape(n_g, t_len * n_h, t_len * n_h)
    rev_r, rev_i = pw_r[t_len - 1::-1][:t_len], pw_i[t_len - 1::-1][:t_len]
    we_r = rev_r[..., None] * bb_r[None] - rev_i[..., None] * bb_i[None]
    we_i = rev_r[..., None] * bb_i[None] + rev_i[..., None] * bb_r[None]
    wend = jnp.concatenate([we_r, we_i], axis=2).transpose(1, 0, 3, 2).reshape(n_g, t_len * n_h, -1)
    wo = jnp.concatenate([cp_r[1:], -cp_i[1:]], axis=3)
    wout = wo.transpose(1, 3, 0, 2).reshape(n_g, -1, t_len * n_h)
    a_r, a_i = pw_r[t_len], pw_i[t_len]
    a1 = jnp.concatenate([a_r, a_r], axis=1)
    a2 = jnp.concatenate([-a_i, a_i], axis=1)
    return toep, wend, wout, a1, a2


def _group_call(body, ins, outs, *, gb, name):
    n_g = ins[0].shape[0]

    def spec(a):
        return pl.BlockSpec((gb,) + tuple(a.shape[1:]), lambda i: (i, 0, 0))

    return pl.pallas_call(
        body, name=name, grid=(n_g // gb,), in_specs=[spec(a) for a in ins],
        out_specs=[spec(o) for o in outs], out_shape=list(outs),
        compiler_params=_params(("parallel",)),
    )(*ins)


def _s5_states(u_g, wend, *, gb=16):
    def body(u_ref, w_ref, s_ref):
        for g in range(gb):
            s_ref[g] = _dot(u_ref[g], w_ref[g])

    n_g, n_c = u_g.shape[0], u_g.shape[1]
    return _group_call(body, [u_g, wend], [_sds((n_g, n_c, wend.shape[2]))], gb=gb, name="s5_states")[0]


def _s5_outputs(u_g, toep, xprev, wout, *, gb=16):
    def body(u_ref, t_ref, x_ref, w_ref, y_ref):
        for g in range(gb):
            y_ref[g] = _dot(u_ref[g], t_ref[g]) + _dot(x_ref[g], w_ref[g])

    return _group_call(body, [u_g, toep, xprev, wout], [_sds(u_g.shape)], gb=gb, name="s5_outputs")[0]


def _s5_outputs_bwd(u_g, d_y, xprev, wout, *, gb=16):
    def body(u_ref, dy_ref, x_ref, w_ref, dt_ref, dw_ref, dx_ref):
        for g in range(gb):
            dy = dy_ref[g]
            dt_ref[g] = _dot(u_ref[g], dy, "tn")
            dw_ref[g] = _dot(x_ref[g], dy, "tn")
            dx_ref[g] = _dot(dy, w_ref[g], "nt")

    n_g, n_c, n_k = u_g.shape
    return _group_call(body, [u_g, d_y, xprev, wout],
                       [_sds((n_g, n_k, n_k)), _sds(wout.shape), _sds(xprev.shape)], gb=gb, name="s5_outputs_bwd")


def _s5_inputs_bwd(u_g, d_y, d_s, toep, wend, *, gb=16):
    def body(u_ref, dy_ref, ds_ref, t_ref, w_ref, du_ref, dw_ref):
        for g in range(gb):
            ds = ds_ref[g]
            du_ref[g] = _dot(dy_ref[g], t_ref[g], "nt") + _dot(ds, w_ref[g], "nt")
            dw_ref[g] = _dot(u_ref[g], ds, "tn")

    return _group_call(body, [u_g, d_y, d_s, toep, wend], [_sds(u_g.shape), _sds(wend.shape)], gb=gb,
                       name="s5_inputs_bwd")


def _swap_halves(x):
    return pltpu.roll(x, x.shape[-1] // 2, axis=x.ndim - 1)


def _s5_scan(s_t, a1, a2, *, gb=64):
    n_c, n_g, n_p = s_t.shape
    gb = min(gb, n_g)

    def body(s_ref, a1_ref, a2_ref, x_ref):
        a1v, a2v = a1_ref[...], a2_ref[...]
        a2s = _swap_halves(a2v)

        def step(c, carry):
            x, xs = carry
            x_ref[c] = x
            s = s_ref[c]
            return x * a1v + xs * a2v + s, xs * a1v + x * a2s + _swap_halves(s)

        zero = jnp.zeros((gb, n_p), F32)
        lax.fori_loop(0, n_c, step, (zero, zero), unroll=4 if n_c % 4 == 0 else 1)

    return pl.pallas_call(
        body, name="s5_scan", grid=(n_g // gb,),
        in_specs=[pl.BlockSpec((n_c, gb, n_p), lambda i: (0, i, 0)), pl.BlockSpec((gb, n_p), lambda i: (i, 0)),
                  pl.BlockSpec((gb, n_p), lambda i: (i, 0))],
        out_specs=pl.BlockSpec((n_c, gb, n_p), lambda i: (0, i, 0)), out_shape=_sds(s_t.shape),
        compiler_params=_params(("parallel",)),
    )(s_t, a1, a2)


def _s5_scan_bwd(d_xprev_t, xprev_t, a1, a2, *, gb=64):
    n_c, n_g, n_p = xprev_t.shape
    gb = min(gb, n_g)

    def body(dx_ref, x_ref, a1_ref, a2_ref, ds_ref, p1_ref, p2_ref):
        a1v, a2v = a1_ref[...], a2_ref[...]
        a2s = _swap_halves(a2v)
        zero = jnp.zeros((gb, n_p), F32)
        ds_ref[n_c - 1] = zero

        def step(k, carry):
            gx_next, gs_next, p1, p2 = carry
            c = n_c - 2 - k
            xp = x_ref[c + 1]
            d = dx_ref[c + 1]
            gx = d + gx_next * a1v - gs_next * a2v
            gs = _swap_halves(d) + gs_next * a1v - gx_next * a2s
            ds_ref[c] = gx
            return gx, gs, p1 + gx_next * xp, p2 + gx_next * _swap_halves(xp)

        _, _, p1, p2 = lax.fori_loop(0, n_c - 1, step, (zero, zero, zero, zero),
                                     unroll=5 if (n_c - 1) % 5 == 0 else 1)
        p1_ref[...] = p1
        p2_ref[...] = p2

    blk = pl.BlockSpec((n_c, gb, n_p), lambda i: (0, i, 0))
    vec = pl.BlockSpec((gb, n_p), lambda i: (i, 0))
    return pl.pallas_call(
        body, name="s5_scan_bwd", grid=(n_g // gb,), in_specs=[blk, blk, vec, vec],
        out_specs=[blk, vec, vec], out_shape=[_sds(xprev_t.shape), _sds(a1.shape), _sds(a1.shape)],
        compiler_params=_params(("parallel",)),
    )(d_xprev_t, xprev_t, a1, a2)


GROUPS_PER_TILE = LANES // S5_GROUP


def _to_groups(t, n_g):
    n_l = t.shape[0]
    n_c = n_l // S5_T
    gpt = min(GROUPS_PER_TILE, n_g)
    width = gpt * S5_GROUP

    def body(x_ref, o_ref):
        tr = [x_ref[pl.ds(s, n_c, stride=S5_T), :].T for s in range(S5_T)]
        for gl in range(gpt):
            rows = slice(gl * S5_GROUP, (gl + 1) * S5_GROUP)
            stacked = jnp.concatenate([tr[s][rows, :] for s in range(S5_T)], axis=0)
            o_ref[gl] = stacked.T.astype(o_ref.dtype)

    return pl.pallas_call(
        body, name="s5_to_groups", grid=(n_g // gpt,),
        in_specs=[pl.BlockSpec((n_l, width), lambda b: (0, b))],
        out_specs=pl.BlockSpec((gpt, n_c, S5_T * S5_GROUP), lambda b: (b, 0, 0)),
        out_shape=_sds((n_g, n_c, S5_T * S5_GROUP), BF16), compiler_params=_params(("parallel",)),
    )(t)


def _from_groups(t, n_l):
    n_g, n_c = t.shape[0], t.shape[1]
    gpt = min(GROUPS_PER_TILE, n_g)
    width = gpt * S5_GROUP

    def body(y_ref, o_ref):
        ytr = [y_ref[gl].T for gl in range(gpt)]
        for s in range(S5_T):
            rows = slice(s * S5_GROUP, (s + 1) * S5_GROUP)
            piece = jnp.concatenate([ytr[gl][rows, :] for gl in range(gpt)], axis=0)
            o_ref[pl.ds(s, n_c, stride=S5_T), :] = piece.T

    return pl.pallas_call(
        body, name="s5_from_groups", grid=(n_g // gpt,),
        in_specs=[pl.BlockSpec((gpt, n_c, S5_T * S5_GROUP), lambda b: (b, 0, 0))],
        out_specs=pl.BlockSpec((n_l, width), lambda b: (0, b)),
        out_shape=_sds((n_l, n_g * S5_GROUP)), compiler_params=_params(("parallel",)),
    )(t)


def _s5_act(ys, uz, d_skip):
    di = d_skip.shape[1]
    return jax.nn.gelu(ys + d_skip * uz[:, :di])


def _s5_gate(g1, glu_pre, uz, b_glu):
    di = b_glu.shape[1]
    return g1 * jax.nn.sigmoid(glu_pre + b_glu) * jax.nn.silu(uz[:, di:])


def _s5_fwd(h, p):
    n_l = h.shape[0]
    di = p["d_skip"].shape[1]
    n_g = di // S5_GROUP
    hn = _rms_fwd(h, p["norm_g"], name="s5_rms")
    uz = _mm(hn, p["w_in"], name="s5_in")
    toep, wend, wout, a1, a2 = p["ops"]
    u_g = _to_groups(uz, n_g)
    s = _s5_states(u_g, wend)
    xprev = _s5_scan(s.transpose(1, 0, 2), a1, a2).transpose(1, 0, 2)
    ys = _from_groups(_s5_outputs(u_g, toep, xprev, wout), n_l)
    g1 = _rowwise(_s5_act, [ys, uz], [p["d_skip"]], [_sds((n_l, di), BF16)], [], tm=512, name="s5_act")[0]
    glu_pre = _mm(g1, p["w_glu"], name="s5_glu")

    def gate(ys_t, pre_t, uz_t, d_skip, b_glu):
        return _s5_gate(_s5_act(ys_t, uz_t, d_skip), pre_t, uz_t, b_glu)

    gated = _rowwise(gate, [ys, glu_pre, uz], [p["d_skip"], p["b_glu"]], [_sds((n_l, di), BF16)], [],
                     tm=512, name="s5_gate")[0]
    h_next = _mm(gated, p["w_out"], add=h, name="s5_out")
    return h_next, (h, hn, uz, u_g, xprev, ys, g1, glu_pre, gated)


def _s5_bwd(dh_out, p, saved, ops_vjp):
    h, hn, uz, u_g, xprev, ys, g1, glu_pre, gated = saved
    n_l = h.shape[0]
    di = p["d_skip"].shape[1]
    n_g = di // S5_GROUP
    toep, wend, wout, a1, a2 = p["ops"]
    d_gated = _mm(dh_out, p["w_out"], mode="nt", out_dtype=BF16, name="s5_dgated")
    g_w_out = _mm(gated, dh_out, mode="tn", name="s5_dwout")

    def gate_bwd(ys_t, pre_t, uz_t, ct, d_skip, b_glu):
        g1_t = _s5_act(ys_t, uz_t, d_skip)
        _, vjp = jax.vjp(_s5_gate, g1_t, pre_t, uz_t, b_glu)
        d_g1, d_pre, d_uz, d_b = vjp(ct.astype(F32))
        return d_g1, d_pre, d_uz, d_b

    d_g1_direct, d_pre, d_uz_gate, g_b_glu = _rowwise(
        gate_bwd, [ys, glu_pre, uz, d_gated], [p["d_skip"], p["b_glu"]],
        [_sds((n_l, di)), _sds((n_l, di), BF16), _sds(uz.shape)], [_sds(p["b_glu"].shape)], tm=256, name="s5_gate_bwd")
    g_w_glu = _mm(g1, d_pre, mode="tn", name="s5_dwglu")
    d_g1 = _mm(d_pre, p["w_glu"], mode="nt", add=d_g1_direct, name="s5_dg1")

    def act_bwd(ys_t, uz_t, ct, d_uz_t, d_skip):
        _, vjp = jax.vjp(_s5_act, ys_t, uz_t, d_skip)
        d_ys, d_uz, d_d = vjp(ct)
        return d_ys, d_uz + d_uz_t, d_d

    d_ys, d_uz_part, g_d_skip = _rowwise(
        act_bwd, [ys, uz, d_g1, d_uz_gate], [p["d_skip"]], [_sds((n_l, di)), _sds(uz.shape)],
        [_sds(p["d_skip"].shape)], tm=256, name="s5_act_bwd")
    d_y = _to_groups(d_ys, n_g)
    d_toep, d_wout, d_xprev = _s5_outputs_bwd(u_g, d_y, xprev, wout)
    d_s_t, p1, p2 = _s5_scan_bwd(d_xprev.transpose(1, 0, 2), xprev.transpose(1, 0, 2), a1, a2)
    d_s = d_s_t.transpose(1, 0, 2)
    d_u_g, d_wend = _s5_inputs_bwd(u_g, d_y, d_s, toep, wend)
    d_u = _from_groups(d_u_g, n_l)
    d_uz = _rowwise(lambda part, du: jnp.concatenate([part[:, :di] + du, part[:, di:]], axis=1),
                    [d_uz_part, d_u], [], [_sds(uz.shape, BF16)], [], tm=512, name="s5_duz")[0]
    g_w_in = _mm(hn, d_uz, mode="tn", name="s5_dwin")
    d_hn = _mm(d_uz, p["w_in"], mode="nt", name="s5_dhn")
    dh_in, g_norm = _rms_bwd(h, p["norm_g"], d_hn, dh_out, name="s5_rms_bwd")
    g_ops = ops_vjp((d_toep, d_wend, d_wout, p1, p2))
    grads = dict(norm_g=g_norm, w_in=g_w_in, a_re=g_ops[0], a_im=g_ops[1], log_step=g_ops[2], b_re=g_ops[3],
                 b_im=g_ops[4], c_re=g_ops[5], c_im=g_ops[6], d_skip=g_d_skip, w_glu=g_w_glu, b_glu=g_b_glu,
                 w_out=g_w_out)
    return dh_in, grads


MLA_Z0 = MLA_Q_RANK + MLA_KV_RANK + LANES


def _rope_tile(t, cos_t, sin_t):
    q = LANES // 4
    lane = lax.broadcasted_iota(jnp.int32, t.shape, 1)
    swapped = jnp.where(lane < q, pltpu.roll(t, LANES - q, axis=1), pltpu.roll(t, q, axis=1))
    return t * cos_t + swapped * sin_t


def _mla_mid(proj, cos_t, sin_t, q_g, kv_g):
    cqn = _rms(proj[:, :MLA_Q_RANK], q_g)
    ckvn = _rms(proj[:, MLA_Q_RANK:MLA_Q_RANK + MLA_KV_RANK], kv_g)
    kr = _rope_tile(proj[:, MLA_Q_RANK + MLA_KV_RANK:MLA_Z0], cos_t, sin_t)
    return cqn, ckvn, kr


def _mla_rope_q(qp, cos_t, sin_t):
    parts = []
    for hd in range(qp.shape[1] // MLA_HEAD_PAD):
        base = hd * MLA_HEAD_PAD
        parts.append(qp[:, base:base + LANES])
        parts.append(_rope_tile(qp[:, base + LANES:base + MLA_HEAD_PAD], cos_t, sin_t))
    return jnp.concatenate(parts, axis=1)


def _mla_gate(o, proj):
    return o * jax.nn.silu(proj[:, MLA_Z0:])


LOG2E = math.log2(math.e)
SCORE_LOG2 = MLA_SCALE * LOG2E
FLASH_SPLIT = 4


def _causal_pairs(n_blk, kv_major):
    if kv_major:
        pairs = [(i, j) for j in range(n_blk) for i in range(j, n_blk)]
    else:
        pairs = [(i, j) for i in range(n_blk) for j in range(i + 1)]
    return (jnp.asarray([p[0] for p in pairs], jnp.int32), jnp.asarray([p[1] for p in pairs], jnp.int32))


def _raw_scores(q, kcat, row0, diagonal):
    s = _dot(q, kcat, "nt")
    if diagonal:
        qpos = row0 + lax.broadcasted_iota(jnp.int32, s.shape, 0)
        kpos = lax.broadcasted_iota(jnp.int32, s.shape, 1)
        s = jnp.where(kpos <= qpos, s, NEG_INF)
    return s


def _lanes(x, width):
    return jnp.tile(x, (1, width // LANES))


def _flash_fwd(qp, kv, kr, *, blk=1024):
    n_l = qp.shape[0]
    heads = qp.shape[1] // MLA_HEAD_PAD
    blk = _pick(n_l, blk)
    n_blk = n_l // blk
    half = blk // FLASH_SPLIT
    qi, kj = _causal_pairs(n_blk, kv_major=False)

    def body(qi_ref, kj_ref, q_ref, kv_ref, kr_ref, o_ref, lse_ref, m_sc, l_sc, acc_sc):
        p = pl.program_id(1)
        i, j = qi_ref[p], kj_ref[p]

        @pl.when(j == 0)
        def _():
            m_sc[...] = jnp.full_like(m_sc, NEG_INF)
            l_sc[...] = jnp.zeros_like(l_sc)
            acc_sc[...] = jnp.zeros_like(acc_sc)

        def update(diagonal):
            kcat = jnp.concatenate([kv_ref[:, :LANES], kr_ref[...]], axis=1)
            v = kv_ref[:, LANES:]
            for r in range(FLASH_SPLIT):
                rows = slice(r * half, (r + 1) * half)
                s = _raw_scores(q_ref[rows, :], kcat, r * half, diagonal)
                m_old = m_sc[rows, :]
                m_new = jnp.maximum(m_old, jnp.max(s, axis=1, keepdims=True))
                alpha = jnp.exp2((m_old - m_new) * SCORE_LOG2)
                pr = jnp.exp2((s - _lanes(m_new, blk)) * SCORE_LOG2)
                l_sc[rows, :] = alpha * l_sc[rows, :] + jnp.sum(pr, axis=1, keepdims=True)
                acc_sc[rows, :] = alpha * acc_sc[rows, :] + _dot(pr, v)
                m_sc[rows, :] = m_new

        @pl.when(j < i)
        def _():
            update(False)

        @pl.when(j == i)
        def _():
            update(True)
            o_ref[...] = acc_sc[...] / l_sc[...]
            lse_ref[...] = m_sc[...] * MLA_SCALE + jnp.log(l_sc[...])

    grid_spec = pltpu.PrefetchScalarGridSpec(
        num_scalar_prefetch=2, grid=(heads, qi.shape[0]),
        in_specs=[pl.BlockSpec((blk, MLA_HEAD_PAD), lambda h, p, qi_r, kj_r: (qi_r[p], h)),
                  pl.BlockSpec((blk, MLA_HEAD_PAD), lambda h, p, qi_r, kj_r: (kj_r[p], h)),
                  pl.BlockSpec((blk, LANES), lambda h, p, qi_r, kj_r: (kj_r[p], 0))],
        out_specs=[pl.BlockSpec((blk, MLA_V), lambda h, p, qi_r, kj_r: (qi_r[p], h)),
                   pl.BlockSpec((None, blk, LANES), lambda h, p, qi_r, kj_r: (h, qi_r[p], 0))],
        scratch_shapes=[pltpu.VMEM((blk, LANES), F32), pltpu.VMEM((blk, LANES), F32), pltpu.VMEM((blk, MLA_V), F32)])
    return pl.pallas_call(
        body, name="mla_flash_fwd", grid_spec=grid_spec,
        out_shape=[_sds((n_l, heads * MLA_V)), _sds((heads, n_l, LANES))],
        compiler_params=_params(("parallel", "arbitrary")),
    )(qi, kj, qp, kv, kr)


def _flash_bwd(qp, kv, kr, d_o, lse, delta, *, blk=1024):
    n_l = qp.shape[0]
    heads = qp.shape[1] // MLA_HEAD_PAD
    blk = _pick(n_l, blk)
    n_blk = n_l // blk
    half = blk // FLASH_SPLIT
    qi, kj = _causal_pairs(n_blk, kv_major=True)
    n_pairs = qi.shape[0]

    def body(qi_ref, kj_ref, q_ref, kv_ref, kr_ref, do_ref, lse_ref, dl_ref, dq_ref, dkv_ref, dkr_ref, dk_sc, dv_sc):
        h, p = pl.program_id(0), pl.program_id(1)
        i, j = qi_ref[p], kj_ref[p]

        @pl.when(p == 0)
        def _():
            dq_ref[...] = jnp.zeros_like(dq_ref)

        @pl.when(jnp.logical_and(p == 0, h == 0))
        def _():
            dkr_ref[...] = jnp.zeros_like(dkr_ref)

        @pl.when(i == j)
        def _():
            dk_sc[...] = jnp.zeros_like(dk_sc)
            dv_sc[...] = jnp.zeros_like(dv_sc)

        def update(diagonal):
            kcat = jnp.concatenate([kv_ref[:, :LANES], kr_ref[...]], axis=1)
            v = kv_ref[:, LANES:]
            for r in range(FLASH_SPLIT):
                rows = slice(r * half, (r + 1) * half)
                q_t, do_t = q_ref[rows, :], do_ref[rows, :]
                s = _raw_scores(q_t, kcat, r * half, diagonal)
                pr = jnp.exp2(s * SCORE_LOG2 - _lanes(lse_ref[rows, :] * LOG2E, blk))
                d_p = _dot(do_t, v, "nt")
                d_s = pr * (d_p - _lanes(dl_ref[rows, :], blk))
                dk_sc[...] += _dot(d_s, q_t, "tn")
                dv_sc[...] += _dot(pr, do_t, "tn")
                q_rows = pl.ds(pl.multiple_of(i * blk + r * half, half), half)
                dq_ref[q_rows, :] += _dot(d_s, kcat)

        @pl.when(i > j)
        def _():
            update(False)

        @pl.when(i == j)
        def _():
            update(True)

        @pl.when(i == n_blk - 1)
        def _():
            dk = dk_sc[...] * MLA_SCALE
            dkv_ref[:, :LANES] = dk[:, :LANES].astype(dkv_ref.dtype)
            dkv_ref[:, LANES:] = dv_sc[...].astype(dkv_ref.dtype)
            k_rows = pl.ds(pl.multiple_of(j * blk, blk), blk)
            dkr_ref[k_rows, :] += dk[:, LANES:]

        @pl.when(p == n_pairs - 1)
        def _():
            dq_ref[...] = dq_ref[...] * MLA_SCALE

    at_q = lambda h, p, qi_r, kj_r: (qi_r[p], h)
    at_kv = lambda h, p, qi_r, kj_r: (kj_r[p], h)
    grid_spec = pltpu.PrefetchScalarGridSpec(
        num_scalar_prefetch=2, grid=(heads, n_pairs),
        in_specs=[pl.BlockSpec((blk, MLA_HEAD_PAD), at_q),
                  pl.BlockSpec((blk, MLA_HEAD_PAD), at_kv),
                  pl.BlockSpec((blk, LANES), lambda h, p, qi_r, kj_r: (kj_r[p], 0)),
                  pl.BlockSpec((blk, MLA_V), at_q),
                  pl.BlockSpec((None, blk, LANES), lambda h, p, qi_r, kj_r: (h, qi_r[p], 0)),
                  pl.BlockSpec((blk, LANES), at_q)],
        out_specs=[pl.BlockSpec((n_l, MLA_HEAD_PAD), lambda h, p, qi_r, kj_r: (0, h)),
                   pl.BlockSpec((blk, MLA_HEAD_PAD), at_kv),
                   pl.BlockSpec((n_l, LANES), lambda h, p, qi_r, kj_r: (0, 0))],
        scratch_shapes=[pltpu.VMEM((blk, MLA_HEAD_PAD), F32), pltpu.VMEM((blk, MLA_V), F32)])
    return pl.pallas_call(
        body, name="mla_flash_bwd", grid_spec=grid_spec,
        out_shape=[_sds(qp.shape), _sds(kv.shape, BF16), _sds(kr.shape)],
        compiler_params=_params(("arbitrary", "arbitrary")),
    )(qi, kj, qp, kv, kr, d_o, lse, delta)


def _mla_fwd(h, p, rope):
    n_l = h.shape[0]
    cos_t, sin_t = rope
    hn = _rms_fwd(h, p["norm_g"], name="mla_rms")
    proj = _mm(hn, p["w_in"], name="mla_in", tn=896)
    cqn, ckvn, kr = _rowwise(_mla_mid, [proj, cos_t, sin_t], [p["q_norm_g"], p["kv_norm_g"]],
                             [_sds((n_l, MLA_Q_RANK), BF16), _sds((n_l, MLA_KV_RANK), BF16), _sds((n_l, LANES), BF16)],
                             [], tm=512, name="mla_mid")
    q_raw = _mm(cqn, p["w_uq"], name="mla_uq")
    qp = _rowwise(_mla_rope_q, [q_raw, cos_t, sin_t], [], [_sds(q_raw.shape, BF16)], [], tm=512, name="mla_rope_q")[0]
    kv = _mm(ckvn, p["w_ukv"], out_dtype=BF16, name="mla_ukv")
    o, lse = _flash_fwd(qp, kv, kr)
    gated = _rowwise(_mla_gate, [o, proj], [], [_sds(o.shape, BF16)], [], tm=512, name="mla_gate")[0]
    h_next = _mm(gated, p["w_out"], add=h, name="mla_out")
    return h_next, (h, hn, proj, cqn, ckvn, kr, qp, kv, o, lse, gated)


def _mla_bwd(dh_out, p, saved, rope):
    h, hn, proj, cqn, ckvn, kr, qp, kv, o, lse, gated = saved
    n_l = h.shape[0]
    cos_t, sin_t = rope
    d_gated = _mm(dh_out, p["w_out"], mode="nt", out_dtype=BF16, name="mla_dgated")
    g_w_out = _mm(gated, dh_out, mode="tn", name="mla_dwout")

    def gate_bwd(o_t, proj_t, ct):
        _, vjp = jax.vjp(lambda a, z: a * jax.nn.silu(z), o_t, proj_t[:, MLA_Z0:])
        d_o_t, d_z_t = vjp(ct.astype(F32))
        prod = d_o_t * o_t
        delta = jnp.concatenate(
            [jnp.broadcast_to(jnp.sum(prod[:, hd * MLA_V:(hd + 1) * MLA_V], axis=1, keepdims=True),
                              (prod.shape[0], MLA_V)) for hd in range(prod.shape[1] // MLA_V)], axis=1)
        return d_o_t, d_z_t, delta

    d_o, d_z, delta = _rowwise(gate_bwd, [o, proj, d_gated], [], [_sds(o.shape, BF16), _sds(o.shape), _sds(o.shape)],
                               [], tm=512, name="mla_gate_bwd")
    d_qp, d_kv, d_kr = _flash_bwd(qp, kv, kr, d_o, lse, delta)

    def rope_q_bwd(ct, c_t, s_t):
        return _mla_rope_q(ct, c_t, -s_t)

    d_q_raw = _rowwise(rope_q_bwd, [d_qp, cos_t, sin_t], [], [_sds(d_qp.shape, BF16)], [], tm=512,
                       name="mla_rope_q_bwd")[0]
    g_w_uq = _mm(cqn, d_q_raw, mode="tn", name="mla_dwuq")
    d_cqn = _mm(d_q_raw, p["w_uq"], mode="nt", name="mla_dcqn")
    g_w_ukv = _mm(ckvn, d_kv, mode="tn", name="mla_dwukv")
    d_ckvn = _mm(d_kv, p["w_ukv"], mode="nt", name="mla_dckvn")

    def mid_bwd(proj_t, c_t, s_t, d_cq, d_ckv, d_kr_t, d_z_t, q_g, kv_g):
        _, vjp_q = jax.vjp(_rms, proj_t[:, :MLA_Q_RANK], q_g)
        _, vjp_kv = jax.vjp(_rms, proj_t[:, MLA_Q_RANK:MLA_Q_RANK + MLA_KV_RANK], kv_g)
        d_q_in, d_qg = vjp_q(d_cq)
        d_kv_in, d_kvg = vjp_kv(d_ckv)
        d_kr_in = _rope_tile(d_kr_t, c_t, -s_t)
        return jnp.concatenate([d_q_in, d_kv_in, d_kr_in, d_z_t], axis=1), d_qg, d_kvg

    d_proj, g_q_norm, g_kv_norm = _rowwise(
        mid_bwd, [proj, cos_t, sin_t, d_cqn, d_ckvn, d_kr, d_z], [p["q_norm_g"], p["kv_norm_g"]],
        [_sds(proj.shape, BF16)], [_sds(p["q_norm_g"].shape), _sds(p["kv_norm_g"].shape)], tm=512, name="mla_mid_bwd")
    g_w_in = _mm(hn, d_proj, mode="tn", name="mla_dwin", tn=896)
    d_hn = _mm(d_proj, p["w_in"], mode="nt", name="mla_dhn", tk=896)
    dh_in, g_norm = _rms_bwd(h, p["norm_g"], d_hn, dh_out, name="mla_rms_bwd")
    grads = dict(norm_g=g_norm, w_in=g_w_in, q_norm_g=g_q_norm, w_uq=g_w_uq, kv_norm_g=g_kv_norm, w_ukv=g_w_ukv,
                 w_out=g_w_out)
    return dh_in, grads


def _loss_head(h, g, target):
    def fn(x, t, gg):
        def local(xx, g2):
            err = _rms(xx, g2) - t
            return 0.5 * jnp.sum(jnp.mean(err * err, axis=-1))

        val, (dx, dg) = jax.value_and_grad(local, argnums=(0, 1))(x, gg)
        return dx, jnp.full((1, LANES), val, F32), dg

    dh, loss, dg = _rowwise(fn, [h, target], [g], [_sds(h.shape)], [_sds((1, LANES)), _sds(g.shape)], tm=512,
                            name="loss_head")
    return loss[0, 0], dh, dg


HBM_SPEC = pl.BlockSpec(memory_space=pltpu.HBM)


AG_COPIES = 7


def _all_gather(shards, *, name):
    n_arr = len(shards)

    def body(*refs):
        x_refs, out_refs = refs[:n_arr], refs[n_arr:2 * n_arr]
        send_sems, recv_sems, local_sems = refs[2 * n_arr:]
        x, y, c = lax.axis_index("x"), lax.axis_index("y"), lax.axis_index("c")
        me, sibling = (x, y, c), (x, y, 1 - c)
        chips = [(1 - x, y), (x, 1 - y), (1 - x, 1 - y)]

        def block_of(t, px, py, pc):
            return out_refs[t].at[4 * px + 2 * py + pc]

        def copy(t, k, block, to, src=None):
            return pltpu.make_async_remote_copy(
                src_ref=block_of(t, *block) if src is None else src, dst_ref=block_of(t, *block),
                send_sem=send_sems.at[t * AG_COPIES + k], recv_sem=recv_sems.at[t * AG_COPIES + k],
                device_id=to, device_id_type=MESH)

        mine = [pltpu.make_async_copy(x_refs[t], block_of(t, *me), local_sems.at[t]) for t in range(n_arr)]
        for cp in mine:
            cp.start()
        first = []
        for t in range(n_arr):
            first.append(copy(t, 0, me, sibling, src=x_refs[t]))
            first += [copy(t, 1 + j, me, (*chip, c), src=x_refs[t]) for j, chip in enumerate(chips)]
        for cp in first:
            cp.start()
        passed = []
        for j, chip in enumerate(chips):
            for t in range(n_arr):
                copy(t, 1 + j, (*chip, c), me).wait_recv()
                passed.append(copy(t, 4 + j, (*chip, c), sibling))
                passed[-1].start()
        for t in range(n_arr):
            copy(t, 0, sibling, me).wait_recv()
        for j, chip in enumerate(chips):
            for t in range(n_arr):
                copy(t, 4 + j, (*chip, 1 - c), me).wait_recv()
        for cp in first + passed:
            cp.wait_send()
        for cp in mine:
            cp.wait()

    return pl.pallas_call(
        body, name=name, out_shape=[jax.ShapeDtypeStruct((N_DEV,) + s.shape, s.dtype) for s in shards],
        in_specs=[HBM_SPEC] * n_arr, out_specs=[HBM_SPEC] * n_arr,
        scratch_shapes=[pltpu.SemaphoreType.DMA((n_arr * AG_COPIES,)), pltpu.SemaphoreType.DMA((n_arr * AG_COPIES,)),
                        pltpu.SemaphoreType.DMA((n_arr,))],
    )(*shards)


def _exchange(src, routes, *, name):
    n_routes = len(routes)

    def body(s_ref, out_ref, send_sems, recv_sems):
        x, y, c = lax.axis_index("x"), lax.axis_index("y"), lax.axis_index("c")
        local, remote = [], []
        for k, (flip, block) in enumerate(routes):
            src_blk = s_ref.at[block(x, y, c)]
            if flip == 0:
                local.append(pltpu.make_async_copy(src_blk, out_ref.at[k], send_sems.at[k]))
            else:
                peer = (1 - x if flip & 4 else x, 1 - y if flip & 2 else y, 1 - c if flip & 1 else c)
                remote.append(pltpu.make_async_remote_copy(
                    src_ref=src_blk, dst_ref=out_ref.at[k], send_sem=send_sems.at[k], recv_sem=recv_sems.at[k],
                    device_id=peer, device_id_type=MESH))
        for cp in local + remote:
            cp.start()
        for cp in remote:
            cp.wait_recv()
        for cp in remote:
            cp.wait_send()
        for cp in local:
            cp.wait()

    return pl.pallas_call(
        body, name=name, out_shape=jax.ShapeDtypeStruct((n_routes,) + src.shape[1:], src.dtype),
        in_specs=[HBM_SPEC], out_specs=HBM_SPEC,
        scratch_shapes=[pltpu.SemaphoreType.DMA((n_routes,)), pltpu.SemaphoreType.DMA((n_routes,))],
    )(src)


def _reduce_scatter(send, *, name):
    def chip_block(k, other_core):
        return lambda x, y, c: (4 * (1 - x if k & 2 else x) + 2 * (1 - y if k & 1 else y)
                                + (1 - c if other_core else c))

    n_chips = 4
    pair = _exchange(send, [(1, chip_block(k, True)) for k in range(n_chips)], name=name + "_pair")
    rows, width = send.shape[1], send.shape[2]
    tr = _pick(rows, 256, 8)
    x, y, c = lax.axis_index("x"), lax.axis_index("y"), lax.axis_index("c")
    own_ids = jnp.stack([chip_block(k, False)(x, y, c) for k in range(n_chips)]).astype(jnp.int32)

    def add_body(ids_ref, *refs):
        own_refs, p_ref, o_ref = refs[:n_chips], refs[n_chips], refs[n_chips + 1]
        for k in range(n_chips):
            o_ref[k] = (own_refs[k][...].astype(F32) + p_ref[k].astype(F32)).astype(o_ref.dtype)

    own_spec = lambda k: pl.BlockSpec((None, tr, width), lambda i, ids: (ids[k], i, 0))
    chip_sums = pl.pallas_call(
        add_body, name=name + "_pair_sum",
        grid_spec=pltpu.PrefetchScalarGridSpec(
            num_scalar_prefetch=1, grid=(rows // tr,),
            in_specs=[own_spec(k) for k in range(n_chips)] + [pl.BlockSpec((n_chips, tr, width), lambda i, ids: (0, i, 0))],
            out_specs=pl.BlockSpec((n_chips, tr, width), lambda i, ids: (0, i, 0))),
        out_shape=jax.ShapeDtypeStruct((n_chips, rows, width), send.dtype), compiler_params=_params(("parallel",)),
    )(own_ids, *([send] * n_chips), pair)
    recv = _exchange(chip_sums, [(2 * k, (lambda kk: lambda x, y, c: kk)(k)) for k in range(1, n_chips)],
                     name=name + "_chips")

    def sum_body(q_ref, r_ref, o_ref):
        acc = q_ref[...].astype(F32)
        for k in range(n_chips - 1):
            acc = acc + r_ref[k].astype(F32)
        o_ref[...] = acc

    return pl.pallas_call(
        sum_body, name=name + "_sum", grid=(rows // tr,),
        in_specs=[pl.BlockSpec((None, tr, width), lambda i: (0, i, 0)),
                  pl.BlockSpec((n_chips - 1, tr, width), lambda i: (0, i, 0))],
        out_specs=pl.BlockSpec((tr, width), lambda i: (i, 0)), out_shape=_sds((rows, width)),
        compiler_params=_params(("parallel",)),
    )(chip_sums, recv)


def _adamw(w, g, m, v, *, name):
    def fn(ww, gg, mm, vv):
        m_new = ADAM_B1 * mm + (1.0 - ADAM_B1) * gg
        v_new = ADAM_B2 * vv + (1.0 - ADAM_B2) * jnp.square(gg)
        m_hat = m_new / (1.0 - ADAM_B1 ** ADAM_STEP)
        v_hat = v_new / (1.0 - ADAM_B2 ** ADAM_STEP)
        return -ADAM_LR * (m_hat / (jnp.sqrt(v_hat) + ADAM_EPS) + ADAM_WD * ww), m_new, v_new

    return _rowwise(fn, [w, g, m, v], [], [_sds(w.shape)] * 3, [], tm=256, name=name)


KINDS = ("gmlp", "s5", "mla", "gmlp")
LAYER_NAMES = {
    "gmlp": ("norm_g", "w_in", "ln_g", "ln_b", "w_s", "b_s", "w_out"),
    "s5": ("norm_g", "w_in", "a_re", "a_im", "log_step", "b_re", "b_im", "c_re", "c_im", "d_skip", "w_glu", "b_glu",
           "w_out"),
    "mla": ("norm_g", "w_in", "q_norm_g", "w_uq", "kv_norm_g", "w_ukv", "w_out"),
}
COL_SHARDED = ("w_in", "w_uq", "w_ukv")
ROW_SHARDED = ("w_out", "w_glu")
WEIGHT_NAMES = tuple("l%d_%s" % (i, n) for i, k in enumerate(KINDS) for n in LAYER_NAMES[k]) + ("final_norm_g",)


def _is_sharded(name):
    return name.split("_", 1)[1] in COL_SHARDED + ROW_SHARDED


def _flatten(arrs, pad_rows_to):
    parts, sizes = [], []
    for a in arrs:
        flat = a.reshape(-1)
        pad = (-flat.shape[0]) % FLAT_W
        if pad:
            flat = jnp.pad(flat, (0, pad))
        parts.append(flat)
        sizes.append(flat.shape[0] // FLAT_W)
    rows = sum(sizes)
    pad_rows = (-rows) % pad_rows_to
    if pad_rows:
        parts.append(jnp.zeros((pad_rows * FLAT_W,), arrs[0].dtype))
    return jnp.concatenate(parts).reshape(-1, FLAT_W), sizes


def _unflatten(flat, shapes, sizes):
    out, row = [], 0
    for shape, n_rows in zip(shapes, sizes):
        n = int(np.prod(shape))
        out.append(flat[row:row + n_rows].reshape(-1)[:n].reshape(shape))
        row += n_rows
    return out


def _full_from_gathered(blocks, name):
    if name.split("_", 1)[1] in COL_SHARDED:
        return blocks.transpose(1, 0, 2).reshape(blocks.shape[1], -1)
    return blocks.reshape(-1, blocks.shape[2])


def _shards_of(full, name):
    if name.split("_", 1)[1] in COL_SHARDED:
        return full.reshape(full.shape[0], N_DEV, -1).transpose(1, 0, 2)
    return full.reshape(N_DEV, -1, full.shape[1])


def _rope_tables(positions):
    inv_freq = ROPE_THETA ** (-jnp.arange(0, MLA_ROPE, 2, dtype=F32) / MLA_ROPE)
    ang = positions.astype(F32)[:, None] * inv_freq
    cos, sin = jnp.cos(ang), jnp.sin(ang)
    zero = jnp.zeros((positions.shape[0], LANES - MLA_ROPE), F32)
    return jnp.concatenate([cos, cos, zero], axis=1), jnp.concatenate([-sin, sin, zero], axis=1)


def _row(v):
    return v.reshape(1, -1)


def kernel(x, positions, l0_norm_g, l0_w_in, l0_ln_g, l0_ln_b, l0_w_s, l0_b_s, l0_w_out, l1_norm_g, l1_w_in, l1_a_re, l1_a_im, l1_log_step, l1_b_re, l1_b_im, l1_c_re, l1_c_im, l1_d_skip, l1_w_glu, l1_b_glu, l1_w_out, l2_norm_g, l2_w_in, l2_q_norm_g, l2_w_uq, l2_kv_norm_g, l2_w_ukv, l2_w_out, l3_norm_g, l3_w_in, l3_ln_g, l3_ln_b, l3_w_s, l3_b_s, l3_w_out, final_norm_g, loss_target, m_l0_norm_g, m_l0_w_in, m_l0_ln_g, m_l0_ln_b, m_l0_w_s, m_l0_b_s, m_l0_w_out, m_l1_norm_g, m_l1_w_in, m_l1_a_re, m_l1_a_im, m_l1_log_step, m_l1_b_re, m_l1_b_im, m_l1_c_re, m_l1_c_im, m_l1_d_skip, m_l1_w_glu, m_l1_b_glu, m_l1_w_out, m_l2_norm_g, m_l2_w_in, m_l2_q_norm_g, m_l2_w_uq, m_l2_kv_norm_g, m_l2_w_ukv, m_l2_w_out, m_l3_norm_g, m_l3_w_in, m_l3_ln_g, m_l3_ln_b, m_l3_w_s, m_l3_b_s, m_l3_w_out, m_final_norm_g, v_l0_norm_g, v_l0_w_in, v_l0_ln_g, v_l0_ln_b, v_l0_w_s, v_l0_b_s, v_l0_w_out, v_l1_norm_g, v_l1_w_in, v_l1_a_re, v_l1_a_im, v_l1_log_step, v_l1_b_re, v_l1_b_im, v_l1_c_re, v_l1_c_im, v_l1_d_skip, v_l1_w_glu, v_l1_b_glu, v_l1_w_out, v_l2_norm_g, v_l2_w_in, v_l2_q_norm_g, v_l2_w_uq, v_l2_kv_norm_g, v_l2_w_ukv, v_l2_w_out, v_l3_norm_g, v_l3_w_in, v_l3_ln_g, v_l3_ln_b, v_l3_w_s, v_l3_b_s, v_l3_w_out, v_final_norm_g):
    args = locals()
    weights = {n: args[n] for n in WEIGHT_NAMES}
    mom_m = {n: args["m_" + n] for n in WEIGHT_NAMES}
    mom_v = {n: args["v_" + n] for n in WEIGHT_NAMES}
    return _train_step(x, positions, loss_target, weights, mom_m, mom_v)


def _train_step(x, positions, loss_target, weights, mom_m, mom_v):
    big = [n for n in WEIGHT_NAMES if _is_sharded(n)]
    small = [n for n in WEIGHT_NAMES if not _is_sharded(n)]

    big_sizes = [weights[n].size // FLAT_W for n in big]
    gathered = _all_gather([weights[n].astype(BF16) for n in big], name="weights_all_gather")
    full = {n: _full_from_gathered(blocks, n) for n, blocks in zip(big, gathered)}

    layers, ops_vjps = [], {}
    for i, kind in enumerate(KINDS):
        pre = "l%d_" % i
        p = {n: (full[pre + n] if _is_sharded(pre + n) else weights[pre + n]) for n in LAYER_NAMES[kind]}
        p["norm_g"] = _row(p["norm_g"])
        if kind == "gmlp":
            p["ln_g"], p["ln_b"], p["b_st"] = _row(p["ln_g"]), _row(p["ln_b"]), p["b_s"].T
        elif kind == "s5":
            p["d_skip"], p["b_glu"] = _row(p["d_skip"]), _row(p["b_glu"])
            ops, ops_vjps[i] = jax.vjp(_s5_operators, *[p[n] for n in ("a_re", "a_im", "log_step", "b_re", "b_im",
                                                                       "c_re", "c_im")])
            p["ops"] = tuple(o.astype(BF16) for o in ops[:3]) + ops[3:]
        else:
            heads = p["w_uq"].shape[1] // MLA_QK_DIM
            w_in = p["w_in"]
            split = MLA_Q_RANK + MLA_KV_RANK + MLA_ROPE
            p["w_in"] = jnp.concatenate([w_in[:, :split], jnp.zeros((w_in.shape[0], LANES - MLA_ROPE), w_in.dtype),
                                         w_in[:, split:]], axis=1)
            p["w_uq"] = jnp.pad(p["w_uq"].reshape(-1, heads, MLA_QK_DIM),
                                ((0, 0), (0, 0), (0, MLA_HEAD_PAD - MLA_QK_DIM))).reshape(-1, heads * MLA_HEAD_PAD)
            p["q_norm_g"], p["kv_norm_g"] = _row(p["q_norm_g"]), _row(p["kv_norm_g"])
        layers.append(p)
    rope = _rope_tables(positions[0])

    h = x[0]
    saved = []
    for kind, p in zip(KINDS, layers):
        if kind == "gmlp":
            h, s = _gmlp_fwd(h, p)
        elif kind == "s5":
            h, s = _s5_fwd(h, p)
        else:
            h, s = _mla_fwd(h, p, rope)
        saved.append(s)
    loss_local, dh, g_final = _loss_head(h, _row(weights["final_norm_g"]), loss_target[0])
    loss = lax.psum(loss_local, ("x", "y", "c"))

    grads = {"final_norm_g": g_final.reshape(-1)}
    for i in reversed(range(len(KINDS))):
        kind, p = KINDS[i], layers[i]
        if kind == "gmlp":
            dh, g = _gmlp_bwd(dh, p, saved[i])
        elif kind == "s5":
            dh, g = _s5_bwd(dh, p, saved[i], ops_vjps[i])
        else:
            dh, g = _mla_bwd(dh, p, saved[i], rope)
            heads = weights["l%d_w_uq" % i].shape[1] * N_DEV // MLA_QK_DIM
            split = MLA_Q_RANK + MLA_KV_RANK + MLA_ROPE
            g["w_in"] = jnp.concatenate([g["w_in"][:, :split], g["w_in"][:, MLA_Z0:]], axis=1)
            g["w_uq"] = g["w_uq"].reshape(-1, heads, MLA_HEAD_PAD)[:, :, :MLA_QK_DIM].reshape(-1, heads * MLA_QK_DIM)
        for n, val in g.items():
            name = "l%d_%s" % (i, n)
            grads[name] = val.reshape(weights[name].shape) if not _is_sharded(name) else val

    small_flat, small_sizes = _flatten([grads[n] for n in small], 8 * N_DEV)
    small_rows = small_flat.shape[0] // N_DEV
    send_parts = [_shards_of(grads[n], n).reshape(N_DEV, -1, FLAT_W) for n in big]
    send_parts.append(small_flat.reshape(N_DEV, small_rows, FLAT_W))
    send = jnp.concatenate(send_parts, axis=1)
    big_rows = send.shape[1] - small_rows
    pad_rows = (-send.shape[1]) % 8
    if pad_rows:
        send = jnp.pad(send, ((0, 0), (0, pad_rows), (0, 0)))
    reduced = _reduce_scatter(send.astype(BF16), name="grads")
    g_big_flat = reduced[:big_rows]
    g_small_all = _all_gather([reduced[big_rows:big_rows + small_rows]], name="small_grads_all_gather")[0]
    g_small_flat = g_small_all.reshape(-1, FLAT_W)

    def flat_of(tree, names, pad_to):
        return _flatten([tree[n] for n in names], pad_to)[0]

    outs = {}
    big_shapes = [weights[n].shape for n in big]
    for n, g_n in zip(big, _unflatten(g_big_flat, big_shapes, big_sizes)):
        outs["grad_" + n] = g_n
        outs["delta_" + n], outs["new_m_" + n], outs["new_v_" + n] = _adamw(
            weights[n], g_n, mom_m[n], mom_v[n], name="adamw_" + n)
    d_s, nm_s, nv_s = _adamw(flat_of(weights, small, 8 * N_DEV), g_small_flat, flat_of(mom_m, small, 8 * N_DEV),
                             flat_of(mom_v, small, 8 * N_DEV), name="adamw_replicated")
    small_shapes = [weights[n].shape for n in small]
    for prefix, fs in (("grad_", g_small_flat), ("delta_", d_s), ("new_m_", nm_s), ("new_v_", nv_s)):
        for n, a in zip(small, _unflatten(fs, small_shapes, small_sizes)):
            outs[prefix + n] = a
    result = [loss, dh[None]]
    for prefix in ("grad_", "delta_", "new_m_", "new_v_"):
        result += [outs[prefix + n] for n in WEIGHT_NAMES]
    return tuple(result)
```
